```python
import math, functools
import jax, jax.numpy as jnp
from jax import lax
import numpy as np

D_MODEL = 1024
BATCH = 8
SEQ = 8192
DEPTH = 2
DEC_BATCH = 32
DEC_SEQ = 16
PAST_LEN = 4096

CHUNK = 64
HEAD_DIM = 64
H_A = 6
H_B = 6
G_C = 4
W_A = H_A * HEAD_DIM
W_B = H_B * HEAD_DIM
W_C = G_C * HEAD_DIM
MIX = W_A + W_B + W_C
Q_BLOCK = 128
GDN_CHUNK = CHUNK
CONV_K = 4
CM_LEN = 128
D_FF = -(-8 * D_MODEL // (3 * 256)) * 256
FORGET_BIAS = 3.0
ATTN_SCALE = HEAD_DIM ** -0.5
SPLIT_SIZES = (W_A, W_A, W_A, H_A, 3 * W_B, H_B, H_B, W_B, W_C, W_C)
SPLIT_IDX = tuple(int(i) for i in np.cumsum(SPLIT_SIZES)[:-1])
D_IN = sum(SPLIT_SIZES)

kernel_name = 'hybrid_stream_fox_gdn_sgu_step'


def rmsnorm(x, g, eps=1e-6):
    xf = x.astype(jnp.float32)
    y = xf * lax.rsqrt(jnp.mean(xf * xf, axis=-1, keepdims=True) + eps)
    return (y * g.astype(jnp.float32)).astype(x.dtype)


def layernorm(x, g, b, eps=1e-5):
    xf = x.astype(jnp.float32)
    mu = jnp.mean(xf, axis=-1, keepdims=True)
    var = jnp.mean(jnp.square(xf - mu), axis=-1, keepdims=True)
    return ((xf - mu) * lax.rsqrt(var + eps) * g.astype(jnp.float32) + b.astype(jnp.float32)).astype(x.dtype)


def l2norm(x, eps=1e-6):
    return x * lax.rsqrt(jnp.sum(x * x, axis=-1, keepdims=True) + eps)


def fox_attend_prompt(q, k, v, logf):
    B, S, H, Dh = q.shape
    nb = S // Q_BLOCK
    cT = jnp.cumsum(logf, axis=1).transpose(0, 2, 1)
    qb = q.reshape(B, nb, Q_BLOCK, H, Dh).swapaxes(0, 1)
    cqb = cT.reshape(B, H, nb, Q_BLOCK).transpose(2, 0, 1, 3)
    kpos = jnp.arange(S)

    def block(args):
        qi, cqi, i = args
        s = jnp.einsum('bqhd,bkhd->bhqk', qi, k, preferred_element_type=jnp.float32) * ATTN_SCALE
        s = s + cqi[..., :, None] - cT[..., None, :]
        qpos = i * Q_BLOCK + jnp.arange(Q_BLOCK)
        s = jnp.where(kpos[None, :] <= qpos[:, None], s, -jnp.inf)
        p = jax.nn.softmax(s, axis=-1)
        return jnp.einsum('bhqk,bkhd->bqhd', p.astype(v.dtype), v)

    o = lax.map(block, (qb, cqb, jnp.arange(nb)))
    return o.swapaxes(0, 1).reshape(B, S, H, Dh)


def fox_attend_sample(q, k, v, logf, ck, cv, clogf):
    B, n, H, Dh = q.shape
    P = ck.shape[1]
    kk = jnp.concatenate([ck.astype(k.dtype), k], axis=1)
    vv = jnp.concatenate([cv.astype(v.dtype), v], axis=1)
    c = jnp.cumsum(jnp.concatenate([clogf.astype(jnp.float32), logf], axis=1), axis=1).transpose(0, 2, 1)
    s = jnp.einsum('bqhd,bkhd->bhqk', q, kk, preferred_element_type=jnp.float32) * ATTN_SCALE
    s = s + c[..., P:, None] - c[..., None, :]
    mask = jnp.arange(P + n)[None, :] <= (P + jnp.arange(n))[:, None]
    p = jax.nn.softmax(jnp.where(mask, s, -jnp.inf), axis=-1)
    return jnp.einsum('bhqk,bkhd->bqhd', p.astype(vv.dtype), vv)


def short_conv(xin, prev, w):
    T = xin.shape[1]
    xp = jnp.concatenate([prev.astype(xin.dtype), xin], axis=1)
    y = xp[:, 0:T] * w[0]
    for i in range(1, CONV_K):
        y = y + xp[:, i:i + T] * w[i]
    return y, xp[:, -(CONV_K - 1):]


def gdn_chunked(q, k, v, beta, gl, S0):
    L = q.shape[2]
    Gh = jnp.cumsum(gl, axis=2).transpose(0, 1, 3, 2)
    incl = jnp.tril(jnp.ones((L, L), dtype=bool))
    strict = jnp.tril(jnp.ones((L, L), dtype=bool), -1)
    diff = Gh[..., :, None] - Gh[..., None, :]
    dec = jnp.exp(jnp.where(incl, diff, -jnp.inf))
    bh = beta.transpose(0, 1, 3, 2)
    kk = jnp.einsum('bnihd,bnjhd->bnhij', k, k)
    A = jnp.where(strict, bh[..., :, None] * kk * dec, 0.0)
    IA = A + jnp.eye(L, dtype=A.dtype)
    kt = k.transpose(0, 1, 3, 2, 4)
    rhs_v = v.transpose(0, 1, 3, 2, 4) * bh[..., None]
    rhs_k = kt * (bh * jnp.exp(Gh))[..., None]
    Uv = lax.linalg.triangular_solve(IA, rhs_v, left_side=True, lower=True, unit_diagonal=True)
    W = lax.linalg.triangular_solve(IA, rhs_k, left_side=True, lower=True, unit_diagonal=True)
    qk = jnp.einsum('bnihd,bnjhd->bnhij', q, k) * dec
    qg = q.transpose(0, 1, 3, 2, 4) * jnp.exp(Gh)[..., None]
    kdec = kt * jnp.exp(Gh[..., -1:] - Gh)[..., None]
    gL = jnp.exp(Gh[..., -1])

    def step(S, inp):
        uv_c, w_c, qk_c, qg_c, kd_c, gl_c = inp
        U = uv_c - jnp.einsum('bhld,bhde->bhle', w_c, S)
        o = jnp.einsum('bhld,bhde->bhle', qg_c, S) + jnp.einsum('bhij,bhje->bhie', qk_c, U)
        S = S * gl_c[..., None, None] + jnp.einsum('bhld,bhle->bhde', kd_c, U)
        return S, o

    xs = tuple(a.swapaxes(0, 1) for a in (Uv, W, qk, qg, kdec, gL))
    S, o = lax.scan(step, S0, xs)
    return o.transpose(1, 0, 3, 2, 4), S


def spatial_gate(u, v, w_s, b_s):
    L = u.shape[2]
    pos = jnp.arange(L)
    mask = (pos[None, :] // CHUNK) <= (pos[:, None] // CHUNK)
    w = jnp.where(mask, w_s[:, :L, :L], 0.0)
    s = jnp.einsum('gij,bnjgc->bnigc', w, v) + b_s[:, :L].T[None, None, :, :, None]
    return u * s


def layer(x, p, conv_prev, S0, attend, gdn_len, cm_len):
    B, T, _ = x.shape
    f32 = jnp.float32
    h = rmsnorm(x, p['g_pre_mix'])
    z = h @ p['w_in']
    aq, ak, av, af, bqkv, ba, bb, bz, cu, cv = jnp.split(z, SPLIT_IDX, axis=-1)
    qa = aq.reshape(B, T, H_A, HEAD_DIM)
    ka = ak.reshape(B, T, H_A, HEAD_DIM)
    va = av.reshape(B, T, H_A, HEAD_DIM)
    logf = jax.nn.log_sigmoid(af.astype(f32) + p['b_f'].astype(f32))
    oa = rmsnorm(attend(qa, ka, va, logf).reshape(B, T, W_A), p['g_a_out']).astype(x.dtype)
    yc, conv_new = short_conv(bqkv, conv_prev, p['conv_w'])
    yc = jax.nn.silu(yc.astype(f32))
    qb, kb, vb = jnp.split(yc, 3, axis=-1)
    qb = l2norm(qb.reshape(B, T, H_B, HEAD_DIM)) * ATTN_SCALE
    kb = l2norm(kb.reshape(B, T, H_B, HEAD_DIM))
    vb = vb.reshape(B, T, H_B, HEAD_DIM)
    beta = jax.nn.sigmoid(bb.astype(f32))
    gl = -jnp.exp(p['a_log'].astype(f32)) * jax.nn.softplus(ba.astype(f32) + p['dt_bias'].astype(f32))
    nc = T // gdn_len
    rs = lambda t: t.reshape((B, nc, gdn_len) + t.shape[2:])
    ob, S_new = gdn_chunked(rs(qb), rs(kb), rs(vb), rs(beta), rs(gl), S0.astype(f32))
    ob = rmsnorm(ob.reshape(B, T, H_B, HEAD_DIM), p['g_b_out']) * jax.nn.silu(bz.astype(f32).reshape(B, T, H_B, HEAD_DIM))
    ob = ob.reshape(B, T, W_B).astype(x.dtype)
    u = jax.nn.gelu(cu)
    vn = layernorm(jax.nn.gelu(cv), p['g_cv'], p['b_cv'])
    nm = T // cm_len
    rc = lambda t: t.reshape(B, nm, cm_len, G_C, W_C // G_C)
    oc = spatial_gate(rc(u), rc(vn), p['w_s'], p['b_s']).reshape(B, T, W_C)
    oc = rmsnorm(oc, p['g_c_out']).astype(x.dtype)
    m = jnp.concatenate([oa, ob, oc], axis=-1) @ p['w_out']
    x = x + rmsnorm(m, p['g_post_mix'])
    gate, up = jnp.split(rmsnorm(x, p['g_pre_ffn']) @ p['w_ffn_in'], 2, axis=-1)
    x = x + rmsnorm((jax.nn.silu(gate) * up) @ p['w_ffn_out'], p['g_post_ffn'])
    return x, (ka, va, logf, conv_new, S_new, vn)


def setup_inputs(seed: int = 0) -> dict:
    key = jax.random.key(seed)
    ks = jax.random.split(key, 26)
    f32 = jnp.float32
    nrm = lambda k, shape, s: jax.random.normal(k, shape, f32) * s
    gain = lambda k, shape: 1.0 + 0.05 * jax.random.normal(k, shape, f32)
    x_prompt = nrm(ks[0], (BATCH, SEQ, D_MODEL), 1.0)
    x_sample = nrm(ks[1], (DEC_BATCH, DEC_SEQ, D_MODEL), 1.0)
    cache_a_k = nrm(ks[2], (DEPTH, DEC_BATCH, PAST_LEN, H_A, HEAD_DIM), 1.0)
    cache_a_v = nrm(ks[3], (DEPTH, DEC_BATCH, PAST_LEN, H_A, HEAD_DIM), 1.0)
    cache_a_logf = jax.nn.log_sigmoid(FORGET_BIAS + nrm(ks[4], (DEPTH, DEC_BATCH, PAST_LEN, H_A), 1.0))
    state_b_conv = nrm(ks[5], (DEPTH, DEC_BATCH, CONV_K - 1, 3 * W_B), 1.0)
    state_b_S = nrm(ks[6], (DEPTH, DEC_BATCH, H_B, HEAD_DIM, HEAD_DIM), 0.1)
    g_pre_mix = gain(ks[7], (DEPTH, D_MODEL))
    w_in = nrm(ks[8], (DEPTH, D_MODEL, D_IN), D_MODEL ** -0.5)
    b_f = FORGET_BIAS + nrm(ks[9], (DEPTH, H_A), 0.1)
    conv_w = nrm(ks[10], (DEPTH, CONV_K, 3 * W_B), 0.5)
    a_log = jnp.log(jax.random.uniform(ks[11], (DEPTH, H_B), f32, 1.0, 16.0))
    dt = jnp.exp(jax.random.uniform(ks[12], (DEPTH, H_B), f32, math.log(1e-3), math.log(1e-1)))
    dt_bias = dt + jnp.log(-jnp.expm1(-dt))
    g_b_out = gain(ks[13], (DEPTH, HEAD_DIM))
    g_a_out = gain(ks[14], (DEPTH, W_A))
    g_cv = gain(ks[15], (DEPTH, W_C))
    b_cv = nrm(ks[16], (DEPTH, W_C), 0.02)
    w_s = nrm(ks[17], (DEPTH, G_C, CM_LEN, CM_LEN), CM_LEN ** -0.5)
    b_s = 1.0 + nrm(ks[18], (DEPTH, G_C, CM_LEN), 0.1)
    g_c_out = gain(ks[19], (DEPTH, W_C))
    w_out = nrm(ks[20], (DEPTH, MIX, D_MODEL), MIX ** -0.5)
    g_post_mix = gain(ks[21], (DEPTH, D_MODEL))
    g_pre_ffn = gain(ks[22], (DEPTH, D_MODEL))
    w_ffn_in = nrm(ks[23], (DEPTH, D_MODEL, 2 * D_FF), D_MODEL ** -0.5)
    w_ffn_out = nrm(ks[24], (DEPTH, D_FF, D_MODEL), D_FF ** -0.5)
    g_post_ffn = gain(ks[25], (DEPTH, D_MODEL))
    return {'x_prompt': x_prompt, 'x_sample': x_sample, 'cache_a_k': cache_a_k, 'cache_a_v': cache_a_v,
            'cache_a_logf': cache_a_logf, 'state_b_conv': state_b_conv, 'state_b_S': state_b_S,
            'g_pre_mix': g_pre_mix, 'w_in': w_in, 'b_f': b_f, 'conv_w': conv_w, 'a_log': a_log,
            'dt_bias': dt_bias, 'g_b_out': g_b_out, 'g_a_out': g_a_out, 'g_cv': g_cv, 'b_cv': b_cv,
            'w_s': w_s, 'b_s': b_s, 'g_c_out': g_c_out, 'w_out': w_out, 'g_post_mix': g_post_mix,
            'g_pre_ffn': g_pre_ffn, 'w_ffn_in': w_ffn_in, 'w_ffn_out': w_ffn_out, 'g_post_ffn': g_post_ffn}


def reference(x_prompt, x_sample, cache_a_k, cache_a_v, cache_a_logf, state_b_conv, state_b_S,
              g_pre_mix, w_in, b_f, conv_w, a_log, dt_bias, g_b_out, g_a_out, g_cv, b_cv,
              w_s, b_s, g_c_out, w_out, g_post_mix, g_pre_ffn, w_ffn_in, w_ffn_out, g_post_ffn):
    bp = x_prompt.shape[0]
    n_new = x_sample.shape[1]
    yp, ys = x_prompt, x_sample
    outs_p, outs_s = [], []
    for l in range(DEPTH):
        p = dict(g_pre_mix=g_pre_mix[l], w_in=w_in[l], b_f=b_f[l], conv_w=conv_w[l], a_log=a_log[l],
                 dt_bias=dt_bias[l], g_b_out=g_b_out[l], g_a_out=g_a_out[l], g_cv=g_cv[l], b_cv=b_cv[l],
                 w_s=w_s[l], b_s=b_s[l], g_c_out=g_c_out[l], w_out=w_out[l], g_post_mix=g_post_mix[l],
                 g_pre_ffn=g_pre_ffn[l], w_ffn_in=w_ffn_in[l], w_ffn_out=w_ffn_out[l], g_post_ffn=g_post_ffn[l])
        conv0 = jnp.zeros((bp, CONV_K - 1, 3 * W_B), x_prompt.dtype)
        S0 = jnp.zeros((bp, H_B, HEAD_DIM, HEAD_DIM), jnp.float32)
        yp, st_p = layer(yp, p, conv0, S0, fox_attend_prompt, GDN_CHUNK, CM_LEN)
        attend_s = functools.partial(fox_attend_sample, ck=cache_a_k[l], cv=cache_a_v[l], clogf=cache_a_logf[l])
        ys, st_s = layer(ys, p, state_b_conv[l], state_b_S[l], attend_s, n_new, n_new)
        outs_p.append(st_p)
        outs_s.append(st_s)
    stk = lambda outs, i: jnp.stack([o[i] for o in outs], axis=0)
    new_a_k_prompt = stk(outs_p, 0)
    new_a_v_prompt = stk(outs_p, 1)
    new_a_logf_prompt = stk(outs_p, 2)
    new_b_conv_prompt = stk(outs_p, 3)
    new_b_S_prompt = stk(outs_p, 4)
    new_a_k_sample = stk(outs_s, 0)
    new_a_v_sample = stk(outs_s, 1)
    new_a_logf_sample = stk(outs_s, 2)
    new_b_conv_sample = stk(outs_s, 3)
    new_b_S_sample = stk(outs_s, 4)
    new_c_v_sample = stk(outs_s, 5)
    return (yp, ys, new_a_k_prompt, new_a_v_prompt, new_a_logf_prompt, new_b_conv_prompt, new_b_S_prompt,
            new_a_k_sample, new_a_v_sample, new_a_logf_sample, new_b_conv_sample, new_b_S_sample, new_c_v_sample)
```

```python
import functools

import jax
import jax.numpy as jnp
from jax import lax
from jax.experimental import pallas as pl
from jax.experimental.pallas import tpu as pltpu

F32 = jnp.float32
BF16 = jnp.bfloat16

LANES = 128
SUBLANES = 8
HEAD_DIM = 64
GDN_BLOCK = 128
VMEM_LIMIT = 56 * 1024 * 1024
NEG_INF = float("-inf")


def _dot(a, b):
    return jnp.dot(a.astype(BF16), b.astype(BF16), preferred_element_type=F32)


def _dot_nt(a, b):
    return lax.dot_general(a.astype(BF16), b.astype(BF16), (((1,), (1,)), ((), ())),
                           preferred_element_type=F32)


def _dot_select_exact(x, sel):
    hi = x.astype(BF16)
    r1 = x - hi.astype(F32)
    mid = r1.astype(BF16)
    lo = (r1 - mid.astype(F32)).astype(BF16)
    d = lambda p: jnp.dot(p, sel, preferred_element_type=F32)
    return (d(hi) + d(mid)) + d(lo)


def _rms(x, g, eps=1e-6):
    return x * lax.rsqrt(jnp.mean(x * x, axis=-1, keepdims=True) + eps) * g


def _softplus(x):
    return jnp.maximum(x, 0.0) + jnp.log1p(jnp.exp(-jnp.abs(x)))


def _sigmoid(x):
    return 1.0 / (1.0 + jnp.exp(-x))


def _seg_cumsum(v, seg):
    row = lax.broadcasted_iota(jnp.int32, v.shape, 0)
    pos = jnp.bitwise_and(row, seg - 1)
    s = 1
    while s < seg:
        v = v + jnp.where(pos >= s, pltpu.roll(v, s, 0), 0.0)
        s *= 2
    return v


def _const_spec(shape):
    nd = len(shape)
    return pl.BlockSpec(shape, lambda *_: (0,) * nd, pipeline_mode=pl.Buffered(1))


def _params(*sem):
    return pltpu.CompilerParams(dimension_semantics=sem, vmem_limit_bytes=VMEM_LIMIT)


def _inproj_kernel(x_ref, gpre_ref, wbig_ref, wsm_ref, sp_ref, convw_ref, convinit_ref, gcv_ref,
                   bcv_ref, ws_ref, bs_ref, gco_ref, hsum_ref,
                   qa_ref, ka_ref, va_ref, kab_ref, vab_ref, elem_ref, cum_ref, qb_ref, kb_ref,
                   vb_ref, bz_ref, oc_ref, vn_ref, ytail_ref,
                   carry_conv, carry_cum, *, tm, cm, nh, wa, wb, wc, scale):
    t = pl.program_id(1)
    h = _rms(x_ref[...], gpre_ref[...]).astype(BF16)
    z = jnp.dot(h, wbig_ref[...], preferred_element_type=F32)
    zs = jnp.dot(h, wsm_ref[...], preferred_element_type=F32)

    o_k, o_v, o_b = wa, 2 * wa, 3 * wa
    o_z = o_b + 3 * wb
    o_u = o_z + wb
    o_cv = o_u + wc
    ka = z[:, o_k:o_v]
    va = z[:, o_v:o_b]
    qa_ref[...] = (z[:, :wa] * scale).astype(BF16)
    ka_ref[...] = ka
    va_ref[...] = va
    kab_ref[...] = ka.astype(BF16)
    vab_ref[...] = va.astype(BF16)

    lane = lax.broadcasted_iota(jnp.int32, (tm, LANES), 1)
    zb = zs + sp_ref[0:1, :]
    soft_tail = jnp.log1p(jnp.exp(-jnp.abs(zb)))
    logf = -(jnp.maximum(-zb, 0.0) + soft_tail)
    gl = -jnp.exp(sp_ref[1:2, :]) * (jnp.maximum(zb, 0.0) + soft_tail)
    beta = _sigmoid(zs)
    elem = jnp.where(lane < nh, logf, jnp.where(lane < 2 * nh, gl, jnp.where(lane < 3 * nh, beta, 0.0)))
    elem_ref[...] = elem

    @pl.when(t == 0)
    def _():
        carry_cum[...] = jnp.zeros_like(carry_cum)
        carry_conv[...] = convinit_ref[...]

    cum = _seg_cumsum(elem, tm) + carry_cum[...]
    cum_ref[...] = cum
    carry_cum[...] = cum[tm - 1:tm, :]

    y = z[:, o_b:o_z]
    prev = carry_conv[...]
    row8 = lax.broadcasted_iota(jnp.int32, prev.shape, 0)
    kw = convw_ref.shape[0]
    acc = y * convw_ref[kw - 1:kw, :]
    for k in range(1, kw):
        yk = pltpu.roll(y, k, 0)
        top = jnp.where(row8 < k, pltpu.roll(prev, k, 0), yk[0:SUBLANES])
        yk = jnp.concatenate([top, yk[SUBLANES:]], axis=0)
        acc = acc + yk * convw_ref[kw - 1 - k:kw - k, :]
    carry_conv[...] = y[tm - SUBLANES:tm]
    ytail_ref[...] = y[tm - SUBLANES:tm]
    yc = acc * _sigmoid(acc)
    qb = yc[:, :wb]
    kb = yc[:, wb:2 * wb]
    sq = jnp.concatenate([qb * qb, kb * kb], axis=-1).astype(BF16)
    hw = hsum_ref.shape[0]
    ss = jnp.concatenate([jnp.dot(sq[:, j:j + hw], hsum_ref[...], preferred_element_type=F32)
                          for j in range(0, 2 * wb, hw)], axis=-1)
    qb_ref[...] = qb * lax.rsqrt(ss[:, :wb] + 1e-6) * scale
    kb_ref[...] = kb * lax.rsqrt(ss[:, wb:] + 1e-6)
    vb_ref[...] = yc[:, 2 * wb:]
    bz_ref[...] = z[:, o_z:o_u]

    u = jax.nn.gelu(z[:, o_u:o_cv])
    gv = jax.nn.gelu(z[:, o_cv:])
    mu = jnp.mean(gv, axis=-1, keepdims=True)
    var = jnp.mean(jnp.square(gv - mu), axis=-1, keepdims=True)
    vn = (gv - mu) * lax.rsqrt(var + 1e-5) * gcv_ref[...] + bcv_ref[...]
    vn_ref[...] = vn
    first = lax.broadcasted_iota(jnp.int32, (cm, LANES), 1) < HEAD_DIM
    kpad = ws_ref.shape[2] - 2 * cm
    rows = []
    for c in range(tm // cm):
        vc = vn[c * cm:(c + 1) * cm]
        cols = []
        for pp in range(wc // LANES):
            vp = vc[:, pp * LANES:(pp + 1) * LANES]
            parts = [jnp.where(first, vp, 0.0), jnp.where(first, 0.0, vp)]
            if kpad:
                parts.append(jnp.zeros((kpad, LANES), F32))
            cols.append(_dot(ws_ref[pp], jnp.concatenate(parts, axis=0)))
        s = jnp.concatenate(cols, axis=-1) + bs_ref[...]
        rows.append(u[c * cm:(c + 1) * cm] * s)
    oc = rows[0] if len(rows) == 1 else jnp.concatenate(rows, axis=0)
    oc_ref[...] = _rms(oc, gco_ref[...])


def _inproj(x, lw, conv_init, *, tm, cm):
    b, t, d = x.shape
    nt = t // tm
    wa, wb, wc, nh = lw["wa"], lw["wb"], lw["wc"], lw["nh"]
    tok = lambda w: pl.BlockSpec((None, tm, w), lambda i, j: (i, j, 0))
    per_b = lambda r, w: pl.BlockSpec((None, r, w), lambda i, j: (i, 0, 0))
    outs = [("qa", wa, BF16), ("ka", wa, F32), ("va", wa, F32), ("kab", wa, BF16), ("vab", wa, BF16),
            ("elem", LANES, F32), ("cum", LANES, F32), ("qb", wb, F32), ("kb", wb, F32), ("vb", wb, F32),
            ("bz", wb, F32), ("oc", wc, F32), ("vn", wc, F32)]
    out_shape = [jax.ShapeDtypeStruct((b, t, w), dt) for _, w, dt in outs]
    out_specs = [tok(w) for _, w, _ in outs]
    out_shape.append(jax.ShapeDtypeStruct((b, SUBLANES, 3 * wb), F32))
    out_specs.append(per_b(SUBLANES, 3 * wb))
    consts = [lw["g_pre_mix"], lw["w_big"], lw["w_small"], lw["sp"], lw["conv_w"]]
    consts2 = [lw["g_cv"], lw["b_cv"], lw["ws_cat"][cm], lw["bs_full"][cm], lw["g_c_out"], lw["hsum"]]
    kern = functools.partial(_inproj_kernel, tm=tm, cm=cm, nh=nh, wa=wa, wb=wb, wc=wc,
                             scale=HEAD_DIM ** -0.5)
    res = pl.pallas_call(
        kern,
        grid=(b, nt),
        in_specs=[tok(d)] + [_const_spec(c.shape) for c in consts] + [per_b(SUBLANES, 3 * wb)]
                 + [_const_spec(c.shape) for c in consts2],
        out_specs=out_specs,
        out_shape=out_shape,
        scratch_shapes=[pltpu.VMEM((SUBLANES, 3 * wb), F32), pltpu.VMEM((1, LANES), F32)],
        compiler_params=_params("arbitrary", "arbitrary"),
        name="inproj",
    )(x, *consts, conv_init, *consts2)
    named = {n: r for (n, _, _), r in zip(outs, res[:-1])}
    named["ytail"] = res[-1]
    return named


def _attn_kernel(q_ref, k_ref, v_ref, cum_ref, ct_ref, o_ref, *, tq, tk):
    p = pl.program_id(1)
    i = pl.program_id(2)
    q = q_ref[...]
    lane = lax.broadcasted_iota(jnp.int32, (tq, LANES), 1)
    first = lane < HEAD_DIM
    zero = jnp.zeros_like(q)
    qm = (jnp.where(first, q, zero), jnp.where(first, zero, q))
    cum = cum_ref[...]
    cq = tuple(jnp.sum(jnp.where(lane == 2 * p + e, cum, 0.0), axis=-1, keepdims=True) for e in range(2))
    qpos = i * tq + lax.broadcasted_iota(jnp.int32, (tq, tk), 0)
    kofs = lax.broadcasted_iota(jnp.int32, (tq, tk), 1)

    def step(j, carry, masked):
        ms, ls, acc = carry
        k0 = pl.multiple_of(j * tk, tk)
        kb = k_ref[pl.ds(k0, tk), :]
        vb = v_ref[pl.ds(k0, tk), :]
        pv, alphas, ms2, ls2 = [], [], [], []
        for e in range(2):
            s = _dot_nt(qm[e], kb) + cq[e] - ct_ref[e:e + 1, pl.ds(k0, tk)]
            if masked:
                s = jnp.where(k0 + kofs <= qpos, s, NEG_INF)
            m_new = jnp.maximum(ms[e], jnp.max(s, axis=-1, keepdims=True))
            alpha = jnp.exp(ms[e] - m_new)
            pr = jnp.exp(s - m_new)
            ls2.append(alpha * ls[e] + jnp.sum(pr, axis=-1, keepdims=True))
            ms2.append(m_new)
            alphas.append(alpha)
            pv.append(jnp.dot(pr.astype(BF16), vb, preferred_element_type=F32))
        acc = acc * jnp.where(first, alphas[0], alphas[1]) + jnp.where(first, pv[0], pv[1])
        return tuple(ms2), tuple(ls2), acc

    neg = jnp.full((tq, 1), NEG_INF, F32)
    zl = jnp.zeros((tq, 1), F32)
    init = ((neg, neg), (zl, zl), jnp.zeros((tq, LANES), F32))
    nfull = (i * tq) // tk
    carry = lax.fori_loop(0, nfull, lambda j, c: step(j, c, False), init)
    for d in range(tq // tk):
        carry = step(nfull + d, carry, True)
    _, ls, acc = carry
    o_ref[...] = acc / jnp.where(first, ls[0], ls[1])


def _attn_prompt(qa, kab, vab, cum, ct, *, tq, tk):
    b, s, wa = qa.shape
    npair = wa // LANES
    kern = functools.partial(_attn_kernel, tq=tq, tk=tk)
    return pl.pallas_call(
        kern,
        grid=(b, npair, s // tq),
        in_specs=[pl.BlockSpec((None, tq, LANES), lambda bi, p, i: (bi, i, p)),
                  pl.BlockSpec((None, s, LANES), lambda bi, p, i: (bi, 0, p)),
                  pl.BlockSpec((None, s, LANES), lambda bi, p, i: (bi, 0, p)),
                  pl.BlockSpec((None, tq, LANES), lambda bi, p, i: (bi, i, 0)),
                  pl.BlockSpec((None, None, 2, s), lambda bi, p, i: (bi, p, 0, 0))],
        out_specs=pl.BlockSpec((None, tq, LANES), lambda bi, p, i: (bi, i, p)),
        out_shape=jax.ShapeDtypeStruct((b, s, wa), F32),
        compiler_params=_params("arbitrary", "arbitrary", "arbitrary"),
        name="attn_prompt",
    )(qa, kab, vab, cum, ct)


def _attn_sample_kernel(q_ref, kc_ref, vc_ref, kn_ref, vn_ref, cum_ref, rrow_ref, crow_ref, o_ref, *, n):
    p = pl.program_id(1)
    q = q_ref[...]
    lane = lax.broadcasted_iota(jnp.int32, (n, LANES), 1)
    first = lane < HEAD_DIM
    zero = jnp.zeros_like(q)
    kc = kc_ref[...].astype(BF16)
    vc = vc_ref[...].astype(BF16)
    pad = jnp.zeros((LANES - n, LANES), BF16)
    kn = jnp.concatenate([kn_ref[...], pad], axis=0)
    vn = jnp.concatenate([vn_ref[...], pad], axis=0)
    cum = cum_ref[...]
    causal = lane <= lax.broadcasted_iota(jnp.int32, (n, LANES), 0)
    outs = []
    for e in range(2):
        qm = jnp.where(first, q, zero) if e == 0 else jnp.where(first, zero, q)
        cq = jnp.sum(jnp.where(lane == 2 * p + e, cum, 0.0), axis=-1, keepdims=True)
        sc = _dot_nt(qm, kc) + cq + rrow_ref[e:e + 1, :]
        sn = jnp.where(causal, _dot_nt(qm, kn) + cq - crow_ref[e:e + 1, :], NEG_INF)
        m = jnp.maximum(jnp.max(sc, axis=-1, keepdims=True), jnp.max(sn, axis=-1, keepdims=True))
        pc = jnp.exp(sc - m)
        pn = jnp.exp(sn - m)
        l = jnp.sum(pc, axis=-1, keepdims=True) + jnp.sum(pn, axis=-1, keepdims=True)
        outs.append((_dot(pc, vc) + _dot(pn, vn)) / l)
    o_ref[...] = jnp.where(first, outs[0], outs[1])


def _attn_sample(qa, kab, vab, cum, cache_k, cache_v, rrow, crow):
    b, n, wa = qa.shape
    past = cache_k.shape[1]
    npair = wa // LANES
    new = lambda: pl.BlockSpec((None, n, LANES), lambda bi, p: (bi, 0, p))
    old = lambda: pl.BlockSpec((None, past, LANES), lambda bi, p: (bi, 0, p))
    return pl.pallas_call(
        functools.partial(_attn_sample_kernel, n=n),
        grid=(b, npair),
        in_specs=[new(), old(), old(), new(), new(),
                  pl.BlockSpec((None, n, LANES), lambda bi, p: (bi, 0, 0)),
                  pl.BlockSpec((None, None, 2, past), lambda bi, p: (bi, p, 0, 0)),
                  pl.BlockSpec((None, None, 2, LANES), lambda bi, p: (bi, p, 0, 0))],
        out_specs=new(),
        out_shape=jax.ShapeDtypeStruct((b, n, wa), F32),
        compiler_params=_params("arbitrary", "arbitrary"),
        name="attn_sample",
    )(qa, cache_k, cache_v, kab, vab, cum, rrow, crow)


def _scan_kernel(x_ref, o_ref, carry, *, tp):
    @pl.when(pl.program_id(1) == 0)
    def _():
        carry[...] = jnp.zeros_like(carry)
    v = _seg_cumsum(x_ref[...], tp) + carry[...]
    o_ref[...] = v
    carry[...] = v[tp - 1:tp, :]


def _seq_cumsum(x, *, tp):
    b, p, w = x.shape
    spec = pl.BlockSpec((None, tp, w), lambda i, j: (i, j, 0))
    return pl.pallas_call(
        functools.partial(_scan_kernel, tp=tp),
        grid=(b, p // tp),
        in_specs=[spec], out_specs=spec,
        out_shape=jax.ShapeDtypeStruct(x.shape, F32),
        scratch_shapes=[pltpu.VMEM((1, w), F32)],
        compiler_params=_params("arbitrary", "arbitrary"),
        name="seq_cumsum",
    )(x)


def _gdn_kernel(q_ref, k_ref, v_ref, bz_ref, elem_ref, s0_ref, gb_ref, esel_ref, hsum_ref,
                o_ref, sout_ref, s_scr, *, nb, nh):
    L = GDN_BLOCK
    t = pl.program_id(1)

    @pl.when(t == 0)
    def _():
        s_scr[...] = s0_ref[...]

    lane = lax.broadcasted_iota(jnp.int32, (L, LANES), 1)
    first = lane < HEAD_DIM
    ri = lax.broadcasted_iota(jnp.int32, (L, L), 0)
    ci = lax.broadcasted_iota(jnp.int32, (L, L), 1)
    incl = ci <= ri
    strict = ci < ri
    same_head = (ri < HEAD_DIM) == (ci < HEAD_DIM)
    lane2 = lax.broadcasted_iota(jnp.int32, (L, 2 * L), 1)
    left = lane2 < L
    first2 = jnp.bitwise_and(lane2, LANES - 1) < HEAD_DIM
    xor2 = jnp.bitwise_xor(lax.broadcasted_iota(jnp.int32, (L, 2 * L), 0), jnp.bitwise_and(lane2, L - 1))

    def halves(x, sel):
        return jnp.concatenate([jnp.where(sel, x, 0.0), jnp.where(sel, 0.0, x)], axis=0)

    def block(n, _):
        r0 = pl.multiple_of(n * L, L)
        rows = pl.ds(r0, L)
        elem = elem_ref[rows, :]
        gsum = _seg_cumsum(elem, L)
        mixed = jnp.where((lane >= nh) & (lane < 2 * nh), gsum, elem)
        ex = _dot_select_exact(mixed, esel_ref[...])
        wbw = ex.shape[1] // 2
        for p in range(wbw // LANES):
            cols = slice(p * LANES, (p + 1) * LANES)
            g = ex[:, cols]
            bt = ex[:, wbw + p * LANES: wbw + (p + 1) * LANES]
            kp = k_ref[rows, cols]
            qp = q_ref[rows, cols]
            vp = v_ref[rows, cols]
            g_sw = pltpu.roll(g, HEAD_DIM, 1)
            b_sw = pltpu.roll(bt, HEAD_DIM, 1)
            g_t = g.T
            a_parts, qk_parts = [], []
            for e in range(2):
                sel = first if e == 0 else jnp.logical_not(first)
                gcol = jnp.where(sel, g, g_sw)
                bcol = jnp.where(sel, bt, b_sw)
                grow = g_t[e * HEAD_DIM:e * HEAD_DIM + 1, :]
                dec = jnp.exp(jnp.where(incl, gcol - grow, NEG_INF))
                kk = _dot_nt(jnp.where(sel, kp, 0.0), kp)
                qk_parts.append(_dot_nt(jnp.where(sel, qp, 0.0), kp) * dec)
                a_parts.append(jnp.where(strict, bcol * kk * dec, 0.0))
            a_cat = jnp.concatenate(a_parts, axis=1)
            qk_cat = jnp.concatenate(qk_parts, axis=1)
            tm1 = -jnp.where(xor2 < 2, a_cat, 0.0)
            s_blk = 2
            while s_blk < L:
                nmat = jnp.where((xor2 >= s_blk) & (xor2 < 2 * s_blk), a_cat, 0.0)
                pm = nmat + _dot(tm1, halves(nmat, left))
                tm1 = tm1 - pm - _dot(pm, halves(tm1, left))
                s_blk *= 2
            eg = jnp.exp(g)
            r = jnp.concatenate([vp * bt, kp * bt * eg], axis=1)
            uw = r + _dot(tm1, halves(r, first2))
            uv = uw[:, :LANES]
            w = uw[:, LANES:]
            glast = g[L - 1:L, :]
            kdec = kp * jnp.exp(glast - g)
            s = s_scr[p]
            ws = _dot(jnp.concatenate([w, qp * eg], axis=0), s)
            u = uv - ws[:L]
            o = ws[L:] + _dot(qk_cat, halves(u, first))
            s_scr[p] = s * jnp.exp(glast) + jnp.where(same_head, _dot(kdec.T, u), 0.0)
            ms = _dot(o * o, hsum_ref[...]) * (1.0 / HEAD_DIM)
            bz = bz_ref[rows, cols]
            o_ref[rows, cols] = o * lax.rsqrt(ms + 1e-6) * gb_ref[...] * (bz * _sigmoid(bz))
        return 0

    lax.fori_loop(0, nb, block, 0)

    @pl.when(t == pl.num_programs(1) - 1)
    def _():
        sout_ref[...] = s_scr[...]


def _gdn(qb, kb, vb, bz, elem, s0, lw, *, nb):
    b, t, wb = qb.shape
    tile = nb * GDN_BLOCK
    npair = wb // LANES
    tok = lambda w: pl.BlockSpec((None, tile, w), lambda i, j: (i, j, 0))
    st = pl.BlockSpec((None, npair, LANES, LANES), lambda i, j: (i, 0, 0, 0))
    consts = [lw["g_b_pair"], lw["esel"], lw["hsum128"]]
    return pl.pallas_call(
        functools.partial(_gdn_kernel, nb=nb, nh=lw["nh"]),
        grid=(b, t // tile),
        in_specs=[tok(wb), tok(wb), tok(wb), tok(wb), tok(LANES), st] + [_const_spec(c.shape) for c in consts],
        out_specs=[tok(wb), st],
        out_shape=[jax.ShapeDtypeStruct((b, t, wb), F32),
                   jax.ShapeDtypeStruct((b, npair, LANES, LANES), F32)],
        scratch_shapes=[pltpu.VMEM((npair, LANES, LANES), F32)],
        compiler_params=_params("arbitrary", "arbitrary"),
        name="gdn",
    )(qb, kb, vb, bz, elem, s0, *consts)


def _outffn_kernel(oa_ref, ob_ref, oc_ref, x_ref, ga_ref, wout_ref, gpm_ref, gpf_ref, wfi_ref, wfo_ref,
                   gpo_ref, y_ref, *, dff):
    oa = _rms(oa_ref[...], ga_ref[...])
    cat = jnp.concatenate([oa, ob_ref[...], oc_ref[...]], axis=-1).astype(BF16)
    m = jnp.dot(cat, wout_ref[...], preferred_element_type=F32)
    x1 = x_ref[...] + _rms(m, gpm_ref[...])
    h = _rms(x1, gpf_ref[...]).astype(BF16)
    gu = jnp.dot(h, wfi_ref[...], preferred_element_type=F32)
    gate = gu[:, :dff]
    a = (gate * _sigmoid(gate) * gu[:, dff:]).astype(BF16)
    f = jnp.dot(a, wfo_ref[...], preferred_element_type=F32)
    y_ref[...] = x1 + _rms(f, gpo_ref[...])


def _outffn(oa, ob, oc, x, lw, *, tm):
    n, d = x.shape
    dff = lw["w_ffn_out"].shape[0]
    tok = lambda w: pl.BlockSpec((tm, w), lambda i: (i, 0))
    consts = [lw["g_a_out"], lw["w_out"], lw["g_post_mix"], lw["g_pre_ffn"], lw["w_ffn_in"],
              lw["w_ffn_out"], lw["g_post_ffn"]]
    return pl.pallas_call(
        functools.partial(_outffn_kernel, dff=dff),
        grid=(n // tm,),
        in_specs=[tok(oa.shape[1]), tok(ob.shape[1]), tok(oc.shape[1]), tok(d)]
                 + [_const_spec(c.shape) for c in consts],
        out_specs=tok(d),
        out_shape=jax.ShapeDtypeStruct((n, d), F32),
        compiler_params=_params("arbitrary"),
        name="outffn",
    )(oa, ob, oc, x, *consts)


def _block_ones(width):
    idx = jnp.arange(width) // HEAD_DIM
    return (idx[:, None] == idx[None, :]).astype(BF16)


def _layer_weights(l, prm, cms):
    w_in = prm["w_in"][l]
    nh = prm["b_f"].shape[1]
    wa = nh * HEAD_DIM
    wb = prm["a_log"].shape[1] * HEAD_DIM
    wc = prm["g_cv"].shape[1]
    ng = prm["w_s"].shape[1]
    assert prm["a_log"].shape[1] == nh and wa % LANES == 0 and wc % LANES == 0 and 3 * nh <= LANES
    sizes = (wa, wa, wa, nh, 3 * wb, nh, nh, wb, wc, wc)
    offs = [0]
    for sz in sizes:
        offs.append(offs[-1] + sz)
    col = lambda i: w_in[:, offs[i]:offs[i + 1]]
    w_big = jnp.concatenate([col(0), col(1), col(2), col(4), col(7), col(8), col(9)], axis=1).astype(BF16)
    w_small = jnp.concatenate([col(3), col(5), col(6), jnp.zeros((w_in.shape[0], LANES - 3 * nh), F32)],
                              axis=1).astype(BF16)
    zpad = jnp.zeros((LANES - 2 * nh,), F32)
    sp = jnp.zeros((SUBLANES, LANES), F32)
    sp = sp.at[0].set(jnp.concatenate([prm["b_f"][l], prm["dt_bias"][l], zpad]))
    sp = sp.at[1].set(jnp.concatenate([jnp.zeros((nh,), F32), prm["a_log"][l], zpad]))
    row = lambda v: v.reshape(1, -1)
    ws_cat, bs_full = {}, {}
    for cm in cms:
        pos = jnp.arange(cm) // HEAD_DIM
        w = jnp.where(pos[None, :] <= pos[:, None], prm["w_s"][l][:, :cm, :cm], 0.0)
        pairs = [jnp.concatenate([w[2 * pp], w[2 * pp + 1]], axis=1) for pp in range(ng // 2)]
        kpad = max(LANES - 2 * cm, 0)
        ws_cat[cm] = jnp.pad(jnp.stack(pairs), ((0, 0), (0, 0), (0, kpad))).astype(BF16)
        bs_full[cm] = jnp.repeat(prm["b_s"][l][:, :cm].T, wc // ng, axis=1)
    src = jnp.arange(LANES)[:, None]
    dst = jnp.arange(wb)[None, :] // HEAD_DIM
    esel = jnp.concatenate([src == nh + dst, src == 2 * nh + dst], axis=1).astype(BF16)
    return dict(
        nh=nh, wa=wa, wb=wb, wc=wc,
        g_pre_mix=row(prm["g_pre_mix"][l]), w_big=w_big, w_small=w_small, sp=sp, conv_w=prm["conv_w"][l],
        g_cv=row(prm["g_cv"][l]), b_cv=row(prm["b_cv"][l]), ws_cat=ws_cat, bs_full=bs_full,
        g_c_out=row(prm["g_c_out"][l]), hsum=_block_ones(2 * LANES), hsum128=_block_ones(LANES),
        g_b_pair=row(jnp.tile(prm["g_b_out"][l], LANES // HEAD_DIM)), esel=esel,
        g_a_out=row(prm["g_a_out"][l]), w_out=prm["w_out"][l].astype(BF16),
        g_post_mix=row(prm["g_post_mix"][l]), g_pre_ffn=row(prm["g_pre_ffn"][l]),
        w_ffn_in=prm["w_ffn_in"][l].astype(BF16), w_ffn_out=prm["w_ffn_out"][l].astype(BF16),
        g_post_ffn=row(prm["g_post_ffn"][l]))


def _pair_state(s):
    b, h, dk, dv = s.shape
    s = s.reshape(b, h // 2, 2, dk, dv)
    z = jnp.zeros_like(s[:, :, 0])
    top = jnp.concatenate([s[:, :, 0], z], axis=-1)
    bot = jnp.concatenate([z, s[:, :, 1]], axis=-1)
    return jnp.concatenate([top, bot], axis=-2)


def _unpair_state(sp):
    b, hp, _, _ = sp.shape
    s0 = sp[:, :, :HEAD_DIM, :HEAD_DIM]
    s1 = sp[:, :, HEAD_DIM:, HEAD_DIM:]
    return jnp.stack([s0, s1], axis=2).reshape(b, 2 * hp, HEAD_DIM, HEAD_DIM)


def _head_rows(cum, nh):
    b, t, _ = cum.shape
    return jnp.transpose(cum[:, :, :nh], (0, 2, 1)).reshape(b, nh // 2, 2, t)


def _pick(n, prefs):
    for c in prefs:
        if n % c == 0:
            return c
    return n


def _layer(x, lw, conv_prev, s0, cache, *, cm):
    b, t, d = x.shape
    nh, wb = lw["nh"], lw["wb"]
    kw1 = conv_prev.shape[1]
    conv_init = jnp.pad(conv_prev, ((0, 0), (SUBLANES - kw1, 0), (0, 0)))
    tm = _pick(t, (256, 128, 64, 32, 16))
    pj = _inproj(x, lw, conv_init, tm=tm, cm=cm)

    if cache is None:
        tq = _pick(t, (256, 128))
        oa = _attn_prompt(pj["qa"], pj["kab"], pj["vab"], pj["cum"], _head_rows(pj["cum"], nh), tq=tq, tk=tq)
    else:
        ck, cv, clogf = cache
        bs, past = clogf.shape[:2]
        rev = jnp.pad(clogf[:, ::-1, :], ((0, 0), (0, 0), (0, LANES - nh)))
        incl = _seq_cumsum(rev, tp=_pick(past, (512, 256, 128)))[:, ::-1, :]
        excl = jnp.concatenate([incl[:, 1:], jnp.zeros_like(incl[:, :1])], axis=1)
        rrow = _head_rows(excl, nh)
        crow = jnp.pad(_head_rows(pj["cum"], nh), ((0, 0), (0, 0), (0, 0), (0, LANES - t)))
        oa = _attn_sample(pj["qa"], pj["kab"], pj["vab"], pj["cum"],
                          ck.reshape(bs, past, -1), cv.reshape(bs, past, -1), rrow, crow)

    tp = -(-t // GDN_BLOCK) * GDN_BLOCK
    padt = lambda a: a if tp == t else jnp.pad(a, ((0, 0), (0, tp - t), (0, 0)))
    nb = _pick(tp // GDN_BLOCK, (4, 2, 1))
    ob, s_new = _gdn(padt(pj["qb"]), padt(pj["kb"]), padt(pj["vb"]), padt(pj["bz"]), padt(pj["elem"]),
                     _pair_state(s0), lw, nb=nb)
    ob = ob[:, :t]

    n = b * t
    y = _outffn(oa.reshape(n, -1), ob.reshape(n, -1), pj["oc"].reshape(n, -1), x.reshape(n, d), lw,
                tm=_pick(n, (256, 128, 64, 32, 16)))
    state = (pj["ka"].reshape(b, t, nh, HEAD_DIM), pj["va"].reshape(b, t, nh, HEAD_DIM), pj["elem"][:, :, :nh],
             pj["ytail"][:, SUBLANES - kw1:, :], _unpair_state(s_new), pj["vn"])
    return y.reshape(b, t, d), state


def kernel(x_prompt, x_sample, cache_a_k, cache_a_v, cache_a_logf, state_b_conv, state_b_S, g_pre_mix, w_in, b_f, conv_w, a_log, dt_bias, g_b_out, g_a_out, g_cv, b_cv, w_s, b_s, g_c_out, w_out, g_post_mix, g_pre_ffn, w_ffn_in, w_ffn_out, g_post_ffn):
    prm = dict(g_pre_mix=g_pre_mix, w_in=w_in, b_f=b_f, conv_w=conv_w, a_log=a_log, dt_bias=dt_bias,
               g_b_out=g_b_out, g_a_out=g_a_out, g_cv=g_cv, b_cv=b_cv, w_s=w_s, b_s=b_s, g_c_out=g_c_out,
               w_out=w_out, g_post_mix=g_post_mix, g_pre_ffn=g_pre_ffn, w_ffn_in=w_ffn_in,
               w_ffn_out=w_ffn_out, g_post_ffn=g_post_ffn)
    depth = w_in.shape[0]
    bp, sp_len, _ = x_prompt.shape
    n_new = x_sample.shape[1]
    cm_p = w_s.shape[2]
    assert sp_len % cm_p == 0 and sp_len % GDN_BLOCK == 0 and n_new <= HEAD_DIM and n_new % SUBLANES == 0
    kw1 = conv_w.shape[1] - 1
    nhb = a_log.shape[1]
    yp, ys = x_prompt, x_sample
    outs_p, outs_s = [], []
    for l in range(depth):
        lw = _layer_weights(l, prm, (cm_p, n_new))
        conv0 = jnp.zeros((bp, kw1, conv_w.shape[2]), F32)
        s0 = jnp.zeros((bp, nhb, HEAD_DIM, HEAD_DIM), F32)
        yp, st_p = _layer(yp, lw, conv0, s0, None, cm=cm_p)
        ys, st_s = _layer(ys, lw, state_b_conv[l], state_b_S[l],
                          (cache_a_k[l], cache_a_v[l], cache_a_logf[l]), cm=n_new)
        outs_p.append(st_p)
        outs_s.append(st_s)
    stk = lambda outs, i: jnp.stack([o[i] for o in outs], axis=0)
    return (yp, ys, stk(outs_p, 0), stk(outs_p, 1), stk(outs_p, 2), stk(outs_p, 3), stk(outs_p, 4),
            stk(outs_s, 0), stk(outs_s, 1), stk(outs_s, 2), stk(outs_s, 3), stk(outs_s, 4), stk(outs_s, 5))
```

```python
import functools

import jax
import jax.numpy as jnp
from jax import lax
from jax.experimental import pallas as pl
from jax.experimental.pallas import tpu as pltpu

F32 = jnp.float32
BF16 = jnp.bfloat16

LANES = 128
SUBLANES = 8
HEAD_DIM = 64
GDN_BLOCK = 128
VMEM_LIMIT = 56 * 1024 * 1024
NEG_INF = float("-inf")
LOG2E = 1.4426950408889634
AUG = 16


def _dot(a, b):
    return jnp.dot(a.astype(BF16), b.astype(BF16), preferred_element_type=F32)


def _dot_nt(a, b):
    return lax.dot_general(a.astype(BF16), b.astype(BF16), (((1,), (1,)), ((), ())),
                           preferred_element_type=F32)


def _dot_select_exact(x, sel):
    hi = x.astype(BF16)
    r1 = x - hi.astype(F32)
    mid = r1.astype(BF16)
    lo = (r1 - mid.astype(F32)).astype(BF16)
    d = lambda p: jnp.dot(p, sel, preferred_element_type=F32)
    return (d(hi) + d(mid)) + d(lo)


def _rms(x, g, eps=1e-6):
    return x * lax.rsqrt(jnp.mean(x * x, axis=-1, keepdims=True) + eps) * g


def _softplus(x):
    return jnp.maximum(x, 0.0) + jnp.log1p(jnp.exp(-jnp.abs(x)))


def _sigmoid(x):
    return 1.0 / (1.0 + jnp.exp(-x))


def _seg_cumsum(v, seg):
    row = lax.broadcasted_iota(jnp.int32, v.shape, 0)
    pos = jnp.bitwise_and(row, seg - 1)
    s = 1
    while s < seg:
        v = v + jnp.where(pos >= s, pltpu.roll(v, s, 0), 0.0)
        s *= 2
    return v


def _const_spec(shape):
    nd = len(shape)
    return pl.BlockSpec(shape, lambda *_: (0,) * nd, pipeline_mode=pl.Buffered(1))


def _params(*sem):
    return pltpu.CompilerParams(dimension_semantics=sem, vmem_limit_bytes=VMEM_LIMIT)


def _inproj_kernel(x_ref, gpre_ref, wbig_ref, wsm_ref, sp_ref, convw_ref, convinit_ref, gcv_ref,
                   bcv_ref, ws_ref, bs_ref, gco_ref, hsum_ref, pmat_ref,
                   qaug_ref, ka_ref, va_ref, kaug_ref, vab_ref, elem_ref, cum_ref, qb_ref, kb_ref,
                   vb_ref, bz_ref, oc_ref, vn_ref, ytail_ref,
                   carry_conv, carry_cum, *, tm, cm, nh, wa, wb, wc, scale):
    t = pl.program_id(1)
    h = _rms(x_ref[...], gpre_ref[...]).astype(BF16)
    z = jnp.dot(h, wbig_ref[...], preferred_element_type=F32)
    zs = jnp.dot(h, wsm_ref[...], preferred_element_type=F32)

    o_k, o_v, o_b = wa, 2 * wa, 3 * wa
    o_z = o_b + 3 * wb
    o_u = o_z + wb
    o_cv = o_u + wc
    ka = z[:, o_k:o_v]
    va = z[:, o_v:o_b]
    ka_ref[...] = ka
    va_ref[...] = va
    vab_ref[...] = va.astype(BF16)

    lane = lax.broadcasted_iota(jnp.int32, (tm, LANES), 1)
    zb = zs + sp_ref[0:1, :]
    soft_tail = jnp.log1p(jnp.exp(-jnp.abs(zb)))
    logf = -(jnp.maximum(-zb, 0.0) + soft_tail)
    gl = -jnp.exp(sp_ref[1:2, :]) * (jnp.maximum(zb, 0.0) + soft_tail)
    beta = _sigmoid(zs)
    elem = jnp.where(lane < nh, logf, jnp.where(lane < 2 * nh, gl, jnp.where(lane < 3 * nh, beta, 0.0)))
    elem_ref[...] = elem

    @pl.when(t == 0)
    def _():
        carry_cum[...] = jnp.zeros_like(carry_cum)
        carry_conv[...] = convinit_ref[...]

    cum = _seg_cumsum(elem, tm) + carry_cum[...]
    cum_ref[...] = cum
    carry_cum[...] = cum[tm - 1:tm, :]

    c2 = jnp.where(lane < nh, cum * LOG2E, 0.0)
    hi = c2.astype(BF16)
    r1 = c2 - hi.astype(F32)
    mid = r1.astype(BF16)
    lo = (r1 - mid.astype(F32)).astype(BF16)
    placed = jnp.dot(jnp.concatenate([hi, mid, lo], axis=1), pmat_ref[...], preferred_element_type=F32)
    augq = (placed[:, :LANES] + sp_ref[2:3, :]).astype(BF16)
    augk = (sp_ref[3:4, :] - placed[:, LANES:]).astype(BF16)
    qs = (z[:, :wa] * (scale * LOG2E)).astype(BF16)
    ks = ka.astype(BF16)
    qaug_ref[...] = jnp.concatenate(
        [a for j in range(0, wa, LANES) for a in (qs[:, j:j + LANES], augq)], axis=1)
    kaug_ref[...] = jnp.concatenate(
        [a for j in range(0, wa, LANES) for a in (ks[:, j:j + LANES], augk)], axis=1)

    y = z[:, o_b:o_z]
    prev = carry_conv[...]
    row8 = lax.broadcasted_iota(jnp.int32, prev.shape, 0)
    kw = convw_ref.shape[0]
    acc = y * convw_ref[kw - 1:kw, :]
    for k in range(1, kw):
        yk = pltpu.roll(y, k, 0)
        top = jnp.where(row8 < k, pltpu.roll(prev, k, 0), yk[0:SUBLANES])
        yk = jnp.concatenate([top, yk[SUBLANES:]], axis=0)
        acc = acc + yk * convw_ref[kw - 1 - k:kw - k, :]
    carry_conv[...] = y[tm - SUBLANES:tm]
    ytail_ref[...] = y[tm - SUBLANES:tm]
    yc = acc * _sigmoid(acc)
    qb = yc[:, :wb]
    kb = yc[:, wb:2 * wb]
    sq = jnp.concatenate([qb * qb, kb * kb], axis=-1).astype(BF16)
    hw = hsum_ref.shape[0]
    ss = jnp.concatenate([jnp.dot(sq[:, j:j + hw], hsum_ref[...], preferred_element_type=F32)
                          for j in range(0, 2 * wb, hw)], axis=-1)
    qb_ref[...] = qb * lax.rsqrt(ss[:, :wb] + 1e-6) * scale
    kb_ref[...] = kb * lax.rsqrt(ss[:, wb:] + 1e-6)
    vb_ref[...] = yc[:, 2 * wb:]
    bz_ref[...] = z[:, o_z:o_u]

    u = jax.nn.gelu(z[:, o_u:o_cv])
    gv = jax.nn.gelu(z[:, o_cv:])
    mu = jnp.mean(gv, axis=-1, keepdims=True)
    var = jnp.mean(jnp.square(gv - mu), axis=-1, keepdims=True)
    vn = (gv - mu) * lax.rsqrt(var + 1e-5) * gcv_ref[...] + bcv_ref[...]
    vn_ref[...] = vn
    first = lax.broadcasted_iota(jnp.int32, (cm, LANES), 1) < HEAD_DIM
    kpad = ws_ref.shape[2] - 2 * cm
    rows = []
    for c in range(tm // cm):
        vc = vn[c * cm:(c + 1) * cm]
        cols = []
        for pp in range(wc // LANES):
            vp = vc[:, pp * LANES:(pp + 1) * LANES]
            parts = [jnp.where(first, vp, 0.0), jnp.where(first, 0.0, vp)]
            if kpad:
                parts.append(jnp.zeros((kpad, LANES), F32))
            cols.append(_dot(ws_ref[pp], jnp.concatenate(parts, axis=0)))
        s = jnp.concatenate(cols, axis=-1) + bs_ref[...]
        rows.append(u[c * cm:(c + 1) * cm] * s)
    oc = rows[0] if len(rows) == 1 else jnp.concatenate(rows, axis=0)
    oc_ref[...] = _rms(oc, gco_ref[...])


def _inproj(x, lw, conv_init, *, tm, cm):
    b, t, d = x.shape
    nt = t // tm
    wa, wb, wc, nh = lw["wa"], lw["wb"], lw["wc"], lw["nh"]
    tok = lambda w: pl.BlockSpec((None, tm, w), lambda i, j: (i, j, 0))
    per_b = lambda r, w: pl.BlockSpec((None, r, w), lambda i, j: (i, 0, 0))
    outs = [("qaug", 2 * wa, BF16), ("ka", wa, F32), ("va", wa, F32), ("kaug", 2 * wa, BF16), ("vab", wa, BF16),
            ("elem", LANES, F32), ("cum", LANES, F32), ("qb", wb, F32), ("kb", wb, F32), ("vb", wb, F32),
            ("bz", wb, F32), ("oc", wc, F32), ("vn", wc, F32)]
    out_shape = [jax.ShapeDtypeStruct((b, t, w), dt) for _, w, dt in outs]
    out_specs = [tok(w) for _, w, _ in outs]
    out_shape.append(jax.ShapeDtypeStruct((b, SUBLANES, 3 * wb), F32))
    out_specs.append(per_b(SUBLANES, 3 * wb))
    consts = [lw["g_pre_mix"], lw["w_big"], lw["w_small"], lw["sp"], lw["conv_w"]]
    consts2 = [lw["g_cv"], lw["b_cv"], lw["ws_cat"][cm], lw["bs_full"][cm], lw["g_c_out"], lw["hsum"], lw["pmat"]]
    kern = functools.partial(_inproj_kernel, tm=tm, cm=cm, nh=nh, wa=wa, wb=wb, wc=wc,
                             scale=HEAD_DIM ** -0.5)
    res = pl.pallas_call(
        kern,
        grid=(b, nt),
        in_specs=[tok(d)] + [_const_spec(c.shape) for c in consts] + [per_b(SUBLANES, 3 * wb)]
                 + [_const_spec(c.shape) for c in consts2],
        out_specs=out_specs,
        out_shape=out_shape,
        scratch_shapes=[pltpu.VMEM((SUBLANES, 3 * wb), F32), pltpu.VMEM((1, LANES), F32)],
        compiler_params=_params("arbitrary", "arbitrary"),
        name="inproj",
    )(x, *consts, conv_init, *consts2)
    named = {n: r for (n, _, _), r in zip(outs, res[:-1])}
    named["ytail"] = res[-1]
    return named


def _head_operands(q, p):
    lane2 = lax.broadcasted_iota(jnp.int32, q.shape, 1)
    zero = jnp.zeros_like(q)
    out = []
    for e in range(2):
        a0 = LANES + AUG * (2 * p + e)
        keep = ((lane2 >= e * HEAD_DIM) & (lane2 < (e + 1) * HEAD_DIM)) | ((lane2 >= a0) & (lane2 < a0 + AUG))
        out.append(jnp.where(keep, q, zero))
    return out


def _attn_kernel(q_ref, k_ref, v_ref, o_ref, *, tq, tk):
    p = pl.program_id(1)
    i = pl.program_id(2)
    qs = _head_operands(q_ref[...], p)
    first = lax.broadcasted_iota(jnp.int32, (tq, LANES), 1) < HEAD_DIM
    qpos = i * tq + lax.broadcasted_iota(jnp.int32, (tq, tk), 0)
    kofs = lax.broadcasted_iota(jnp.int32, (tq, tk), 1)
    ones = jnp.ones((tk, LANES), BF16)

    def scores(j):
        kb = k_ref[pl.ds(pl.multiple_of(j * tk, tk), tk), :]
        return tuple(lax.dot_general(qs[e], kb, (((1,), (1,)), ((), ())), preferred_element_type=F32)
                     for e in range(2))

    def update(j, ss, carry, masked):
        ms, ls, acc = carry
        k0 = pl.multiple_of(j * tk, tk)
        vb = jnp.concatenate([v_ref[pl.ds(k0, tk), :], ones], axis=1)
        pv, alphas, ms2, ls2 = [], [], [], []
        for e in range(2):
            s = ss[e]
            if masked:
                s = jnp.where(k0 + kofs <= qpos, s, NEG_INF)
            m_new = jnp.maximum(ms[e], jnp.max(s, axis=-1, keepdims=True))
            alpha = jnp.exp2(ms[e] - m_new)
            pr = jnp.exp2(s - m_new).astype(BF16)
            r = jnp.dot(pr, vb, preferred_element_type=F32)
            ls2.append(alpha * ls[e] + r[:, LANES:])
            ms2.append(m_new)
            alphas.append(alpha)
            pv.append(r[:, :LANES])
        acc = acc * jnp.where(first, alphas[0], alphas[1]) + jnp.where(first, pv[0], pv[1])
        return tuple(ms2), tuple(ls2), acc

    neg = jnp.full((tq, 1), NEG_INF, F32)
    zl = jnp.zeros((tq, LANES), F32)
    init = ((neg, neg), (zl, zl), zl)
    carry = lax.fori_loop(0, i, lambda j, c: update(j, scores(j), c, False), init)
    _, ls, acc = update(i, scores(i), carry, True)
    o_ref[...] = acc / jnp.where(first, ls[0], ls[1])


def _attn_prompt(qaug, kaug, vab, *, tq, tk):
    b, s, wa = vab.shape
    npair = wa // LANES
    kern = functools.partial(_attn_kernel, tq=tq, tk=tk)
    return pl.pallas_call(
        kern,
        grid=(b, npair, s // tq),
        in_specs=[pl.BlockSpec((None, tq, 2 * LANES), lambda bi, p, i: (bi, i, p)),
                  pl.BlockSpec((None, s, 2 * LANES), lambda bi, p, i: (bi, 0, p)),
                  pl.BlockSpec((None, s, LANES), lambda bi, p, i: (bi, 0, p))],
        out_specs=pl.BlockSpec((None, tq, LANES), lambda bi, p, i: (bi, i, p)),
        out_shape=jax.ShapeDtypeStruct((b, s, wa), F32),
        compiler_params=_params("arbitrary", "arbitrary", "arbitrary"),
        name="attn_prompt",
    )(qaug, kaug, vab)


def _attn_sample_kernel(q_ref, kc_ref, vc_ref, kn_ref, vn_ref, cum_ref, rrow_ref, crow_ref, o_ref, *, n):
    p = pl.program_id(1)
    q = q_ref[:, :LANES]
    lane = lax.broadcasted_iota(jnp.int32, (n, LANES), 1)
    first = lane < HEAD_DIM
    zero = jnp.zeros_like(q)
    kc = kc_ref[...].astype(BF16)
    vc = vc_ref[...].astype(BF16)
    pad = jnp.zeros((LANES - n, LANES), BF16)
    kn = jnp.concatenate([kn_ref[:, :LANES], pad], axis=0)
    vn = jnp.concatenate([vn_ref[...], pad], axis=0)
    cum = cum_ref[...]
    causal = lane <= lax.broadcasted_iota(jnp.int32, (n, LANES), 0)
    outs = []
    for e in range(2):
        qm = jnp.where(first, q, zero) if e == 0 else jnp.where(first, zero, q)
        cq = jnp.sum(jnp.where(lane == 2 * p + e, cum, 0.0), axis=-1, keepdims=True)
        sc = _dot_nt(qm, kc) + LOG2E * (cq + rrow_ref[e:e + 1, :])
        sn = jnp.where(causal, _dot_nt(qm, kn) + LOG2E * (cq - crow_ref[e:e + 1, :]), NEG_INF)
        m = jnp.maximum(jnp.max(sc, axis=-1, keepdims=True), jnp.max(sn, axis=-1, keepdims=True))
        pc = jnp.exp2(sc - m)
        pn = jnp.exp2(sn - m)
        l = jnp.sum(pc, axis=-1, keepdims=True) + jnp.sum(pn, axis=-1, keepdims=True)
        outs.append((_dot(pc, vc) + _dot(pn, vn)) / l)
    o_ref[...] = jnp.where(first, outs[0], outs[1])


def _attn_sample(qaug, kaug, vab, cum, cache_k, cache_v, rrow, crow):
    b, n, wa = vab.shape
    past = cache_k.shape[1]
    npair = wa // LANES
    new = lambda w: pl.BlockSpec((None, n, w), lambda bi, p: (bi, 0, p))
    old = lambda: pl.BlockSpec((None, past, LANES), lambda bi, p: (bi, 0, p))
    return pl.pallas_call(
        functools.partial(_attn_sample_kernel, n=n),
        grid=(b, npair),
        in_specs=[new(2 * LANES), old(), old(), new(2 * LANES), new(LANES),
                  pl.BlockSpec((None, n, LANES), lambda bi, p: (bi, 0, 0)),
                  pl.BlockSpec((None, None, 2, past), lambda bi, p: (bi, p, 0, 0)),
                  pl.BlockSpec((None, None, 2, LANES), lambda bi, p: (bi, p, 0, 0))],
        out_specs=new(LANES),
        out_shape=jax.ShapeDtypeStruct((b, n, wa), F32),
        compiler_params=_params("arbitrary", "arbitrary"),
        name="attn_sample",
    )(qaug, cache_k, cache_v, kaug, vab, cum, rrow, crow)


def _scan_kernel(x_ref, o_ref, carry, *, tp):
    @pl.when(pl.program_id(1) == 0)
    def _():
        carry[...] = jnp.zeros_like(carry)
    v = _seg_cumsum(x_ref[...], tp) + carry[...]
    o_ref[...] = v
    carry[...] = v[tp - 1:tp, :]


def _seq_cumsum(x, *, tp):
    b, p, w = x.shape
    spec = pl.BlockSpec((None, tp, w), lambda i, j: (i, j, 0))
    return pl.pallas_call(
        functools.partial(_scan_kernel, tp=tp),
        grid=(b, p // tp),
        in_specs=[spec], out_specs=spec,
        out_shape=jax.ShapeDtypeStruct(x.shape, F32),
        scratch_shapes=[pltpu.VMEM((1, w), F32)],
        compiler_params=_params("arbitrary", "arbitrary"),
        name="seq_cumsum",
    )(x)


def _gdn_kernel(q_ref, k_ref, v_ref, bz_ref, elem_ref, s0_ref, gb_ref, esel_ref, hsum_ref,
                o_ref, sout_ref, s_scr, *, nb, nh):
    L = GDN_BLOCK
    t = pl.program_id(1)

    @pl.when(t == 0)
    def _():
        s_scr[...] = s0_ref[...]

    lane = lax.broadcasted_iota(jnp.int32, (L, LANES), 1)
    first = lane < HEAD_DIM
    ri = lax.broadcasted_iota(jnp.int32, (L, L), 0)
    ci = lax.broadcasted_iota(jnp.int32, (L, L), 1)
    incl = ci <= ri
    strict = ci < ri
    same_head = (ri < HEAD_DIM) == (ci < HEAD_DIM)
    lane2 = lax.broadcasted_iota(jnp.int32, (L, 2 * L), 1)
    left = lane2 < L
    first2 = jnp.bitwise_and(lane2, LANES - 1) < HEAD_DIM
    xor2 = jnp.bitwise_xor(lax.broadcasted_iota(jnp.int32, (L, 2 * L), 0), jnp.bitwise_and(lane2, L - 1))

    def halves(x, sel):
        return jnp.concatenate([jnp.where(sel, x, 0.0), jnp.where(sel, 0.0, x)], axis=0)

    npair = s_scr.shape[0]
    wbw = npair * LANES
    chains = [(n, p) for n in range(nb) for p in range(npair)]
    ex = {}
    for n in range(nb):
        elem = elem_ref[n * L:(n + 1) * L, :]
        gsum = _seg_cumsum(elem, L)
        mixed = jnp.where((lane >= nh) & (lane < 2 * nh), gsum, elem)
        ex[n] = _dot_select_exact(mixed, esel_ref[...])

    c = {}
    for n, p in chains:
        rows = slice(n * L, (n + 1) * L)
        cols = slice(p * LANES, (p + 1) * LANES)
        g = ex[n][:, cols]
        bt = ex[n][:, wbw + p * LANES: wbw + (p + 1) * LANES]
        kp = k_ref[rows, cols]
        qp = q_ref[rows, cols]
        g_sw = pltpu.roll(g, HEAD_DIM, 1)
        b_sw = pltpu.roll(bt, HEAD_DIM, 1)
        g_t = g.T
        a_parts, qk_parts = [], []
        for e in range(2):
            sel = first if e == 0 else jnp.logical_not(first)
            gcol = jnp.where(sel, g, g_sw)
            bcol = jnp.where(sel, bt, b_sw)
            grow = g_t[e * HEAD_DIM:e * HEAD_DIM + 1, :]
            dec = jnp.exp(jnp.where(incl, gcol - grow, NEG_INF))
            kk = _dot_nt(jnp.where(sel, kp, 0.0), kp)
            qk_parts.append(_dot_nt(jnp.where(sel, qp, 0.0), kp) * dec)
            a_parts.append(jnp.where(strict, bcol * kk * dec, 0.0))
        a_cat = jnp.concatenate(a_parts, axis=1)
        eg = jnp.exp(g)
        glast = g[L - 1:L, :]
        c[n, p] = dict(a=a_cat, qk=jnp.concatenate(qk_parts, axis=1), glast=glast, qg=qp * eg,
                       kdec=kp * jnp.exp(glast - g),
                       r=jnp.concatenate([v_ref[rows, cols] * bt, kp * bt * eg], axis=1),
                       tm1=-jnp.where(xor2 < 2, a_cat, 0.0))

    s_blk = 2
    while s_blk < L:
        pm = {}
        for key in chains:
            nmat = jnp.where((xor2 >= s_blk) & (xor2 < 2 * s_blk), c[key]["a"], 0.0)
            pm[key] = nmat + _dot(c[key]["tm1"], halves(nmat, left))
        for key in chains:
            c[key]["tm1"] = c[key]["tm1"] - pm[key] - _dot(pm[key], halves(c[key]["tm1"], left))
        s_blk *= 2
    for key in chains:
        r = c[key]["r"]
        c[key]["uw"] = r + _dot(c[key]["tm1"], halves(r, first2))

    for n in range(nb):
        rows = slice(n * L, (n + 1) * L)
        pairs = range(npair)
        ws = [_dot(jnp.concatenate([c[n, p]["uw"][:, LANES:], c[n, p]["qg"]], axis=0), s_scr[p]) for p in pairs]
        u = [c[n, p]["uw"][:, :LANES] - ws[p][:L] for p in pairs]
        o = [ws[p][L:] + _dot(c[n, p]["qk"], halves(u[p], first)) for p in pairs]
        for p in pairs:
            s_scr[p] = (s_scr[p] * jnp.exp(c[n, p]["glast"])
                        + jnp.where(same_head, _dot(c[n, p]["kdec"].T, u[p]), 0.0))
        for p in pairs:
            cols = slice(p * LANES, (p + 1) * LANES)
            ms = _dot(o[p] * o[p], hsum_ref[...]) * (1.0 / HEAD_DIM)
            bz = bz_ref[rows, cols]
            o_ref[rows, cols] = o[p] * lax.rsqrt(ms + 1e-6) * gb_ref[...] * (bz * _sigmoid(bz))

    @pl.when(t == pl.num_programs(1) - 1)
    def _():
        sout_ref[...] = s_scr[...]


def _gdn(qb, kb, vb, bz, elem, s0, lw, *, nb):
    b, t, wb = qb.shape
    tile = nb * GDN_BLOCK
    npair = wb // LANES
    tok = lambda w: pl.BlockSpec((None, tile, w), lambda i, j: (i, j, 0))
    st = pl.BlockSpec((None, npair, LANES, LANES), lambda i, j: (i, 0, 0, 0))
    consts = [lw["g_b_pair"], lw["esel"], lw["hsum128"]]
    return pl.pallas_call(
        functools.partial(_gdn_kernel, nb=nb, nh=lw["nh"]),
        grid=(b, t // tile),
        in_specs=[tok(wb), tok(wb), tok(wb), tok(wb), tok(LANES), st] + [_const_spec(c.shape) for c in consts],
        out_specs=[tok(wb), st],
        out_shape=[jax.ShapeDtypeStruct((b, t, wb), F32),
                   jax.ShapeDtypeStruct((b, npair, LANES, LANES), F32)],
        scratch_shapes=[pltpu.VMEM((npair, LANES, LANES), F32)],
        compiler_params=_params("arbitrary", "arbitrary"),
        name="gdn",
    )(qb, kb, vb, bz, elem, s0, *consts)


def _outffn_kernel(oa_ref, ob_ref, oc_ref, x_ref, ga_ref, wout_ref, gpm_ref, gpf_ref, wfi_ref, wfo_ref,
                   gpo_ref, y_ref, *, dff):
    oa = _rms(oa_ref[...], ga_ref[...])
    cat = jnp.concatenate([oa, ob_ref[...], oc_ref[...]], axis=-1).astype(BF16)
    m = jnp.dot(cat, wout_ref[...], preferred_element_type=F32)
    x1 = x_ref[...] + _rms(m, gpm_ref[...])
    h = _rms(x1, gpf_ref[...]).astype(BF16)
    gu = jnp.dot(h, wfi_ref[...], preferred_element_type=F32)
    gate = gu[:, :dff]
    a = (gate * _sigmoid(gate) * gu[:, dff:]).astype(BF16)
    f = jnp.dot(a, wfo_ref[...], preferred_element_type=F32)
    y_ref[...] = x1 + _rms(f, gpo_ref[...])


def _outffn(oa, ob, oc, x, lw, *, tm):
    n, d = x.shape
    dff = lw["w_ffn_out"].shape[0]
    tok = lambda w: pl.BlockSpec((tm, w), lambda i: (i, 0))
    consts = [lw["g_a_out"], lw["w_out"], lw["g_post_mix"], lw["g_pre_ffn"], lw["w_ffn_in"],
              lw["w_ffn_out"], lw["g_post_ffn"]]
    return pl.pallas_call(
        functools.partial(_outffn_kernel, dff=dff),
        grid=(n // tm,),
        in_specs=[tok(oa.shape[1]), tok(ob.shape[1]), tok(oc.shape[1]), tok(d)]
                 + [_const_spec(c.shape) for c in consts],
        out_specs=tok(d),
        out_shape=jax.ShapeDtypeStruct((n, d), F32),
        compiler_params=_params("arbitrary"),
        name="outffn",
    )(oa, ob, oc, x, *consts)


def _block_ones(width):
    idx = jnp.arange(width) // HEAD_DIM
    return (idx[:, None] == idx[None, :]).astype(BF16)


def _layer_weights(l, prm, cms):
    w_in = prm["w_in"][l]
    nh = prm["b_f"].shape[1]
    wa = nh * HEAD_DIM
    wb = prm["a_log"].shape[1] * HEAD_DIM
    wc = prm["g_cv"].shape[1]
    ng = prm["w_s"].shape[1]
    assert prm["a_log"].shape[1] == nh and wa % LANES == 0 and wc % LANES == 0 and AUG * nh <= LANES
    sizes = (wa, wa, wa, nh, 3 * wb, nh, nh, wb, wc, wc)
    offs = [0]
    for sz in sizes:
        offs.append(offs[-1] + sz)
    col = lambda i: w_in[:, offs[i]:offs[i + 1]]
    w_big = jnp.concatenate([col(0), col(1), col(2), col(4), col(7), col(8), col(9)], axis=1).astype(BF16)
    w_small = jnp.concatenate([col(3), col(5), col(6), jnp.zeros((w_in.shape[0], LANES - 3 * nh), F32)],
                              axis=1).astype(BF16)
    zpad = jnp.zeros((LANES - 2 * nh,), F32)
    sp = jnp.zeros((SUBLANES, LANES), F32)
    sp = sp.at[0].set(jnp.concatenate([prm["b_f"][l], prm["dt_bias"][l], zpad]))
    sp = sp.at[1].set(jnp.concatenate([jnp.zeros((nh,), F32), prm["a_log"][l], zpad]))
    hl = jnp.arange(nh) * AUG
    sp = sp.at[2, (hl[:, None] + jnp.arange(3, 6)[None, :]).reshape(-1)].set(1.0)
    sp = sp.at[3, (hl[:, None] + jnp.arange(0, 3)[None, :]).reshape(-1)].set(1.0)
    pmat = jnp.zeros((3 * LANES, 2 * LANES), F32)
    for piece in range(3):
        pmat = pmat.at[piece * LANES + jnp.arange(nh), hl + piece].set(1.0)
        pmat = pmat.at[piece * LANES + jnp.arange(nh), LANES + hl + 3 + piece].set(1.0)
    row = lambda v: v.reshape(1, -1)
    ws_cat, bs_full = {}, {}
    for cm in cms:
        pos = jnp.arange(cm) // HEAD_DIM
        w = jnp.where(pos[None, :] <= pos[:, None], prm["w_s"][l][:, :cm, :cm], 0.0)
        pairs = [jnp.concatenate([w[2 * pp], w[2 * pp + 1]], axis=1) for pp in range(ng // 2)]
        kpad = max(LANES - 2 * cm, 0)
        ws_cat[cm] = jnp.pad(jnp.stack(pairs), ((0, 0), (0, 0), (0, kpad))).astype(BF16)
        bs_full[cm] = jnp.repeat(prm["b_s"][l][:, :cm].T, wc // ng, axis=1)
    src = jnp.arange(LANES)[:, None]
    dst = jnp.arange(wb)[None, :] // HEAD_DIM
    esel = jnp.concatenate([src == nh + dst, src == 2 * nh + dst], axis=1).astype(BF16)
    return dict(
        nh=nh, wa=wa, wb=wb, wc=wc,
        g_pre_mix=row(prm["g_pre_mix"][l]), w_big=w_big, w_small=w_small, sp=sp, conv_w=prm["conv_w"][l],
        g_cv=row(prm["g_cv"][l]), b_cv=row(prm["b_cv"][l]), ws_cat=ws_cat, bs_full=bs_full,
        g_c_out=row(prm["g_c_out"][l]), hsum=_block_ones(2 * LANES), hsum128=_block_ones(LANES), pmat=pmat.astype(BF16),
        g_b_pair=row(jnp.tile(prm["g_b_out"][l], LANES // HEAD_DIM)), esel=esel,
        g_a_out=row(prm["g_a_out"][l]), w_out=prm["w_out"][l].astype(BF16),
        g_post_mix=row(prm["g_post_mix"][l]), g_pre_ffn=row(prm["g_pre_ffn"][l]),
        w_ffn_in=prm["w_ffn_in"][l].astype(BF16), w_ffn_out=prm["w_ffn_out"][l].astype(BF16),
        g_post_ffn=row(prm["g_post_ffn"][l]))


def _pair_state(s):
    b, h, dk, dv = s.shape
    s = s.reshape(b, h // 2, 2, dk, dv)
    z = jnp.zeros_like(s[:, :, 0])
    top = jnp.concatenate([s[:, :, 0], z], axis=-1)
    bot = jnp.concatenate([z, s[:, :, 1]], axis=-1)
    return jnp.concatenate([top, bot], axis=-2)


def _unpair_state(sp):
    b, hp, _, _ = sp.shape
    s0 = sp[:, :, :HEAD_DIM, :HEAD_DIM]
    s1 = sp[:, :, HEAD_DIM:, HEAD_DIM:]
    return jnp.stack([s0, s1], axis=2).reshape(b, 2 * hp, HEAD_DIM, HEAD_DIM)


def _head_rows(cum, nh):
    b, t, _ = cum.shape
    return jnp.transpose(cum[:, :, :nh], (0, 2, 1)).reshape(b, nh // 2, 2, t)


def _pick(n, prefs):
    for c in prefs:
        if n % c == 0:
            return c
    return n


def _layer(x, lw, conv_prev, s0, cache, *, cm):
    b, t, d = x.shape
    nh, wb = lw["nh"], lw["wb"]
    kw1 = conv_prev.shape[1]
    conv_init = jnp.pad(conv_prev, ((0, 0), (SUBLANES - kw1, 0), (0, 0)))
    tm = _pick(t, (256, 128, 64, 32, 16))
    pj = _inproj(x, lw, conv_init, tm=tm, cm=cm)

    if cache is None:
        tq = _pick(t, (512, 256, 128))
        oa = _attn_prompt(pj["qaug"], pj["kaug"], pj["vab"], tq=tq, tk=tq)
    else:
        ck, cv, clogf = cache
        bs, past = clogf.shape[:2]
        rev = jnp.transpose(jnp.transpose(clogf, (0, 2, 1))[:, :, ::-1], (0, 2, 1))
        rev = jnp.pad(rev, ((0, 0), (0, 0), (0, LANES - nh)))
        incl = _seq_cumsum(rev, tp=_pick(past, (512, 256, 128)))[:, :, :nh]
        incl = jnp.transpose(incl, (0, 2, 1))[:, :, ::-1]
        excl = jnp.concatenate([incl[:, :, 1:], jnp.zeros_like(incl[:, :, :1])], axis=2)
        rrow = excl.reshape(bs, nh // 2, 2, past)
        crow = jnp.pad(_head_rows(pj["cum"], nh), ((0, 0), (0, 0), (0, 0), (0, LANES - t)))
        oa = _attn_sample(pj["qaug"], pj["kaug"], pj["vab"], pj["cum"],
                          ck.reshape(bs, past, -1), cv.reshape(bs, past, -1), rrow, crow)

    tp = -(-t // GDN_BLOCK) * GDN_BLOCK
    padt = lambda a: a if tp == t else jnp.pad(a, ((0, 0), (0, tp - t), (0, 0)))
    nb = _pick(tp // GDN_BLOCK, (4, 2, 1))
    ob, s_new = _gdn(padt(pj["qb"]), padt(pj["kb"]), padt(pj["vb"]), padt(pj["bz"]), padt(pj["elem"]),
                     _pair_state(s0), lw, nb=nb)
    ob = ob[:, :t]

    n = b * t
    y = _outffn(oa.reshape(n, -1), ob.reshape(n, -1), pj["oc"].reshape(n, -1), x.reshape(n, d), lw,
                tm=_pick(n, (256, 128, 64, 32, 16)))
    state = (pj["ka"].reshape(b, t, nh, HEAD_DIM), pj["va"].reshape(b, t, nh, HEAD_DIM), pj["elem"][:, :, :nh],
             pj["ytail"][:, SUBLANES - kw1:, :], _unpair_state(s_new), pj["vn"])
    return y.reshape(b, t, d), state


def kernel(x_prompt, x_sample, cache_a_k, cache_a_v, cache_a_logf, state_b_conv, state_b_S, g_pre_mix, w_in, b_f, conv_w, a_log, dt_bias, g_b_out, g_a_out, g_cv, b_cv, w_s, b_s, g_c_out, w_out, g_post_mix, g_pre_ffn, w_ffn_in, w_ffn_out, g_post_ffn):
    prm = dict(g_pre_mix=g_pre_mix, w_in=w_in, b_f=b_f, conv_w=conv_w, a_log=a_log, dt_bias=dt_bias,
               g_b_out=g_b_out, g_a_out=g_a_out, g_cv=g_cv, b_cv=b_cv, w_s=w_s, b_s=b_s, g_c_out=g_c_out,
               w_out=w_out, g_post_mix=g_post_mix, g_pre_ffn=g_pre_ffn, w_ffn_in=w_ffn_in,
               w_ffn_out=w_ffn_out, g_post_ffn=g_post_ffn)
    depth = w_in.shape[0]
    bp, sp_len, _ = x_prompt.shape
    n_new = x_sample.shape[1]
    cm_p = w_s.shape[2]
    assert sp_len % cm_p == 0 and sp_len % GDN_BLOCK == 0 and n_new <= HEAD_DIM and n_new % SUBLANES == 0
    kw1 = conv_w.shape[1] - 1
    nhb = a_log.shape[1]
    yp, ys = x_prompt, x_sample
    outs_p, outs_s = [], []
    for l in range(depth):
        lw = _layer_weights(l, prm, (cm_p, n_new))
        conv0 = jnp.zeros((bp, kw1, conv_w.shape[2]), F32)
        s0 = jnp.zeros((bp, nhb, HEAD_DIM, HEAD_DIM), F32)
        yp, st_p = _layer(yp, lw, conv0, s0, None, cm=cm_p)
        ys, st_s = _layer(ys, lw, state_b_conv[l], state_b_S[l],
                          (cache_a_k[l], cache_a_v[l], cache_a_logf[l]), cm=n_new)
        outs_p.append(st_p)
        outs_s.append(st_s)
    stk = lambda outs, i: jnp.stack([o[i] for o in outs], axis=0)
    return (yp, ys, stk(outs_p, 0), stk(outs_p, 1), stk(outs_p, 2), stk(outs_p, 3), stk(outs_p, 4),
            stk(outs_s, 0), stk(outs_s, 1), stk(outs_s, 2), stk(outs_s, 3), stk(outs_s, 4), stk(outs_s, 5))
```

```python
import functools

import jax
import jax.numpy as jnp
from jax import lax
from jax.experimental import pallas as pl
from jax.experimental.pallas import tpu as pltpu

F32 = jnp.float32
BF16 = jnp.bfloat16

LANES = 128
SUBLANES = 8
HEAD_DIM = 64
GDN_BLOCK = 128
ATTN_TQ = 1024
ATTN_KC = 256
ATTN_LOOK = 2
VMEM_LIMIT = 56 * 1024 * 1024
NEG_INF = float("-inf")
LOG2E = 1.4426950408889634
AUG = 16


def _dot(a, b):
    return jnp.dot(a.astype(BF16), b.astype(BF16), preferred_element_type=F32)


def _dot_nt(a, b):
    return lax.dot_general(a.astype(BF16), b.astype(BF16), (((1,), (1,)), ((), ())),
                           preferred_element_type=F32)


def _dot_select_exact(x, sel):
    hi = x.astype(BF16)
    r1 = x - hi.astype(F32)
    mid = r1.astype(BF16)
    lo = (r1 - mid.astype(F32)).astype(BF16)
    d = lambda p: jnp.dot(p, sel, preferred_element_type=F32)
    return (d(hi) + d(mid)) + d(lo)


def _rms(x, g, eps=1e-6):
    return x * lax.rsqrt(jnp.mean(x * x, axis=-1, keepdims=True) + eps) * g


def _softplus(x):
    return jnp.maximum(x, 0.0) + jnp.log1p(jnp.exp(-jnp.abs(x)))


def _sigmoid(x):
    return 1.0 / (1.0 + jnp.exp(-x))


def _seg_cumsum(v, seg):
    row = lax.broadcasted_iota(jnp.int32, v.shape, 0)
    pos = jnp.bitwise_and(row, seg - 1)
    s = 1
    while s < seg:
        v = v + jnp.where(pos >= s, pltpu.roll(v, s, 0), 0.0)
        s *= 2
    return v


def _const_spec(shape):
    nd = len(shape)
    return pl.BlockSpec(shape, lambda *_: (0,) * nd, pipeline_mode=pl.Buffered(1))


def _params(*sem):
    return pltpu.CompilerParams(dimension_semantics=sem, vmem_limit_bytes=VMEM_LIMIT)


def _inproj_kernel(x_ref, gpre_ref, wbig_ref, wsm_ref, sp_ref, convw_ref, convinit_ref, gcv_ref,
                   bcv_ref, ws_ref, bs_ref, gco_ref, hsum_ref, pmat_ref,
                   qaug_ref, ka_ref, va_ref, kaug_ref, vab_ref, elem_ref, cum_ref, qb_ref, kb_ref,
                   vb_ref, bz_ref, oc_ref, vn_ref, ytail_ref,
                   carry_conv, carry_cum, *, tm, cm, nh, wa, wb, wc, scale):
    @pl.when(pl.program_id(1) == 0)
    def _():
        carry_cum[...] = jnp.zeros_like(carry_cum)
        carry_conv[...] = convinit_ref[...]

    h = _rms(x_ref[...], gpre_ref[...]).astype(BF16)
    o_c = 3 * wb
    o_a = o_c + 2 * wc
    y = jnp.dot(h, wbig_ref[:, :o_c], preferred_element_type=F32)
    zc = jnp.dot(h, wbig_ref[:, o_c:o_a], preferred_element_type=F32)
    zs = jnp.dot(h, wsm_ref[...], preferred_element_type=F32)
    za = jnp.dot(h, wbig_ref[:, o_a:], preferred_element_type=F32)

    ka = za[:, wa:2 * wa]
    va = za[:, 2 * wa:3 * wa]
    ka_ref[...] = ka
    va_ref[...] = va
    vab_ref[...] = va.astype(BF16)
    bz_ref[...] = za[:, 3 * wa:]

    lane = lax.broadcasted_iota(jnp.int32, (tm, LANES), 1)
    zb = zs + sp_ref[0:1, :]
    soft_tail = jnp.log1p(jnp.exp(-jnp.abs(zb)))
    logf = -(jnp.maximum(-zb, 0.0) + soft_tail)
    gl = -jnp.exp(sp_ref[1:2, :]) * (jnp.maximum(zb, 0.0) + soft_tail)
    beta = _sigmoid(zs)
    elem = jnp.where(lane < nh, logf, jnp.where(lane < 2 * nh, gl, jnp.where(lane < 3 * nh, beta, 0.0)))
    elem_ref[...] = elem

    cum = _seg_cumsum(elem, tm) + carry_cum[...]
    cum_ref[...] = cum
    carry_cum[...] = cum[tm - 1:tm, :]

    c2 = jnp.where(lane < nh, cum * LOG2E, 0.0)
    hi = c2.astype(BF16)
    r1 = c2 - hi.astype(F32)
    mid = r1.astype(BF16)
    lo = (r1 - mid.astype(F32)).astype(BF16)
    placed = jnp.dot(jnp.concatenate([hi, mid, lo], axis=1), pmat_ref[...], preferred_element_type=F32)
    augq = (placed[:, :LANES] + sp_ref[2:3, :]).astype(BF16)
    augk = (sp_ref[3:4, :] - placed[:, LANES:]).astype(BF16)
    qs = (za[:, :wa] * (scale * LOG2E)).astype(BF16)
    ks = ka.astype(BF16)
    qaug_ref[...] = jnp.concatenate(
        [a for j in range(0, wa, LANES) for a in (qs[:, j:j + LANES], augq)], axis=1)
    kaug_ref[...] = jnp.concatenate(
        [a for j in range(0, wa, LANES) for a in (ks[:, j:j + LANES], augk)], axis=1)

    prev = carry_conv[...]
    row8 = lax.broadcasted_iota(jnp.int32, prev.shape, 0)
    kw = convw_ref.shape[0]
    acc = y * convw_ref[kw - 1:kw, :]
    for k in range(1, kw):
        yk = pltpu.roll(y, k, 0)
        top = jnp.where(row8 < k, pltpu.roll(prev, k, 0), yk[0:SUBLANES])
        yk = jnp.concatenate([top, yk[SUBLANES:]], axis=0)
        acc = acc + yk * convw_ref[kw - 1 - k:kw - k, :]
    carry_conv[...] = y[tm - SUBLANES:tm]
    ytail_ref[...] = y[tm - SUBLANES:tm]
    yc = acc * _sigmoid(acc)
    qb = yc[:, :wb]
    kb = yc[:, wb:2 * wb]
    sq = jnp.concatenate([qb * qb, kb * kb], axis=-1).astype(BF16)
    hw = hsum_ref.shape[0]
    ss = jnp.concatenate([jnp.dot(sq[:, j:j + hw], hsum_ref[...], preferred_element_type=F32)
                          for j in range(0, 2 * wb, hw)], axis=-1)
    qb_ref[...] = qb * lax.rsqrt(ss[:, :wb] + 1e-6) * scale
    kb_ref[...] = kb * lax.rsqrt(ss[:, wb:] + 1e-6)
    vb_ref[...] = yc[:, 2 * wb:]

    u = jax.nn.gelu(zc[:, :wc])
    gv = jax.nn.gelu(zc[:, wc:])
    mu = jnp.mean(gv, axis=-1, keepdims=True)
    var = jnp.mean(jnp.square(gv - mu), axis=-1, keepdims=True)
    vn = (gv - mu) * lax.rsqrt(var + 1e-5) * gcv_ref[...] + bcv_ref[...]
    vn_ref[...] = vn
    first = lax.broadcasted_iota(jnp.int32, (cm, LANES), 1) < HEAD_DIM
    kpad = ws_ref.shape[2] - 2 * cm
    rows = []
    for c in range(tm // cm):
        vc = vn[c * cm:(c + 1) * cm]
        cols = []
        for pp in range(wc // LANES):
            vp = vc[:, pp * LANES:(pp + 1) * LANES]
            parts = [jnp.where(first, vp, 0.0), jnp.where(first, 0.0, vp)]
            if kpad:
                parts.append(jnp.zeros((kpad, LANES), F32))
            cols.append(_dot(ws_ref[pp], jnp.concatenate(parts, axis=0)))
        s = jnp.concatenate(cols, axis=-1) + bs_ref[...]
        rows.append(u[c * cm:(c + 1) * cm] * s)
    oc = rows[0] if len(rows) == 1 else jnp.concatenate(rows, axis=0)
    oc_ref[...] = _rms(oc, gco_ref[...])


def _inproj(x, lw, conv_init, *, tm, cm):
    b, t, d = x.shape
    nt = t // tm
    wa, wb, wc, nh = lw["wa"], lw["wb"], lw["wc"], lw["nh"]
    tok = lambda w: pl.BlockSpec((None, tm, w), lambda i, j: (i, j, 0))
    per_b = lambda r, w: pl.BlockSpec((None, r, w), lambda i, j: (i, 0, 0))
    outs = [("qaug", 2 * wa, BF16), ("ka", wa, F32), ("va", wa, F32), ("kaug", 2 * wa, BF16), ("vab", wa, BF16),
            ("elem", LANES, F32), ("cum", LANES, F32), ("qb", wb, F32), ("kb", wb, F32), ("vb", wb, F32),
            ("bz", wb, F32), ("oc", wc, F32), ("vn", wc, F32)]
    out_shape = [jax.ShapeDtypeStruct((b, t, w), dt) for _, w, dt in outs]
    out_specs = [tok(w) for _, w, _ in outs]
    out_shape.append(jax.ShapeDtypeStruct((b, SUBLANES, 3 * wb), F32))
    out_specs.append(per_b(SUBLANES, 3 * wb))
    consts = [lw["g_pre_mix"], lw["w_big"], lw["w_small"], lw["sp"], lw["conv_w"]]
    consts2 = [lw["g_cv"], lw["b_cv"], lw["ws_cat"][cm], lw["bs_full"][cm], lw["g_c_out"], lw["hsum"], lw["pmat"]]
    kern = functools.partial(_inproj_kernel, tm=tm, cm=cm, nh=nh, wa=wa, wb=wb, wc=wc,
                             scale=HEAD_DIM ** -0.5)
    res = pl.pallas_call(
        kern,
        grid=(b, nt),
        in_specs=[tok(d)] + [_const_spec(c.shape) for c in consts] + [per_b(SUBLANES, 3 * wb)]
                 + [_const_spec(c.shape) for c in consts2],
        out_specs=out_specs,
        out_shape=out_shape,
        scratch_shapes=[pltpu.VMEM((SUBLANES, 3 * wb), F32), pltpu.VMEM((1, LANES), F32)],
        compiler_params=_params("arbitrary", "arbitrary"),
        name="inproj",
    )(x, *consts, conv_init, *consts2)
    named = {n: r for (n, _, _), r in zip(outs, res[:-1])}
    named["ytail"] = res[-1]
    return named


def _attn_kernel(qt_ref, k_ref, vt_ref, o_ref, *, tq, kc, look):
    p = pl.program_id(1)
    i = pl.program_id(2)
    qt = qt_ref[...]
    rowi = lax.broadcasted_iota(jnp.int32, qt.shape, 0)
    zero = jnp.zeros_like(qt)
    qts = []
    for e in range(2):
        a0 = LANES + AUG * (2 * p + e)
        keep = ((rowi >= e * HEAD_DIM) & (rowi < (e + 1) * HEAD_DIM)) | ((rowi >= a0) & (rowi < a0 + AUG))
        qts.append(jnp.where(keep, qt, zero))
    qpos = i * tq + lax.broadcasted_iota(jnp.int32, (kc, tq), 1)
    kofs = lax.broadcasted_iota(jnp.int32, (kc, tq), 0)
    ones = jnp.ones((2 * SUBLANES, kc), BF16)
    units = [(c, e) for c in range(tq // kc) for e in range(2)]

    def scores(j, c, e):
        k0 = pl.multiple_of(j * tq + c * kc, kc)
        return jnp.dot(k_ref[pl.ds(k0, kc), :], qts[e], preferred_element_type=F32)

    def fold(j, c, e, s, st, masked):
        m, l, acc = st
        k0 = pl.multiple_of(j * tq + c * kc, kc)
        if masked:
            s = jnp.where(k0 + kofs <= qpos, s, NEG_INF)
        m_new = jnp.maximum(m, jnp.max(s, axis=0, keepdims=True))
        alpha = jnp.exp2(m - m_new)
        pt = jnp.exp2(s - m_new).astype(BF16)
        vt = jnp.concatenate([vt_ref[e * HEAD_DIM:(e + 1) * HEAD_DIM, pl.ds(k0, kc)], ones], axis=0)
        r = jnp.dot(vt, pt, preferred_element_type=F32)
        return m_new, alpha * l + r[HEAD_DIM:HEAD_DIM + 1], alpha * acc + r[:HEAD_DIM]

    def block(j, state, masked):
        state = list(state)
        pend = {}
        for k in range(min(look, len(units))):
            pend[k] = scores(j, *units[k])
        for k, (c, e) in enumerate(units):
            if k + look < len(units):
                pend[k + look] = scores(j, *units[k + look])
            state[e] = fold(j, c, e, pend.pop(k), state[e], masked)
        return tuple(state)

    st0 = (jnp.full((1, tq), NEG_INF, F32), jnp.zeros((1, tq), F32), jnp.zeros((HEAD_DIM, tq), F32))
    state = lax.fori_loop(0, i, lambda j, s: block(j, s, False), (st0, st0))
    state = block(i, state, True)
    ot = jnp.concatenate([acc / l for _, l, acc in state], axis=0)
    o_ref[...] = ot.T


def _attn_prompt(qaug, kaug, vab, *, tq, kc, look):
    b, s, wa = vab.shape
    npair = wa // LANES
    qt = jnp.transpose(qaug, (0, 2, 1))
    vt = jnp.transpose(vab, (0, 2, 1))
    kern = functools.partial(_attn_kernel, tq=tq, kc=kc, look=look)
    return pl.pallas_call(
        kern,
        grid=(b, npair, s // tq),
        in_specs=[pl.BlockSpec((None, 2 * LANES, tq), lambda bi, p, i: (bi, p, i)),
                  pl.BlockSpec((None, s, 2 * LANES), lambda bi, p, i: (bi, 0, p)),
                  pl.BlockSpec((None, LANES, s), lambda bi, p, i: (bi, p, 0))],
        out_specs=pl.BlockSpec((None, tq, LANES), lambda bi, p, i: (bi, i, p)),
        out_shape=jax.ShapeDtypeStruct((b, s, wa), F32),
        compiler_params=_params("arbitrary", "arbitrary", "arbitrary"),
        name="attn_prompt",
    )(qt, kaug, vt)


def _attn_sample_kernel(q_ref, kc_ref, vc_ref, kn_ref, vn_ref, cum_ref, rrow_ref, crow_ref, o_ref, *, n):
    p = pl.program_id(1)
    q = q_ref[:, :LANES]
    lane = lax.broadcasted_iota(jnp.int32, (n, LANES), 1)
    first = lane < HEAD_DIM
    zero = jnp.zeros_like(q)
    kc = kc_ref[...].astype(BF16)
    vc = vc_ref[...].astype(BF16)
    pad = jnp.zeros((LANES - n, LANES), BF16)
    kn = jnp.concatenate([kn_ref[:, :LANES], pad], axis=0)
    vn = jnp.concatenate([vn_ref[...], pad], axis=0)
    cum = cum_ref[...]
    causal = lane <= lax.broadcasted_iota(jnp.int32, (n, LANES), 0)
    outs = []
    for e in range(2):
        qm = jnp.where(first, q, zero) if e == 0 else jnp.where(first, zero, q)
        cq = jnp.sum(jnp.where(lane == 2 * p + e, cum, 0.0), axis=-1, keepdims=True)
        sc = _dot_nt(qm, kc) + LOG2E * (cq + rrow_ref[e:e + 1, :])
        sn = jnp.where(causal, _dot_nt(qm, kn) + LOG2E * (cq - crow_ref[e:e + 1, :]), NEG_INF)
        m = jnp.maximum(jnp.max(sc, axis=-1, keepdims=True), jnp.max(sn, axis=-1, keepdims=True))
        pc = jnp.exp2(sc - m)
        pn = jnp.exp2(sn - m)
        l = jnp.sum(pc, axis=-1, keepdims=True) + jnp.sum(pn, axis=-1, keepdims=True)
        outs.append((_dot(pc, vc) + _dot(pn, vn)) / l)
    o_ref[...] = jnp.where(first, outs[0], outs[1])


def _attn_sample(qaug, kaug, vab, cum, cache_k, cache_v, rrow, crow):
    b, n, wa = vab.shape
    past = cache_k.shape[1]
    npair = wa // LANES
    new = lambda w: pl.BlockSpec((None, n, w), lambda bi, p: (bi, 0, p))
    old = lambda: pl.BlockSpec((None, past, LANES), lambda bi, p: (bi, 0, p))
    return pl.pallas_call(
        functools.partial(_attn_sample_kernel, n=n),
        grid=(b, npair),
        in_specs=[new(2 * LANES), old(), old(), new(2 * LANES), new(LANES),
                  pl.BlockSpec((None, n, LANES), lambda bi, p: (bi, 0, 0)),
                  pl.BlockSpec((None, None, 2, past), lambda bi, p: (bi, p, 0, 0)),
                  pl.BlockSpec((None, None, 2, LANES), lambda bi, p: (bi, p, 0, 0))],
        out_specs=new(LANES),
        out_shape=jax.ShapeDtypeStruct((b, n, wa), F32),
        compiler_params=_params("arbitrary", "arbitrary"),
        name="attn_sample",
    )(qaug, cache_k, cache_v, kaug, vab, cum, rrow, crow)


def _suffix_kernel(x_ref, o_ref, carry, *, tp):
    @pl.when(pl.program_id(1) == 0)
    def _():
        carry[...] = jnp.zeros_like(carry)
    v = x_ref[...]
    row = lax.broadcasted_iota(jnp.int32, v.shape, 0)
    s = 1
    while s < tp:
        v = v + jnp.where(row + s < tp, pltpu.roll(v, tp - s, 0), 0.0)
        s *= 2
    v = v + carry[...]
    o_ref[...] = v
    carry[...] = v[0:1, :]


def _seq_suffix_sum(x, *, tp):
    b, p, w = x.shape
    nt = p // tp
    spec = pl.BlockSpec((None, tp, w), lambda i, j: (i, nt - 1 - j, 0))
    return pl.pallas_call(
        functools.partial(_suffix_kernel, tp=tp),
        grid=(b, nt),
        in_specs=[spec], out_specs=spec,
        out_shape=jax.ShapeDtypeStruct(x.shape, F32),
        scratch_shapes=[pltpu.VMEM((1, w), F32)],
        compiler_params=_params("arbitrary", "arbitrary"),
        name="seq_suffix_sum",
    )(x)


def _gdn_kernel(q_ref, k_ref, v_ref, bz_ref, elem_ref, s0_ref, gb_ref, esel_ref, hsum_ref,
                o_ref, sout_ref, s_scr, *, nb, nh):
    L = GDN_BLOCK
    t = pl.program_id(1)

    @pl.when(t == 0)
    def _():
        s_scr[...] = s0_ref[...]

    lane = lax.broadcasted_iota(jnp.int32, (L, LANES), 1)
    first = lane < HEAD_DIM
    ri = lax.broadcasted_iota(jnp.int32, (L, L), 0)
    ci = lax.broadcasted_iota(jnp.int32, (L, L), 1)
    incl = ci <= ri
    strict = ci < ri
    same_head = (ri < HEAD_DIM) == (ci < HEAD_DIM)
    lane2 = lax.broadcasted_iota(jnp.int32, (L, 2 * L), 1)
    left = lane2 < L
    first2 = jnp.bitwise_and(lane2, LANES - 1) < HEAD_DIM
    xor2 = jnp.bitwise_xor(lax.broadcasted_iota(jnp.int32, (L, 2 * L), 0), jnp.bitwise_and(lane2, L - 1))

    def halves(x, sel):
        return jnp.concatenate([jnp.where(sel, x, 0.0), jnp.where(sel, 0.0, x)], axis=0)

    npair = s_scr.shape[0]
    wbw = npair * LANES
    chains = [(n, p) for n in range(nb) for p in range(npair)]
    ex = {}
    for n in range(nb):
        elem = elem_ref[n * L:(n + 1) * L, :]
        gsum = _seg_cumsum(elem, L)
        mixed = jnp.where((lane >= nh) & (lane < 2 * nh), gsum, elem)
        ex[n] = _dot_select_exact(mixed, esel_ref[...])

    c = {}
    for n, p in chains:
        rows = slice(n * L, (n + 1) * L)
        cols = slice(p * LANES, (p + 1) * LANES)
        g = ex[n][:, cols]
        bt = ex[n][:, wbw + p * LANES: wbw + (p + 1) * LANES]
        kp = k_ref[rows, cols]
        qp = q_ref[rows, cols]
        g_sw = pltpu.roll(g, HEAD_DIM, 1)
        b_sw = pltpu.roll(bt, HEAD_DIM, 1)
        g_t = g.T
        a_parts, qk_parts = [], []
        for e in range(2):
            sel = first if e == 0 else jnp.logical_not(first)
            gcol = jnp.where(sel, g, g_sw)
            bcol = jnp.where(sel, bt, b_sw)
            grow = g_t[e * HEAD_DIM:e * HEAD_DIM + 1, :]
            dec = jnp.exp(jnp.where(incl, gcol - grow, NEG_INF))
            kk = _dot_nt(jnp.where(sel, kp, 0.0), kp)
            qk_parts.append(_dot_nt(jnp.where(sel, qp, 0.0), kp) * dec)
            a_parts.append(jnp.where(strict, bcol * kk * dec, 0.0))
        a_cat = jnp.concatenate(a_parts, axis=1)
        eg = jnp.exp(g)
        glast = g[L - 1:L, :]
        c[n, p] = dict(a=a_cat, qk=jnp.concatenate(qk_parts, axis=1), glast=glast, qg=qp * eg,
                       kdec=kp * jnp.exp(glast - g),
                       r=jnp.concatenate([v_ref[rows, cols] * bt, kp * bt * eg], axis=1),
                       tm1=-jnp.where(xor2 < 2, a_cat, 0.0))

    s_blk = 2
    while s_blk < L:
        pm = {}
        for key in chains:
            nmat = jnp.where((xor2 >= s_blk) & (xor2 < 2 * s_blk), c[key]["a"], 0.0)
            pm[key] = nmat + _dot(c[key]["tm1"], halves(nmat, left))
        for key in chains:
            c[key]["tm1"] = c[key]["tm1"] - pm[key] - _dot(pm[key], halves(c[key]["tm1"], left))
        s_blk *= 2
    for key in chains:
        r = c[key]["r"]
        c[key]["uw"] = r + _dot(c[key]["tm1"], halves(r, first2))

    for n in range(nb):
        rows = slice(n * L, (n + 1) * L)
        pairs = range(npair)
        ws = [_dot(jnp.concatenate([c[n, p]["uw"][:, LANES:], c[n, p]["qg"]], axis=0), s_scr[p]) for p in pairs]
        u = [c[n, p]["uw"][:, :LANES] - ws[p][:L] for p in pairs]
        o = [ws[p][L:] + _dot(c[n, p]["qk"], halves(u[p], first)) for p in pairs]
        for p in pairs:
            s_scr[p] = (s_scr[p] * jnp.exp(c[n, p]["glast"])
                        + jnp.where(same_head, _dot(c[n, p]["kdec"].T, u[p]), 0.0))
        for p in pairs:
            cols = slice(p * LANES, (p + 1) * LANES)
            ms = _dot(o[p] * o[p], hsum_ref[...]) * (1.0 / HEAD_DIM)
            bz = bz_ref[rows, cols]
            o_ref[rows, cols] = o[p] * lax.rsqrt(ms + 1e-6) * gb_ref[...] * (bz * _sigmoid(bz))

    @pl.when(t == pl.num_programs(1) - 1)
    def _():
        sout_ref[...] = s_scr[...]


def _gdn(qb, kb, vb, bz, elem, s0, lw, *, nb):
    b, t, wb = qb.shape
    tile = nb * GDN_BLOCK
    npair = wb // LANES
    tok = lambda w: pl.BlockSpec((None, tile, w), lambda i, j: (i, j, 0))
    st = pl.BlockSpec((None, npair, LANES, LANES), lambda i, j: (i, 0, 0, 0))
    consts = [lw["g_b_pair"], lw["esel"], lw["hsum128"]]
    return pl.pallas_call(
        functools.partial(_gdn_kernel, nb=nb, nh=lw["nh"]),
        grid=(b, t // tile),
        in_specs=[tok(wb), tok(wb), tok(wb), tok(wb), tok(LANES), st] + [_const_spec(c.shape) for c in consts],
        out_specs=[tok(wb), st],
        out_shape=[jax.ShapeDtypeStruct((b, t, wb), F32),
                   jax.ShapeDtypeStruct((b, npair, LANES, LANES), F32)],
        scratch_shapes=[pltpu.VMEM((npair, LANES, LANES), F32)],
        compiler_params=_params("arbitrary", "arbitrary"),
        name="gdn",
    )(qb, kb, vb, bz, elem, s0, *consts)


def _outffn_kernel(oa_ref, ob_ref, oc_ref, x_ref, ga_ref, wout_ref, gpm_ref, gpf_ref, wfi_ref, wfo_ref,
                   gpo_ref, y_ref, *, dff):
    oa = _rms(oa_ref[...], ga_ref[...])
    cat = jnp.concatenate([oa, ob_ref[...], oc_ref[...]], axis=-1).astype(BF16)
    m = jnp.dot(cat, wout_ref[...], preferred_element_type=F32)
    x1 = x_ref[...] + _rms(m, gpm_ref[...])
    h = _rms(x1, gpf_ref[...]).astype(BF16)
    gu = jnp.dot(h, wfi_ref[...], preferred_element_type=F32)
    gate = gu[:, :dff]
    a = (gate * _sigmoid(gate) * gu[:, dff:]).astype(BF16)
    f = jnp.dot(a, wfo_ref[...], preferred_element_type=F32)
    y_ref[...] = x1 + _rms(f, gpo_ref[...])


def _outffn(oa, ob, oc, x, lw, *, tm):
    n, d = x.shape
    dff = lw["w_ffn_out"].shape[0]
    tok = lambda w: pl.BlockSpec((tm, w), lambda i: (i, 0))
    consts = [lw["g_a_out"], lw["w_out"], lw["g_post_mix"], lw["g_pre_ffn"], lw["w_ffn_in"],
              lw["w_ffn_out"], lw["g_post_ffn"]]
    return pl.pallas_call(
        functools.partial(_outffn_kernel, dff=dff),
        grid=(n // tm,),
        in_specs=[tok(oa.shape[1]), tok(ob.shape[1]), tok(oc.shape[1]), tok(d)]
                 + [_const_spec(c.shape) for c in consts],
        out_specs=tok(d),
        out_shape=jax.ShapeDtypeStruct((n, d), F32),
        compiler_params=_params("arbitrary"),
        name="outffn",
    )(oa, ob, oc, x, *consts)


def _block_ones(width):
    idx = jnp.arange(width) // HEAD_DIM
    return (idx[:, None] == idx[None, :]).astype(BF16)


def _layer_weights(l, prm, cms):
    w_in = prm["w_in"][l]
    nh = prm["b_f"].shape[1]
    wa = nh * HEAD_DIM
    wb = prm["a_log"].shape[1] * HEAD_DIM
    wc = prm["g_cv"].shape[1]
    ng = prm["w_s"].shape[1]
    assert prm["a_log"].shape[1] == nh and wa % LANES == 0 and wc % LANES == 0 and AUG * nh <= LANES
    sizes = (wa, wa, wa, nh, 3 * wb, nh, nh, wb, wc, wc)
    offs = [0]
    for sz in sizes:
        offs.append(offs[-1] + sz)
    col = lambda i: w_in[:, offs[i]:offs[i + 1]]
    w_big = jnp.concatenate([col(4), col(8), col(9), col(0), col(1), col(2), col(7)], axis=1).astype(BF16)
    w_small = jnp.concatenate([col(3), col(5), col(6), jnp.zeros((w_in.shape[0], LANES - 3 * nh), F32)],
                              axis=1).astype(BF16)
    zpad = jnp.zeros((LANES - 2 * nh,), F32)
    sp = jnp.zeros((SUBLANES, LANES), F32)
    sp = sp.at[0].set(jnp.concatenate([prm["b_f"][l], prm["dt_bias"][l], zpad]))
    sp = sp.at[1].set(jnp.concatenate([jnp.zeros((nh,), F32), prm["a_log"][l], zpad]))
    hl = jnp.arange(nh) * AUG
    sp = sp.at[2, (hl[:, None] + jnp.arange(3, 6)[None, :]).reshape(-1)].set(1.0)
    sp = sp.at[3, (hl[:, None] + jnp.arange(0, 3)[None, :]).reshape(-1)].set(1.0)
    pmat = jnp.zeros((3 * LANES, 2 * LANES), F32)
    for piece in range(3):
        pmat = pmat.at[piece * LANES + jnp.arange(nh), hl + piece].set(1.0)
        pmat = pmat.at[piece * LANES + jnp.arange(nh), LANES + hl + 3 + piece].set(1.0)
    row = lambda v: v.reshape(1, -1)
    ws_cat, bs_full = {}, {}
    for cm in cms:
        pos = jnp.arange(cm) // HEAD_DIM
        w = jnp.where(pos[None, :] <= pos[:, None], prm["w_s"][l][:, :cm, :cm], 0.0)
        pairs = [jnp.concatenate([w[2 * pp], w[2 * pp + 1]], axis=1) for pp in range(ng // 2)]
        kpad = max(LANES - 2 * cm, 0)
        ws_cat[cm] = jnp.pad(jnp.stack(pairs), ((0, 0), (0, 0), (0, kpad))).astype(BF16)
        bs_full[cm] = jnp.repeat(prm["b_s"][l][:, :cm].T, wc // ng, axis=1)
    src = jnp.arange(LANES)[:, None]
    dst = jnp.arange(wb)[None, :] // HEAD_DIM
    esel = jnp.concatenate([src == nh + dst, src == 2 * nh + dst], axis=1).astype(BF16)
    return dict(
        nh=nh, wa=wa, wb=wb, wc=wc,
        g_pre_mix=row(prm["g_pre_mix"][l]), w_big=w_big, w_small=w_small, sp=sp, conv_w=prm["conv_w"][l],
        g_cv=row(prm["g_cv"][l]), b_cv=row(prm["b_cv"][l]), ws_cat=ws_cat, bs_full=bs_full,
        g_c_out=row(prm["g_c_out"][l]), hsum=_block_ones(2 * LANES), hsum128=_block_ones(LANES), pmat=pmat.astype(BF16),
        g_b_pair=row(jnp.tile(prm["g_b_out"][l], LANES // HEAD_DIM)), esel=esel,
        g_a_out=row(prm["g_a_out"][l]), w_out=prm["w_out"][l].astype(BF16),
        g_post_mix=row(prm["g_post_mix"][l]), g_pre_ffn=row(prm["g_pre_ffn"][l]),
        w_ffn_in=prm["w_ffn_in"][l].astype(BF16), w_ffn_out=prm["w_ffn_out"][l].astype(BF16),
        g_post_ffn=row(prm["g_post_ffn"][l]))


def _pair_state(s):
    b, h, dk, dv = s.shape
    s = s.reshape(b, h // 2, 2, dk, dv)
    z = jnp.zeros_like(s[:, :, 0])
    top = jnp.concatenate([s[:, :, 0], z], axis=-1)
    bot = jnp.concatenate([z, s[:, :, 1]], axis=-1)
    return jnp.concatenate([top, bot], axis=-2)


def _unpair_state(sp):
    b, hp, _, _ = sp.shape
    s0 = sp[:, :, :HEAD_DIM, :HEAD_DIM]
    s1 = sp[:, :, HEAD_DIM:, HEAD_DIM:]
    return jnp.stack([s0, s1], axis=2).reshape(b, 2 * hp, HEAD_DIM, HEAD_DIM)


def _head_rows(cum, nh):
    b, t, _ = cum.shape
    return jnp.transpose(cum[:, :, :nh], (0, 2, 1)).reshape(b, nh // 2, 2, t)


def _pick(n, prefs):
    for c in prefs:
        if n % c == 0:
            return c
    return n


def _layer(x, lw, conv_prev, s0, cache, *, cm):
    b, t, d = x.shape
    nh, wb = lw["nh"], lw["wb"]
    kw1 = conv_prev.shape[1]
    conv_init = jnp.pad(conv_prev, ((0, 0), (SUBLANES - kw1, 0), (0, 0)))
    tm = _pick(t, (256, 128, 64, 32, 16))
    pj = _inproj(x, lw, conv_init, tm=tm, cm=cm)

    if cache is None:
        tq = _pick(t, (ATTN_TQ, 256, 128))
        oa = _attn_prompt(pj["qaug"], pj["kaug"], pj["vab"], tq=tq, kc=min(ATTN_KC, tq), look=ATTN_LOOK)
    else:
        ck, cv, clogf = cache
        bs, past = clogf.shape[:2]
        incl = _seq_suffix_sum(jnp.pad(clogf, ((0, 0), (0, 0), (0, LANES - nh))), tp=_pick(past, (512, 256, 128)))
        incl = jnp.transpose(incl[:, :, :nh], (0, 2, 1))
        excl = jnp.concatenate([incl[:, :, 1:], jnp.zeros_like(incl[:, :, :1])], axis=2)
        rrow = excl.reshape(bs, nh // 2, 2, past)
        crow = jnp.pad(_head_rows(pj["cum"], nh), ((0, 0), (0, 0), (0, 0), (0, LANES - t)))
        oa = _attn_sample(pj["qaug"], pj["kaug"], pj["vab"], pj["cum"],
                          ck.reshape(bs, past, -1), cv.reshape(bs, past, -1), rrow, crow)

    tp = -(-t // GDN_BLOCK) * GDN_BLOCK
    padt = lambda a: a if tp == t else jnp.pad(a, ((0, 0), (0, tp - t), (0, 0)))
    nb = _pick(tp // GDN_BLOCK, (4, 2, 1))
    ob, s_new = _gdn(padt(pj["qb"]), padt(pj["kb"]), padt(pj["vb"]), padt(pj["bz"]), padt(pj["elem"]),
                     _pair_state(s0), lw, nb=nb)
    ob = ob[:, :t]

    n = b * t
    y = _outffn(oa.reshape(n, -1), ob.reshape(n, -1), pj["oc"].reshape(n, -1), x.reshape(n, d), lw,
                tm=_pick(n, (256, 128, 64, 32, 16)))
    state = (pj["ka"].reshape(b, t, nh, HEAD_DIM), pj["va"].reshape(b, t, nh, HEAD_DIM), pj["elem"][:, :, :nh],
             pj["ytail"][:, SUBLANES - kw1:, :], _unpair_state(s_new), pj["vn"])
    return y.reshape(b, t, d), state


def kernel(x_prompt, x_sample, cache_a_k, cache_a_v, cache_a_logf, state_b_conv, state_b_S, g_pre_mix, w_in, b_f, conv_w, a_log, dt_bias, g_b_out, g_a_out, g_cv, b_cv, w_s, b_s, g_c_out, w_out, g_post_mix, g_pre_ffn, w_ffn_in, w_ffn_out, g_post_ffn):
    prm = dict(g_pre_mix=g_pre_mix, w_in=w_in, b_f=b_f, conv_w=conv_w, a_log=a_log, dt_bias=dt_bias,
               g_b_out=g_b_out, g_a_out=g_a_out, g_cv=g_cv, b_cv=b_cv, w_s=w_s, b_s=b_s, g_c_out=g_c_out,
               w_out=w_out, g_post_mix=g_post_mix, g_pre_ffn=g_pre_ffn, w_ffn_in=w_ffn_in,
               w_ffn_out=w_ffn_out, g_post_ffn=g_post_ffn)
    depth = w_in.shape[0]
    bp, sp_len, _ = x_prompt.shape
    n_new = x_sample.shape[1]
    cm_p = w_s.shape[2]
    assert sp_len % cm_p == 0 and sp_len % GDN_BLOCK == 0 and n_new <= HEAD_DIM and n_new % SUBLANES == 0
    kw1 = conv_w.shape[1] - 1
    nhb = a_log.shape[1]
    yp, ys = x_prompt, x_sample
    outs_p, outs_s = [], []
    for l in range(depth):
        lw = _layer_weights(l, prm, (cm_p, n_new))
        conv0 = jnp.zeros((bp, kw1, conv_w.shape[2]), F32)
        s0 = jnp.zeros((bp, nhb, HEAD_DIM, HEAD_DIM), F32)
        yp, st_p = _layer(yp, lw, conv0, s0, None, cm=cm_p)
        ys, st_s = _layer(ys, lw, state_b_conv[l], state_b_S[l],
                          (cache_a_k[l], cache_a_v[l], cache_a_logf[l]), cm=n_new)
        outs_p.append(st_p)
        outs_s.append(st_s)
    stk = lambda outs, i: jnp.stack([o[i] for o in outs], axis=0)
    return (yp, ys, stk(outs_p, 0), stk(outs_p, 1), stk(outs_p, 2), stk(outs_p, 3), stk(outs_p, 4),
            stk(outs_s, 0), stk(outs_s, 1), stk(outs_s, 2), stk(outs_s, 3), stk(outs_s, 4), stk(outs_s, 5))
```

```python
import functools

import jax
import jax.numpy as jnp
from jax import lax
from jax.experimental import pallas as pl
from jax.experimental.pallas import tpu as pltpu

F32 = jnp.float32
BF16 = jnp.bfloat16

LANES = 128
SUBLANES = 8
HEAD_DIM = 64
GDN_BLOCK = 128
ATTN_TQ = 1024
ATTN_KC = 256
ATTN_LOOK = 2
VMEM_LIMIT = 56 * 1024 * 1024
NEG_INF = float("-inf")
LOG2E = 1.4426950408889634
AUG = 16


def _dot(a, b):
    return jnp.dot(a.astype(BF16), b.astype(BF16), preferred_element_type=F32)


def _dot_nt(a, b):
    return lax.dot_general(a.astype(BF16), b.astype(BF16), (((1,), (1,)), ((), ())),
                           preferred_element_type=F32)


def _dot_select_exact(x, sel):
    hi = x.astype(BF16)
    r1 = x - hi.astype(F32)
    mid = r1.astype(BF16)
    lo = (r1 - mid.astype(F32)).astype(BF16)
    d = lambda p: jnp.dot(p, sel, preferred_element_type=F32)
    return (d(hi) + d(mid)) + d(lo)


def _rms(x, g, eps=1e-6):
    return x * lax.rsqrt(jnp.mean(x * x, axis=-1, keepdims=True) + eps) * g


def _softplus(x):
    return jnp.maximum(x, 0.0) + jnp.log1p(jnp.exp(-jnp.abs(x)))


def _sigmoid(x):
    return 1.0 / (1.0 + jnp.exp(-x))


def _seg_cumsum(v, seg):
    row = lax.broadcasted_iota(jnp.int32, v.shape, 0)
    pos = jnp.bitwise_and(row, seg - 1)
    s = 1
    while s < seg:
        v = v + jnp.where(pos >= s, pltpu.roll(v, s, 0), 0.0)
        s *= 2
    return v


def _const_spec(shape):
    nd = len(shape)
    return pl.BlockSpec(shape, lambda *_: (0,) * nd, pipeline_mode=pl.Buffered(1))


def _params(*sem):
    return pltpu.CompilerParams(dimension_semantics=sem, vmem_limit_bytes=VMEM_LIMIT)


def _inproj_kernel(x_ref, gpre_ref, wbig_ref, wsm_ref, sp_ref, convw_ref, convinit_ref, gcv_ref,
                   bcv_ref, ws_ref, bs_ref, gco_ref, hsum_ref, pmat_ref,
                   qaug_ref, ka_ref, va_ref, kaug_ref, vab_ref, elem_ref, cum_ref, qb_ref, kb_ref,
                   vb_ref, bz_ref, oc_ref, vn_ref, ytail_ref,
                   carry_conv, carry_cum, *, tm, cm, nh, wa, wb, wc, scale):
    @pl.when(pl.program_id(1) == 0)
    def _():
        carry_cum[...] = jnp.zeros_like(carry_cum)
        carry_conv[...] = convinit_ref[...]

    h = _rms(x_ref[...], gpre_ref[...]).astype(BF16)
    o_c = 3 * wb
    o_a = o_c + 2 * wc
    proj = lambda w: lax.dot_general(h, w, (((1,), (1,)), ((), ())), preferred_element_type=F32)
    y = proj(wbig_ref[:o_c, :])
    zc = proj(wbig_ref[o_c:o_a, :])
    zs = proj(wsm_ref[...])
    za = proj(wbig_ref[o_a:, :])

    ka = za[:, wa:2 * wa]
    va = za[:, 2 * wa:3 * wa]
    ka_ref[...] = ka
    va_ref[...] = va
    vab_ref[...] = va.astype(BF16)
    bz_ref[...] = za[:, 3 * wa:]

    lane = lax.broadcasted_iota(jnp.int32, (tm, LANES), 1)
    zb = zs + sp_ref[0:1, :]
    soft_tail = jnp.log1p(jnp.exp(-jnp.abs(zb)))
    logf = -(jnp.maximum(-zb, 0.0) + soft_tail)
    gl = -jnp.exp(sp_ref[1:2, :]) * (jnp.maximum(zb, 0.0) + soft_tail)
    beta = _sigmoid(zs)
    elem = jnp.where(lane < nh, logf, jnp.where(lane < 2 * nh, gl, jnp.where(lane < 3 * nh, beta, 0.0)))
    elem_ref[...] = elem

    cum = _seg_cumsum(elem, tm) + carry_cum[...]
    cum_ref[...] = cum
    carry_cum[...] = cum[tm - 1:tm, :]

    c2 = jnp.where(lane < nh, cum * LOG2E, 0.0)
    hi = c2.astype(BF16)
    r1 = c2 - hi.astype(F32)
    mid = r1.astype(BF16)
    lo = (r1 - mid.astype(F32)).astype(BF16)
    placed = jnp.dot(jnp.concatenate([hi, mid, lo], axis=1), pmat_ref[...], preferred_element_type=F32)
    augq = (placed[:, :LANES] + sp_ref[2:3, :]).astype(BF16)
    augk = (sp_ref[3:4, :] - placed[:, LANES:]).astype(BF16)
    qs = (za[:, :wa] * (scale * LOG2E)).astype(BF16)
    ks = ka.astype(BF16)
    qaug_ref[...] = jnp.concatenate(
        [a for j in range(0, wa, LANES) for a in (qs[:, j:j + LANES], augq)], axis=1)
    kaug_ref[...] = jnp.concatenate(
        [a for j in range(0, wa, LANES) for a in (ks[:, j:j + LANES], augk)], axis=1)

    prev = carry_conv[...]
    row8 = lax.broadcasted_iota(jnp.int32, prev.shape, 0)
    kw = convw_ref.shape[0]
    acc = y * convw_ref[kw - 1:kw, :]
    for k in range(1, kw):
        yk = pltpu.roll(y, k, 0)
        top = jnp.where(row8 < k, pltpu.roll(prev, k, 0), yk[0:SUBLANES])
        yk = jnp.concatenate([top, yk[SUBLANES:]], axis=0)
        acc = acc + yk * convw_ref[kw - 1 - k:kw - k, :]
    carry_conv[...] = y[tm - SUBLANES:tm]
    ytail_ref[...] = y[tm - SUBLANES:tm]
    yc = acc * _sigmoid(acc)
    qb = yc[:, :wb]
    kb = yc[:, wb:2 * wb]
    sq = jnp.concatenate([qb * qb, kb * kb], axis=-1).astype(BF16)
    hw = hsum_ref.shape[0]
    ss = jnp.concatenate([jnp.dot(sq[:, j:j + hw], hsum_ref[...], preferred_element_type=F32)
                          for j in range(0, 2 * wb, hw)], axis=-1)
    qb_ref[...] = qb * lax.rsqrt(ss[:, :wb] + 1e-6) * scale
    kb_ref[...] = kb * lax.rsqrt(ss[:, wb:] + 1e-6)
    vb_ref[...] = yc[:, 2 * wb:]

    u = jax.nn.gelu(zc[:, :wc])
    gv = jax.nn.gelu(zc[:, wc:])
    mu = jnp.mean(gv, axis=-1, keepdims=True)
    var = jnp.mean(jnp.square(gv - mu), axis=-1, keepdims=True)
    vn = (gv - mu) * lax.rsqrt(var + 1e-5) * gcv_ref[...] + bcv_ref[...]
    vn_ref[...] = vn
    first = lax.broadcasted_iota(jnp.int32, (cm, LANES), 1) < HEAD_DIM
    kpad = ws_ref.shape[2] - 2 * cm
    rows = []
    for c in range(tm // cm):
        vc = vn[c * cm:(c + 1) * cm]
        cols = []
        for pp in range(wc // LANES):
            vp = vc[:, pp * LANES:(pp + 1) * LANES]
            parts = [jnp.where(first, vp, 0.0), jnp.where(first, 0.0, vp)]
            if kpad:
                parts.append(jnp.zeros((kpad, LANES), F32))
            cols.append(_dot(ws_ref[pp], jnp.concatenate(parts, axis=0)))
        s = jnp.concatenate(cols, axis=-1) + bs_ref[...]
        rows.append(u[c * cm:(c + 1) * cm] * s)
    oc = rows[0] if len(rows) == 1 else jnp.concatenate(rows, axis=0)
    oc_ref[...] = _rms(oc, gco_ref[...])


def _inproj(x, lw, conv_init, *, tm, cm):
    b, t, d = x.shape
    nt = t // tm
    wa, wb, wc, nh = lw["wa"], lw["wb"], lw["wc"], lw["nh"]
    tok = lambda w: pl.BlockSpec((None, tm, w), lambda i, j: (i, j, 0))
    per_b = lambda r, w: pl.BlockSpec((None, r, w), lambda i, j: (i, 0, 0))
    outs = [("qaug", 2 * wa, BF16), ("ka", wa, F32), ("va", wa, F32), ("kaug", 2 * wa, BF16), ("vab", wa, BF16),
            ("elem", LANES, F32), ("cum", LANES, F32), ("qb", wb, F32), ("kb", wb, F32), ("vb", wb, F32),
            ("bz", wb, F32), ("oc", wc, F32), ("vn", wc, F32)]
    out_shape = [jax.ShapeDtypeStruct((b, t, w), dt) for _, w, dt in outs]
    out_specs = [tok(w) for _, w, _ in outs]
    out_shape.append(jax.ShapeDtypeStruct((b, SUBLANES, 3 * wb), F32))
    out_specs.append(per_b(SUBLANES, 3 * wb))
    consts = [lw["g_pre_mix"], lw["w_big"], lw["w_small"], lw["sp"], lw["conv_w"]]
    consts2 = [lw["g_cv"], lw["b_cv"], lw["ws_cat"][cm], lw["bs_full"][cm], lw["g_c_out"], lw["hsum"], lw["pmat"]]
    kern = functools.partial(_inproj_kernel, tm=tm, cm=cm, nh=nh, wa=wa, wb=wb, wc=wc,
                             scale=HEAD_DIM ** -0.5)
    res = pl.pallas_call(
        kern,
        grid=(b, nt),
        in_specs=[tok(d)] + [_const_spec(c.shape) for c in consts] + [per_b(SUBLANES, 3 * wb)]
                 + [_const_spec(c.shape) for c in consts2],
        out_specs=out_specs,
        out_shape=out_shape,
        scratch_shapes=[pltpu.VMEM((SUBLANES, 3 * wb), F32), pltpu.VMEM((1, LANES), F32)],
        compiler_params=_params("arbitrary", "arbitrary"),
        name="inproj",
    )(x, *consts, conv_init, *consts2)
    named = {n: r for (n, _, _), r in zip(outs, res[:-1])}
    named["ytail"] = res[-1]
    return named


def _attn_kernel(qt_ref, k_ref, vt_ref, o_ref, *, tq, kc, look):
    p = pl.program_id(1)
    i = pl.program_id(2)
    qt = qt_ref[...]
    rowi = lax.broadcasted_iota(jnp.int32, qt.shape, 0)
    zero = jnp.zeros_like(qt)
    qts = []
    for e in range(2):
        a0 = LANES + AUG * (2 * p + e)
        keep = ((rowi >= e * HEAD_DIM) & (rowi < (e + 1) * HEAD_DIM)) | ((rowi >= a0) & (rowi < a0 + AUG))
        qts.append(jnp.where(keep, qt, zero))
    qpos = i * tq + lax.broadcasted_iota(jnp.int32, (kc, tq), 1)
    kofs = lax.broadcasted_iota(jnp.int32, (kc, tq), 0)
    ones = jnp.ones((2 * SUBLANES, kc), BF16)
    units = [(c, e) for c in range(tq // kc) for e in range(2)]

    def scores(j, c, e):
        k0 = pl.multiple_of(j * tq + c * kc, kc)
        return jnp.dot(k_ref[pl.ds(k0, kc), :], qts[e], preferred_element_type=F32)

    def fold(j, c, e, s, st, masked):
        m, l, acc = st
        k0 = pl.multiple_of(j * tq + c * kc, kc)
        if masked:
            s = jnp.where(k0 + kofs <= qpos, s, NEG_INF)
        m_new = jnp.maximum(m, jnp.max(s, axis=0, keepdims=True))
        alpha = jnp.exp2(m - m_new)
        pt = jnp.exp2(s - m_new).astype(BF16)
        vt = jnp.concatenate([vt_ref[e * HEAD_DIM:(e + 1) * HEAD_DIM, pl.ds(k0, kc)], ones], axis=0)
        r = jnp.dot(vt, pt, preferred_element_type=F32)
        return m_new, alpha * l + r[HEAD_DIM:HEAD_DIM + 1], alpha * acc + r[:HEAD_DIM]

    def block(j, state, masked):
        state = list(state)
        pend = {}
        for k in range(min(look, len(units))):
            pend[k] = scores(j, *units[k])
        for k, (c, e) in enumerate(units):
            if k + look < len(units):
                pend[k + look] = scores(j, *units[k + look])
            state[e] = fold(j, c, e, pend.pop(k), state[e], masked)
        return tuple(state)

    st0 = (jnp.full((1, tq), NEG_INF, F32), jnp.zeros((1, tq), F32), jnp.zeros((HEAD_DIM, tq), F32))
    state = lax.fori_loop(0, i, lambda j, s: block(j, s, False), (st0, st0))
    state = block(i, state, True)
    ot = jnp.concatenate([acc / l for _, l, acc in state], axis=0)
    o_ref[...] = ot.T


def _attn_prompt(qaug, kaug, vab, *, tq, kc, look):
    b, s, wa = vab.shape
    npair = wa // LANES
    qt = jnp.transpose(qaug, (0, 2, 1))
    vt = jnp.transpose(vab, (0, 2, 1))
    kern = functools.partial(_attn_kernel, tq=tq, kc=kc, look=look)
    return pl.pallas_call(
        kern,
        grid=(b, npair, s // tq),
        in_specs=[pl.BlockSpec((None, 2 * LANES, tq), lambda bi, p, i: (bi, p, i)),
                  pl.BlockSpec((None, s, 2 * LANES), lambda bi, p, i: (bi, 0, p)),
                  pl.BlockSpec((None, LANES, s), lambda bi, p, i: (bi, p, 0))],
        out_specs=pl.BlockSpec((None, tq, LANES), lambda bi, p, i: (bi, i, p)),
        out_shape=jax.ShapeDtypeStruct((b, s, wa), F32),
        compiler_params=_params("arbitrary", "arbitrary", "arbitrary"),
        name="attn_prompt",
    )(qt, kaug, vt)


def _attn_sample_kernel(q_ref, kc_ref, vc_ref, kn_ref, vn_ref, cum_ref, rrow_ref, crow_ref, o_ref, *, n):
    p = pl.program_id(1)
    q = q_ref[:, :LANES]
    lane = lax.broadcasted_iota(jnp.int32, (n, LANES), 1)
    first = lane < HEAD_DIM
    zero = jnp.zeros_like(q)
    past = kc_ref.shape[-1]
    kc = kc_ref[...].reshape(LANES, past).astype(BF16)
    vc = vc_ref[...].reshape(LANES, past).astype(BF16)
    pad = jnp.zeros((LANES - n, LANES), BF16)
    kn = jnp.concatenate([kn_ref[:, :LANES], pad], axis=0)
    vn = jnp.concatenate([vn_ref[...], pad], axis=0)
    cum = cum_ref[...]
    causal = lane <= lax.broadcasted_iota(jnp.int32, (n, LANES), 0)
    outs = []
    for e in range(2):
        qm = jnp.where(first, q, zero) if e == 0 else jnp.where(first, zero, q)
        cq = jnp.sum(jnp.where(lane == 2 * p + e, cum, 0.0), axis=-1, keepdims=True)
        sc = _dot(qm, kc) + LOG2E * (cq + rrow_ref[e:e + 1, :])
        sn = jnp.where(causal, _dot_nt(qm, kn) + LOG2E * (cq - crow_ref[e:e + 1, :]), NEG_INF)
        m = jnp.maximum(jnp.max(sc, axis=-1, keepdims=True), jnp.max(sn, axis=-1, keepdims=True))
        pc = jnp.exp2(sc - m)
        pn = jnp.exp2(sn - m)
        l = jnp.sum(pc, axis=-1, keepdims=True) + jnp.sum(pn, axis=-1, keepdims=True)
        outs.append((_dot_nt(pc, vc) + _dot(pn, vn)) / l)
    o_ref[...] = jnp.where(first, outs[0], outs[1])


def _attn_sample(qaug, kaug, vab, cum, cache_kt, cache_vt, rrow, crow):
    b, n, wa = vab.shape
    past = cache_kt.shape[-1]
    npair = wa // LANES
    new = lambda w: pl.BlockSpec((None, n, w), lambda bi, p: (bi, 0, p))
    old = lambda: pl.BlockSpec((None, 2, HEAD_DIM, past), lambda bi, p: (bi, p, 0, 0))
    return pl.pallas_call(
        functools.partial(_attn_sample_kernel, n=n),
        grid=(b, npair),
        in_specs=[new(2 * LANES), old(), old(), new(2 * LANES), new(LANES),
                  pl.BlockSpec((None, n, LANES), lambda bi, p: (bi, 0, 0)),
                  pl.BlockSpec((None, None, 2, past), lambda bi, p: (bi, p, 0, 0)),
                  pl.BlockSpec((None, None, 2, LANES), lambda bi, p: (bi, p, 0, 0))],
        out_specs=new(LANES),
        out_shape=jax.ShapeDtypeStruct((b, n, wa), F32),
        compiler_params=_params("arbitrary", "arbitrary"),
        name="attn_sample",
    )(qaug, cache_kt, cache_vt, kaug, vab, cum, rrow, crow)


def _suffix_kernel(x_ref, o_ref, carry, *, tp):
    @pl.when(pl.program_id(1) == 0)
    def _():
        carry[...] = jnp.zeros_like(carry)
    v = x_ref[...]
    row = lax.broadcasted_iota(jnp.int32, v.shape, 0)
    s = 1
    while s < tp:
        v = v + jnp.where(row + s < tp, pltpu.roll(v, tp - s, 0), 0.0)
        s *= 2
    v = v + carry[...]
    o_ref[...] = v
    carry[...] = v[0:1, :]


def _seq_suffix_sum(x, *, tp):
    b, p, w = x.shape
    nt = p // tp
    spec = pl.BlockSpec((None, tp, w), lambda i, j: (i, nt - 1 - j, 0))
    return pl.pallas_call(
        functools.partial(_suffix_kernel, tp=tp),
        grid=(b, nt),
        in_specs=[spec], out_specs=spec,
        out_shape=jax.ShapeDtypeStruct(x.shape, F32),
        scratch_shapes=[pltpu.VMEM((1, w), F32)],
        compiler_params=_params("arbitrary", "arbitrary"),
        name="seq_suffix_sum",
    )(x)


def _gdn_kernel(q_ref, k_ref, v_ref, bz_ref, elem_ref, s0_ref, gb_ref, esel_ref, hsum_ref,
                o_ref, sout_ref, s_scr, *, nb, nh):
    L = GDN_BLOCK
    t = pl.program_id(1)

    @pl.when(t == 0)
    def _():
        s_scr[...] = s0_ref[...]

    lane = lax.broadcasted_iota(jnp.int32, (L, LANES), 1)
    first = lane < HEAD_DIM
    ri = lax.broadcasted_iota(jnp.int32, (L, L), 0)
    ci = lax.broadcasted_iota(jnp.int32, (L, L), 1)
    incl = ci <= ri
    strict = ci < ri
    same_head = (ri < HEAD_DIM) == (ci < HEAD_DIM)
    lane2 = lax.broadcasted_iota(jnp.int32, (L, 2 * L), 1)
    first2 = jnp.bitwise_and(lane2, LANES - 1) < HEAD_DIM
    xor2 = jnp.bitwise_xor(lax.broadcasted_iota(jnp.int32, (L, 2 * L), 0), jnp.bitwise_and(lane2, L - 1))
    zero_ll = jnp.zeros((L, L), BF16)

    def halves(x, sel):
        return jnp.concatenate([jnp.where(sel, x, 0.0), jnp.where(sel, 0.0, x)], axis=0)

    def dot_heads(y, x):
        xb = x.astype(BF16)
        bd = jnp.concatenate([jnp.concatenate([xb[:, :L], zero_ll], axis=1),
                              jnp.concatenate([zero_ll, xb[:, L:]], axis=1)], axis=0)
        return jnp.dot(y.astype(BF16), bd, preferred_element_type=F32)

    npair = s_scr.shape[0]
    wbw = npair * LANES
    chains = [(n, p) for n in range(nb) for p in range(npair)]
    ex = {}
    for n in range(nb):
        elem = elem_ref[n * L:(n + 1) * L, :]
        gsum = _seg_cumsum(elem, L)
        mixed = jnp.where((lane >= nh) & (lane < 2 * nh), gsum, elem)
        ex[n] = _dot_select_exact(mixed, esel_ref[...])

    c = {}
    for n, p in chains:
        rows = slice(n * L, (n + 1) * L)
        cols = slice(p * LANES, (p + 1) * LANES)
        g = ex[n][:, cols]
        bt = ex[n][:, wbw + p * LANES: wbw + (p + 1) * LANES]
        kp = k_ref[rows, cols]
        qp = q_ref[rows, cols]
        g_sw = pltpu.roll(g, HEAD_DIM, 1)
        b_sw = pltpu.roll(bt, HEAD_DIM, 1)
        g_t = g.T
        a_parts, qk_parts = [], []
        for e in range(2):
            sel = first if e == 0 else jnp.logical_not(first)
            gcol = jnp.where(sel, g, g_sw)
            bcol = jnp.where(sel, bt, b_sw)
            grow = g_t[e * HEAD_DIM:e * HEAD_DIM + 1, :]
            dec = jnp.exp(jnp.where(incl, gcol - grow, NEG_INF))
            kk = _dot_nt(jnp.where(sel, kp, 0.0), kp)
            qk_parts.append(_dot_nt(jnp.where(sel, qp, 0.0), kp) * dec)
            a_parts.append(jnp.where(strict, bcol * kk * dec, 0.0))
        a_cat = jnp.concatenate(a_parts, axis=1)
        eg = jnp.exp(g)
        glast = g[L - 1:L, :]
        c[n, p] = dict(a=a_cat, qk=jnp.concatenate(qk_parts, axis=1), glast=glast, qg=qp * eg,
                       kdec=kp * jnp.exp(glast - g),
                       r=jnp.concatenate([v_ref[rows, cols] * bt, kp * bt * eg], axis=1),
                       tm1=-jnp.where(xor2 < 2, a_cat, 0.0))

    s_blk = 2
    while s_blk < L:
        pm = {}
        for key in chains:
            nmat = jnp.where((xor2 >= s_blk) & (xor2 < 2 * s_blk), c[key]["a"], 0.0)
            pm[key] = nmat + dot_heads(c[key]["tm1"], nmat)
        for key in chains:
            c[key]["tm1"] = c[key]["tm1"] - pm[key] - dot_heads(pm[key], c[key]["tm1"])
        s_blk *= 2
    for key in chains:
        r = c[key]["r"]
        c[key]["uw"] = r + _dot(c[key]["tm1"], halves(r, first2))

    for n in range(nb):
        rows = slice(n * L, (n + 1) * L)
        pairs = range(npair)
        ws = [_dot(jnp.concatenate([c[n, p]["uw"][:, LANES:], c[n, p]["qg"]], axis=0), s_scr[p]) for p in pairs]
        u = [c[n, p]["uw"][:, :LANES] - ws[p][:L] for p in pairs]
        o = [ws[p][L:] + _dot(c[n, p]["qk"], halves(u[p], first)) for p in pairs]
        for p in pairs:
            s_scr[p] = (s_scr[p] * jnp.exp(c[n, p]["glast"])
                        + jnp.where(same_head, _dot(c[n, p]["kdec"].T, u[p]), 0.0))
        for p in pairs:
            cols = slice(p * LANES, (p + 1) * LANES)
            ms = _dot(o[p] * o[p], hsum_ref[...]) * (1.0 / HEAD_DIM)
            bz = bz_ref[rows, cols]
            o_ref[rows, cols] = o[p] * lax.rsqrt(ms + 1e-6) * gb_ref[...] * (bz * _sigmoid(bz))

    @pl.when(t == pl.num_programs(1) - 1)
    def _():
        sout_ref[...] = s_scr[...]


def _gdn(qb, kb, vb, bz, elem, s0, lw, *, nb):
    b, t, wb = qb.shape
    tile = nb * GDN_BLOCK
    npair = wb // LANES
    tok = lambda w: pl.BlockSpec((None, tile, w), lambda i, j: (i, j, 0))
    st = pl.BlockSpec((None, npair, LANES, LANES), lambda i, j: (i, 0, 0, 0))
    consts = [lw["g_b_pair"], lw["esel"], lw["hsum128"]]
    return pl.pallas_call(
        functools.partial(_gdn_kernel, nb=nb, nh=lw["nh"]),
        grid=(b, t // tile),
        in_specs=[tok(wb), tok(wb), tok(wb), tok(wb), tok(LANES), st] + [_const_spec(c.shape) for c in consts],
        out_specs=[tok(wb), st],
        out_shape=[jax.ShapeDtypeStruct((b, t, wb), F32),
                   jax.ShapeDtypeStruct((b, npair, LANES, LANES), F32)],
        scratch_shapes=[pltpu.VMEM((npair, LANES, LANES), F32)],
        compiler_params=_params("arbitrary", "arbitrary"),
        name="gdn",
    )(qb, kb, vb, bz, elem, s0, *consts)


def _outffn_kernel(oa_ref, ob_ref, oc_ref, x_ref, ga_ref, wout_ref, gpm_ref, gpf_ref, wfi_ref, wfo_ref,
                   gpo_ref, y_ref, *, dff):
    oa = _rms(oa_ref[...], ga_ref[...])
    cat = jnp.concatenate([oa, ob_ref[...], oc_ref[...]], axis=-1).astype(BF16)
    m = jnp.dot(cat, wout_ref[...], preferred_element_type=F32)
    x1 = x_ref[...] + _rms(m, gpm_ref[...])
    h = _rms(x1, gpf_ref[...]).astype(BF16)
    gu = jnp.dot(h, wfi_ref[...], preferred_element_type=F32)
    gate = gu[:, :dff]
    a = (gate * _sigmoid(gate) * gu[:, dff:]).astype(BF16)
    f = jnp.dot(a, wfo_ref[...], preferred_element_type=F32)
    y_ref[...] = x1 + _rms(f, gpo_ref[...])


def _outffn(oa, ob, oc, x, lw, *, tm):
    n, d = x.shape
    dff = lw["w_ffn_out"].shape[0]
    tok = lambda w: pl.BlockSpec((tm, w), lambda i: (i, 0))
    consts = [lw["g_a_out"], lw["w_out"], lw["g_post_mix"], lw["g_pre_ffn"], lw["w_ffn_in"],
              lw["w_ffn_out"], lw["g_post_ffn"]]
    return pl.pallas_call(
        functools.partial(_outffn_kernel, dff=dff),
        grid=(n // tm,),
        in_specs=[tok(oa.shape[1]), tok(ob.shape[1]), tok(oc.shape[1]), tok(d)]
                 + [_const_spec(c.shape) for c in consts],
        out_specs=tok(d),
        out_shape=jax.ShapeDtypeStruct((n, d), F32),
        compiler_params=_params("arbitrary"),
        name="outffn",
    )(oa, ob, oc, x, *consts)


def _block_ones(width):
    idx = jnp.arange(width) // HEAD_DIM
    return (idx[:, None] == idx[None, :]).astype(BF16)


def _layer_weights(l, prm, cms):
    w_in = prm["w_in"][l]
    nh = prm["b_f"].shape[1]
    wa = nh * HEAD_DIM
    wb = prm["a_log"].shape[1] * HEAD_DIM
    wc = prm["g_cv"].shape[1]
    ng = prm["w_s"].shape[1]
    assert prm["a_log"].shape[1] == nh and wa % LANES == 0 and wc % LANES == 0 and AUG * nh <= LANES
    sizes = (wa, wa, wa, nh, 3 * wb, nh, nh, wb, wc, wc)
    offs = [0]
    for sz in sizes:
        offs.append(offs[-1] + sz)
    w_in_t = w_in.T
    col = lambda i: w_in_t[offs[i]:offs[i + 1]]
    w_big = jnp.concatenate([col(4), col(8), col(9), col(0), col(1), col(2), col(7)], axis=0).astype(BF16)
    w_small = jnp.concatenate([col(3), col(5), col(6), jnp.zeros((LANES - 3 * nh, w_in.shape[0]), F32)],
                              axis=0).astype(BF16)
    zpad = jnp.zeros((LANES - 2 * nh,), F32)
    sp = jnp.zeros((SUBLANES, LANES), F32)
    sp = sp.at[0].set(jnp.concatenate([prm["b_f"][l], prm["dt_bias"][l], zpad]))
    sp = sp.at[1].set(jnp.concatenate([jnp.zeros((nh,), F32), prm["a_log"][l], zpad]))
    hl = jnp.arange(nh) * AUG
    sp = sp.at[2, (hl[:, None] + jnp.arange(3, 6)[None, :]).reshape(-1)].set(1.0)
    sp = sp.at[3, (hl[:, None] + jnp.arange(0, 3)[None, :]).reshape(-1)].set(1.0)
    pmat = jnp.zeros((3 * LANES, 2 * LANES), F32)
    for piece in range(3):
        pmat = pmat.at[piece * LANES + jnp.arange(nh), hl + piece].set(1.0)
        pmat = pmat.at[piece * LANES + jnp.arange(nh), LANES + hl + 3 + piece].set(1.0)
    row = lambda v: v.reshape(1, -1)
    ws_cat, bs_full = {}, {}
    for cm in cms:
        pos = jnp.arange(cm) // HEAD_DIM
        w = jnp.where(pos[None, :] <= pos[:, None], prm["w_s"][l][:, :cm, :cm], 0.0)
        pairs = [jnp.concatenate([w[2 * pp], w[2 * pp + 1]], axis=1) for pp in range(ng // 2)]
        kpad = max(LANES - 2 * cm, 0)
        ws_cat[cm] = jnp.pad(jnp.stack(pairs), ((0, 0), (0, 0), (0, kpad))).astype(BF16)
        bs_full[cm] = jnp.repeat(prm["b_s"][l][:, :cm].T, wc // ng, axis=1)
    src = jnp.arange(LANES)[:, None]
    dst = jnp.arange(wb)[None, :] // HEAD_DIM
    esel = jnp.concatenate([src == nh + dst, src == 2 * nh + dst], axis=1).astype(BF16)
    return dict(
        nh=nh, wa=wa, wb=wb, wc=wc,
        g_pre_mix=row(prm["g_pre_mix"][l]), w_big=w_big, w_small=w_small, sp=sp, conv_w=prm["conv_w"][l],
        g_cv=row(prm["g_cv"][l]), b_cv=row(prm["b_cv"][l]), ws_cat=ws_cat, bs_full=bs_full,
        g_c_out=row(prm["g_c_out"][l]), hsum=_block_ones(2 * LANES), hsum128=_block_ones(LANES), pmat=pmat.astype(BF16),
        g_b_pair=row(jnp.tile(prm["g_b_out"][l], LANES // HEAD_DIM)), esel=esel,
        g_a_out=row(prm["g_a_out"][l]), w_out=prm["w_out"][l].astype(BF16),
        g_post_mix=row(prm["g_post_mix"][l]), g_pre_ffn=row(prm["g_pre_ffn"][l]),
        w_ffn_in=prm["w_ffn_in"][l].astype(BF16), w_ffn_out=prm["w_ffn_out"][l].astype(BF16),
        g_post_ffn=row(prm["g_post_ffn"][l]))


def _pair_state(s):
    b, h, dk, dv = s.shape
    s = s.reshape(b, h // 2, 2, dk, dv)
    z = jnp.zeros_like(s[:, :, 0])
    top = jnp.concatenate([s[:, :, 0], z], axis=-1)
    bot = jnp.concatenate([z, s[:, :, 1]], axis=-1)
    return jnp.concatenate([top, bot], axis=-2)


def _unpair_state(sp):
    b, hp, _, _ = sp.shape
    s0 = sp[:, :, :HEAD_DIM, :HEAD_DIM]
    s1 = sp[:, :, HEAD_DIM:, HEAD_DIM:]
    return jnp.stack([s0, s1], axis=2).reshape(b, 2 * hp, HEAD_DIM, HEAD_DIM)


def _head_rows(cum, nh):
    b, t, _ = cum.shape
    return jnp.transpose(cum[:, :, :nh], (0, 2, 1)).reshape(b, nh // 2, 2, t)


def _pick(n, prefs):
    for c in prefs:
        if n % c == 0:
            return c
    return n


def _layer(x, lw, conv_prev, s0, cache, *, cm):
    b, t, d = x.shape
    nh, wb = lw["nh"], lw["wb"]
    kw1 = conv_prev.shape[1]
    conv_init = jnp.pad(conv_prev, ((0, 0), (SUBLANES - kw1, 0), (0, 0)))
    tm = _pick(t, (256, 128, 64, 32, 16))
    pj = _inproj(x, lw, conv_init, tm=tm, cm=cm)

    if cache is None:
        tq = _pick(t, (ATTN_TQ, 256, 128))
        oa = _attn_prompt(pj["qaug"], pj["kaug"], pj["vab"], tq=tq, kc=min(ATTN_KC, tq), look=ATTN_LOOK)
    else:
        ck, cv, clogf = cache
        bs, past = clogf.shape[:2]
        incl = _seq_suffix_sum(jnp.pad(clogf, ((0, 0), (0, 0), (0, LANES - nh))), tp=_pick(past, (4096, 2048, 1024, 512)))
        incl = jnp.transpose(incl[:, :, :nh], (0, 2, 1))
        excl = jnp.concatenate([incl[:, :, 1:], jnp.zeros_like(incl[:, :, :1])], axis=2)
        rrow = excl.reshape(bs, nh // 2, 2, past)
        crow = jnp.pad(_head_rows(pj["cum"], nh), ((0, 0), (0, 0), (0, 0), (0, LANES - t)))
        oa = _attn_sample(pj["qaug"], pj["kaug"], pj["vab"], pj["cum"],
                          jnp.transpose(ck, (0, 2, 3, 1)), jnp.transpose(cv, (0, 2, 3, 1)), rrow, crow)

    tp = -(-t // GDN_BLOCK) * GDN_BLOCK
    padt = lambda a: a if tp == t else jnp.pad(a, ((0, 0), (0, tp - t), (0, 0)))
    nb = _pick(tp // GDN_BLOCK, (2, 1))
    ob, s_new = _gdn(padt(pj["qb"]), padt(pj["kb"]), padt(pj["vb"]), padt(pj["bz"]), padt(pj["elem"]),
                     _pair_state(s0), lw, nb=nb)
    ob = ob[:, :t]

    n = b * t
    y = _outffn(oa.reshape(n, -1), ob.reshape(n, -1), pj["oc"].reshape(n, -1), x.reshape(n, d), lw,
                tm=_pick(n, (256, 128, 64, 32, 16)))
    state = (pj["ka"].reshape(b, t, nh, HEAD_DIM), pj["va"].reshape(b, t, nh, HEAD_DIM), pj["elem"][:, :, :nh],
             pj["ytail"][:, SUBLANES - kw1:, :], _unpair_state(s_new), pj["vn"])
    return y.reshape(b, t, d), state


def kernel(x_prompt, x_sample, cache_a_k, cache_a_v, cache_a_logf, state_b_conv, state_b_S, g_pre_mix, w_in, b_f, conv_w, a_log, dt_bias, g_b_out, g_a_out, g_cv, b_cv, w_s, b_s, g_c_out, w_out, g_post_mix, g_pre_ffn, w_ffn_in, w_ffn_out, g_post_ffn):
    prm = dict(g_pre_mix=g_pre_mix, w_in=w_in, b_f=b_f, conv_w=conv_w, a_log=a_log, dt_bias=dt_bias,
               g_b_out=g_b_out, g_a_out=g_a_out, g_cv=g_cv, b_cv=b_cv, w_s=w_s, b_s=b_s, g_c_out=g_c_out,
               w_out=w_out, g_post_mix=g_post_mix, g_pre_ffn=g_pre_ffn, w_ffn_in=w_ffn_in,
               w_ffn_out=w_ffn_out, g_post_ffn=g_post_ffn)
    depth = w_in.shape[0]
    bp, sp_len, _ = x_prompt.shape
    n_new = x_sample.shape[1]
    cm_p = w_s.shape[2]
    assert sp_len % cm_p == 0 and sp_len % GDN_BLOCK == 0 and n_new <= HEAD_DIM and n_new % SUBLANES == 0
    kw1 = conv_w.shape[1] - 1
    nhb = a_log.shape[1]
    yp, ys = x_prompt, x_sample
    outs_p, outs_s = [], []
    for l in range(depth):
        lw = _layer_weights(l, prm, (cm_p, n_new))
        conv0 = jnp.zeros((bp, kw1, conv_w.shape[2]), F32)
        s0 = jnp.zeros((bp, nhb, HEAD_DIM, HEAD_DIM), F32)
        yp, st_p = _layer(yp, lw, conv0, s0, None, cm=cm_p)
        ys, st_s = _layer(ys, lw, state_b_conv[l], state_b_S[l],
                          (cache_a_k[l], cache_a_v[l], cache_a_logf[l]), cm=n_new)
        outs_p.append(st_p)
        outs_s.append(st_s)
    stk = lambda outs, i: jnp.stack([o[i] for o in outs], axis=0)
    return (yp, ys, stk(outs_p, 0), stk(outs_p, 1), stk(outs_p, 2), stk(outs_p, 3), stk(outs_p, 4),
            stk(outs_s, 0), stk(outs_s, 1), stk(outs_s, 2), stk(outs_s, 3), stk(outs_s, 4), stk(outs_s, 5))
```

```python
import functools

import jax
import jax.numpy as jnp
from jax import lax
from jax.experimental import pallas as pl
from jax.experimental.pallas import tpu as pltpu

F32 = jnp.float32
BF16 = jnp.bfloat16

LANES = 128
SUBLANES = 8
HEAD_DIM = 64
GDN_BLOCK = 128
ATTN_TQ = 1024
ATTN_KC = 256
ATTN_LOOK = 2
VMEM_LIMIT = 56 * 1024 * 1024
NEG_INF = float("-inf")
LOG2E = 1.4426950408889634
AUG = 16


def _dot(a, b):
    return jnp.dot(a.astype(BF16), b.astype(BF16), preferred_element_type=F32)


def _dot_nt(a, b):
    return lax.dot_general(a.astype(BF16), b.astype(BF16), (((1,), (1,)), ((), ())),
                           preferred_element_type=F32)


def _dot_select_exact(x, sel):
    hi = x.astype(BF16)
    r1 = x - hi.astype(F32)
    mid = r1.astype(BF16)
    lo = (r1 - mid.astype(F32)).astype(BF16)
    d = lambda p: jnp.dot(p, sel, preferred_element_type=F32)
    return (d(hi) + d(mid)) + d(lo)


def _rms(x, g, eps=1e-6):
    return x * lax.rsqrt(jnp.mean(x * x, axis=-1, keepdims=True) + eps) * g


def _softplus(x):
    return jnp.maximum(x, 0.0) + jnp.log1p(jnp.exp(-jnp.abs(x)))


def _sigmoid(x):
    return 1.0 / (1.0 + jnp.exp(-x))


def _seg_cumsum(v, seg):
    row = lax.broadcasted_iota(jnp.int32, v.shape, 0)
    pos = jnp.bitwise_and(row, seg - 1)
    s = 1
    while s < seg:
        v = v + jnp.where(pos >= s, pltpu.roll(v, s, 0), 0.0)
        s *= 2
    return v


def _const_spec(shape):
    nd = len(shape)
    return pl.BlockSpec(shape, lambda *_: (0,) * nd, pipeline_mode=pl.Buffered(1))


def _params(*sem):
    return pltpu.CompilerParams(dimension_semantics=sem, vmem_limit_bytes=VMEM_LIMIT)


def _inproj_kernel(x_ref, gpre_ref, wbig_ref, wsm_ref, sp_ref, convw_ref, convinit_ref, gcv_ref,
                   bcv_ref, ws_ref, bs_ref, gco_ref, hsum_ref, pmat_ref,
                   qaug_ref, ka_ref, va_ref, kaug_ref, vab_ref, elem_ref, cum_ref, qb_ref, kb_ref,
                   vb_ref, bz_ref, oc_ref, vn_ref, ytail_ref,
                   carry_conv, carry_cum, *, tm, cm, nh, wa, wb, wc, scale):
    @pl.when(pl.program_id(1) == 0)
    def _():
        carry_cum[...] = jnp.zeros_like(carry_cum)
        carry_conv[...] = convinit_ref[...]

    h = _rms(x_ref[...], gpre_ref[...]).astype(BF16)
    o_c = 3 * wb
    o_a = o_c + 2 * wc
    proj = lambda w: lax.dot_general(h, w, (((1,), (1,)), ((), ())), preferred_element_type=F32)
    y = proj(wbig_ref[:o_c, :])
    zc = proj(wbig_ref[o_c:o_a, :])
    zs = proj(wsm_ref[...])
    za = proj(wbig_ref[o_a:, :])

    ka = za[:, wa:2 * wa]
    va = za[:, 2 * wa:3 * wa]
    ka_ref[...] = ka
    va_ref[...] = va
    vab_ref[...] = va.astype(BF16)
    bz_ref[...] = za[:, 3 * wa:]

    lane = lax.broadcasted_iota(jnp.int32, (tm, LANES), 1)
    zb = zs + sp_ref[0:1, :]
    soft_tail = jnp.log1p(jnp.exp(-jnp.abs(zb)))
    logf = -(jnp.maximum(-zb, 0.0) + soft_tail)
    gl = -jnp.exp(sp_ref[1:2, :]) * (jnp.maximum(zb, 0.0) + soft_tail)
    beta = _sigmoid(zs)
    elem = jnp.where(lane < nh, logf, jnp.where(lane < 2 * nh, gl, jnp.where(lane < 3 * nh, beta, 0.0)))
    elem_ref[...] = elem

    cum = _seg_cumsum(elem, tm) + carry_cum[...]
    cum_ref[...] = cum
    carry_cum[...] = cum[tm - 1:tm, :]

    c2 = jnp.where(lane < nh, cum * LOG2E, 0.0)
    hi = c2.astype(BF16)
    r1 = c2 - hi.astype(F32)
    mid = r1.astype(BF16)
    lo = (r1 - mid.astype(F32)).astype(BF16)
    placed = jnp.dot(jnp.concatenate([hi, mid, lo], axis=1), pmat_ref[...], preferred_element_type=F32)
    augq = (placed[:, :LANES] + sp_ref[2:3, :]).astype(BF16)
    augk = (sp_ref[3:4, :] - placed[:, LANES:]).astype(BF16)
    qs = (za[:, :wa] * (scale * LOG2E)).astype(BF16)
    ks = ka.astype(BF16)
    qaug_ref[...] = jnp.concatenate(
        [a for j in range(0, wa, LANES) for a in (qs[:, j:j + LANES], augq)], axis=1)
    kaug_ref[...] = jnp.concatenate(
        [a for j in range(0, wa, LANES) for a in (ks[:, j:j + LANES], augk)], axis=1)

    prev = carry_conv[...]
    row8 = lax.broadcasted_iota(jnp.int32, prev.shape, 0)
    kw = convw_ref.shape[0]
    acc = y * convw_ref[kw - 1:kw, :]
    for k in range(1, kw):
        yk = pltpu.roll(y, k, 0)
        top = jnp.where(row8 < k, pltpu.roll(prev, k, 0), yk[0:SUBLANES])
        yk = jnp.concatenate([top, yk[SUBLANES:]], axis=0)
        acc = acc + yk * convw_ref[kw - 1 - k:kw - k, :]
    carry_conv[...] = y[tm - SUBLANES:tm]
    ytail_ref[...] = y[tm - SUBLANES:tm]
    yc = acc * _sigmoid(acc)
    qb = yc[:, :wb]
    kb = yc[:, wb:2 * wb]
    sq = jnp.concatenate([qb * qb, kb * kb], axis=-1).astype(BF16)
    hw = hsum_ref.shape[0]
    ss = jnp.concatenate([jnp.dot(sq[:, j:j + hw], hsum_ref[...], preferred_element_type=F32)
                          for j in range(0, 2 * wb, hw)], axis=-1)
    qb_ref[...] = qb * lax.rsqrt(ss[:, :wb] + 1e-6) * scale
    kb_ref[...] = kb * lax.rsqrt(ss[:, wb:] + 1e-6)
    vb_ref[...] = yc[:, 2 * wb:]

    u = jax.nn.gelu(zc[:, :wc])
    gv = jax.nn.gelu(zc[:, wc:])
    mu = jnp.mean(gv, axis=-1, keepdims=True)
    var = jnp.mean(jnp.square(gv - mu), axis=-1, keepdims=True)
    vn = (gv - mu) * lax.rsqrt(var + 1e-5) * gcv_ref[...] + bcv_ref[...]
    vn_ref[...] = vn
    first = lax.broadcasted_iota(jnp.int32, (cm, LANES), 1) < HEAD_DIM
    kpad = ws_ref.shape[2] - 2 * cm
    rows = []
    for c in range(tm // cm):
        vc = vn[c * cm:(c + 1) * cm]
        cols = []
        for pp in range(wc // LANES):
            vp = vc[:, pp * LANES:(pp + 1) * LANES]
            parts = [jnp.where(first, vp, 0.0), jnp.where(first, 0.0, vp)]
            if kpad:
                parts.append(jnp.zeros((kpad, LANES), F32))
            cols.append(_dot(ws_ref[pp], jnp.concatenate(parts, axis=0)))
        s = jnp.concatenate(cols, axis=-1) + bs_ref[...]
        rows.append(u[c * cm:(c + 1) * cm] * s)
    oc = rows[0] if len(rows) == 1 else jnp.concatenate(rows, axis=0)
    oc_ref[...] = _rms(oc, gco_ref[...])


def _inproj(x, lw, conv_init, *, tm, cm):
    b, t, d = x.shape
    nt = t // tm
    wa, wb, wc, nh = lw["wa"], lw["wb"], lw["wc"], lw["nh"]
    tok = lambda w: pl.BlockSpec((None, tm, w), lambda i, j: (i, j, 0))
    per_b = lambda r, w: pl.BlockSpec((None, r, w), lambda i, j: (i, 0, 0))
    outs = [("qaug", 2 * wa, BF16), ("ka", wa, F32), ("va", wa, F32), ("kaug", 2 * wa, BF16), ("vab", wa, BF16),
            ("elem", LANES, F32), ("cum", LANES, F32), ("qb", wb, F32), ("kb", wb, F32), ("vb", wb, F32),
            ("bz", wb, F32), ("oc", wc, F32), ("vn", wc, F32)]
    out_shape = [jax.ShapeDtypeStruct((b, t, w), dt) for _, w, dt in outs]
    out_specs = [tok(w) for _, w, _ in outs]
    out_shape.append(jax.ShapeDtypeStruct((b, SUBLANES, 3 * wb), F32))
    out_specs.append(per_b(SUBLANES, 3 * wb))
    consts = [lw["g_pre_mix"], lw["w_big"], lw["w_small"], lw["sp"], lw["conv_w"]]
    consts2 = [lw["g_cv"], lw["b_cv"], lw["ws_cat"][cm], lw["bs_full"][cm], lw["g_c_out"], lw["hsum"], lw["pmat"]]
    kern = functools.partial(_inproj_kernel, tm=tm, cm=cm, nh=nh, wa=wa, wb=wb, wc=wc,
                             scale=HEAD_DIM ** -0.5)
    res = pl.pallas_call(
        kern,
        grid=(b, nt),
        in_specs=[tok(d)] + [_const_spec(c.shape) for c in consts] + [per_b(SUBLANES, 3 * wb)]
                 + [_const_spec(c.shape) for c in consts2],
        out_specs=out_specs,
        out_shape=out_shape,
        scratch_shapes=[pltpu.VMEM((SUBLANES, 3 * wb), F32), pltpu.VMEM((1, LANES), F32)],
        compiler_params=_params("arbitrary", "arbitrary"),
        name="inproj",
    )(x, *consts, conv_init, *consts2)
    named = {n: r for (n, _, _), r in zip(outs, res[:-1])}
    named["ytail"] = res[-1]
    return named


def _attn_kernel(qt_ref, k_ref, vt_ref, o_ref, *, tq, kc, look):
    p = pl.program_id(1)
    i = pl.program_id(2)
    qt = qt_ref[...]
    rowi = lax.broadcasted_iota(jnp.int32, qt.shape, 0)
    zero = jnp.zeros_like(qt)
    qts = []
    for e in range(2):
        a0 = LANES + AUG * (2 * p + e)
        keep = ((rowi >= e * HEAD_DIM) & (rowi < (e + 1) * HEAD_DIM)) | ((rowi >= a0) & (rowi < a0 + AUG))
        qts.append(jnp.where(keep, qt, zero))
    qpos = i * tq + lax.broadcasted_iota(jnp.int32, (kc, tq), 1)
    kofs = lax.broadcasted_iota(jnp.int32, (kc, tq), 0)
    ones = jnp.ones((2 * SUBLANES, kc), BF16)
    units = [(c, e) for c in range(tq // kc) for e in range(2)]

    def scores(j, c, e, masked):
        k0 = pl.multiple_of(j * tq + c * kc, kc)
        lo = c * kc if masked else 0
        s = jnp.dot(k_ref[pl.ds(k0, kc), :], qts[e][:, lo:], preferred_element_type=F32)
        return jnp.concatenate([jnp.full((kc, lo), NEG_INF, F32), s], axis=1) if lo else s

    def fold(j, c, e, s, st, masked):
        m, l, acc = st
        k0 = pl.multiple_of(j * tq + c * kc, kc)
        lo = c * kc if masked else 0
        if masked:
            s = jnp.where(k0 + kofs <= qpos, s, NEG_INF)
        m_new = jnp.maximum(m, jnp.max(s, axis=0, keepdims=True))
        alpha = jnp.exp2(m - m_new)
        pt = jnp.exp2(s - m_new).astype(BF16)
        vt = jnp.concatenate([vt_ref[e * HEAD_DIM:(e + 1) * HEAD_DIM, pl.ds(k0, kc)], ones], axis=0)
        r = jnp.dot(vt, pt[:, lo:], preferred_element_type=F32)
        if lo:
            r = jnp.concatenate([jnp.zeros((r.shape[0], lo), F32), r], axis=1)
        return m_new, alpha * l + r[HEAD_DIM:HEAD_DIM + 1], alpha * acc + r[:HEAD_DIM]

    def block(j, state, masked):
        state = list(state)
        pend = {}
        for k in range(min(look, len(units))):
            pend[k] = scores(j, *units[k], masked)
        for k, (c, e) in enumerate(units):
            if k + look < len(units):
                pend[k + look] = scores(j, *units[k + look], masked)
            state[e] = fold(j, c, e, pend.pop(k), state[e], masked)
        return tuple(state)

    st0 = (jnp.full((1, tq), NEG_INF, F32), jnp.zeros((1, tq), F32), jnp.zeros((HEAD_DIM, tq), F32))
    state = lax.fori_loop(0, i, lambda j, s: block(j, s, False), (st0, st0))
    state = block(i, state, True)
    ot = jnp.concatenate([acc / l for _, l, acc in state], axis=0)
    o_ref[...] = ot.T


def _attn_prompt(qaug, kaug, vab, *, tq, kc, look):
    b, s, wa = vab.shape
    npair = wa // LANES
    qt = jnp.transpose(qaug, (0, 2, 1))
    vt = jnp.transpose(vab, (0, 2, 1))
    kern = functools.partial(_attn_kernel, tq=tq, kc=kc, look=look)
    return pl.pallas_call(
        kern,
        grid=(b, npair, s // tq),
        in_specs=[pl.BlockSpec((None, 2 * LANES, tq), lambda bi, p, i: (bi, p, i)),
                  pl.BlockSpec((None, s, 2 * LANES), lambda bi, p, i: (bi, 0, p)),
                  pl.BlockSpec((None, LANES, s), lambda bi, p, i: (bi, p, 0))],
        out_specs=pl.BlockSpec((None, tq, LANES), lambda bi, p, i: (bi, i, p)),
        out_shape=jax.ShapeDtypeStruct((b, s, wa), F32),
        compiler_params=_params("arbitrary", "arbitrary", "arbitrary"),
        name="attn_prompt",
    )(qt, kaug, vt)


def _attn_sample_kernel(q_ref, kc_ref, vc_ref, kn_ref, vn_ref, cum_ref, rrow_ref, crow_ref, o_ref, *, n):
    p = pl.program_id(1)
    q = q_ref[:, :LANES]
    lane = lax.broadcasted_iota(jnp.int32, (n, LANES), 1)
    first = lane < HEAD_DIM
    zero = jnp.zeros_like(q)
    past = kc_ref.shape[-1]
    kc = kc_ref[...].reshape(LANES, past).astype(BF16)
    vc = vc_ref[...].reshape(LANES, past).astype(BF16)
    pad = jnp.zeros((LANES - n, LANES), BF16)
    kn = jnp.concatenate([kn_ref[:, :LANES], pad], axis=0)
    vn = jnp.concatenate([vn_ref[...], pad], axis=0)
    cum = cum_ref[...]
    causal = lane <= lax.broadcasted_iota(jnp.int32, (n, LANES), 0)
    outs = []
    for e in range(2):
        qm = jnp.where(first, q, zero) if e == 0 else jnp.where(first, zero, q)
        cq = jnp.sum(jnp.where(lane == 2 * p + e, cum, 0.0), axis=-1, keepdims=True)
        sc = _dot(qm, kc) + LOG2E * (cq + rrow_ref[e:e + 1, :])
        sn = jnp.where(causal, _dot_nt(qm, kn) + LOG2E * (cq - crow_ref[e:e + 1, :]), NEG_INF)
        m = jnp.maximum(jnp.max(sc, axis=-1, keepdims=True), jnp.max(sn, axis=-1, keepdims=True))
        pc = jnp.exp2(sc - m)
        pn = jnp.exp2(sn - m)
        l = jnp.sum(pc, axis=-1, keepdims=True) + jnp.sum(pn, axis=-1, keepdims=True)
        outs.append((_dot_nt(pc, vc) + _dot(pn, vn)) / l)
    o_ref[...] = jnp.where(first, outs[0], outs[1])


def _attn_sample(qaug, kaug, vab, cum, cache_kt, cache_vt, layer, rrow, crow):
    b, n, wa = vab.shape
    past = cache_kt.shape[-1]
    npair = wa // LANES
    new = lambda w: pl.BlockSpec((None, n, w), lambda bi, p: (bi, 0, p))
    old = lambda: pl.BlockSpec((None, None, 2, HEAD_DIM, past), lambda bi, p: (layer, bi, p, 0, 0))
    return pl.pallas_call(
        functools.partial(_attn_sample_kernel, n=n),
        grid=(b, npair),
        in_specs=[new(2 * LANES), old(), old(), new(2 * LANES), new(LANES),
                  pl.BlockSpec((None, n, LANES), lambda bi, p: (bi, 0, 0)),
                  pl.BlockSpec((None, None, 2, past), lambda bi, p: (bi, p, 0, 0)),
                  pl.BlockSpec((None, None, 2, LANES), lambda bi, p: (bi, p, 0, 0))],
        out_specs=new(LANES),
        out_shape=jax.ShapeDtypeStruct((b, n, wa), F32),
        compiler_params=_params("arbitrary", "arbitrary"),
        name="attn_sample",
    )(qaug, cache_kt, cache_vt, kaug, vab, cum, rrow, crow)


def _suffix_kernel(x_ref, o_ref, carry, *, tp):
    @pl.when(pl.program_id(1) == 0)
    def _():
        carry[...] = jnp.zeros_like(carry)
    v = x_ref[...]
    row = lax.broadcasted_iota(jnp.int32, v.shape, 0)
    s = 1
    while s < tp:
        v = v + jnp.where(row + s < tp, pltpu.roll(v, tp - s, 0), 0.0)
        s *= 2
    v = v + carry[...]
    o_ref[...] = v
    carry[...] = v[0:1, :]


def _seq_suffix_sum(x, *, tp):
    b, p, w = x.shape
    nt = p // tp
    spec = pl.BlockSpec((None, tp, w), lambda i, j: (i, nt - 1 - j, 0))
    return pl.pallas_call(
        functools.partial(_suffix_kernel, tp=tp),
        grid=(b, nt),
        in_specs=[spec], out_specs=spec,
        out_shape=jax.ShapeDtypeStruct(x.shape, F32),
        scratch_shapes=[pltpu.VMEM((1, w), F32)],
        compiler_params=_params("arbitrary", "arbitrary"),
        name="seq_suffix_sum",
    )(x)


def _gdn_kernel(q_ref, k_ref, v_ref, bz_ref, elem_ref, s0_ref, gb_ref, esel_ref, hsum_ref,
                o_ref, sout_ref, s_scr, *, nb, nh):
    L = GDN_BLOCK
    t = pl.program_id(1)

    @pl.when(t == 0)
    def _():
        s_scr[...] = s0_ref[...]

    lane = lax.broadcasted_iota(jnp.int32, (L, LANES), 1)
    first = lane < HEAD_DIM
    ri = lax.broadcasted_iota(jnp.int32, (L, L), 0)
    ci = lax.broadcasted_iota(jnp.int32, (L, L), 1)
    incl = ci <= ri
    strict = ci < ri
    same_head = (ri < HEAD_DIM) == (ci < HEAD_DIM)
    lane2 = lax.broadcasted_iota(jnp.int32, (L, 2 * L), 1)
    first2 = jnp.bitwise_and(lane2, LANES - 1) < HEAD_DIM
    xor2 = jnp.bitwise_xor(lax.broadcasted_iota(jnp.int32, (L, 2 * L), 0), jnp.bitwise_and(lane2, L - 1))
    zero_ll = jnp.zeros((L, L), BF16)

    def halves(x, sel):
        return jnp.concatenate([jnp.where(sel, x, 0.0), jnp.where(sel, 0.0, x)], axis=0)

    def dot_heads(y, x):
        xb = x.astype(BF16)
        bd = jnp.concatenate([jnp.concatenate([xb[:, :L], zero_ll], axis=1),
                              jnp.concatenate([zero_ll, xb[:, L:]], axis=1)], axis=0)
        return jnp.dot(y.astype(BF16), bd, preferred_element_type=F32)

    npair = s_scr.shape[0]
    wbw = npair * LANES
    chains = [(n, p) for n in range(nb) for p in range(npair)]
    ex = {}
    for n in range(nb):
        elem = elem_ref[n * L:(n + 1) * L, :]
        gsum = _seg_cumsum(elem, L)
        mixed = jnp.where((lane >= nh) & (lane < 2 * nh), gsum, elem)
        ex[n] = _dot_select_exact(mixed, esel_ref[...])

    c = {}
    for n, p in chains:
        rows = slice(n * L, (n + 1) * L)
        cols = slice(p * LANES, (p + 1) * LANES)
        g = ex[n][:, cols]
        bt = ex[n][:, wbw + p * LANES: wbw + (p + 1) * LANES]
        kp = k_ref[rows, cols]
        qp = q_ref[rows, cols]
        g_sw = pltpu.roll(g, HEAD_DIM, 1)
        b_sw = pltpu.roll(bt, HEAD_DIM, 1)
        g_t = g.T
        a_parts, qk_parts = [], []
        for e in range(2):
            sel = first if e == 0 else jnp.logical_not(first)
            gcol = jnp.where(sel, g, g_sw)
            bcol = jnp.where(sel, bt, b_sw)
            grow = g_t[e * HEAD_DIM:e * HEAD_DIM + 1, :]
            dec = jnp.exp(jnp.where(incl, gcol - grow, NEG_INF))
            kk = _dot_nt(jnp.where(sel, kp, 0.0), kp)
            qk_parts.append(_dot_nt(jnp.where(sel, qp, 0.0), kp) * dec)
            a_parts.append(jnp.where(strict, bcol * kk * dec, 0.0))
        a_cat = jnp.concatenate(a_parts, axis=1)
        eg = jnp.exp(g)
        glast = g[L - 1:L, :]
        c[n, p] = dict(a=a_cat, qk=jnp.concatenate(qk_parts, axis=1), glast=glast, qg=qp * eg,
                       kdec=kp * jnp.exp(glast - g),
                       r=jnp.concatenate([v_ref[rows, cols] * bt, kp * bt * eg], axis=1),
                       tm1=-jnp.where(xor2 < 2, a_cat, 0.0))

    s_blk = 2
    while s_blk < L:
        pm = {}
        for key in chains:
            nmat = jnp.where((xor2 >= s_blk) & (xor2 < 2 * s_blk), c[key]["a"], 0.0)
            pm[key] = nmat + dot_heads(c[key]["tm1"], nmat)
        for key in chains:
            c[key]["tm1"] = c[key]["tm1"] - pm[key] - dot_heads(pm[key], c[key]["tm1"])
        s_blk *= 2
    for key in chains:
        r = c[key]["r"]
        c[key]["uw"] = r + _dot(c[key]["tm1"], halves(r, first2))

    for n in range(nb):
        rows = slice(n * L, (n + 1) * L)
        pairs = range(npair)
        ws = [_dot(jnp.concatenate([c[n, p]["uw"][:, LANES:], c[n, p]["qg"]], axis=0), s_scr[p]) for p in pairs]
        u = [c[n, p]["uw"][:, :LANES] - ws[p][:L] for p in pairs]
        o = [ws[p][L:] + _dot(c[n, p]["qk"], halves(u[p], first)) for p in pairs]
        for p in pairs:
            s_scr[p] = (s_scr[p] * jnp.exp(c[n, p]["glast"])
                        + jnp.where(same_head, _dot(c[n, p]["kdec"].T, u[p]), 0.0))
        for p in pairs:
            cols = slice(p * LANES, (p + 1) * LANES)
            ms = _dot(o[p] * o[p], hsum_ref[...]) * (1.0 / HEAD_DIM)
            bz = bz_ref[rows, cols]
            o_ref[rows, cols] = o[p] * lax.rsqrt(ms + 1e-6) * gb_ref[...] * (bz * _sigmoid(bz))

    @pl.when(t == pl.num_programs(1) - 1)
    def _():
        sout_ref[...] = s_scr[...]


def _gdn(qb, kb, vb, bz, elem, s0, lw, *, nb):
    b, t, wb = qb.shape
    tile = nb * GDN_BLOCK
    npair = wb // LANES
    tok = lambda w: pl.BlockSpec((None, tile, w), lambda i, j: (i, j, 0))
    st = pl.BlockSpec((None, npair, LANES, LANES), lambda i, j: (i, 0, 0, 0))
    consts = [lw["g_b_pair"], lw["esel"], lw["hsum128"]]
    return pl.pallas_call(
        functools.partial(_gdn_kernel, nb=nb, nh=lw["nh"]),
        grid=(b, t // tile),
        in_specs=[tok(wb), tok(wb), tok(wb), tok(wb), tok(LANES), st] + [_const_spec(c.shape) for c in consts],
        out_specs=[tok(wb), st],
        out_shape=[jax.ShapeDtypeStruct((b, t, wb), F32),
                   jax.ShapeDtypeStruct((b, npair, LANES, LANES), F32)],
        scratch_shapes=[pltpu.VMEM((npair, LANES, LANES), F32)],
        compiler_params=_params("arbitrary", "arbitrary"),
        name="gdn",
    )(qb, kb, vb, bz, elem, s0, *consts)


def _outffn_kernel(oa_ref, ob_ref, oc_ref, x_ref, ga_ref, wout_ref, gpm_ref, gpf_ref, wfi_ref, wfo_ref,
                   gpo_ref, y_ref, *, dff, nsub):
    r = x_ref.shape[0] // nsub
    rows = [slice(i * r, (i + 1) * r) for i in range(nsub)]
    cat = [jnp.concatenate([_rms(oa_ref[rs, :], ga_ref[...]), ob_ref[rs, :], oc_ref[rs, :]], axis=-1).astype(BF16)
           for rs in rows]
    m = [jnp.dot(c, wout_ref[...], preferred_element_type=F32) for c in cat]
    x1 = [x_ref[rs, :] + _rms(mi, gpm_ref[...]) for rs, mi in zip(rows, m)]
    h = [_rms(xi, gpf_ref[...]).astype(BF16) for xi in x1]
    gu = [jnp.dot(hi, wfi_ref[...], preferred_element_type=F32) for hi in h]
    a = [(g[:, :dff] * _sigmoid(g[:, :dff]) * g[:, dff:]).astype(BF16) for g in gu]
    f = [jnp.dot(ai, wfo_ref[...], preferred_element_type=F32) for ai in a]
    for rs, xi, fi in zip(rows, x1, f):
        y_ref[rs, :] = xi + _rms(fi, gpo_ref[...])


def _outffn(oa, ob, oc, x, lw, *, tm):
    n, d = x.shape
    dff = lw["w_ffn_out"].shape[0]
    tok = lambda w: pl.BlockSpec((tm, w), lambda i: (i, 0))
    consts = [lw["g_a_out"], lw["w_out"], lw["g_post_mix"], lw["g_pre_ffn"], lw["w_ffn_in"],
              lw["w_ffn_out"], lw["g_post_ffn"]]
    return pl.pallas_call(
        functools.partial(_outffn_kernel, dff=dff, nsub=2 if tm % (4 * SUBLANES) == 0 else 1),
        grid=(n // tm,),
        in_specs=[tok(oa.shape[1]), tok(ob.shape[1]), tok(oc.shape[1]), tok(d)]
                 + [_const_spec(c.shape) for c in consts],
        out_specs=tok(d),
        out_shape=jax.ShapeDtypeStruct((n, d), F32),
        compiler_params=_params("arbitrary"),
        name="outffn",
    )(oa, ob, oc, x, *consts)


def _block_ones(width):
    idx = jnp.arange(width) // HEAD_DIM
    return (idx[:, None] == idx[None, :]).astype(BF16)


def _layer_weights(l, prm, cms):
    w_in = prm["w_in"][l]
    nh = prm["b_f"].shape[1]
    wa = nh * HEAD_DIM
    wb = prm["a_log"].shape[1] * HEAD_DIM
    wc = prm["g_cv"].shape[1]
    ng = prm["w_s"].shape[1]
    assert prm["a_log"].shape[1] == nh and wa % LANES == 0 and wc % LANES == 0 and AUG * nh <= LANES
    sizes = (wa, wa, wa, nh, 3 * wb, nh, nh, wb, wc, wc)
    offs = [0]
    for sz in sizes:
        offs.append(offs[-1] + sz)
    w_in_t = w_in.T
    col = lambda i: w_in_t[offs[i]:offs[i + 1]]
    w_big = jnp.concatenate([col(4), col(8), col(9), col(0), col(1), col(2), col(7)], axis=0).astype(BF16)
    w_small = jnp.concatenate([col(3), col(5), col(6), jnp.zeros((LANES - 3 * nh, w_in.shape[0]), F32)],
                              axis=0).astype(BF16)
    zpad = jnp.zeros((LANES - 2 * nh,), F32)
    sp = jnp.zeros((SUBLANES, LANES), F32)
    sp = sp.at[0].set(jnp.concatenate([prm["b_f"][l], prm["dt_bias"][l], zpad]))
    sp = sp.at[1].set(jnp.concatenate([jnp.zeros((nh,), F32), prm["a_log"][l], zpad]))
    hl = jnp.arange(nh) * AUG
    sp = sp.at[2, (hl[:, None] + jnp.arange(3, 6)[None, :]).reshape(-1)].set(1.0)
    sp = sp.at[3, (hl[:, None] + jnp.arange(0, 3)[None, :]).reshape(-1)].set(1.0)
    pmat = jnp.zeros((3 * LANES, 2 * LANES), F32)
    for piece in range(3):
        pmat = pmat.at[piece * LANES + jnp.arange(nh), hl + piece].set(1.0)
        pmat = pmat.at[piece * LANES + jnp.arange(nh), LANES + hl + 3 + piece].set(1.0)
    row = lambda v: v.reshape(1, -1)
    ws_cat, bs_full = {}, {}
    for cm in cms:
        pos = jnp.arange(cm) // HEAD_DIM
        w = jnp.where(pos[None, :] <= pos[:, None], prm["w_s"][l][:, :cm, :cm], 0.0)
        pairs = [jnp.concatenate([w[2 * pp], w[2 * pp + 1]], axis=1) for pp in range(ng // 2)]
        kpad = max(LANES - 2 * cm, 0)
        ws_cat[cm] = jnp.pad(jnp.stack(pairs), ((0, 0), (0, 0), (0, kpad))).astype(BF16)
        bs_full[cm] = jnp.repeat(prm["b_s"][l][:, :cm].T, wc // ng, axis=1)
    src = jnp.arange(LANES)[:, None]
    dst = jnp.arange(wb)[None, :] // HEAD_DIM
    esel = jnp.concatenate([src == nh + dst, src == 2 * nh + dst], axis=1).astype(BF16)
    return dict(
        nh=nh, wa=wa, wb=wb, wc=wc,
        g_pre_mix=row(prm["g_pre_mix"][l]), w_big=w_big, w_small=w_small, sp=sp, conv_w=prm["conv_w"][l],
        g_cv=row(prm["g_cv"][l]), b_cv=row(prm["b_cv"][l]), ws_cat=ws_cat, bs_full=bs_full,
        g_c_out=row(prm["g_c_out"][l]), hsum=_block_ones(2 * LANES), hsum128=_block_ones(LANES), pmat=pmat.astype(BF16),
        g_b_pair=row(jnp.tile(prm["g_b_out"][l], LANES // HEAD_DIM)), esel=esel,
        g_a_out=row(prm["g_a_out"][l]), w_out=prm["w_out"][l].astype(BF16),
        g_post_mix=row(prm["g_post_mix"][l]), g_pre_ffn=row(prm["g_pre_ffn"][l]),
        w_ffn_in=prm["w_ffn_in"][l].astype(BF16), w_ffn_out=prm["w_ffn_out"][l].astype(BF16),
        g_post_ffn=row(prm["g_post_ffn"][l]))


def _pair_state(s):
    b, h, dk, dv = s.shape
    s = s.reshape(b, h // 2, 2, dk, dv)
    z = jnp.zeros_like(s[:, :, 0])
    top = jnp.concatenate([s[:, :, 0], z], axis=-1)
    bot = jnp.concatenate([z, s[:, :, 1]], axis=-1)
    return jnp.concatenate([top, bot], axis=-2)


def _unpair_state(sp):
    b, hp, _, _ = sp.shape
    s0 = sp[:, :, :HEAD_DIM, :HEAD_DIM]
    s1 = sp[:, :, HEAD_DIM:, HEAD_DIM:]
    return jnp.stack([s0, s1], axis=2).reshape(b, 2 * hp, HEAD_DIM, HEAD_DIM)


def _head_rows(cum, nh):
    b, t, _ = cum.shape
    return jnp.transpose(cum[:, :, :nh], (0, 2, 1)).reshape(b, nh // 2, 2, t)


def _pick(n, prefs):
    for c in prefs:
        if n % c == 0:
            return c
    return n


def _layer(x, lw, conv_prev, s0, cache, *, cm):
    b, t, d = x.shape
    nh, wb = lw["nh"], lw["wb"]
    kw1 = conv_prev.shape[1]
    conv_init = jnp.pad(conv_prev, ((0, 0), (SUBLANES - kw1, 0), (0, 0)))
    tm = _pick(t, (256, 128, 64, 32, 16))
    pj = _inproj(x, lw, conv_init, tm=tm, cm=cm)

    if cache is None:
        tq = _pick(t, (ATTN_TQ, 256, 128))
        oa = _attn_prompt(pj["qaug"], pj["kaug"], pj["vab"], tq=tq, kc=min(ATTN_KC, tq), look=ATTN_LOOK)
    else:
        ck, cv, layer, clogf = cache
        bs, past = clogf.shape[:2]
        incl = _seq_suffix_sum(jnp.pad(clogf, ((0, 0), (0, 0), (0, LANES - nh))), tp=_pick(past, (4096, 2048, 1024, 512)))
        incl = jnp.transpose(incl[:, :, :nh], (0, 2, 1))
        excl = jnp.concatenate([incl[:, :, 1:], jnp.zeros_like(incl[:, :, :1])], axis=2)
        rrow = excl.reshape(bs, nh // 2, 2, past)
        crow = jnp.pad(_head_rows(pj["cum"], nh), ((0, 0), (0, 0), (0, 0), (0, LANES - t)))
        oa = _attn_sample(pj["qaug"], pj["kaug"], pj["vab"], pj["cum"], ck, cv, layer, rrow, crow)

    tp = -(-t // GDN_BLOCK) * GDN_BLOCK
    padt = lambda a: a if tp == t else jnp.pad(a, ((0, 0), (0, tp - t), (0, 0)))
    nb = _pick(tp // GDN_BLOCK, (2, 1))
    ob, s_new = _gdn(padt(pj["qb"]), padt(pj["kb"]), padt(pj["vb"]), padt(pj["bz"]), padt(pj["elem"]),
                     _pair_state(s0), lw, nb=nb)
    ob = ob[:, :t]

    n = b * t
    y = _outffn(oa.reshape(n, -1), ob.reshape(n, -1), pj["oc"].reshape(n, -1), x.reshape(n, d), lw,
                tm=_pick(n, (512, 256, 128, 64, 32, 16)))
    state = (pj["ka"].reshape(b, t, nh, HEAD_DIM), pj["va"].reshape(b, t, nh, HEAD_DIM), pj["elem"][:, :, :nh],
             pj["ytail"][:, SUBLANES - kw1:, :], _unpair_state(s_new), pj["vn"])
    return y.reshape(b, t, d), state


def kernel(x_prompt, x_sample, cache_a_k, cache_a_v, cache_a_logf, state_b_conv, state_b_S, g_pre_mix, w_in, b_f, conv_w, a_log, dt_bias, g_b_out, g_a_out, g_cv, b_cv, w_s, b_s, g_c_out, w_out, g_post_mix, g_pre_ffn, w_ffn_in, w_ffn_out, g_post_ffn):
    prm = dict(g_pre_mix=g_pre_mix, w_in=w_in, b_f=b_f, conv_w=conv_w, a_log=a_log, dt_bias=dt_bias,
               g_b_out=g_b_out, g_a_out=g_a_out, g_cv=g_cv, b_cv=b_cv, w_s=w_s, b_s=b_s, g_c_out=g_c_out,
               w_out=w_out, g_post_mix=g_post_mix, g_pre_ffn=g_pre_ffn, w_ffn_in=w_ffn_in,
               w_ffn_out=w_ffn_out, g_post_ffn=g_post_ffn)
    depth = w_in.shape[0]
    bp, sp_len, _ = x_prompt.shape
    n_new = x_sample.shape[1]
    cm_p = w_s.shape[2]
    assert sp_len % cm_p == 0 and sp_len % GDN_BLOCK == 0 and n_new <= HEAD_DIM and n_new % SUBLANES == 0
    kw1 = conv_w.shape[1] - 1
    nhb = a_log.shape[1]
    yp, ys = x_prompt, x_sample
    outs_p, outs_s = [], []
    cache_kt = jnp.transpose(cache_a_k, (0, 1, 3, 4, 2))
    cache_vt = jnp.transpose(cache_a_v, (0, 1, 3, 4, 2))
    for l in range(depth):
        lw = _layer_weights(l, prm, (cm_p, n_new))
        conv0 = jnp.zeros((bp, kw1, conv_w.shape[2]), F32)
        s0 = jnp.zeros((bp, nhb, HEAD_DIM, HEAD_DIM), F32)
        yp, st_p = _layer(yp, lw, conv0, s0, None, cm=cm_p)
        ys, st_s = _layer(ys, lw, state_b_conv[l], state_b_S[l],
                          (cache_kt, cache_vt, l, cache_a_logf[l]), cm=n_new)
        outs_p.append(st_p)
        outs_s.append(st_s)
    stk = lambda outs, i: jnp.stack([o[i] for o in outs], axis=0)
    return (yp, ys, stk(outs_p, 0), stk(outs_p, 1), stk(outs_p, 2), stk(outs_p, 3), stk(outs_p, 4),
            stk(outs_s, 0), stk(outs_s, 1), stk(outs_s, 2), stk(outs_s, 3), stk(outs_s, 4), stk(outs_s, 5))
```

```python
import functools

import jax
import jax.numpy as jnp
from jax import lax
from jax.experimental import pallas as pl
from jax.experimental.pallas import tpu as pltpu

F32 = jnp.float32
BF16 = jnp.bfloat16

LANES = 128
SUBLANES = 8
HEAD_DIM = 64
GDN_BLOCK = 128
GDN_GROUP = 2
GDN_TILE_BLOCKS = 4
ATTN_TQ = 1024
ATTN_KC = 256
ATTN_LOOK = 2
VMEM_LIMIT = 56 * 1024 * 1024
NEG_INF = float("-inf")
LOG2E = 1.4426950408889634
AUG = 16


def _dot(a, b):
    return jnp.dot(a.astype(BF16), b.astype(BF16), preferred_element_type=F32)


def _dot_nt(a, b):
    return lax.dot_general(a.astype(BF16), b.astype(BF16), (((1,), (1,)), ((), ())),
                           preferred_element_type=F32)


def _dot_select_exact(x, sel):
    hi = x.astype(BF16)
    r1 = x - hi.astype(F32)
    mid = r1.astype(BF16)
    lo = (r1 - mid.astype(F32)).astype(BF16)
    d = lambda p: jnp.dot(p, sel, preferred_element_type=F32)
    return (d(hi) + d(mid)) + d(lo)


def _rms(x, g, eps=1e-6):
    return x * lax.rsqrt(jnp.mean(x * x, axis=-1, keepdims=True) + eps) * g


def _softplus(x):
    return jnp.maximum(x, 0.0) + jnp.log1p(jnp.exp(-jnp.abs(x)))


def _sigmoid(x):
    return 1.0 / (1.0 + jnp.exp(-x))


def _seg_cumsum(v, seg):
    row = lax.broadcasted_iota(jnp.int32, v.shape, 0)
    pos = jnp.bitwise_and(row, seg - 1)
    s = 1
    while s < seg:
        v = v + jnp.where(pos >= s, pltpu.roll(v, s, 0), 0.0)
        s *= 2
    return v


def _const_spec(shape):
    nd = len(shape)
    return pl.BlockSpec(shape, lambda *_: (0,) * nd, pipeline_mode=pl.Buffered(1))


def _params(*sem):
    return pltpu.CompilerParams(dimension_semantics=sem, vmem_limit_bytes=VMEM_LIMIT)


def _inproj_kernel(x_ref, gpre_ref, wbig_ref, wsm_ref, sp_ref, convw_ref, convinit_ref, gcv_ref,
                   bcv_ref, ws_ref, bs_ref, gco_ref, hsum_ref, pmat_ref,
                   qaug_ref, ka_ref, va_ref, kaug_ref, vab_ref, elem_ref, cum_ref, qb_ref, kb_ref,
                   vb_ref, bz_ref, oc_ref, vn_ref, ytail_ref,
                   carry_conv, carry_cum, *, tm, cm, nh, wa, wb, wc, scale):
    @pl.when(pl.program_id(1) == 0)
    def _():
        carry_cum[...] = jnp.zeros_like(carry_cum)
        carry_conv[...] = convinit_ref[...]

    h = _rms(x_ref[...], gpre_ref[...]).astype(BF16)
    o_c = 3 * wb
    o_a = o_c + 2 * wc
    proj = lambda w: lax.dot_general(h, w, (((1,), (1,)), ((), ())), preferred_element_type=F32)
    y = proj(wbig_ref[:o_c, :])
    zc = proj(wbig_ref[o_c:o_a, :])
    zs = proj(wsm_ref[...])
    za = proj(wbig_ref[o_a:, :])

    ka = za[:, wa:2 * wa]
    va = za[:, 2 * wa:3 * wa]
    ka_ref[...] = ka
    va_ref[...] = va
    vab_ref[...] = va.astype(BF16)
    bz_ref[...] = za[:, 3 * wa:]

    lane = lax.broadcasted_iota(jnp.int32, (tm, LANES), 1)
    zb = zs + sp_ref[0:1, :]
    soft_tail = jnp.log1p(jnp.exp(-jnp.abs(zb)))
    logf = -(jnp.maximum(-zb, 0.0) + soft_tail)
    gl = -jnp.exp(sp_ref[1:2, :]) * (jnp.maximum(zb, 0.0) + soft_tail)
    beta = _sigmoid(zs)
    elem = jnp.where(lane < nh, logf, jnp.where(lane < 2 * nh, gl, jnp.where(lane < 3 * nh, beta, 0.0)))
    elem_ref[...] = elem

    cum = _seg_cumsum(elem, tm) + carry_cum[...]
    cum_ref[...] = cum
    carry_cum[...] = cum[tm - 1:tm, :]

    c2 = jnp.where(lane < nh, cum * LOG2E, 0.0)
    hi = c2.astype(BF16)
    r1 = c2 - hi.astype(F32)
    mid = r1.astype(BF16)
    lo = (r1 - mid.astype(F32)).astype(BF16)
    placed = jnp.dot(jnp.concatenate([hi, mid, lo], axis=1), pmat_ref[...], preferred_element_type=F32)
    augq = (placed[:, :LANES] + sp_ref[2:3, :]).astype(BF16)
    augk = (sp_ref[3:4, :] - placed[:, LANES:]).astype(BF16)
    qs = (za[:, :wa] * (scale * LOG2E)).astype(BF16)
    ks = ka.astype(BF16)
    qaug_ref[...] = jnp.concatenate(
        [a for j in range(0, wa, LANES) for a in (qs[:, j:j + LANES], augq)], axis=1)
    kaug_ref[...] = jnp.concatenate(
        [a for j in range(0, wa, LANES) for a in (ks[:, j:j + LANES], augk)], axis=1)

    prev = carry_conv[...]
    row8 = lax.broadcasted_iota(jnp.int32, prev.shape, 0)
    kw = convw_ref.shape[0]
    acc = y * convw_ref[kw - 1:kw, :]
    for k in range(1, kw):
        yk = pltpu.roll(y, k, 0)
        top = jnp.where(row8 < k, pltpu.roll(prev, k, 0), yk[0:SUBLANES])
        yk = jnp.concatenate([top, yk[SUBLANES:]], axis=0)
        acc = acc + yk * convw_ref[kw - 1 - k:kw - k, :]
    carry_conv[...] = y[tm - SUBLANES:tm]
    ytail_ref[...] = y[tm - SUBLANES:tm]
    yc = acc * _sigmoid(acc)
    qb = yc[:, :wb]
    kb = yc[:, wb:2 * wb]
    sq = jnp.concatenate([qb * qb, kb * kb], axis=-1).astype(BF16)
    hw = hsum_ref.shape[0]
    ss = jnp.concatenate([jnp.dot(sq[:, j:j + hw], hsum_ref[...], preferred_element_type=F32)
                          for j in range(0, 2 * wb, hw)], axis=-1)
    qb_ref[...] = qb * lax.rsqrt(ss[:, :wb] + 1e-6) * scale
    kb_ref[...] = kb * lax.rsqrt(ss[:, wb:] + 1e-6)
    vb_ref[...] = yc[:, 2 * wb:]

    u = jax.nn.gelu(zc[:, :wc])
    gv = jax.nn.gelu(zc[:, wc:])
    mu = jnp.mean(gv, axis=-1, keepdims=True)
    var = jnp.mean(jnp.square(gv - mu), axis=-1, keepdims=True)
    vn = (gv - mu) * lax.rsqrt(var + 1e-5) * gcv_ref[...] + bcv_ref[...]
    vn_ref[...] = vn
    first = lax.broadcasted_iota(jnp.int32, (cm, LANES), 1) < HEAD_DIM
    kpad = ws_ref.shape[2] - 2 * cm
    rows = []
    for c in range(tm // cm):
        vc = vn[c * cm:(c + 1) * cm]
        cols = []
        for pp in range(wc // LANES):
            vp = vc[:, pp * LANES:(pp + 1) * LANES]
            parts = [jnp.where(first, vp, 0.0), jnp.where(first, 0.0, vp)]
            if kpad:
                parts.append(jnp.zeros((kpad, LANES), F32))
            cols.append(_dot(ws_ref[pp], jnp.concatenate(parts, axis=0)))
        s = jnp.concatenate(cols, axis=-1) + bs_ref[...]
        rows.append(u[c * cm:(c + 1) * cm] * s)
    oc = rows[0] if len(rows) == 1 else jnp.concatenate(rows, axis=0)
    oc_ref[...] = _rms(oc, gco_ref[...])


def _inproj(x, lw, conv_init, *, tm, cm):
    b, t, d = x.shape
    nt = t // tm
    wa, wb, wc, nh = lw["wa"], lw["wb"], lw["wc"], lw["nh"]
    tok = lambda w: pl.BlockSpec((None, tm, w), lambda i, j: (i, j, 0))
    per_b = lambda r, w: pl.BlockSpec((None, r, w), lambda i, j: (i, 0, 0))
    outs = [("qaug", 2 * wa, BF16), ("ka", wa, F32), ("va", wa, F32), ("kaug", 2 * wa, BF16), ("vab", wa, BF16),
            ("elem", LANES, F32), ("cum", LANES, F32), ("qb", wb, F32), ("kb", wb, F32), ("vb", wb, F32),
            ("bz", wb, F32), ("oc", wc, F32), ("vn", wc, F32)]
    out_shape = [jax.ShapeDtypeStruct((b, t, w), dt) for _, w, dt in outs]
    out_specs = [tok(w) for _, w, _ in outs]
    out_shape.append(jax.ShapeDtypeStruct((b, SUBLANES, 3 * wb), F32))
    out_specs.append(per_b(SUBLANES, 3 * wb))
    consts = [lw["g_pre_mix"], lw["w_big"], lw["w_small"], lw["sp"], lw["conv_w"]]
    consts2 = [lw["g_cv"], lw["b_cv"], lw["ws_cat"][cm], lw["bs_full"][cm], lw["g_c_out"], lw["hsum"], lw["pmat"]]
    kern = functools.partial(_inproj_kernel, tm=tm, cm=cm, nh=nh, wa=wa, wb=wb, wc=wc,
                             scale=HEAD_DIM ** -0.5)
    res = pl.pallas_call(
        kern,
        grid=(b, nt),
        in_specs=[tok(d)] + [_const_spec(c.shape) for c in consts] + [per_b(SUBLANES, 3 * wb)]
                 + [_const_spec(c.shape) for c in consts2],
        out_specs=out_specs,
        out_shape=out_shape,
        scratch_shapes=[pltpu.VMEM((SUBLANES, 3 * wb), F32), pltpu.VMEM((1, LANES), F32)],
        compiler_params=_params("arbitrary", "arbitrary"),
        name="inproj",
    )(x, *consts, conv_init, *consts2)
    named = {n: r for (n, _, _), r in zip(outs, res[:-1])}
    named["ytail"] = res[-1]
    return named


def _attn_kernel(qt_ref, k_ref, vt_ref, o_ref, *, tq, kc, look):
    p = pl.program_id(1)
    i = pl.program_id(2)
    qt = qt_ref[...]
    rowi = lax.broadcasted_iota(jnp.int32, qt.shape, 0)
    zero = jnp.zeros_like(qt)
    qts = []
    for e in range(2):
        a0 = LANES + AUG * (2 * p + e)
        keep = ((rowi >= e * HEAD_DIM) & (rowi < (e + 1) * HEAD_DIM)) | ((rowi >= a0) & (rowi < a0 + AUG))
        qts.append(jnp.where(keep, qt, zero))
    qpos = i * tq + lax.broadcasted_iota(jnp.int32, (kc, tq), 1)
    kofs = lax.broadcasted_iota(jnp.int32, (kc, tq), 0)
    ones = jnp.ones((2 * SUBLANES, kc), BF16)
    units = [(c, e) for c in range(tq // kc) for e in range(2)]

    def scores(j, c, e, masked):
        k0 = pl.multiple_of(j * tq + c * kc, kc)
        lo = c * kc if masked else 0
        s = jnp.dot(k_ref[pl.ds(k0, kc), :], qts[e][:, lo:], preferred_element_type=F32)
        return jnp.concatenate([jnp.full((kc, lo), NEG_INF, F32), s], axis=1) if lo else s

    def fold(j, c, e, s, st, masked):
        m, l, acc = st
        k0 = pl.multiple_of(j * tq + c * kc, kc)
        lo = c * kc if masked else 0
        if masked:
            s = jnp.where(k0 + kofs <= qpos, s, NEG_INF)
        m_new = jnp.maximum(m, jnp.max(s, axis=0, keepdims=True))
        alpha = jnp.exp2(m - m_new)
        pt = jnp.exp2(s - m_new).astype(BF16)
        vt = jnp.concatenate([vt_ref[e * HEAD_DIM:(e + 1) * HEAD_DIM, pl.ds(k0, kc)], ones], axis=0)
        r = jnp.dot(vt, pt[:, lo:], preferred_element_type=F32)
        if lo:
            r = jnp.concatenate([jnp.zeros((r.shape[0], lo), F32), r], axis=1)
        return m_new, alpha * l + r[HEAD_DIM:HEAD_DIM + 1], alpha * acc + r[:HEAD_DIM]

    def block(j, state, masked):
        state = list(state)
        pend = {}
        for k in range(min(look, len(units))):
            pend[k] = scores(j, *units[k], masked)
        for k, (c, e) in enumerate(units):
            if k + look < len(units):
                pend[k + look] = scores(j, *units[k + look], masked)
            state[e] = fold(j, c, e, pend.pop(k), state[e], masked)
        return tuple(state)

    st0 = (jnp.full((1, tq), NEG_INF, F32), jnp.zeros((1, tq), F32), jnp.zeros((HEAD_DIM, tq), F32))
    state = lax.fori_loop(0, i, lambda j, s: block(j, s, False), (st0, st0))
    state = block(i, state, True)
    ot = jnp.concatenate([acc / l for _, l, acc in state], axis=0)
    o_ref[...] = ot.T


def _attn_prompt(qaug, kaug, vab, *, tq, kc, look):
    b, s, wa = vab.shape
    npair = wa // LANES
    qt = jnp.transpose(qaug, (0, 2, 1))
    vt = jnp.transpose(vab, (0, 2, 1))
    kern = functools.partial(_attn_kernel, tq=tq, kc=kc, look=look)
    return pl.pallas_call(
        kern,
        grid=(b, npair, s // tq),
        in_specs=[pl.BlockSpec((None, 2 * LANES, tq), lambda bi, p, i: (bi, p, i)),
                  pl.BlockSpec((None, s, 2 * LANES), lambda bi, p, i: (bi, 0, p)),
                  pl.BlockSpec((None, LANES, s), lambda bi, p, i: (bi, p, 0))],
        out_specs=pl.BlockSpec((None, tq, LANES), lambda bi, p, i: (bi, i, p)),
        out_shape=jax.ShapeDtypeStruct((b, s, wa), F32),
        compiler_params=_params("arbitrary", "arbitrary", "arbitrary"),
        name="attn_prompt",
    )(qt, kaug, vt)


def _attn_sample_kernel(q_ref, kc_ref, vc_ref, kn_ref, vn_ref, cum_ref, rrow_ref, crow_ref, o_ref, *, n):
    p = pl.program_id(1)
    q = q_ref[:, :LANES]
    lane = lax.broadcasted_iota(jnp.int32, (n, LANES), 1)
    first = lane < HEAD_DIM
    zero = jnp.zeros_like(q)
    past = kc_ref.shape[-1]
    kc = kc_ref[...].reshape(LANES, past).astype(BF16)
    vc = vc_ref[...].reshape(LANES, past).astype(BF16)
    pad = jnp.zeros((LANES - n, LANES), BF16)
    kn = jnp.concatenate([kn_ref[:, :LANES], pad], axis=0)
    vn = jnp.concatenate([vn_ref[...], pad], axis=0)
    cum = cum_ref[...]
    causal = lane <= lax.broadcasted_iota(jnp.int32, (n, LANES), 0)
    outs = []
    for e in range(2):
        qm = jnp.where(first, q, zero) if e == 0 else jnp.where(first, zero, q)
        cq = jnp.sum(jnp.where(lane == 2 * p + e, cum, 0.0), axis=-1, keepdims=True)
        sc = _dot(qm, kc) + LOG2E * (cq + rrow_ref[e:e + 1, :])
        sn = jnp.where(causal, _dot_nt(qm, kn) + LOG2E * (cq - crow_ref[e:e + 1, :]), NEG_INF)
        m = jnp.maximum(jnp.max(sc, axis=-1, keepdims=True), jnp.max(sn, axis=-1, keepdims=True))
        pc = jnp.exp2(sc - m)
        pn = jnp.exp2(sn - m)
        l = jnp.sum(pc, axis=-1, keepdims=True) + jnp.sum(pn, axis=-1, keepdims=True)
        outs.append((_dot_nt(pc, vc) + _dot(pn, vn)) / l)
    o_ref[...] = jnp.where(first, outs[0], outs[1])


def _attn_sample(qaug, kaug, vab, cum, cache_kt, cache_vt, layer, rrow, crow):
    b, n, wa = vab.shape
    past = cache_kt.shape[-1]
    npair = wa // LANES
    new = lambda w: pl.BlockSpec((None, n, w), lambda bi, p: (bi, 0, p))
    old = lambda: pl.BlockSpec((None, None, 2, HEAD_DIM, past), lambda bi, p: (layer, bi, p, 0, 0))
    return pl.pallas_call(
        functools.partial(_attn_sample_kernel, n=n),
        grid=(b, npair),
        in_specs=[new(2 * LANES), old(), old(), new(2 * LANES), new(LANES),
                  pl.BlockSpec((None, n, LANES), lambda bi, p: (bi, 0, 0)),
                  pl.BlockSpec((None, None, 2, past), lambda bi, p: (bi, p, 0, 0)),
                  pl.BlockSpec((None, None, 2, LANES), lambda bi, p: (bi, p, 0, 0))],
        out_specs=new(LANES),
        out_shape=jax.ShapeDtypeStruct((b, n, wa), F32),
        compiler_params=_params("arbitrary", "arbitrary"),
        name="attn_sample",
    )(qaug, cache_kt, cache_vt, kaug, vab, cum, rrow, crow)


def _suffix_kernel(x_ref, o_ref):
    v = x_ref[...]
    n = v.shape[1]
    lane = lax.broadcasted_iota(jnp.int32, v.shape, 1)
    s = 1
    while s < n:
        v = v + jnp.where(lane + s < n, pltpu.roll(v, n - s, 1), 0.0)
        s *= 2
    o_ref[...] = jnp.where(lane + 1 < n, pltpu.roll(v, n - 1, 1), 0.0)


def _exclusive_suffix_sum(x, layer):
    _, h, b, p = x.shape
    return pl.pallas_call(
        _suffix_kernel,
        grid=(h,),
        in_specs=[pl.BlockSpec((None, None, b, p), lambda i: (layer, i, 0, 0))],
        out_specs=pl.BlockSpec((None, b, p), lambda i: (i, 0, 0)),
        out_shape=jax.ShapeDtypeStruct((h, b, p), F32),
        compiler_params=_params("arbitrary"),
        name="suffix_sum",
    )(x)


def _gdn_kernel(q_ref, k_ref, v_ref, bz_ref, elem_ref, s0_ref, gb_ref, esel_ref, hsum_ref,
                o_ref, sout_ref, s_scr, *, nb, nh):
    L = GDN_BLOCK
    t = pl.program_id(1)

    @pl.when(t == 0)
    def _():
        s_scr[...] = s0_ref[...]

    lane = lax.broadcasted_iota(jnp.int32, (L, LANES), 1)
    first = lane < HEAD_DIM
    ri = lax.broadcasted_iota(jnp.int32, (L, L), 0)
    ci = lax.broadcasted_iota(jnp.int32, (L, L), 1)
    incl = ci <= ri
    strict = ci < ri
    same_head = (ri < HEAD_DIM) == (ci < HEAD_DIM)
    lane2 = lax.broadcasted_iota(jnp.int32, (L, 2 * L), 1)
    first2 = jnp.bitwise_and(lane2, LANES - 1) < HEAD_DIM
    xor2 = jnp.bitwise_xor(lax.broadcasted_iota(jnp.int32, (L, 2 * L), 0), jnp.bitwise_and(lane2, L - 1))
    zero_ll = jnp.zeros((L, L), BF16)

    def halves(x, sel):
        return jnp.concatenate([jnp.where(sel, x, 0.0), jnp.where(sel, 0.0, x)], axis=0)

    def dot_heads(y, x):
        xb = x.astype(BF16)
        bd = jnp.concatenate([jnp.concatenate([xb[:, :L], zero_ll], axis=1),
                              jnp.concatenate([zero_ll, xb[:, L:]], axis=1)], axis=0)
        return jnp.dot(y.astype(BF16), bd, preferred_element_type=F32)

    npair = s_scr.shape[0]
    wbw = npair * LANES
    c = {}

    def solve_stages(blocks):
        chains = [(n, p) for n in blocks for p in range(npair)]
        ex = {}
        for n in blocks:
            elem = elem_ref[n * L:(n + 1) * L, :]
            gsum = _seg_cumsum(elem, L)
            mixed = jnp.where((lane >= nh) & (lane < 2 * nh), gsum, elem)
            ex[n] = _dot_select_exact(mixed, esel_ref[...])
        yield
        for n, p in chains:
            rows = slice(n * L, (n + 1) * L)
            cols = slice(p * LANES, (p + 1) * LANES)
            g = ex[n][:, cols]
            bt = ex[n][:, wbw + p * LANES: wbw + (p + 1) * LANES]
            kp = k_ref[rows, cols]
            qp = q_ref[rows, cols]
            g_sw = pltpu.roll(g, HEAD_DIM, 1)
            b_sw = pltpu.roll(bt, HEAD_DIM, 1)
            g_t = g.T
            a_parts, qk_parts = [], []
            for e in range(2):
                sel = first if e == 0 else jnp.logical_not(first)
                gcol = jnp.where(sel, g, g_sw)
                bcol = jnp.where(sel, bt, b_sw)
                grow = g_t[e * HEAD_DIM:e * HEAD_DIM + 1, :]
                dec = jnp.exp(jnp.where(incl, gcol - grow, NEG_INF))
                kk = _dot_nt(jnp.where(sel, kp, 0.0), kp)
                qk_parts.append(_dot_nt(jnp.where(sel, qp, 0.0), kp) * dec)
                a_parts.append(jnp.where(strict, bcol * kk * dec, 0.0))
            a_cat = jnp.concatenate(a_parts, axis=1)
            eg = jnp.exp(g)
            glast = g[L - 1:L, :]
            c[n, p] = dict(a=a_cat, qk=jnp.concatenate(qk_parts, axis=1), glast=glast, qg=qp * eg,
                           kdec=kp * jnp.exp(glast - g),
                           r=jnp.concatenate([v_ref[rows, cols] * bt, kp * bt * eg], axis=1),
                           tm1=-jnp.where(xor2 < 2, a_cat, 0.0))
        yield
        s_blk = 2
        while s_blk < L:
            pm = {}
            for key in chains:
                nmat = jnp.where((xor2 >= s_blk) & (xor2 < 2 * s_blk), c[key]["a"], 0.0)
                pm[key] = nmat + dot_heads(c[key]["tm1"], nmat)
            yield
            for key in chains:
                c[key]["tm1"] = c[key]["tm1"] - pm[key] - dot_heads(pm[key], c[key]["tm1"])
            yield
            s_blk *= 2
        for key in chains:
            r = c[key]["r"]
            c[key]["uw"] = r + _dot(c[key]["tm1"], halves(r, first2))
        yield

    def state_stages(blocks):
        pairs = range(npair)
        for n in blocks:
            rows = slice(n * L, (n + 1) * L)
            ws = [_dot(jnp.concatenate([c[n, p]["uw"][:, LANES:], c[n, p]["qg"]], axis=0), s_scr[p]) for p in pairs]
            yield
            u = [c[n, p]["uw"][:, :LANES] - ws[p][:L] for p in pairs]
            o = [ws[p][L:] + _dot(c[n, p]["qk"], halves(u[p], first)) for p in pairs]
            yield
            for p in pairs:
                s_scr[p] = (s_scr[p] * jnp.exp(c[n, p]["glast"])
                            + jnp.where(same_head, _dot(c[n, p]["kdec"].T, u[p]), 0.0))
            yield
            for p in pairs:
                cols = slice(p * LANES, (p + 1) * LANES)
                ms = _dot(o[p] * o[p], hsum_ref[...]) * (1.0 / HEAD_DIM)
                bz = bz_ref[rows, cols]
                o_ref[rows, cols] = o[p] * lax.rsqrt(ms + 1e-6) * gb_ref[...] * (bz * _sigmoid(bz))
            yield

    groups = [list(range(g0, min(g0 + GDN_GROUP, nb))) for g0 in range(0, nb, GDN_GROUP)]
    pending = iter(())
    for grp in groups:
        for _ in solve_stages(grp):
            next(pending, None)
        for _ in pending:
            pass
        pending = state_stages(grp)
    for _ in pending:
        pass

    @pl.when(t == pl.num_programs(1) - 1)
    def _():
        sout_ref[...] = s_scr[...]


def _gdn(qb, kb, vb, bz, elem, s0, lw, *, nb):
    b, t, wb = qb.shape
    tile = nb * GDN_BLOCK
    npair = wb // LANES
    tok = lambda w: pl.BlockSpec((None, tile, w), lambda i, j: (i, j, 0))
    st = pl.BlockSpec((None, npair, LANES, LANES), lambda i, j: (i, 0, 0, 0))
    consts = [lw["g_b_pair"], lw["esel"], lw["hsum128"]]
    return pl.pallas_call(
        functools.partial(_gdn_kernel, nb=nb, nh=lw["nh"]),
        grid=(b, t // tile),
        in_specs=[tok(wb), tok(wb), tok(wb), tok(wb), tok(LANES), st] + [_const_spec(c.shape) for c in consts],
        out_specs=[tok(wb), st],
        out_shape=[jax.ShapeDtypeStruct((b, t, wb), F32),
                   jax.ShapeDtypeStruct((b, npair, LANES, LANES), F32)],
        scratch_shapes=[pltpu.VMEM((npair, LANES, LANES), F32)],
        compiler_params=_params("arbitrary", "arbitrary"),
        name="gdn",
    )(qb, kb, vb, bz, elem, s0, *consts)


def _outffn_kernel(oa_ref, ob_ref, oc_ref, x_ref, ga_ref, wout_ref, gpm_ref, gpf_ref, wfi_ref, wfo_ref,
                   gpo_ref, y_ref, *, dff, nsub):
    r = x_ref.shape[0] // nsub
    rows = [slice(i * r, (i + 1) * r) for i in range(nsub)]
    cat = [jnp.concatenate([_rms(oa_ref[rs, :], ga_ref[...]), ob_ref[rs, :], oc_ref[rs, :]], axis=-1).astype(BF16)
           for rs in rows]
    m = [jnp.dot(c, wout_ref[...], preferred_element_type=F32) for c in cat]
    x1 = [x_ref[rs, :] + _rms(mi, gpm_ref[...]) for rs, mi in zip(rows, m)]
    h = [_rms(xi, gpf_ref[...]).astype(BF16) for xi in x1]
    gu = [jnp.dot(hi, wfi_ref[...], preferred_element_type=F32) for hi in h]
    a = [(g[:, :dff] * _sigmoid(g[:, :dff]) * g[:, dff:]).astype(BF16) for g in gu]
    f = [jnp.dot(ai, wfo_ref[...], preferred_element_type=F32) for ai in a]
    for rs, xi, fi in zip(rows, x1, f):
        y_ref[rs, :] = xi + _rms(fi, gpo_ref[...])


def _outffn(oa, ob, oc, x, lw, *, tm):
    n, d = x.shape
    dff = lw["w_ffn_out"].shape[0]
    tok = lambda w: pl.BlockSpec((tm, w), lambda i: (i, 0))
    consts = [lw["g_a_out"], lw["w_out"], lw["g_post_mix"], lw["g_pre_ffn"], lw["w_ffn_in"],
              lw["w_ffn_out"], lw["g_post_ffn"]]
    return pl.pallas_call(
        functools.partial(_outffn_kernel, dff=dff, nsub=2 if tm % (4 * SUBLANES) == 0 else 1),
        grid=(n // tm,),
        in_specs=[tok(oa.shape[1]), tok(ob.shape[1]), tok(oc.shape[1]), tok(d)]
                 + [_const_spec(c.shape) for c in consts],
        out_specs=tok(d),
        out_shape=jax.ShapeDtypeStruct((n, d), F32),
        compiler_params=_params("arbitrary"),
        name="outffn",
    )(oa, ob, oc, x, *consts)


def _block_ones(width):
    idx = jnp.arange(width) // HEAD_DIM
    return (idx[:, None] == idx[None, :]).astype(BF16)


def _layer_weights(l, prm, cms):
    w_in = prm["w_in"][l]
    nh = prm["b_f"].shape[1]
    wa = nh * HEAD_DIM
    wb = prm["a_log"].shape[1] * HEAD_DIM
    wc = prm["g_cv"].shape[1]
    ng = prm["w_s"].shape[1]
    assert prm["a_log"].shape[1] == nh and wa % LANES == 0 and wc % LANES == 0 and AUG * nh <= LANES
    sizes = (wa, wa, wa, nh, 3 * wb, nh, nh, wb, wc, wc)
    offs = [0]
    for sz in sizes:
        offs.append(offs[-1] + sz)
    w_in_t = w_in.T
    col = lambda i: w_in_t[offs[i]:offs[i + 1]]
    w_big = jnp.concatenate([col(4), col(8), col(9), col(0), col(1), col(2), col(7)], axis=0).astype(BF16)
    w_small = jnp.concatenate([col(3), col(5), col(6), jnp.zeros((LANES - 3 * nh, w_in.shape[0]), F32)],
                              axis=0).astype(BF16)
    zpad = jnp.zeros((LANES - 2 * nh,), F32)
    sp = jnp.zeros((SUBLANES, LANES), F32)
    sp = sp.at[0].set(jnp.concatenate([prm["b_f"][l], prm["dt_bias"][l], zpad]))
    sp = sp.at[1].set(jnp.concatenate([jnp.zeros((nh,), F32), prm["a_log"][l], zpad]))
    hl = jnp.arange(nh) * AUG
    sp = sp.at[2, (hl[:, None] + jnp.arange(3, 6)[None, :]).reshape(-1)].set(1.0)
    sp = sp.at[3, (hl[:, None] + jnp.arange(0, 3)[None, :]).reshape(-1)].set(1.0)
    pmat = jnp.zeros((3 * LANES, 2 * LANES), F32)
    for piece in range(3):
        pmat = pmat.at[piece * LANES + jnp.arange(nh), hl + piece].set(1.0)
        pmat = pmat.at[piece * LANES + jnp.arange(nh), LANES + hl + 3 + piece].set(1.0)
    row = lambda v: v.reshape(1, -1)
    ws_cat, bs_full = {}, {}
    for cm in cms:
        pos = jnp.arange(cm) // HEAD_DIM
        w = jnp.where(pos[None, :] <= pos[:, None], prm["w_s"][l][:, :cm, :cm], 0.0)
        pairs = [jnp.concatenate([w[2 * pp], w[2 * pp + 1]], axis=1) for pp in range(ng // 2)]
        kpad = max(LANES - 2 * cm, 0)
        ws_cat[cm] = jnp.pad(jnp.stack(pairs), ((0, 0), (0, 0), (0, kpad))).astype(BF16)
        bs_full[cm] = jnp.repeat(prm["b_s"][l][:, :cm].T, wc // ng, axis=1)
    src = jnp.arange(LANES)[:, None]
    dst = jnp.arange(wb)[None, :] // HEAD_DIM
    esel = jnp.concatenate([src == nh + dst, src == 2 * nh + dst], axis=1).astype(BF16)
    return dict(
        nh=nh, wa=wa, wb=wb, wc=wc,
        g_pre_mix=row(prm["g_pre_mix"][l]), w_big=w_big, w_small=w_small, sp=sp, conv_w=prm["conv_w"][l],
        g_cv=row(prm["g_cv"][l]), b_cv=row(prm["b_cv"][l]), ws_cat=ws_cat, bs_full=bs_full,
        g_c_out=row(prm["g_c_out"][l]), hsum=_block_ones(2 * LANES), hsum128=_block_ones(LANES), pmat=pmat.astype(BF16),
        g_b_pair=row(jnp.tile(prm["g_b_out"][l], LANES // HEAD_DIM)), esel=esel,
        g_a_out=row(prm["g_a_out"][l]), w_out=prm["w_out"][l].astype(BF16),
        g_post_mix=row(prm["g_post_mix"][l]), g_pre_ffn=row(prm["g_pre_ffn"][l]),
        w_ffn_in=prm["w_ffn_in"][l].astype(BF16), w_ffn_out=prm["w_ffn_out"][l].astype(BF16),
        g_post_ffn=row(prm["g_post_ffn"][l]))


def _pair_state(s):
    b, h, dk, dv = s.shape
    s = s.reshape(b, h // 2, 2, dk, dv)
    z = jnp.zeros_like(s[:, :, 0])
    top = jnp.concatenate([s[:, :, 0], z], axis=-1)
    bot = jnp.concatenate([z, s[:, :, 1]], axis=-1)
    return jnp.concatenate([top, bot], axis=-2)


def _unpair_state(sp):
    b, hp, _, _ = sp.shape
    s0 = sp[:, :, :HEAD_DIM, :HEAD_DIM]
    s1 = sp[:, :, HEAD_DIM:, HEAD_DIM:]
    return jnp.stack([s0, s1], axis=2).reshape(b, 2 * hp, HEAD_DIM, HEAD_DIM)


def _head_rows(cum, nh):
    b, t, _ = cum.shape
    return jnp.transpose(cum[:, :, :nh], (0, 2, 1)).reshape(b, nh // 2, 2, t)


def _pick(n, prefs):
    for c in prefs:
        if n % c == 0:
            return c
    return n


def _layer(x, lw, conv_prev, s0, cache, *, cm):
    b, t, d = x.shape
    nh, wb = lw["nh"], lw["wb"]
    kw1 = conv_prev.shape[1]
    conv_init = jnp.pad(conv_prev, ((0, 0), (SUBLANES - kw1, 0), (0, 0)))
    tm = _pick(t, (512, 256, 128, 64, 32, 16))
    pj = _inproj(x, lw, conv_init, tm=tm, cm=cm)

    if cache is None:
        tq = _pick(t, (ATTN_TQ, 256, 128))
        oa = _attn_prompt(pj["qaug"], pj["kaug"], pj["vab"], tq=tq, kc=min(ATTN_KC, tq), look=ATTN_LOOK)
    else:
        ck, cv, layer, clogf_t = cache
        _, _, bs, past = clogf_t.shape
        excl = _exclusive_suffix_sum(clogf_t, layer)
        rrow = jnp.transpose(excl, (1, 0, 2)).reshape(bs, nh // 2, 2, past)
        crow = jnp.pad(_head_rows(pj["cum"], nh), ((0, 0), (0, 0), (0, 0), (0, LANES - t)))
        oa = _attn_sample(pj["qaug"], pj["kaug"], pj["vab"], pj["cum"], ck, cv, layer, rrow, crow)

    tp = -(-t // GDN_BLOCK) * GDN_BLOCK
    padt = lambda a: a if tp == t else jnp.pad(a, ((0, 0), (0, tp - t), (0, 0)))
    nb = _pick(tp // GDN_BLOCK, (GDN_TILE_BLOCKS, 2, 1))
    ob, s_new = _gdn(padt(pj["qb"]), padt(pj["kb"]), padt(pj["vb"]), padt(pj["bz"]), padt(pj["elem"]),
                     _pair_state(s0), lw, nb=nb)
    ob = ob[:, :t]

    n = b * t
    y = _outffn(oa.reshape(n, -1), ob.reshape(n, -1), pj["oc"].reshape(n, -1), x.reshape(n, d), lw,
                tm=_pick(n, (512, 256, 128, 64, 32, 16)))
    state = (pj["ka"].reshape(b, t, nh, HEAD_DIM), pj["va"].reshape(b, t, nh, HEAD_DIM), pj["elem"][:, :, :nh],
             pj["ytail"][:, SUBLANES - kw1:, :], _unpair_state(s_new), pj["vn"])
    return y.reshape(b, t, d), state


def kernel(x_prompt, x_sample, cache_a_k, cache_a_v, cache_a_logf, state_b_conv, state_b_S, g_pre_mix, w_in, b_f, conv_w, a_log, dt_bias, g_b_out, g_a_out, g_cv, b_cv, w_s, b_s, g_c_out, w_out, g_post_mix, g_pre_ffn, w_ffn_in, w_ffn_out, g_post_ffn):
    prm = dict(g_pre_mix=g_pre_mix, w_in=w_in, b_f=b_f, conv_w=conv_w, a_log=a_log, dt_bias=dt_bias,
               g_b_out=g_b_out, g_a_out=g_a_out, g_cv=g_cv, b_cv=b_cv, w_s=w_s, b_s=b_s, g_c_out=g_c_out,
               w_out=w_out, g_post_mix=g_post_mix, g_pre_ffn=g_pre_ffn, w_ffn_in=w_ffn_in,
               w_ffn_out=w_ffn_out, g_post_ffn=g_post_ffn)
    depth = w_in.shape[0]
    bp, sp_len, _ = x_prompt.shape
    n_new = x_sample.shape[1]
    cm_p = w_s.shape[2]
    assert sp_len % cm_p == 0 and sp_len % GDN_BLOCK == 0 and n_new <= HEAD_DIM and n_new % SUBLANES == 0
    kw1 = conv_w.shape[1] - 1
    nhb = a_log.shape[1]
    yp, ys = x_prompt, x_sample
    outs_p, outs_s = [], []
    cache_kt = jnp.transpose(cache_a_k, (0, 1, 3, 4, 2))
    cache_vt = jnp.transpose(cache_a_v, (0, 1, 3, 4, 2))
    clogf_t = jnp.transpose(cache_a_logf, (0, 3, 1, 2))
    for l in range(depth):
        lw = _layer_weights(l, prm, (cm_p, n_new))
        conv0 = jnp.zeros((bp, kw1, conv_w.shape[2]), F32)
        s0 = jnp.zeros((bp, nhb, HEAD_DIM, HEAD_DIM), F32)
        yp, st_p = _layer(yp, lw, conv0, s0, None, cm=cm_p)
        ys, st_s = _layer(ys, lw, state_b_conv[l], state_b_S[l],
                          (cache_kt, cache_vt, l, clogf_t), cm=n_new)
        outs_p.append(st_p)
        outs_s.append(st_s)
    stk = lambda outs, i: jnp.stack([o[i] for o in outs], axis=0)
    return (yp, ys, stk(outs_p, 0), stk(outs_p, 1), stk(outs_p, 2), stk(outs_p, 3), stk(outs_p, 4),
            stk(outs_s, 0), stk(outs_s, 1), stk(outs_s, 2), stk(outs_s, 3), stk(outs_s, 4), stk(outs_s, 5))
```

```python
import functools

import jax
import jax.numpy as jnp
from jax import lax
from jax.experimental import pallas as pl
from jax.experimental.pallas import tpu as pltpu

F32 = jnp.float32
BF16 = jnp.bfloat16

LANES = 128
SUBLANES = 8
HEAD_DIM = 64
GDN_BLOCK = 128
GDN_GROUP = 2
GDN_TILE_BLOCKS = 4
ATTN_TQ = 1024
ATTN_KC = 256
ATTN_LOOK = 2
VMEM_LIMIT = 56 * 1024 * 1024
NEG_INF = float("-inf")
LOG2E = 1.4426950408889634
AUG = 16


def _dot(a, b):
    return jnp.dot(a.astype(BF16), b.astype(BF16), preferred_element_type=F32)


def _dot_nt(a, b):
    return lax.dot_general(a.astype(BF16), b.astype(BF16), (((1,), (1,)), ((), ())),
                           preferred_element_type=F32)


def _dot_select_exact(x, sel):
    hi = x.astype(BF16)
    r1 = x - hi.astype(F32)
    mid = r1.astype(BF16)
    lo = (r1 - mid.astype(F32)).astype(BF16)
    d = lambda p: jnp.dot(p, sel, preferred_element_type=F32)
    return (d(hi) + d(mid)) + d(lo)


def _rms(x, g, eps=1e-6):
    return x * lax.rsqrt(jnp.mean(x * x, axis=-1, keepdims=True) + eps) * g


def _softplus(x):
    return jnp.maximum(x, 0.0) + jnp.log1p(jnp.exp(-jnp.abs(x)))


def _sigmoid(x):
    return 1.0 / (1.0 + jnp.exp(-x))


def _seg_cumsum(v, seg):
    row = lax.broadcasted_iota(jnp.int32, v.shape, 0)
    pos = jnp.bitwise_and(row, seg - 1)
    s = 1
    while s < seg:
        v = v + jnp.where(pos >= s, pltpu.roll(v, s, 0), 0.0)
        s *= 2
    return v


def _const_spec(shape):
    nd = len(shape)
    return pl.BlockSpec(shape, lambda *_: (0,) * nd, pipeline_mode=pl.Buffered(1))


def _params(*sem):
    return pltpu.CompilerParams(dimension_semantics=sem, vmem_limit_bytes=VMEM_LIMIT)


def _inproj_kernel(x_ref, gpre_ref, wbig_ref, wsm_ref, sp_ref, convw_ref, convinit_ref, gcv_ref,
                   bcv_ref, ws_ref, bs_ref, gco_ref, hsum_ref, pmat_ref,
                   qaug_ref, ka_ref, va_ref, kaug_ref, vab_ref, elem_ref, cum_ref, qb_ref, kb_ref,
                   vb_ref, bz_ref, oc_ref, vn_ref, ytail_ref,
                   carry_conv, carry_cum, *, tm, cm, nh, wa, wb, wc, scale):
    @pl.when(pl.program_id(1) == 0)
    def _():
        carry_cum[...] = jnp.zeros_like(carry_cum)
        carry_conv[...] = convinit_ref[...]

    h = _rms(x_ref[...], gpre_ref[...]).astype(BF16)
    o_c = 3 * wb
    o_a = o_c + 2 * wc
    proj = lambda w: lax.dot_general(h, w, (((1,), (1,)), ((), ())), preferred_element_type=F32)
    y = proj(wbig_ref[:o_c, :])
    zc = proj(wbig_ref[o_c:o_a, :])
    zs = proj(wsm_ref[...])
    za = proj(wbig_ref[o_a:, :])

    ka = za[:, wa:2 * wa]
    va = za[:, 2 * wa:3 * wa]
    ka_ref[...] = ka
    va_ref[...] = va
    vab_ref[...] = va.astype(BF16)
    bz_ref[...] = za[:, 3 * wa:]

    lane = lax.broadcasted_iota(jnp.int32, (tm, LANES), 1)
    zb = zs + sp_ref[0:1, :]
    soft_tail = jnp.log1p(jnp.exp(-jnp.abs(zb)))
    logf = -(jnp.maximum(-zb, 0.0) + soft_tail)
    gl = -jnp.exp(sp_ref[1:2, :]) * (jnp.maximum(zb, 0.0) + soft_tail)
    beta = _sigmoid(zs)
    elem = jnp.where(lane < nh, logf, jnp.where(lane < 2 * nh, gl, jnp.where(lane < 3 * nh, beta, 0.0)))
    elem_ref[...] = elem

    cum = _seg_cumsum(elem, tm) + carry_cum[...]
    cum_ref[...] = cum
    carry_cum[...] = cum[tm - 1:tm, :]

    c2 = jnp.where(lane < nh, cum * LOG2E, 0.0)
    hi = c2.astype(BF16)
    r1 = c2 - hi.astype(F32)
    mid = r1.astype(BF16)
    lo = (r1 - mid.astype(F32)).astype(BF16)
    placed = jnp.dot(jnp.concatenate([hi, mid, lo], axis=1), pmat_ref[...], preferred_element_type=F32)
    augq = (placed[:, :LANES] + sp_ref[2:3, :]).astype(BF16)
    augk = (sp_ref[3:4, :] - placed[:, LANES:]).astype(BF16)
    qs = (za[:, :wa] * (scale * LOG2E)).astype(BF16)
    ks = ka.astype(BF16)
    qaug_ref[...] = jnp.concatenate(
        [a for j in range(0, wa, LANES) for a in (qs[:, j:j + LANES], augq)], axis=1)
    kaug_ref[...] = jnp.concatenate(
        [a for j in range(0, wa, LANES) for a in (ks[:, j:j + LANES], augk)], axis=1)

    prev = carry_conv[...]
    row8 = lax.broadcasted_iota(jnp.int32, prev.shape, 0)
    kw = convw_ref.shape[0]
    acc = y * convw_ref[kw - 1:kw, :]
    for k in range(1, kw):
        yk = pltpu.roll(y, k, 0)
        top = jnp.where(row8 < k, pltpu.roll(prev, k, 0), yk[0:SUBLANES])
        yk = jnp.concatenate([top, yk[SUBLANES:]], axis=0)
        acc = acc + yk * convw_ref[kw - 1 - k:kw - k, :]
    carry_conv[...] = y[tm - SUBLANES:tm]
    ytail_ref[...] = y[tm - SUBLANES:tm]
    yc = acc * _sigmoid(acc)
    qb = yc[:, :wb]
    kb = yc[:, wb:2 * wb]
    sq = jnp.concatenate([qb * qb, kb * kb], axis=-1).astype(BF16)
    hw = hsum_ref.shape[0]
    ss = jnp.concatenate([jnp.dot(sq[:, j:j + hw], hsum_ref[...], preferred_element_type=F32)
                          for j in range(0, 2 * wb, hw)], axis=-1)
    qb_ref[...] = qb * lax.rsqrt(ss[:, :wb] + 1e-6) * scale
    kb_ref[...] = kb * lax.rsqrt(ss[:, wb:] + 1e-6)
    vb_ref[...] = yc[:, 2 * wb:]

    u = jax.nn.gelu(zc[:, :wc])
    gv = jax.nn.gelu(zc[:, wc:])
    mu = jnp.mean(gv, axis=-1, keepdims=True)
    var = jnp.mean(jnp.square(gv - mu), axis=-1, keepdims=True)
    vn = (gv - mu) * lax.rsqrt(var + 1e-5) * gcv_ref[...] + bcv_ref[...]
    vn_ref[...] = vn
    first = lax.broadcasted_iota(jnp.int32, (cm, LANES), 1) < HEAD_DIM
    kpad = ws_ref.shape[2] - 2 * cm
    rows = []
    for c in range(tm // cm):
        vc = vn[c * cm:(c + 1) * cm]
        cols = []
        for pp in range(wc // LANES):
            vp = vc[:, pp * LANES:(pp + 1) * LANES]
            parts = [jnp.where(first, vp, 0.0), jnp.where(first, 0.0, vp)]
            if kpad:
                parts.append(jnp.zeros((kpad, LANES), F32))
            cols.append(_dot(ws_ref[pp], jnp.concatenate(parts, axis=0)))
        s = jnp.concatenate(cols, axis=-1) + bs_ref[...]
        rows.append(u[c * cm:(c + 1) * cm] * s)
    oc = rows[0] if len(rows) == 1 else jnp.concatenate(rows, axis=0)
    oc_ref[...] = _rms(oc, gco_ref[...])


def _inproj(x, lw, conv_init, *, tm, cm):
    b, t, d = x.shape
    nt = t // tm
    wa, wb, wc, nh = lw["wa"], lw["wb"], lw["wc"], lw["nh"]
    tok = lambda w: pl.BlockSpec((None, tm, w), lambda i, j: (i, j, 0))
    per_b = lambda r, w: pl.BlockSpec((None, r, w), lambda i, j: (i, 0, 0))
    outs = [("qaug", 2 * wa, BF16), ("ka", wa, F32), ("va", wa, F32), ("kaug", 2 * wa, BF16), ("vab", wa, BF16),
            ("elem", LANES, F32), ("cum", LANES, F32), ("qb", wb, F32), ("kb", wb, F32), ("vb", wb, F32),
            ("bz", wb, F32), ("oc", wc, F32), ("vn", wc, F32)]
    out_shape = [jax.ShapeDtypeStruct((b, t, w), dt) for _, w, dt in outs]
    out_specs = [tok(w) for _, w, _ in outs]
    out_shape.append(jax.ShapeDtypeStruct((b, SUBLANES, 3 * wb), F32))
    out_specs.append(per_b(SUBLANES, 3 * wb))
    consts = [lw["g_pre_mix"], lw["w_big"], lw["w_small"], lw["sp"], lw["conv_w"]]
    consts2 = [lw["g_cv"], lw["b_cv"], lw["ws_cat"][cm], lw["bs_full"][cm], lw["g_c_out"], lw["hsum"], lw["pmat"]]
    kern = functools.partial(_inproj_kernel, tm=tm, cm=cm, nh=nh, wa=wa, wb=wb, wc=wc,
                             scale=HEAD_DIM ** -0.5)
    res = pl.pallas_call(
        kern,
        grid=(b, nt),
        in_specs=[tok(d)] + [_const_spec(c.shape) for c in consts] + [per_b(SUBLANES, 3 * wb)]
                 + [_const_spec(c.shape) for c in consts2],
        out_specs=out_specs,
        out_shape=out_shape,
        scratch_shapes=[pltpu.VMEM((SUBLANES, 3 * wb), F32), pltpu.VMEM((1, LANES), F32)],
        compiler_params=_params("arbitrary", "arbitrary"),
        name="inproj",
    )(x, *consts, conv_init, *consts2)
    named = {n: r for (n, _, _), r in zip(outs, res[:-1])}
    named["ytail"] = res[-1]
    return named


def _attn_kernel(qt_ref, k_ref, vt_ref, o_ref, *, tq, kc, look):
    p = pl.program_id(1)
    i = pl.program_id(2)
    qt = qt_ref[...]
    rowi = lax.broadcasted_iota(jnp.int32, qt.shape, 0)
    zero = jnp.zeros_like(qt)
    qts = []
    for e in range(2):
        a0 = LANES + AUG * (2 * p + e)
        keep = ((rowi >= e * HEAD_DIM) & (rowi < (e + 1) * HEAD_DIM)) | ((rowi >= a0) & (rowi < a0 + AUG))
        qts.append(jnp.where(keep, qt, zero))
    qpos = i * tq + lax.broadcasted_iota(jnp.int32, (kc, tq), 1)
    kofs = lax.broadcasted_iota(jnp.int32, (kc, tq), 0)
    ones = jnp.ones((2 * SUBLANES, kc), BF16)
    units = [(c, e) for c in range(tq // kc) for e in range(2)]

    def scores(j, c, e, masked):
        k0 = pl.multiple_of(j * tq + c * kc, kc)
        lo = c * kc if masked else 0
        s = jnp.dot(k_ref[pl.ds(k0, kc), :], qts[e][:, lo:], preferred_element_type=F32)
        return jnp.concatenate([jnp.full((kc, lo), NEG_INF, F32), s], axis=1) if lo else s

    def fold(j, c, e, s, st, masked):
        m, l, acc = st
        k0 = pl.multiple_of(j * tq + c * kc, kc)
        lo = c * kc if masked else 0
        if masked:
            s = jnp.where(k0 + kofs <= qpos, s, NEG_INF)
        m_new = jnp.maximum(m, jnp.max(s, axis=0, keepdims=True))
        alpha = jnp.exp2(m - m_new)
        pt = jnp.exp2(s - m_new).astype(BF16)
        vt = jnp.concatenate([vt_ref[e * HEAD_DIM:(e + 1) * HEAD_DIM, pl.ds(k0, kc)], ones], axis=0)
        r = jnp.dot(vt, pt[:, lo:], preferred_element_type=F32)
        if lo:
            r = jnp.concatenate([jnp.zeros((r.shape[0], lo), F32), r], axis=1)
        return m_new, alpha * l + r[HEAD_DIM:HEAD_DIM + 1], alpha * acc + r[:HEAD_DIM]

    def run(blocks, state):
        state = list(state)
        todo = [(j, c, e, masked) for j, masked in blocks for c, e in units]
        pend = {}
        for k in range(min(look, len(todo))):
            pend[k] = scores(*todo[k])
        for k, (j, c, e, masked) in enumerate(todo):
            if k + look < len(todo):
                pend[k + look] = scores(*todo[k + look])
            state[e] = fold(j, c, e, pend.pop(k), state[e], masked)
        return tuple(state)

    st0 = (jnp.full((1, tq), NEG_INF, F32), jnp.zeros((1, tq), F32), jnp.zeros((HEAD_DIM, tq), F32))
    state = lax.fori_loop(0, i // 2, lambda t, s: run([(2 * t, False), (2 * t + 1, False)], s), (st0, st0))
    state = lax.cond(i % 2 == 1,
                     lambda s: run([(i - 1, False), (i, True)], s),
                     lambda s: run([(i, True)], s), state)
    ot = jnp.concatenate([acc / l for _, l, acc in state], axis=0)
    o_ref[...] = ot.T


def _attn_prompt(qaug, kaug, vab, *, tq, kc, look):
    b, s, wa = vab.shape
    npair = wa // LANES
    qt = jnp.transpose(qaug, (0, 2, 1))
    vt = jnp.transpose(vab, (0, 2, 1))
    kern = functools.partial(_attn_kernel, tq=tq, kc=kc, look=look)
    return pl.pallas_call(
        kern,
        grid=(b, npair, s // tq),
        in_specs=[pl.BlockSpec((None, 2 * LANES, tq), lambda bi, p, i: (bi, p, i)),
                  pl.BlockSpec((None, s, 2 * LANES), lambda bi, p, i: (bi, 0, p)),
                  pl.BlockSpec((None, LANES, s), lambda bi, p, i: (bi, p, 0))],
        out_specs=pl.BlockSpec((None, tq, LANES), lambda bi, p, i: (bi, i, p)),
        out_shape=jax.ShapeDtypeStruct((b, s, wa), F32),
        compiler_params=_params("arbitrary", "arbitrary", "arbitrary"),
        name="attn_prompt",
    )(qt, kaug, vt)


def _attn_sample_kernel(q_ref, kc_ref, vc_ref, kn_ref, vn_ref, cum_ref, rrow_ref, crow_ref, o_ref, *, n):
    p = pl.program_id(1)
    q = q_ref[:, :LANES]
    lane = lax.broadcasted_iota(jnp.int32, (n, LANES), 1)
    first = lane < HEAD_DIM
    zero = jnp.zeros_like(q)
    past = kc_ref.shape[-1]
    kc = kc_ref[...].reshape(LANES, past).astype(BF16)
    vc = vc_ref[...].reshape(LANES, past).astype(BF16)
    pad = jnp.zeros((LANES - n, LANES), BF16)
    kn = jnp.concatenate([kn_ref[:, :LANES], pad], axis=0)
    vn = jnp.concatenate([vn_ref[...], pad], axis=0)
    cum = cum_ref[...]
    qm = jnp.concatenate([jnp.where(first, q, zero), jnp.where(first, zero, q)], axis=0)
    cq = jnp.concatenate([jnp.sum(jnp.where(lane == 2 * p + e, cum, 0.0), axis=-1, keepdims=True)
                          for e in range(2)], axis=0)
    top = lax.broadcasted_iota(jnp.int32, (2 * n, 1), 0) < n
    rrow = jnp.where(top, rrow_ref[0:1, :], rrow_ref[1:2, :])
    crow = jnp.where(top, crow_ref[0:1, :], crow_ref[1:2, :])
    qrow = lax.broadcasted_iota(jnp.int32, (2 * n, LANES), 0)
    causal = lax.broadcasted_iota(jnp.int32, (2 * n, LANES), 1) <= jnp.where(qrow < n, qrow, qrow - n)
    sc = _dot(qm, kc) + LOG2E * (cq + rrow)
    sn = jnp.where(causal, _dot_nt(qm, kn) + LOG2E * (cq - crow), NEG_INF)
    m = jnp.maximum(jnp.max(sc, axis=-1, keepdims=True), jnp.max(sn, axis=-1, keepdims=True))
    pc = jnp.exp2(sc - m)
    pn = jnp.exp2(sn - m)
    l = jnp.sum(pc, axis=-1, keepdims=True) + jnp.sum(pn, axis=-1, keepdims=True)
    o = (_dot_nt(pc, vc) + _dot(pn, vn)) / l
    o_ref[...] = jnp.where(first, o[:n], o[n:])


def _attn_sample(qaug, kaug, vab, cum, cache_kt, cache_vt, layer, rrow, crow):
    b, n, wa = vab.shape
    past = cache_kt.shape[-1]
    npair = wa // LANES
    new = lambda w: pl.BlockSpec((None, n, w), lambda bi, p: (bi, 0, p))
    old = lambda: pl.BlockSpec((None, None, 2, HEAD_DIM, past), lambda bi, p: (layer, bi, p, 0, 0))
    return pl.pallas_call(
        functools.partial(_attn_sample_kernel, n=n),
        grid=(b, npair),
        in_specs=[new(2 * LANES), old(), old(), new(2 * LANES), new(LANES),
                  pl.BlockSpec((None, n, LANES), lambda bi, p: (bi, 0, 0)),
                  pl.BlockSpec((None, None, 2, past), lambda bi, p: (bi, p, 0, 0)),
                  pl.BlockSpec((None, None, 2, LANES), lambda bi, p: (bi, p, 0, 0))],
        out_specs=new(LANES),
        out_shape=jax.ShapeDtypeStruct((b, n, wa), F32),
        compiler_params=_params("arbitrary", "arbitrary"),
        name="attn_sample",
    )(qaug, cache_kt, cache_vt, kaug, vab, cum, rrow, crow)


def _suffix_kernel(x_ref, o_ref):
    v = x_ref[...]
    n = v.shape[1]
    lane = lax.broadcasted_iota(jnp.int32, v.shape, 1)
    s = 1
    while s < n:
        v = v + jnp.where(lane + s < n, pltpu.roll(v, n - s, 1), 0.0)
        s *= 2
    o_ref[...] = jnp.where(lane + 1 < n, pltpu.roll(v, n - 1, 1), 0.0)


def _exclusive_suffix_sum(x, layer):
    _, h, b, p = x.shape
    return pl.pallas_call(
        _suffix_kernel,
        grid=(h,),
        in_specs=[pl.BlockSpec((None, None, b, p), lambda i: (layer, i, 0, 0))],
        out_specs=pl.BlockSpec((None, b, p), lambda i: (i, 0, 0)),
        out_shape=jax.ShapeDtypeStruct((h, b, p), F32),
        compiler_params=_params("arbitrary"),
        name="suffix_sum",
    )(x)


def _gdn_kernel(q_ref, k_ref, v_ref, bz_ref, elem_ref, s0_ref, gb_ref, esel_ref, hsum_ref,
                o_ref, sout_ref, s_scr, *, nb, nh):
    L = GDN_BLOCK
    t = pl.program_id(1)

    @pl.when(t == 0)
    def _():
        s_scr[...] = s0_ref[...]

    lane = lax.broadcasted_iota(jnp.int32, (L, LANES), 1)
    first = lane < HEAD_DIM
    ri = lax.broadcasted_iota(jnp.int32, (L, L), 0)
    ci = lax.broadcasted_iota(jnp.int32, (L, L), 1)
    incl = ci <= ri
    strict = ci < ri
    same_head = (ri < HEAD_DIM) == (ci < HEAD_DIM)
    lane2 = lax.broadcasted_iota(jnp.int32, (L, 2 * L), 1)
    first2 = jnp.bitwise_and(lane2, LANES - 1) < HEAD_DIM
    xor2 = jnp.bitwise_xor(lax.broadcasted_iota(jnp.int32, (L, 2 * L), 0), jnp.bitwise_and(lane2, L - 1))
    zero_ll = jnp.zeros((L, L), BF16)

    def halves(x, sel):
        return jnp.concatenate([jnp.where(sel, x, 0.0), jnp.where(sel, 0.0, x)], axis=0)

    def dot_heads(y, x):
        xb = x.astype(BF16)
        bd = jnp.concatenate([jnp.concatenate([xb[:, :L], zero_ll], axis=1),
                              jnp.concatenate([zero_ll, xb[:, L:]], axis=1)], axis=0)
        return jnp.dot(y.astype(BF16), bd, preferred_element_type=F32)

    npair = s_scr.shape[0]
    wbw = npair * LANES
    c = {}

    def solve_stages(blocks):
        chains = [(n, p) for n in blocks for p in range(npair)]
        ex = {}
        for n in blocks:
            elem = elem_ref[n * L:(n + 1) * L, :]
            gsum = _seg_cumsum(elem, L)
            mixed = jnp.where((lane >= nh) & (lane < 2 * nh), gsum, elem)
            ex[n] = _dot_select_exact(mixed, esel_ref[...])
        yield
        for n, p in chains:
            rows = slice(n * L, (n + 1) * L)
            cols = slice(p * LANES, (p + 1) * LANES)
            g = ex[n][:, cols]
            bt = ex[n][:, wbw + p * LANES: wbw + (p + 1) * LANES]
            kp = k_ref[rows, cols]
            qp = q_ref[rows, cols]
            g_sw = pltpu.roll(g, HEAD_DIM, 1)
            b_sw = pltpu.roll(bt, HEAD_DIM, 1)
            g_t = g.T
            a_parts, qk_parts = [], []
            for e in range(2):
                sel = first if e == 0 else jnp.logical_not(first)
                gcol = jnp.where(sel, g, g_sw)
                bcol = jnp.where(sel, bt, b_sw)
                grow = g_t[e * HEAD_DIM:e * HEAD_DIM + 1, :]
                dec = jnp.exp(jnp.where(incl, gcol - grow, NEG_INF))
                kk = _dot_nt(jnp.where(sel, kp, 0.0), kp)
                qk_parts.append(_dot_nt(jnp.where(sel, qp, 0.0), kp) * dec)
                a_parts.append(jnp.where(strict, bcol * kk * dec, 0.0))
            a_cat = jnp.concatenate(a_parts, axis=1)
            eg = jnp.exp(g)
            glast = g[L - 1:L, :]
            c[n, p] = dict(a=a_cat, qk=jnp.concatenate(qk_parts, axis=1), glast=glast, qg=qp * eg,
                           kdec=kp * jnp.exp(glast - g),
                           r=jnp.concatenate([v_ref[rows, cols] * bt, kp * bt * eg], axis=1),
                           tm1=-jnp.where(xor2 < 2, a_cat, 0.0))
        yield
        s_blk = 2
        while s_blk < L:
            pm = {}
            for key in chains:
                nmat = jnp.where((xor2 >= s_blk) & (xor2 < 2 * s_blk), c[key]["a"], 0.0)
                pm[key] = nmat + dot_heads(c[key]["tm1"], nmat)
            yield
            for key in chains:
                c[key]["tm1"] = c[key]["tm1"] - pm[key] - dot_heads(pm[key], c[key]["tm1"])
            yield
            s_blk *= 2
        for key in chains:
            r = c[key]["r"]
            c[key]["uw"] = r + _dot(c[key]["tm1"], halves(r, first2))
        yield

    def state_stages(blocks):
        pairs = range(npair)
        for n in blocks:
            rows = slice(n * L, (n + 1) * L)
            ws = [_dot(jnp.concatenate([c[n, p]["uw"][:, LANES:], c[n, p]["qg"]], axis=0), s_scr[p]) for p in pairs]
            yield
            u = [c[n, p]["uw"][:, :LANES] - ws[p][:L] for p in pairs]
            o = [ws[p][L:] + _dot(c[n, p]["qk"], halves(u[p], first)) for p in pairs]
            yield
            for p in pairs:
                s_scr[p] = (s_scr[p] * jnp.exp(c[n, p]["glast"])
                            + jnp.where(same_head, _dot(c[n, p]["kdec"].T, u[p]), 0.0))
            yield
            for p in pairs:
                cols = slice(p * LANES, (p + 1) * LANES)
                ms = _dot(o[p] * o[p], hsum_ref[...]) * (1.0 / HEAD_DIM)
                bz = bz_ref[rows, cols]
                o_ref[rows, cols] = o[p] * lax.rsqrt(ms + 1e-6) * gb_ref[...] * (bz * _sigmoid(bz))
            yield

    groups = [list(range(g0, min(g0 + GDN_GROUP, nb))) for g0 in range(0, nb, GDN_GROUP)]
    pending = iter(())
    for grp in groups:
        for _ in solve_stages(grp):
            next(pending, None)
        for _ in pending:
            pass
        pending = state_stages(grp)
    for _ in pending:
        pass

    @pl.when(t == pl.num_programs(1) - 1)
    def _():
        sout_ref[...] = s_scr[...]


def _gdn(qb, kb, vb, bz, elem, s0, lw, *, nb):
    b, t, wb = qb.shape
    tile = nb * GDN_BLOCK
    npair = wb // LANES
    tok = lambda w: pl.BlockSpec((None, tile, w), lambda i, j: (i, j, 0))
    st = pl.BlockSpec((None, npair, LANES, LANES), lambda i, j: (i, 0, 0, 0))
    consts = [lw["g_b_pair"], lw["esel"], lw["hsum128"]]
    return pl.pallas_call(
        functools.partial(_gdn_kernel, nb=nb, nh=lw["nh"]),
        grid=(b, t // tile),
        in_specs=[tok(wb), tok(wb), tok(wb), tok(wb), tok(LANES), st] + [_const_spec(c.shape) for c in consts],
        out_specs=[tok(wb), st],
        out_shape=[jax.ShapeDtypeStruct((b, t, wb), F32),
                   jax.ShapeDtypeStruct((b, npair, LANES, LANES), F32)],
        scratch_shapes=[pltpu.VMEM((npair, LANES, LANES), F32)],
        compiler_params=_params("arbitrary", "arbitrary"),
        name="gdn",
    )(qb, kb, vb, bz, elem, s0, *consts)


def _outffn_kernel(oa_ref, ob_ref, oc_ref, x_ref, ga_ref, wout_ref, gpm_ref, gpf_ref, wfi_ref, wfo_ref,
                   gpo_ref, y_ref, *, dff, nsub):
    r = x_ref.shape[0] // nsub
    rows = [slice(i * r, (i + 1) * r) for i in range(nsub)]
    cat = [jnp.concatenate([_rms(oa_ref[rs, :], ga_ref[...]), ob_ref[rs, :], oc_ref[rs, :]], axis=-1).astype(BF16)
           for rs in rows]
    m = [jnp.dot(c, wout_ref[...], preferred_element_type=F32) for c in cat]
    x1 = [x_ref[rs, :] + _rms(mi, gpm_ref[...]) for rs, mi in zip(rows, m)]
    h = [_rms(xi, gpf_ref[...]).astype(BF16) for xi in x1]
    gu = [jnp.dot(hi, wfi_ref[...], preferred_element_type=F32) for hi in h]
    a = [(g[:, :dff] * _sigmoid(g[:, :dff]) * g[:, dff:]).astype(BF16) for g in gu]
    f = [jnp.dot(ai, wfo_ref[...], preferred_element_type=F32) for ai in a]
    for rs, xi, fi in zip(rows, x1, f):
        y_ref[rs, :] = xi + _rms(fi, gpo_ref[...])


def _outffn(oa, ob, oc, x, lw, *, tm):
    n, d = x.shape
    dff = lw["w_ffn_out"].shape[0]
    tok = lambda w: pl.BlockSpec((tm, w), lambda i: (i, 0))
    consts = [lw["g_a_out"], lw["w_out"], lw["g_post_mix"], lw["g_pre_ffn"], lw["w_ffn_in"],
              lw["w_ffn_out"], lw["g_post_ffn"]]
    return pl.pallas_call(
        functools.partial(_outffn_kernel, dff=dff, nsub=2 if tm % (4 * SUBLANES) == 0 else 1),
        grid=(n // tm,),
        in_specs=[tok(oa.shape[1]), tok(ob.shape[1]), tok(oc.shape[1]), tok(d)]
                 + [_const_spec(c.shape) for c in consts],
        out_specs=tok(d),
        out_shape=jax.ShapeDtypeStruct((n, d), F32),
        compiler_params=_params("arbitrary"),
        name="outffn",
    )(oa, ob, oc, x, *consts)


def _block_ones(width):
    idx = jnp.arange(width) // HEAD_DIM
    return (idx[:, None] == idx[None, :]).astype(BF16)


def _layer_weights(l, prm, cms):
    w_in = prm["w_in"][l]
    nh = prm["b_f"].shape[1]
    wa = nh * HEAD_DIM
    wb = prm["a_log"].shape[1] * HEAD_DIM
    wc = prm["g_cv"].shape[1]
    ng = prm["w_s"].shape[1]
    assert prm["a_log"].shape[1] == nh and wa % LANES == 0 and wc % LANES == 0 and AUG * nh <= LANES
    sizes = (wa, wa, wa, nh, 3 * wb, nh, nh, wb, wc, wc)
    offs = [0]
    for sz in sizes:
        offs.append(offs[-1] + sz)
    w_in_t = w_in.T
    col = lambda i: w_in_t[offs[i]:offs[i + 1]]
    w_big = jnp.concatenate([col(4), col(8), col(9), col(0), col(1), col(2), col(7)], axis=0).astype(BF16)
    w_small = jnp.concatenate([col(3), col(5), col(6), jnp.zeros((LANES - 3 * nh, w_in.shape[0]), F32)],
                              axis=0).astype(BF16)
    zpad = jnp.zeros((LANES - 2 * nh,), F32)
    sp = jnp.zeros((SUBLANES, LANES), F32)
    sp = sp.at[0].set(jnp.concatenate([prm["b_f"][l], prm["dt_bias"][l], zpad]))
    sp = sp.at[1].set(jnp.concatenate([jnp.zeros((nh,), F32), prm["a_log"][l], zpad]))
    hl = jnp.arange(nh) * AUG
    sp = sp.at[2, (hl[:, None] + jnp.arange(3, 6)[None, :]).reshape(-1)].set(1.0)
    sp = sp.at[3, (hl[:, None] + jnp.arange(0, 3)[None, :]).reshape(-1)].set(1.0)
    pmat = jnp.zeros((3 * LANES, 2 * LANES), F32)
    for piece in range(3):
        pmat = pmat.at[piece * LANES + jnp.arange(nh), hl + piece].set(1.0)
        pmat = pmat.at[piece * LANES + jnp.arange(nh), LANES + hl + 3 + piece].set(1.0)
    row = lambda v: v.reshape(1, -1)
    ws_cat, bs_full = {}, {}
    for cm in cms:
        pos = jnp.arange(cm) // HEAD_DIM
        w = jnp.where(pos[None, :] <= pos[:, None], prm["w_s"][l][:, :cm, :cm], 0.0)
        pairs = [jnp.concatenate([w[2 * pp], w[2 * pp + 1]], axis=1) for pp in range(ng // 2)]
        kpad = max(LANES - 2 * cm, 0)
        ws_cat[cm] = jnp.pad(jnp.stack(pairs), ((0, 0), (0, 0), (0, kpad))).astype(BF16)
        bs_full[cm] = jnp.repeat(prm["b_s"][l][:, :cm].T, wc // ng, axis=1)
    src = jnp.arange(LANES)[:, None]
    dst = jnp.arange(wb)[None, :] // HEAD_DIM
    esel = jnp.concatenate([src == nh + dst, src == 2 * nh + dst], axis=1).astype(BF16)
    return dict(
        nh=nh, wa=wa, wb=wb, wc=wc,
        g_pre_mix=row(prm["g_pre_mix"][l]), w_big=w_big, w_small=w_small, sp=sp, conv_w=prm["conv_w"][l],
        g_cv=row(prm["g_cv"][l]), b_cv=row(prm["b_cv"][l]), ws_cat=ws_cat, bs_full=bs_full,
        g_c_out=row(prm["g_c_out"][l]), hsum=_block_ones(2 * LANES), hsum128=_block_ones(LANES), pmat=pmat.astype(BF16),
        g_b_pair=row(jnp.tile(prm["g_b_out"][l], LANES // HEAD_DIM)), esel=esel,
        g_a_out=row(prm["g_a_out"][l]), w_out=prm["w_out"][l].astype(BF16),
        g_post_mix=row(prm["g_post_mix"][l]), g_pre_ffn=row(prm["g_pre_ffn"][l]),
        w_ffn_in=prm["w_ffn_in"][l].astype(BF16), w_ffn_out=prm["w_ffn_out"][l].astype(BF16),
        g_post_ffn=row(prm["g_post_ffn"][l]))


def _pair_state(s):
    b, h, dk, dv = s.shape
    s = s.reshape(b, h // 2, 2, dk, dv)
    z = jnp.zeros_like(s[:, :, 0])
    top = jnp.concatenate([s[:, :, 0], z], axis=-1)
    bot = jnp.concatenate([z, s[:, :, 1]], axis=-1)
    return jnp.concatenate([top, bot], axis=-2)


def _unpair_state(sp):
    b, hp, _, _ = sp.shape
    s0 = sp[:, :, :HEAD_DIM, :HEAD_DIM]
    s1 = sp[:, :, HEAD_DIM:, HEAD_DIM:]
    return jnp.stack([s0, s1], axis=2).reshape(b, 2 * hp, HEAD_DIM, HEAD_DIM)


def _head_rows(cum, nh):
    b, t, _ = cum.shape
    return jnp.transpose(cum[:, :, :nh], (0, 2, 1)).reshape(b, nh // 2, 2, t)


def _pick(n, prefs):
    for c in prefs:
        if n % c == 0:
            return c
    return n


def _layer(x, lw, conv_prev, s0, cache, *, cm):
    b, t, d = x.shape
    nh, wb = lw["nh"], lw["wb"]
    kw1 = conv_prev.shape[1]
    conv_init = jnp.pad(conv_prev, ((0, 0), (SUBLANES - kw1, 0), (0, 0)))
    tm = _pick(t, (512, 256, 128, 64, 32, 16))
    pj = _inproj(x, lw, conv_init, tm=tm, cm=cm)

    if cache is None:
        tq = _pick(t, (ATTN_TQ, 256, 128))
        oa = _attn_prompt(pj["qaug"], pj["kaug"], pj["vab"], tq=tq, kc=min(ATTN_KC, tq), look=ATTN_LOOK)
    else:
        ck, cv, layer, clogf_t = cache
        _, _, bs, past = clogf_t.shape
        excl = _exclusive_suffix_sum(clogf_t, layer)
        rrow = jnp.transpose(excl, (1, 0, 2)).reshape(bs, nh // 2, 2, past)
        crow = jnp.pad(_head_rows(pj["cum"], nh), ((0, 0), (0, 0), (0, 0), (0, LANES - t)))
        oa = _attn_sample(pj["qaug"], pj["kaug"], pj["vab"], pj["cum"], ck, cv, layer, rrow, crow)

    tp = -(-t // GDN_BLOCK) * GDN_BLOCK
    padt = lambda a: a if tp == t else jnp.pad(a, ((0, 0), (0, tp - t), (0, 0)))
    nb = _pick(tp // GDN_BLOCK, (GDN_TILE_BLOCKS, 2, 1))
    ob, s_new = _gdn(padt(pj["qb"]), padt(pj["kb"]), padt(pj["vb"]), padt(pj["bz"]), padt(pj["elem"]),
                     _pair_state(s0), lw, nb=nb)
    ob = ob[:, :t]

    n = b * t
    y = _outffn(oa.reshape(n, -1), ob.reshape(n, -1), pj["oc"].reshape(n, -1), x.reshape(n, d), lw,
                tm=_pick(n, (512, 256, 128, 64, 32, 16)))
    state = (pj["ka"].reshape(b, t, nh, HEAD_DIM), pj["va"].reshape(b, t, nh, HEAD_DIM), pj["elem"][:, :, :nh],
             pj["ytail"][:, SUBLANES - kw1:, :], _unpair_state(s_new), pj["vn"])
    return y.reshape(b, t, d), state


def kernel(x_prompt, x_sample, cache_a_k, cache_a_v, cache_a_logf, state_b_conv, state_b_S, g_pre_mix, w_in, b_f, conv_w, a_log, dt_bias, g_b_out, g_a_out, g_cv, b_cv, w_s, b_s, g_c_out, w_out, g_post_mix, g_pre_ffn, w_ffn_in, w_ffn_out, g_post_ffn):
    prm = dict(g_pre_mix=g_pre_mix, w_in=w_in, b_f=b_f, conv_w=conv_w, a_log=a_log, dt_bias=dt_bias,
               g_b_out=g_b_out, g_a_out=g_a_out, g_cv=g_cv, b_cv=b_cv, w_s=w_s, b_s=b_s, g_c_out=g_c_out,
               w_out=w_out, g_post_mix=g_post_mix, g_pre_ffn=g_pre_ffn, w_ffn_in=w_ffn_in,
               w_ffn_out=w_ffn_out, g_post_ffn=g_post_ffn)
    depth = w_in.shape[0]
    bp, sp_len, _ = x_prompt.shape
    n_new = x_sample.shape[1]
    cm_p = w_s.shape[2]
    assert sp_len % cm_p == 0 and sp_len % GDN_BLOCK == 0 and n_new <= HEAD_DIM and n_new % SUBLANES == 0
    kw1 = conv_w.shape[1] - 1
    nhb = a_log.shape[1]
    yp, ys = x_prompt, x_sample
    outs_p, outs_s = [], []
    cache_kt = jnp.transpose(cache_a_k, (0, 1, 3, 4, 2))
    cache_vt = jnp.transpose(cache_a_v, (0, 1, 3, 4, 2))
    clogf_t = jnp.transpose(cache_a_logf, (0, 3, 1, 2))
    for l in range(depth):
        lw = _layer_weights(l, prm, (cm_p, n_new))
        conv0 = jnp.zeros((bp, kw1, conv_w.shape[2]), F32)
        s0 = jnp.zeros((bp, nhb, HEAD_DIM, HEAD_DIM), F32)
        yp, st_p = _layer(yp, lw, conv0, s0, None, cm=cm_p)
        ys, st_s = _layer(ys, lw, state_b_conv[l], state_b_S[l],
                          (cache_kt, cache_vt, l, clogf_t), cm=n_new)
        outs_p.append(st_p)
        outs_s.append(st_s)
    stk = lambda outs, i: jnp.stack([o[i] for o in outs], axis=0)
    return (yp, ys, stk(outs_p, 0), stk(outs_p, 1), stk(outs_p, 2), stk(outs_p, 3), stk(outs_p, 4),
            stk(outs_s, 0), stk(outs_s, 1), stk(outs_s, 2), stk(outs_s, 3), stk(outs_s, 4), stk(outs_s, 5))
```

```python
import functools

import jax
import jax.numpy as jnp
from jax import lax
from jax.experimental import pallas as pl
from jax.experimental.pallas import tpu as pltpu

F32 = jnp.float32
BF16 = jnp.bfloat16

LANES = 128
SUBLANES = 8
HEAD_DIM = 64
GDN_BLOCK = 128
GDN_GROUP = 2
GDN_TILE_BLOCKS = 4
ATTN_TQ = 1024
ATTN_KC = 256
ATTN_LOOK = 2
VMEM_LIMIT = 56 * 1024 * 1024
NEG_INF = float("-inf")
LOG2E = 1.4426950408889634
AUG = 16


def _dot(a, b):
    return jnp.dot(a.astype(BF16), b.astype(BF16), preferred_element_type=F32)


def _dot_nt(a, b):
    return lax.dot_general(a.astype(BF16), b.astype(BF16), (((1,), (1,)), ((), ())),
                           preferred_element_type=F32)


def _dot_select_exact(x, sel):
    hi = x.astype(BF16)
    r1 = x - hi.astype(F32)
    mid = r1.astype(BF16)
    lo = (r1 - mid.astype(F32)).astype(BF16)
    d = lambda p: jnp.dot(p, sel, preferred_element_type=F32)
    return (d(hi) + d(mid)) + d(lo)


def _rms(x, g, eps=1e-6):
    return x * lax.rsqrt(jnp.mean(x * x, axis=-1, keepdims=True) + eps) * g


def _softplus(x):
    return jnp.maximum(x, 0.0) + jnp.log1p(jnp.exp(-jnp.abs(x)))


def _sigmoid(x):
    return 1.0 / (1.0 + jnp.exp(-x))


def _seg_cumsum(v, seg):
    row = lax.broadcasted_iota(jnp.int32, v.shape, 0)
    pos = jnp.bitwise_and(row, seg - 1)
    s = 1
    while s < seg:
        v = v + jnp.where(pos >= s, pltpu.roll(v, s, 0), 0.0)
        s *= 2
    return v


def _const_spec(shape):
    nd = len(shape)
    return pl.BlockSpec(shape, lambda *_: (0,) * nd, pipeline_mode=pl.Buffered(1))


def _params(*sem):
    return pltpu.CompilerParams(dimension_semantics=sem, vmem_limit_bytes=VMEM_LIMIT)


def _inproj_kernel(x_ref, gpre_ref, wbig_ref, wsm_ref, sp_ref, convw_ref, convinit_ref, gcv_ref,
                   bcv_ref, ws_ref, bs_ref, gco_ref, hsum_ref, pmat_ref,
                   qaug_ref, ka_ref, va_ref, kaug_ref, vab_ref, elem_ref, cum_ref, qb_ref, kb_ref,
                   vb_ref, bz_ref, oc_ref, vn_ref, ytail_ref,
                   carry_conv, carry_cum, *, tm, cm, nh, wa, wb, wc, scale, kv_time_minor):
    @pl.when(pl.program_id(1) == 0)
    def _():
        carry_cum[...] = jnp.zeros_like(carry_cum)
        carry_conv[...] = convinit_ref[...]

    h = _rms(x_ref[...], gpre_ref[...]).astype(BF16)
    o_c = 3 * wb
    o_a = o_c + 2 * wc
    proj = lambda w: lax.dot_general(h, w, (((1,), (1,)), ((), ())), preferred_element_type=F32)
    y = proj(wbig_ref[:o_c, :])
    zc = proj(wbig_ref[o_c:o_a, :])
    zs = proj(wsm_ref[...])
    za = proj(wbig_ref[o_a:, :])

    ka = za[:, wa:2 * wa]
    va = za[:, 2 * wa:3 * wa]
    ka_ref[...] = ka.T if kv_time_minor else ka
    va_ref[...] = va.T if kv_time_minor else va
    vab_ref[...] = va.astype(BF16)
    bz_ref[...] = za[:, 3 * wa:]

    lane = lax.broadcasted_iota(jnp.int32, (tm, LANES), 1)
    zb = zs + sp_ref[0:1, :]
    soft_tail = jnp.log1p(jnp.exp(-jnp.abs(zb)))
    logf = -(jnp.maximum(-zb, 0.0) + soft_tail)
    gl = -jnp.exp(sp_ref[1:2, :]) * (jnp.maximum(zb, 0.0) + soft_tail)
    beta = _sigmoid(zs)
    elem = jnp.where(lane < nh, logf, jnp.where(lane < 2 * nh, gl, jnp.where(lane < 3 * nh, beta, 0.0)))
    elem_ref[...] = elem

    cum = _seg_cumsum(elem, tm) + carry_cum[...]
    cum_ref[...] = cum
    carry_cum[...] = cum[tm - 1:tm, :]

    c2 = jnp.where(lane < nh, cum * LOG2E, 0.0)
    hi = c2.astype(BF16)
    r1 = c2 - hi.astype(F32)
    mid = r1.astype(BF16)
    lo = (r1 - mid.astype(F32)).astype(BF16)
    placed = jnp.dot(jnp.concatenate([hi, mid, lo], axis=1), pmat_ref[...], preferred_element_type=F32)
    augq = (placed[:, :LANES] + sp_ref[2:3, :]).astype(BF16)
    augk = (sp_ref[3:4, :] - placed[:, LANES:]).astype(BF16)
    qs = (za[:, :wa] * (scale * LOG2E)).astype(BF16)
    ks = ka.astype(BF16)
    qaug_ref[...] = jnp.concatenate(
        [a for j in range(0, wa, LANES) for a in (qs[:, j:j + LANES], augq)], axis=1)
    kaug_ref[...] = jnp.concatenate(
        [a for j in range(0, wa, LANES) for a in (ks[:, j:j + LANES], augk)], axis=1)

    prev = carry_conv[...]
    row8 = lax.broadcasted_iota(jnp.int32, prev.shape, 0)
    kw = convw_ref.shape[0]
    acc = y * convw_ref[kw - 1:kw, :]
    for k in range(1, kw):
        yk = pltpu.roll(y, k, 0)
        top = jnp.where(row8 < k, pltpu.roll(prev, k, 0), yk[0:SUBLANES])
        yk = jnp.concatenate([top, yk[SUBLANES:]], axis=0)
        acc = acc + yk * convw_ref[kw - 1 - k:kw - k, :]
    carry_conv[...] = y[tm - SUBLANES:tm]
    ytail_ref[...] = y[tm - SUBLANES:tm]
    yc = acc * _sigmoid(acc)
    qb = yc[:, :wb]
    kb = yc[:, wb:2 * wb]
    sq = jnp.concatenate([qb * qb, kb * kb], axis=-1).astype(BF16)
    hw = hsum_ref.shape[0]
    ss = jnp.concatenate([jnp.dot(sq[:, j:j + hw], hsum_ref[...], preferred_element_type=F32)
                          for j in range(0, 2 * wb, hw)], axis=-1)
    qb_ref[...] = qb * lax.rsqrt(ss[:, :wb] + 1e-6) * scale
    kb_ref[...] = kb * lax.rsqrt(ss[:, wb:] + 1e-6)
    vb_ref[...] = yc[:, 2 * wb:]

    u = jax.nn.gelu(zc[:, :wc])
    gv = jax.nn.gelu(zc[:, wc:])
    mu = jnp.mean(gv, axis=-1, keepdims=True)
    var = jnp.mean(jnp.square(gv - mu), axis=-1, keepdims=True)
    vn = (gv - mu) * lax.rsqrt(var + 1e-5) * gcv_ref[...] + bcv_ref[...]
    vn_ref[...] = vn
    first = lax.broadcasted_iota(jnp.int32, (cm, LANES), 1) < HEAD_DIM
    kpad = ws_ref.shape[2] - 2 * cm
    rows = []
    for c in range(tm // cm):
        vc = vn[c * cm:(c + 1) * cm]
        cols = []
        for pp in range(wc // LANES):
            vp = vc[:, pp * LANES:(pp + 1) * LANES]
            parts = [jnp.where(first, vp, 0.0), jnp.where(first, 0.0, vp)]
            if kpad:
                parts.append(jnp.zeros((kpad, LANES), F32))
            cols.append(_dot(ws_ref[pp], jnp.concatenate(parts, axis=0)))
        s = jnp.concatenate(cols, axis=-1) + bs_ref[...]
        rows.append(u[c * cm:(c + 1) * cm] * s)
    oc = rows[0] if len(rows) == 1 else jnp.concatenate(rows, axis=0)
    oc_ref[...] = _rms(oc, gco_ref[...])


def _inproj_kernel_inplace(*refs, n_in, **kw):
    return _inproj_kernel(*refs[:n_in], *refs[n_in + 2:], **kw)


def _inproj(x, lw, conv_init, *, tm, cm, layer, depth, kv_prev):
    b, t, d = x.shape
    nt = t // tm
    wa, wb, wc, nh = lw["wa"], lw["wb"], lw["wc"], lw["nh"]
    kv_time_minor = tm % LANES == 0
    tok = lambda w: pl.BlockSpec((None, tm, w), lambda i, j: (i, j, 0))
    per_b = lambda r, w: pl.BlockSpec((None, r, w), lambda i, j: (i, 0, 0))
    outs = [("qaug", 2 * wa, BF16), ("ka", wa, F32), ("va", wa, F32), ("kaug", 2 * wa, BF16), ("vab", wa, BF16),
            ("elem", LANES, F32), ("cum", LANES, F32), ("qb", wb, F32), ("kb", wb, F32), ("vb", wb, F32),
            ("bz", wb, F32), ("oc", wc, F32), ("vn", wc, F32)]
    out_shape = [jax.ShapeDtypeStruct((b, t, w), dt) for _, w, dt in outs]
    out_specs = [tok(w) for _, w, _ in outs]
    if kv_time_minor:
        for k in (1, 2):
            out_shape[k] = jax.ShapeDtypeStruct((depth, b, wa, t), F32)
            out_specs[k] = pl.BlockSpec((None, None, wa, tm), lambda i, j: (layer, i, 0, j))
    out_shape.append(jax.ShapeDtypeStruct((b, SUBLANES, 3 * wb), F32))
    out_specs.append(per_b(SUBLANES, 3 * wb))
    consts = [lw["g_pre_mix"], lw["w_big"], lw["w_small"], lw["sp"], lw["conv_w"]]
    consts2 = [lw["g_cv"], lw["b_cv"], lw["ws_cat"][cm], lw["bs_full"][cm], lw["g_c_out"], lw["hsum"], lw["pmat"]]
    kw = dict(tm=tm, cm=cm, nh=nh, wa=wa, wb=wb, wc=wc, scale=HEAD_DIM ** -0.5, kv_time_minor=kv_time_minor)
    in_specs = ([tok(d)] + [_const_spec(c.shape) for c in consts] + [per_b(SUBLANES, 3 * wb)]
                + [_const_spec(c.shape) for c in consts2])
    args = [x, *consts, conv_init, *consts2]
    inplace = kv_time_minor and kv_prev is not None
    if inplace:
        kern = functools.partial(_inproj_kernel_inplace, n_in=len(args), **kw)
        aliases = {len(args): 1, len(args) + 1: 2}
        in_specs = in_specs + [pl.BlockSpec(memory_space=pl.ANY)] * 2
        args = args + list(kv_prev)
    else:
        kern = functools.partial(_inproj_kernel, **kw)
        aliases = {}
    res = pl.pallas_call(
        kern,
        grid=(b, nt),
        in_specs=in_specs,
        out_specs=out_specs,
        out_shape=out_shape,
        input_output_aliases=aliases,
        scratch_shapes=[pltpu.VMEM((SUBLANES, 3 * wb), F32), pltpu.VMEM((1, LANES), F32)],
        compiler_params=_params("arbitrary", "arbitrary"),
        name="inproj",
    )(*args)
    named = {n: r for (n, _, _), r in zip(outs, res[:-1])}
    named["ytail"] = res[-1]
    named["kv_time_minor"] = kv_time_minor
    return named


def _attn_kernel(qt_ref, k_ref, vt_ref, o_ref, *, tq, kc, look):
    p = pl.program_id(1)
    i = pl.program_id(2)
    qt = qt_ref[...]
    rowi = lax.broadcasted_iota(jnp.int32, qt.shape, 0)
    zero = jnp.zeros_like(qt)
    qts = []
    for e in range(2):
        a0 = LANES + AUG * (2 * p + e)
        keep = ((rowi >= e * HEAD_DIM) & (rowi < (e + 1) * HEAD_DIM)) | ((rowi >= a0) & (rowi < a0 + AUG))
        qts.append(jnp.where(keep, qt, zero))
    qpos = i * tq + lax.broadcasted_iota(jnp.int32, (kc, tq), 1)
    kofs = lax.broadcasted_iota(jnp.int32, (kc, tq), 0)
    ones = jnp.ones((2 * SUBLANES, kc), BF16)
    units = [(c, e) for c in range(tq // kc) for e in range(2)]

    def scores(j, c, e, masked):
        k0 = pl.multiple_of(j * tq + c * kc, kc)
        lo = c * kc if masked else 0
        s = jnp.dot(k_ref[pl.ds(k0, kc), :], qts[e][:, lo:], preferred_element_type=F32)
        return jnp.concatenate([jnp.full((kc, lo), NEG_INF, F32), s], axis=1) if lo else s

    def fold(j, c, e, s, st, masked):
        m, l, acc = st
        k0 = pl.multiple_of(j * tq + c * kc, kc)
        lo = c * kc if masked else 0
        if masked:
            s = jnp.where(k0 + kofs <= qpos, s, NEG_INF)
        m_new = jnp.maximum(m, jnp.max(s, axis=0, keepdims=True))
        alpha = jnp.exp2(m - m_new)
        pt = jnp.exp2(s - m_new).astype(BF16)
        vt = jnp.concatenate([vt_ref[e * HEAD_DIM:(e + 1) * HEAD_DIM, pl.ds(k0, kc)], ones], axis=0)
        r = jnp.dot(vt, pt[:, lo:], preferred_element_type=F32)
        if lo:
            r = jnp.concatenate([jnp.zeros((r.shape[0], lo), F32), r], axis=1)
        return m_new, alpha * l + r[HEAD_DIM:HEAD_DIM + 1], alpha * acc + r[:HEAD_DIM]

    def run(blocks, state):
        state = list(state)
        todo = [(j, c, e, masked) for j, masked in blocks for c, e in units]
        pend = {}
        for k in range(min(look, len(todo))):
            pend[k] = scores(*todo[k])
        for k, (j, c, e, masked) in enumerate(todo):
            if k + look < len(todo):
                pend[k + look] = scores(*todo[k + look])
            state[e] = fold(j, c, e, pend.pop(k), state[e], masked)
        return tuple(state)

    st0 = (jnp.full((1, tq), NEG_INF, F32), jnp.zeros((1, tq), F32), jnp.zeros((HEAD_DIM, tq), F32))
    state = lax.fori_loop(0, i // 2, lambda t, s: run([(2 * t, False), (2 * t + 1, False)], s), (st0, st0))
    state = lax.cond(i % 2 == 1,
                     lambda s: run([(i - 1, False), (i, True)], s),
                     lambda s: run([(i, True)], s), state)
    ot = jnp.concatenate([acc / l for _, l, acc in state], axis=0)
    o_ref[...] = ot.T


def _attn_prompt(qaug, kaug, vab, *, tq, kc, look):
    b, s, wa = vab.shape
    npair = wa // LANES
    qt = jnp.transpose(qaug, (0, 2, 1))
    vt = jnp.transpose(vab, (0, 2, 1))
    kern = functools.partial(_attn_kernel, tq=tq, kc=kc, look=look)
    return pl.pallas_call(
        kern,
        grid=(b, npair, s // tq),
        in_specs=[pl.BlockSpec((None, 2 * LANES, tq), lambda bi, p, i: (bi, p, i)),
                  pl.BlockSpec((None, s, 2 * LANES), lambda bi, p, i: (bi, 0, p)),
                  pl.BlockSpec((None, LANES, s), lambda bi, p, i: (bi, p, 0))],
        out_specs=pl.BlockSpec((None, tq, LANES), lambda bi, p, i: (bi, i, p)),
        out_shape=jax.ShapeDtypeStruct((b, s, wa), F32),
        compiler_params=_params("arbitrary", "arbitrary", "arbitrary"),
        name="attn_prompt",
    )(qt, kaug, vt)


def _attn_sample_kernel(q_ref, kc_ref, vc_ref, kn_ref, vn_ref, cum_ref, rrow_ref, crow_ref, o_ref, *, n):
    p = pl.program_id(1)
    q = q_ref[:, :LANES]
    lane = lax.broadcasted_iota(jnp.int32, (n, LANES), 1)
    first = lane < HEAD_DIM
    zero = jnp.zeros_like(q)
    past = kc_ref.shape[-1]
    kc = kc_ref[...].reshape(LANES, past).astype(BF16)
    vc = vc_ref[...].reshape(LANES, past).astype(BF16)
    pad = jnp.zeros((LANES - n, LANES), BF16)
    kn = jnp.concatenate([kn_ref[:, :LANES], pad], axis=0)
    vn = jnp.concatenate([vn_ref[...], pad], axis=0)
    cum = cum_ref[...]
    qm = jnp.concatenate([jnp.where(first, q, zero), jnp.where(first, zero, q)], axis=0)
    cq = jnp.concatenate([jnp.sum(jnp.where(lane == 2 * p + e, cum, 0.0), axis=-1, keepdims=True)
                          for e in range(2)], axis=0)
    top = lax.broadcasted_iota(jnp.int32, (2 * n, 1), 0) < n
    rrow = jnp.where(top, rrow_ref[0:1, :], rrow_ref[1:2, :])
    crow = jnp.where(top, crow_ref[0:1, :], crow_ref[1:2, :])
    qrow = lax.broadcasted_iota(jnp.int32, (2 * n, LANES), 0)
    causal = lax.broadcasted_iota(jnp.int32, (2 * n, LANES), 1) <= jnp.where(qrow < n, qrow, qrow - n)
    sc = _dot(qm, kc) + LOG2E * (cq + rrow)
    sn = jnp.where(causal, _dot_nt(qm, kn) + LOG2E * (cq - crow), NEG_INF)
    m = jnp.maximum(jnp.max(sc, axis=-1, keepdims=True), jnp.max(sn, axis=-1, keepdims=True))
    pc = jnp.exp2(sc - m)
    pn = jnp.exp2(sn - m)
    l = jnp.sum(pc, axis=-1, keepdims=True) + jnp.sum(pn, axis=-1, keepdims=True)
    o = (_dot_nt(pc, vc) + _dot(pn, vn)) / l
    o_ref[...] = jnp.where(first, o[:n], o[n:])


def _attn_sample(qaug, kaug, vab, cum, cache_kt, cache_vt, layer, rrow, crow):
    b, n, wa = vab.shape
    past = cache_kt.shape[-1]
    npair = wa // LANES
    new = lambda w: pl.BlockSpec((None, n, w), lambda bi, p: (bi, 0, p))
    old = lambda: pl.BlockSpec((None, None, 2, HEAD_DIM, past), lambda bi, p: (layer, bi, p, 0, 0))
    return pl.pallas_call(
        functools.partial(_attn_sample_kernel, n=n),
        grid=(b, npair),
        in_specs=[new(2 * LANES), old(), old(), new(2 * LANES), new(LANES),
                  pl.BlockSpec((None, n, LANES), lambda bi, p: (bi, 0, 0)),
                  pl.BlockSpec((None, None, 2, past), lambda bi, p: (bi, p, 0, 0)),
                  pl.BlockSpec((None, None, 2, LANES), lambda bi, p: (bi, p, 0, 0))],
        out_specs=new(LANES),
        out_shape=jax.ShapeDtypeStruct((b, n, wa), F32),
        compiler_params=_params("arbitrary", "arbitrary"),
        name="attn_sample",
    )(qaug, cache_kt, cache_vt, kaug, vab, cum, rrow, crow)


def _suffix_kernel(x_ref, o_ref):
    v = x_ref[...]
    n = v.shape[1]
    lane = lax.broadcasted_iota(jnp.int32, v.shape, 1)
    s = 1
    while s < n:
        v = v + jnp.where(lane + s < n, pltpu.roll(v, n - s, 1), 0.0)
        s *= 2
    o_ref[...] = jnp.where(lane + 1 < n, pltpu.roll(v, n - 1, 1), 0.0)


def _exclusive_suffix_sum(x, layer):
    _, h, b, p = x.shape
    return pl.pallas_call(
        _suffix_kernel,
        grid=(h,),
        in_specs=[pl.BlockSpec((None, None, b, p), lambda i: (layer, i, 0, 0))],
        out_specs=pl.BlockSpec((None, b, p), lambda i: (i, 0, 0)),
        out_shape=jax.ShapeDtypeStruct((h, b, p), F32),
        compiler_params=_params("arbitrary"),
        name="suffix_sum",
    )(x)


def _gdn_kernel(q_ref, k_ref, v_ref, bz_ref, elem_ref, s0_ref, gb_ref, esel_ref, hsum_ref,
                o_ref, sout_ref, s_scr, *, nb, nh):
    L = GDN_BLOCK
    t = pl.program_id(1)

    @pl.when(t == 0)
    def _():
        s_scr[...] = s0_ref[...]

    lane = lax.broadcasted_iota(jnp.int32, (L, LANES), 1)
    first = lane < HEAD_DIM
    ri = lax.broadcasted_iota(jnp.int32, (L, L), 0)
    ci = lax.broadcasted_iota(jnp.int32, (L, L), 1)
    incl = ci <= ri
    strict = ci < ri
    same_head = (ri < HEAD_DIM) == (ci < HEAD_DIM)
    lane2 = lax.broadcasted_iota(jnp.int32, (L, 2 * L), 1)
    first2 = jnp.bitwise_and(lane2, LANES - 1) < HEAD_DIM
    xor2 = jnp.bitwise_xor(lax.broadcasted_iota(jnp.int32, (L, 2 * L), 0), jnp.bitwise_and(lane2, L - 1))
    zero_ll = jnp.zeros((L, L), BF16)

    def halves(x, sel):
        return jnp.concatenate([jnp.where(sel, x, 0.0), jnp.where(sel, 0.0, x)], axis=0)

    def dot_heads(y, x):
        xb = x.astype(BF16)
        bd = jnp.concatenate([jnp.concatenate([xb[:, :L], zero_ll], axis=1),
                              jnp.concatenate([zero_ll, xb[:, L:]], axis=1)], axis=0)
        return jnp.dot(y.astype(BF16), bd, preferred_element_type=F32)

    npair = s_scr.shape[0]
    wbw = npair * LANES
    c = {}

    def solve_stages(blocks):
        chains = [(n, p) for n in blocks for p in range(npair)]
        ex = {}
        for n in blocks:
            elem = elem_ref[n * L:(n + 1) * L, :]
            gsum = _seg_cumsum(elem, L)
            mixed = jnp.where((lane >= nh) & (lane < 2 * nh), gsum, elem)
            ex[n] = _dot_select_exact(mixed, esel_ref[...])
        yield
        for n, p in chains:
            rows = slice(n * L, (n + 1) * L)
            cols = slice(p * LANES, (p + 1) * LANES)
            g = ex[n][:, cols]
            bt = ex[n][:, wbw + p * LANES: wbw + (p + 1) * LANES]
            kp = k_ref[rows, cols]
            qp = q_ref[rows, cols]
            g_sw = pltpu.roll(g, HEAD_DIM, 1)
            b_sw = pltpu.roll(bt, HEAD_DIM, 1)
            g_t = g.T
            a_parts, qk_parts = [], []
            for e in range(2):
                sel = first if e == 0 else jnp.logical_not(first)
                gcol = jnp.where(sel, g, g_sw)
                bcol = jnp.where(sel, bt, b_sw)
                grow = g_t[e * HEAD_DIM:e * HEAD_DIM + 1, :]
                dec = jnp.exp(jnp.where(incl, gcol - grow, NEG_INF))
                kk = _dot_nt(jnp.where(sel, kp, 0.0), kp)
                qk_parts.append(_dot_nt(jnp.where(sel, qp, 0.0), kp) * dec)
                a_parts.append(jnp.where(strict, bcol * kk * dec, 0.0))
            a_cat = jnp.concatenate(a_parts, axis=1)
            eg = jnp.exp(g)
            glast = g[L - 1:L, :]
            c[n, p] = dict(a=a_cat, qk=jnp.concatenate(qk_parts, axis=1), glast=glast, qg=qp * eg,
                           kdec=kp * jnp.exp(glast - g),
                           r=jnp.concatenate([v_ref[rows, cols] * bt, kp * bt * eg], axis=1),
                           tm1=-jnp.where(xor2 < 2, a_cat, 0.0))
        yield
        s_blk = 2
        while s_blk < L:
            pm = {}
            for key in chains:
                nmat = jnp.where((xor2 >= s_blk) & (xor2 < 2 * s_blk), c[key]["a"], 0.0)
                pm[key] = nmat + dot_heads(c[key]["tm1"], nmat)
            yield
            for key in chains:
                c[key]["tm1"] = c[key]["tm1"] - pm[key] - dot_heads(pm[key], c[key]["tm1"])
            yield
            s_blk *= 2
        for key in chains:
            r = c[key]["r"]
            c[key]["uw"] = r + _dot(c[key]["tm1"], halves(r, first2))
        yield

    def state_stages(blocks):
        pairs = range(npair)
        for n in blocks:
            rows = slice(n * L, (n + 1) * L)
            ws = [_dot(jnp.concatenate([c[n, p]["uw"][:, LANES:], c[n, p]["qg"]], axis=0), s_scr[p]) for p in pairs]
            yield
            u = [c[n, p]["uw"][:, :LANES] - ws[p][:L] for p in pairs]
            o = [ws[p][L:] + _dot(c[n, p]["qk"], halves(u[p], first)) for p in pairs]
            yield
            for p in pairs:
                s_scr[p] = (s_scr[p] * jnp.exp(c[n, p]["glast"])
                            + jnp.where(same_head, _dot(c[n, p]["kdec"].T, u[p]), 0.0))
            yield
            for p in pairs:
                cols = slice(p * LANES, (p + 1) * LANES)
                ms = _dot(o[p] * o[p], hsum_ref[...]) * (1.0 / HEAD_DIM)
                bz = bz_ref[rows, cols]
                o_ref[rows, cols] = o[p] * lax.rsqrt(ms + 1e-6) * gb_ref[...] * (bz * _sigmoid(bz))
            yield

    groups = [list(range(g0, min(g0 + GDN_GROUP, nb))) for g0 in range(0, nb, GDN_GROUP)]
    pending = iter(())
    for grp in groups:
        for _ in solve_stages(grp):
            next(pending, None)
        for _ in pending:
            pass
        pending = state_stages(grp)
    for _ in pending:
        pass

    @pl.when(t == pl.num_programs(1) - 1)
    def _():
        sout_ref[...] = s_scr[...]


def _gdn(qb, kb, vb, bz, elem, s0, lw, *, nb):
    b, t, wb = qb.shape
    tile = nb * GDN_BLOCK
    npair = wb // LANES
    tok = lambda w: pl.BlockSpec((None, tile, w), lambda i, j: (i, j, 0))
    st = pl.BlockSpec((None, npair, LANES, LANES), lambda i, j: (i, 0, 0, 0))
    consts = [lw["g_b_pair"], lw["esel"], lw["hsum128"]]
    return pl.pallas_call(
        functools.partial(_gdn_kernel, nb=nb, nh=lw["nh"]),
        grid=(b, t // tile),
        in_specs=[tok(wb), tok(wb), tok(wb), tok(wb), tok(LANES), st] + [_const_spec(c.shape) for c in consts],
        out_specs=[tok(wb), st],
        out_shape=[jax.ShapeDtypeStruct((b, t, wb), F32),
                   jax.ShapeDtypeStruct((b, npair, LANES, LANES), F32)],
        scratch_shapes=[pltpu.VMEM((npair, LANES, LANES), F32)],
        compiler_params=_params("arbitrary", "arbitrary"),
        name="gdn",
    )(qb, kb, vb, bz, elem, s0, *consts)


def _outffn_kernel(oa_ref, ob_ref, oc_ref, x_ref, ga_ref, wout_ref, gpm_ref, gpf_ref, wfi_ref, wfo_ref,
                   gpo_ref, y_ref, *, dff, nsub):
    r = x_ref.shape[0] // nsub
    rows = [slice(i * r, (i + 1) * r) for i in range(nsub)]
    cat = [jnp.concatenate([_rms(oa_ref[rs, :], ga_ref[...]), ob_ref[rs, :], oc_ref[rs, :]], axis=-1).astype(BF16)
           for rs in rows]
    m = [jnp.dot(c, wout_ref[...], preferred_element_type=F32) for c in cat]
    x1 = [x_ref[rs, :] + _rms(mi, gpm_ref[...]) for rs, mi in zip(rows, m)]
    h = [_rms(xi, gpf_ref[...]).astype(BF16) for xi in x1]
    gu = [jnp.dot(hi, wfi_ref[...], preferred_element_type=F32) for hi in h]
    a = [(g[:, :dff] * _sigmoid(g[:, :dff]) * g[:, dff:]).astype(BF16) for g in gu]
    f = [jnp.dot(ai, wfo_ref[...], preferred_element_type=F32) for ai in a]
    for rs, xi, fi in zip(rows, x1, f):
        y_ref[rs, :] = xi + _rms(fi, gpo_ref[...])


def _outffn(oa, ob, oc, x, lw, *, tm):
    n, d = x.shape
    dff = lw["w_ffn_out"].shape[0]
    tok = lambda w: pl.BlockSpec((tm, w), lambda i: (i, 0))
    consts = [lw["g_a_out"], lw["w_out"], lw["g_post_mix"], lw["g_pre_ffn"], lw["w_ffn_in"],
              lw["w_ffn_out"], lw["g_post_ffn"]]
    return pl.pallas_call(
        functools.partial(_outffn_kernel, dff=dff, nsub=2 if tm % (4 * SUBLANES) == 0 else 1),
        grid=(n // tm,),
        in_specs=[tok(oa.shape[1]), tok(ob.shape[1]), tok(oc.shape[1]), tok(d)]
                 + [_const_spec(c.shape) for c in consts],
        out_specs=tok(d),
        out_shape=jax.ShapeDtypeStruct((n, d), F32),
        compiler_params=_params("arbitrary"),
        name="outffn",
    )(oa, ob, oc, x, *consts)


def _block_ones(width):
    idx = jnp.arange(width) // HEAD_DIM
    return (idx[:, None] == idx[None, :]).astype(BF16)


def _layer_weights(l, prm, cms):
    w_in = prm["w_in"][l]
    nh = prm["b_f"].shape[1]
    wa = nh * HEAD_DIM
    wb = prm["a_log"].shape[1] * HEAD_DIM
    wc = prm["g_cv"].shape[1]
    ng = prm["w_s"].shape[1]
    assert prm["a_log"].shape[1] == nh and wa % LANES == 0 and wc % LANES == 0 and AUG * nh <= LANES
    sizes = (wa, wa, wa, nh, 3 * wb, nh, nh, wb, wc, wc)
    offs = [0]
    for sz in sizes:
        offs.append(offs[-1] + sz)
    w_in_t = w_in.T
    col = lambda i: w_in_t[offs[i]:offs[i + 1]]
    w_big = jnp.concatenate([col(4), col(8), col(9), col(0), col(1), col(2), col(7)], axis=0).astype(BF16)
    w_small = jnp.concatenate([col(3), col(5), col(6), jnp.zeros((LANES - 3 * nh, w_in.shape[0]), F32)],
                              axis=0).astype(BF16)
    zpad = jnp.zeros((LANES - 2 * nh,), F32)
    sp = jnp.zeros((SUBLANES, LANES), F32)
    sp = sp.at[0].set(jnp.concatenate([prm["b_f"][l], prm["dt_bias"][l], zpad]))
    sp = sp.at[1].set(jnp.concatenate([jnp.zeros((nh,), F32), prm["a_log"][l], zpad]))
    hl = jnp.arange(nh) * AUG
    sp = sp.at[2, (hl[:, None] + jnp.arange(3, 6)[None, :]).reshape(-1)].set(1.0)
    sp = sp.at[3, (hl[:, None] + jnp.arange(0, 3)[None, :]).reshape(-1)].set(1.0)
    pmat = jnp.zeros((3 * LANES, 2 * LANES), F32)
    for piece in range(3):
        pmat = pmat.at[piece * LANES + jnp.arange(nh), hl + piece].set(1.0)
        pmat = pmat.at[piece * LANES + jnp.arange(nh), LANES + hl + 3 + piece].set(1.0)
    row = lambda v: v.reshape(1, -1)
    ws_cat, bs_full = {}, {}
    for cm in cms:
        pos = jnp.arange(cm) // HEAD_DIM
        w = jnp.where(pos[None, :] <= pos[:, None], prm["w_s"][l][:, :cm, :cm], 0.0)
        pairs = [jnp.concatenate([w[2 * pp], w[2 * pp + 1]], axis=1) for pp in range(ng // 2)]
        kpad = max(LANES - 2 * cm, 0)
        ws_cat[cm] = jnp.pad(jnp.stack(pairs), ((0, 0), (0, 0), (0, kpad))).astype(BF16)
        bs_full[cm] = jnp.repeat(prm["b_s"][l][:, :cm].T, wc // ng, axis=1)
    src = jnp.arange(LANES)[:, None]
    dst = jnp.arange(wb)[None, :] // HEAD_DIM
    esel = jnp.concatenate([src == nh + dst, src == 2 * nh + dst], axis=1).astype(BF16)
    return dict(
        nh=nh, wa=wa, wb=wb, wc=wc,
        g_pre_mix=row(prm["g_pre_mix"][l]), w_big=w_big, w_small=w_small, sp=sp, conv_w=prm["conv_w"][l],
        g_cv=row(prm["g_cv"][l]), b_cv=row(prm["b_cv"][l]), ws_cat=ws_cat, bs_full=bs_full,
        g_c_out=row(prm["g_c_out"][l]), hsum=_block_ones(2 * LANES), hsum128=_block_ones(LANES), pmat=pmat.astype(BF16),
        g_b_pair=row(jnp.tile(prm["g_b_out"][l], LANES // HEAD_DIM)), esel=esel,
        g_a_out=row(prm["g_a_out"][l]), w_out=prm["w_out"][l].astype(BF16),
        g_post_mix=row(prm["g_post_mix"][l]), g_pre_ffn=row(prm["g_pre_ffn"][l]),
        w_ffn_in=prm["w_ffn_in"][l].astype(BF16), w_ffn_out=prm["w_ffn_out"][l].astype(BF16),
        g_post_ffn=row(prm["g_post_ffn"][l]))


def _pair_state(s):
    b, h, dk, dv = s.shape
    s = s.reshape(b, h // 2, 2, dk, dv)
    z = jnp.zeros_like(s[:, :, 0])
    top = jnp.concatenate([s[:, :, 0], z], axis=-1)
    bot = jnp.concatenate([z, s[:, :, 1]], axis=-1)
    return jnp.concatenate([top, bot], axis=-2)


def _unpair_state(sp):
    b, hp, _, _ = sp.shape
    s0 = sp[:, :, :HEAD_DIM, :HEAD_DIM]
    s1 = sp[:, :, HEAD_DIM:, HEAD_DIM:]
    return jnp.stack([s0, s1], axis=2).reshape(b, 2 * hp, HEAD_DIM, HEAD_DIM)


def _head_rows(cum, nh):
    b, t, _ = cum.shape
    return jnp.transpose(cum[:, :, :nh], (0, 2, 1)).reshape(b, nh // 2, 2, t)


def _pick(n, prefs):
    for c in prefs:
        if n % c == 0:
            return c
    return n


def _layer(x, lw, conv_prev, s0, cache, *, cm, layer, depth, kv_prev):
    b, t, d = x.shape
    nh, wb = lw["nh"], lw["wb"]
    kw1 = conv_prev.shape[1]
    conv_init = jnp.pad(conv_prev, ((0, 0), (SUBLANES - kw1, 0), (0, 0)))
    tm = _pick(t, (512, 256, 128, 64, 32, 16))
    pj = _inproj(x, lw, conv_init, tm=tm, cm=cm, layer=layer, depth=depth, kv_prev=kv_prev)

    if cache is None:
        tq = _pick(t, (ATTN_TQ, 256, 128))
        oa = _attn_prompt(pj["qaug"], pj["kaug"], pj["vab"], tq=tq, kc=min(ATTN_KC, tq), look=ATTN_LOOK)
    else:
        ck, cv, clogf_t = cache
        _, _, bs, past = clogf_t.shape
        excl = _exclusive_suffix_sum(clogf_t, layer)
        rrow = jnp.transpose(excl, (1, 0, 2)).reshape(bs, nh // 2, 2, past)
        crow = jnp.pad(_head_rows(pj["cum"], nh), ((0, 0), (0, 0), (0, 0), (0, LANES - t)))
        oa = _attn_sample(pj["qaug"], pj["kaug"], pj["vab"], pj["cum"], ck, cv, layer, rrow, crow)

    tp = -(-t // GDN_BLOCK) * GDN_BLOCK
    padt = lambda a: a if tp == t else jnp.pad(a, ((0, 0), (0, tp - t), (0, 0)))
    nb = _pick(tp // GDN_BLOCK, (GDN_TILE_BLOCKS, 2, 1))
    ob, s_new = _gdn(padt(pj["qb"]), padt(pj["kb"]), padt(pj["vb"]), padt(pj["bz"]), padt(pj["elem"]),
                     _pair_state(s0), lw, nb=nb)
    ob = ob[:, :t]

    n = b * t
    y = _outffn(oa.reshape(n, -1), ob.reshape(n, -1), pj["oc"].reshape(n, -1), x.reshape(n, d), lw,
                tm=_pick(n, (512, 256, 128, 64, 32, 16)))
    if pj["kv_time_minor"]:
        new_kv = (pj["ka"], pj["va"])
    else:
        new_kv = (pj["ka"].reshape(b, t, nh, HEAD_DIM), pj["va"].reshape(b, t, nh, HEAD_DIM))
    state = (new_kv[0], new_kv[1], pj["elem"][:, :, :nh],
             pj["ytail"][:, SUBLANES - kw1:, :], _unpair_state(s_new), pj["vn"])
    return y.reshape(b, t, d), state, pj["kv_time_minor"]


def kernel(x_prompt, x_sample, cache_a_k, cache_a_v, cache_a_logf, state_b_conv, state_b_S, g_pre_mix, w_in, b_f, conv_w, a_log, dt_bias, g_b_out, g_a_out, g_cv, b_cv, w_s, b_s, g_c_out, w_out, g_post_mix, g_pre_ffn, w_ffn_in, w_ffn_out, g_post_ffn):
    prm = dict(g_pre_mix=g_pre_mix, w_in=w_in, b_f=b_f, conv_w=conv_w, a_log=a_log, dt_bias=dt_bias,
               g_b_out=g_b_out, g_a_out=g_a_out, g_cv=g_cv, b_cv=b_cv, w_s=w_s, b_s=b_s, g_c_out=g_c_out,
               w_out=w_out, g_post_mix=g_post_mix, g_pre_ffn=g_pre_ffn, w_ffn_in=w_ffn_in,
               w_ffn_out=w_ffn_out, g_post_ffn=g_post_ffn)
    depth = w_in.shape[0]
    bp, sp_len, _ = x_prompt.shape
    n_new = x_sample.shape[1]
    cm_p = w_s.shape[2]
    assert sp_len % cm_p == 0 and sp_len % GDN_BLOCK == 0 and n_new <= HEAD_DIM and n_new % SUBLANES == 0
    kw1 = conv_w.shape[1] - 1
    nhb = a_log.shape[1]
    yp, ys = x_prompt, x_sample
    outs_p, outs_s = [], []
    cache_kt = jnp.transpose(cache_a_k, (0, 1, 3, 4, 2))
    cache_vt = jnp.transpose(cache_a_v, (0, 1, 3, 4, 2))
    clogf_t = jnp.transpose(cache_a_logf, (0, 3, 1, 2))
    for l in range(depth):
        lw = _layer_weights(l, prm, (cm_p, n_new))
        conv0 = jnp.zeros((bp, kw1, conv_w.shape[2]), F32)
        s0 = jnp.zeros((bp, nhb, HEAD_DIM, HEAD_DIM), F32)
        kv_p = (outs_p[-1][0], outs_p[-1][1]) if outs_p else None
        kv_s = (outs_s[-1][0], outs_s[-1][1]) if outs_s else None
        yp, st_p, shared_p = _layer(yp, lw, conv0, s0, None, cm=cm_p, layer=l, depth=depth, kv_prev=kv_p)
        ys, st_s, shared_s = _layer(ys, lw, state_b_conv[l], state_b_S[l], (cache_kt, cache_vt, clogf_t),
                                    cm=n_new, layer=l, depth=depth, kv_prev=kv_s)
        outs_p.append(st_p)
        outs_s.append(st_s)
    stk = lambda outs, i: jnp.stack([o[i] for o in outs], axis=0)

    def new_cache(outs, i, shared):
        if not shared:
            return stk(outs, i)
        buf = outs[-1][i]
        dp, b, _, t = buf.shape
        return jnp.transpose(buf.reshape(dp, b, -1, HEAD_DIM, t), (0, 1, 4, 2, 3))

    return (yp, ys, new_cache(outs_p, 0, shared_p), new_cache(outs_p, 1, shared_p),
            stk(outs_p, 2), stk(outs_p, 3), stk(outs_p, 4),
            new_cache(outs_s, 0, shared_s), new_cache(outs_s, 1, shared_s),
            stk(outs_s, 2), stk(outs_s, 3), stk(outs_s, 4), stk(outs_s, 5))
```

```python
import functools

import jax
import jax.numpy as jnp
from jax import lax
from jax.experimental import pallas as pl
from jax.experimental.pallas import tpu as pltpu

F32 = jnp.float32
BF16 = jnp.bfloat16

LANES = 128
SUBLANES = 8
HEAD_DIM = 64
GDN_BLOCK = 128
GDN_GROUP = 2
GDN_TILE_BLOCKS = 4
ATTN_TQ = 1024
ATTN_KC = 256
ATTN_LOOK = 2
VMEM_LIMIT = 56 * 1024 * 1024
NEG_INF = float("-inf")
LOG2E = 1.4426950408889634
AUG = 16


def _dot(a, b):
    return jnp.dot(a.astype(BF16), b.astype(BF16), preferred_element_type=F32)


def _dot_nt(a, b):
    return lax.dot_general(a.astype(BF16), b.astype(BF16), (((1,), (1,)), ((), ())),
                           preferred_element_type=F32)


def _dot_select_exact(x, sel):
    hi = x.astype(BF16)
    r1 = x - hi.astype(F32)
    mid = r1.astype(BF16)
    lo = (r1 - mid.astype(F32)).astype(BF16)
    d = lambda p: jnp.dot(p, sel, preferred_element_type=F32)
    return (d(hi) + d(mid)) + d(lo)


def _rms(x, g, eps=1e-6):
    return x * lax.rsqrt(jnp.mean(x * x, axis=-1, keepdims=True) + eps) * g


def _softplus(x):
    return jnp.maximum(x, 0.0) + jnp.log1p(jnp.exp(-jnp.abs(x)))


def _sigmoid(x):
    return 1.0 / (1.0 + jnp.exp(-x))


def _seg_cumsum(v, seg):
    row = lax.broadcasted_iota(jnp.int32, v.shape, 0)
    pos = jnp.bitwise_and(row, seg - 1)
    s = 1
    while s < seg:
        v = v + jnp.where(pos >= s, pltpu.roll(v, s, 0), 0.0)
        s *= 2
    return v


def _const_spec(shape):
    nd = len(shape)
    return pl.BlockSpec(shape, lambda *_: (0,) * nd, pipeline_mode=pl.Buffered(1))


def _params(*sem):
    return pltpu.CompilerParams(dimension_semantics=sem, vmem_limit_bytes=VMEM_LIMIT)


def _inproj_kernel(x_ref, gpre_ref, wbig_ref, wsm_ref, sp_ref, convw_ref, convinit_ref, gcv_ref,
                   bcv_ref, ws_ref, bs_ref, gco_ref, hsum_ref, pmat_ref,
                   qaug_ref, ka_ref, va_ref, kaug_ref, vab_ref, elem_ref, cum_ref, qb_ref, kb_ref,
                   vb_ref, bz_ref, oc_ref, vn_ref, ytail_ref,
                   carry_conv, carry_cum, *, tm, cm, nh, wa, wb, wc, scale, kv_time_minor):
    @pl.when(pl.program_id(1) == 0)
    def _():
        carry_cum[...] = jnp.zeros_like(carry_cum)
        carry_conv[...] = convinit_ref[...]

    h = _rms(x_ref[...], gpre_ref[...]).astype(BF16)
    o_c = 3 * wb
    o_a = o_c + 2 * wc
    proj = lambda w: lax.dot_general(h, w, (((1,), (1,)), ((), ())), preferred_element_type=F32)
    y = proj(wbig_ref[:o_c, :])
    zc = proj(wbig_ref[o_c:o_a, :])
    zs = proj(wsm_ref[...])
    za = proj(wbig_ref[o_a:, :])

    ka = za[:, wa:2 * wa]
    va = za[:, 2 * wa:3 * wa]
    ka_ref[...] = ka.T if kv_time_minor else ka
    va_ref[...] = va.T if kv_time_minor else va
    vab_ref[...] = va.astype(BF16)
    bz_ref[...] = za[:, 3 * wa:]

    lane = lax.broadcasted_iota(jnp.int32, (tm, LANES), 1)
    zb = zs + sp_ref[0:1, :]
    soft_tail = jnp.log1p(jnp.exp(-jnp.abs(zb)))
    logf = -(jnp.maximum(-zb, 0.0) + soft_tail)
    gl = -jnp.exp(sp_ref[1:2, :]) * (jnp.maximum(zb, 0.0) + soft_tail)
    beta = _sigmoid(zs)
    elem = jnp.where(lane < nh, logf, jnp.where(lane < 2 * nh, gl, jnp.where(lane < 3 * nh, beta, 0.0)))
    elem_ref[...] = elem

    cum = _seg_cumsum(elem, tm) + carry_cum[...]
    cum_ref[...] = cum
    carry_cum[...] = cum[tm - 1:tm, :]

    c2 = jnp.where(lane < nh, cum * LOG2E, 0.0)
    hi = c2.astype(BF16)
    r1 = c2 - hi.astype(F32)
    mid = r1.astype(BF16)
    lo = (r1 - mid.astype(F32)).astype(BF16)
    placed = jnp.dot(jnp.concatenate([hi, mid, lo], axis=1), pmat_ref[...], preferred_element_type=F32)
    augq = (placed[:, :LANES] + sp_ref[2:3, :]).astype(BF16)
    augk = (sp_ref[3:4, :] - placed[:, LANES:]).astype(BF16)
    qs = (za[:, :wa] * (scale * LOG2E)).astype(BF16)
    ks = ka.astype(BF16)
    qaug_ref[...] = jnp.concatenate(
        [a for j in range(0, wa, LANES) for a in (qs[:, j:j + LANES], augq)], axis=1)
    kaug_ref[...] = jnp.concatenate(
        [a for j in range(0, wa, LANES) for a in (ks[:, j:j + LANES], augk)], axis=1)

    prev = carry_conv[...]
    row8 = lax.broadcasted_iota(jnp.int32, prev.shape, 0)
    kw = convw_ref.shape[0]
    acc = y * convw_ref[kw - 1:kw, :]
    for k in range(1, kw):
        yk = pltpu.roll(y, k, 0)
        top = jnp.where(row8 < k, pltpu.roll(prev, k, 0), yk[0:SUBLANES])
        yk = jnp.concatenate([top, yk[SUBLANES:]], axis=0)
        acc = acc + yk * convw_ref[kw - 1 - k:kw - k, :]
    carry_conv[...] = y[tm - SUBLANES:tm]
    ytail_ref[...] = y[tm - SUBLANES:tm]
    yc = acc * _sigmoid(acc)
    qb = yc[:, :wb]
    kb = yc[:, wb:2 * wb]
    sq = jnp.concatenate([qb * qb, kb * kb], axis=-1).astype(BF16)
    hw = hsum_ref.shape[0]
    ss = jnp.concatenate([jnp.dot(sq[:, j:j + hw], hsum_ref[...], preferred_element_type=F32)
                          for j in range(0, 2 * wb, hw)], axis=-1)
    qb_ref[...] = qb * lax.rsqrt(ss[:, :wb] + 1e-6) * scale
    kb_ref[...] = kb * lax.rsqrt(ss[:, wb:] + 1e-6)
    vb_ref[...] = yc[:, 2 * wb:]

    u = jax.nn.gelu(zc[:, :wc])
    gv = jax.nn.gelu(zc[:, wc:])
    mu = jnp.mean(gv, axis=-1, keepdims=True)
    var = jnp.mean(jnp.square(gv - mu), axis=-1, keepdims=True)
    vn = (gv - mu) * lax.rsqrt(var + 1e-5) * gcv_ref[...] + bcv_ref[...]
    vn_ref[...] = vn
    first = lax.broadcasted_iota(jnp.int32, (cm, LANES), 1) < HEAD_DIM
    kpad = ws_ref.shape[2] - 2 * cm
    rows = []
    for c in range(tm // cm):
        vc = vn[c * cm:(c + 1) * cm]
        cols = []
        for pp in range(wc // LANES):
            vp = vc[:, pp * LANES:(pp + 1) * LANES]
            parts = [jnp.where(first, vp, 0.0), jnp.where(first, 0.0, vp)]
            if kpad:
                parts.append(jnp.zeros((kpad, LANES), F32))
            cols.append(_dot(ws_ref[pp], jnp.concatenate(parts, axis=0)))
        s = jnp.concatenate(cols, axis=-1) + bs_ref[...]
        rows.append(u[c * cm:(c + 1) * cm] * s)
    oc = rows[0] if len(rows) == 1 else jnp.concatenate(rows, axis=0)
    oc_ref[...] = _rms(oc, gco_ref[...])


def _inproj_kernel_inplace(*refs, n_in, **kw):
    return _inproj_kernel(*refs[:n_in], *refs[n_in + 2:], **kw)


def _inproj(x, lw, conv_init, *, tm, cm, layer, depth, kv_prev):
    b, t, d = x.shape
    nt = t // tm
    wa, wb, wc, nh = lw["wa"], lw["wb"], lw["wc"], lw["nh"]
    kv_time_minor = tm % LANES == 0
    tok = lambda w: pl.BlockSpec((None, tm, w), lambda i, j: (i, j, 0))
    per_b = lambda r, w: pl.BlockSpec((None, r, w), lambda i, j: (i, 0, 0))
    outs = [("qaug", 2 * wa, BF16), ("ka", wa, F32), ("va", wa, F32), ("kaug", 2 * wa, BF16), ("vab", wa, BF16),
            ("elem", LANES, F32), ("cum", LANES, F32), ("qb", wb, F32), ("kb", wb, F32), ("vb", wb, F32),
            ("bz", wb, F32), ("oc", wc, F32), ("vn", wc, F32)]
    out_shape = [jax.ShapeDtypeStruct((b, t, w), dt) for _, w, dt in outs]
    out_specs = [tok(w) for _, w, _ in outs]
    if kv_time_minor:
        for k in (1, 2):
            out_shape[k] = jax.ShapeDtypeStruct((depth, b, wa, t), F32)
            out_specs[k] = pl.BlockSpec((None, None, wa, tm), lambda i, j: (layer, i, 0, j))
    out_shape.append(jax.ShapeDtypeStruct((b, SUBLANES, 3 * wb), F32))
    out_specs.append(per_b(SUBLANES, 3 * wb))
    consts = [lw["g_pre_mix"], lw["w_big"], lw["w_small"], lw["sp"], lw["conv_w"]]
    consts2 = [lw["g_cv"], lw["b_cv"], lw["ws_cat"][cm], lw["bs_full"][cm], lw["g_c_out"], lw["hsum"], lw["pmat"]]
    kw = dict(tm=tm, cm=cm, nh=nh, wa=wa, wb=wb, wc=wc, scale=HEAD_DIM ** -0.5, kv_time_minor=kv_time_minor)
    in_specs = ([tok(d)] + [_const_spec(c.shape) for c in consts] + [per_b(SUBLANES, 3 * wb)]
                + [_const_spec(c.shape) for c in consts2])
    args = [x, *consts, conv_init, *consts2]
    inplace = kv_time_minor and kv_prev is not None
    if inplace:
        kern = functools.partial(_inproj_kernel_inplace, n_in=len(args), **kw)
        aliases = {len(args): 1, len(args) + 1: 2}
        in_specs = in_specs + [pl.BlockSpec(memory_space=pl.ANY)] * 2
        args = args + list(kv_prev)
    else:
        kern = functools.partial(_inproj_kernel, **kw)
        aliases = {}
    res = pl.pallas_call(
        kern,
        grid=(b, nt),
        in_specs=in_specs,
        out_specs=out_specs,
        out_shape=out_shape,
        input_output_aliases=aliases,
        scratch_shapes=[pltpu.VMEM((SUBLANES, 3 * wb), F32), pltpu.VMEM((1, LANES), F32)],
        compiler_params=_params("arbitrary", "arbitrary"),
        name="inproj",
    )(*args)
    named = {n: r for (n, _, _), r in zip(outs, res[:-1])}
    named["ytail"] = res[-1]
    named["kv_time_minor"] = kv_time_minor
    return named


def _attn_kernel(qt_ref, k_ref, vt_ref, o_ref, *, tq, kc, look):
    p = pl.program_id(1)
    i = pl.program_id(2)
    qt = qt_ref[...]
    rowi = lax.broadcasted_iota(jnp.int32, qt.shape, 0)
    zero = jnp.zeros_like(qt)
    qts = []
    for e in range(2):
        a0 = LANES + AUG * (2 * p + e)
        keep = ((rowi >= e * HEAD_DIM) & (rowi < (e + 1) * HEAD_DIM)) | ((rowi >= a0) & (rowi < a0 + AUG))
        qts.append(jnp.where(keep, qt, zero))
    qpos = i * tq + lax.broadcasted_iota(jnp.int32, (kc, tq), 1)
    kofs = lax.broadcasted_iota(jnp.int32, (kc, tq), 0)
    ones = jnp.ones((2 * SUBLANES, kc), BF16)
    units = [(c, e) for c in range(tq // kc) for e in range(2)]

    def scores(j, c, e, masked):
        k0 = pl.multiple_of(j * tq + c * kc, kc)
        lo = c * kc if masked else 0
        s = jnp.dot(k_ref[pl.ds(k0, kc), :], qts[e][:, lo:], preferred_element_type=F32)
        return jnp.concatenate([jnp.full((kc, lo), NEG_INF, F32), s], axis=1) if lo else s

    def fold(j, c, e, s, st, masked):
        m, l, acc = st
        k0 = pl.multiple_of(j * tq + c * kc, kc)
        lo = c * kc if masked else 0
        if masked:
            s = jnp.where(k0 + kofs <= qpos, s, NEG_INF)
        m_new = jnp.maximum(m, jnp.max(s, axis=0, keepdims=True))
        alpha = jnp.exp2(m - m_new)
        pt = jnp.exp2(s - m_new).astype(BF16)
        vt = jnp.concatenate([vt_ref[e * HEAD_DIM:(e + 1) * HEAD_DIM, pl.ds(k0, kc)], ones], axis=0)
        r = jnp.dot(vt, pt[:, lo:], preferred_element_type=F32)
        if lo:
            r = jnp.concatenate([jnp.zeros((r.shape[0], lo), F32), r], axis=1)
        return m_new, alpha * l + r[HEAD_DIM:HEAD_DIM + 1], alpha * acc + r[:HEAD_DIM]

    def run(blocks, state):
        state = list(state)
        todo = [(j, c, e, masked) for j, masked in blocks for c, e in units]
        pend = {}
        for k in range(min(look, len(todo))):
            pend[k] = scores(*todo[k])
        for k, (j, c, e, masked) in enumerate(todo):
            if k + look < len(todo):
                pend[k + look] = scores(*todo[k + look])
            state[e] = fold(j, c, e, pend.pop(k), state[e], masked)
        return tuple(state)

    st0 = (jnp.full((1, tq), NEG_INF, F32), jnp.zeros((1, tq), F32), jnp.zeros((HEAD_DIM, tq), F32))
    state = lax.fori_loop(0, i // 2, lambda t, s: run([(2 * t, False), (2 * t + 1, False)], s), (st0, st0))
    state = lax.cond(i % 2 == 1,
                     lambda s: run([(i - 1, False), (i, True)], s),
                     lambda s: run([(i, True)], s), state)
    ot = jnp.concatenate([acc / l for _, l, acc in state], axis=0)
    o_ref[...] = ot.T


def _attn_prompt(qaug, kaug, vab, *, tq, kc, look):
    b, s, wa = vab.shape
    npair = wa // LANES
    qt = jnp.transpose(qaug, (0, 2, 1))
    vt = jnp.transpose(vab, (0, 2, 1))
    kern = functools.partial(_attn_kernel, tq=tq, kc=kc, look=look)
    return pl.pallas_call(
        kern,
        grid=(b, npair, s // tq),
        in_specs=[pl.BlockSpec((None, 2 * LANES, tq), lambda bi, p, i: (bi, p, i)),
                  pl.BlockSpec((None, s, 2 * LANES), lambda bi, p, i: (bi, 0, p)),
                  pl.BlockSpec((None, LANES, s), lambda bi, p, i: (bi, p, 0))],
        out_specs=pl.BlockSpec((None, tq, LANES), lambda bi, p, i: (bi, i, p)),
        out_shape=jax.ShapeDtypeStruct((b, s, wa), F32),
        compiler_params=_params("arbitrary", "arbitrary", "arbitrary"),
        name="attn_prompt",
    )(qt, kaug, vt)


def _attn_sample_kernel(q_ref, kc_ref, vc_ref, kn_ref, vn_ref, cum_ref, rrow_ref, crow_ref, o_ref, *, n):
    p = pl.program_id(1)
    q = q_ref[:, :LANES]
    lane = lax.broadcasted_iota(jnp.int32, (n, LANES), 1)
    first = lane < HEAD_DIM
    zero = jnp.zeros_like(q)
    past = kc_ref.shape[-1]
    kc = kc_ref[...].reshape(LANES, past).astype(BF16)
    vc = vc_ref[...].reshape(LANES, past).astype(BF16)
    pad = jnp.zeros((LANES - n, LANES), BF16)
    kn = jnp.concatenate([kn_ref[:, :LANES], pad], axis=0)
    vn = jnp.concatenate([vn_ref[...], pad], axis=0)
    cum = cum_ref[...]
    qm = jnp.concatenate([jnp.where(first, q, zero), jnp.where(first, zero, q)], axis=0)
    cq = jnp.concatenate([jnp.sum(jnp.where(lane == 2 * p + e, cum, 0.0), axis=-1, keepdims=True)
                          for e in range(2)], axis=0)
    top = lax.broadcasted_iota(jnp.int32, (2 * n, 1), 0) < n
    rrow = jnp.where(top, rrow_ref[0:1, :], rrow_ref[1:2, :])
    crow = jnp.where(top, crow_ref[0:1, :], crow_ref[1:2, :])
    qrow = lax.broadcasted_iota(jnp.int32, (2 * n, LANES), 0)
    causal = lax.broadcasted_iota(jnp.int32, (2 * n, LANES), 1) <= jnp.where(qrow < n, qrow, qrow - n)
    sc = _dot(qm, kc) + LOG2E * (cq + rrow)
    sn = jnp.where(causal, _dot_nt(qm, kn) + LOG2E * (cq - crow), NEG_INF)
    m = jnp.maximum(jnp.max(sc, axis=-1, keepdims=True), jnp.max(sn, axis=-1, keepdims=True))
    pc = jnp.exp2(sc - m)
    pn = jnp.exp2(sn - m)
    l = jnp.sum(pc, axis=-1, keepdims=True) + jnp.sum(pn, axis=-1, keepdims=True)
    o = (_dot_nt(pc, vc) + _dot(pn, vn)) / l
    o_ref[...] = jnp.where(first, o[:n], o[n:])


def _attn_sample(qaug, kaug, vab, cum, cache_kt, cache_vt, layer, rrow, crow):
    b, n, wa = vab.shape
    past = cache_kt.shape[-1]
    npair = wa // LANES
    new = lambda w: pl.BlockSpec((None, n, w), lambda bi, p: (bi, 0, p))
    old = lambda: pl.BlockSpec((None, None, 2, HEAD_DIM, past), lambda bi, p: (layer, bi, p, 0, 0))
    return pl.pallas_call(
        functools.partial(_attn_sample_kernel, n=n),
        grid=(b, npair),
        in_specs=[new(2 * LANES), old(), old(), new(2 * LANES), new(LANES),
                  pl.BlockSpec((None, n, LANES), lambda bi, p: (bi, 0, 0)),
                  pl.BlockSpec((None, None, 2, past), lambda bi, p: (bi, p, 0, 0)),
                  pl.BlockSpec((None, None, 2, LANES), lambda bi, p: (bi, p, 0, 0))],
        out_specs=new(LANES),
        out_shape=jax.ShapeDtypeStruct((b, n, wa), F32),
        compiler_params=_params("arbitrary", "arbitrary"),
        name="attn_sample",
    )(qaug, cache_kt, cache_vt, kaug, vab, cum, rrow, crow)


def _suffix_kernel(x_ref, o_ref):
    v = x_ref[...]
    n = v.shape[1]
    lane = lax.broadcasted_iota(jnp.int32, v.shape, 1)
    s = 1
    while s < n:
        v = v + jnp.where(lane + s < n, pltpu.roll(v, n - s, 1), 0.0)
        s *= 2
    o_ref[...] = jnp.where(lane + 1 < n, pltpu.roll(v, n - 1, 1), 0.0)


def _exclusive_suffix_sum(x, layer):
    _, h, b, p = x.shape
    return pl.pallas_call(
        _suffix_kernel,
        grid=(h,),
        in_specs=[pl.BlockSpec((None, None, b, p), lambda i: (layer, i, 0, 0))],
        out_specs=pl.BlockSpec((None, b, p), lambda i: (i, 0, 0)),
        out_shape=jax.ShapeDtypeStruct((h, b, p), F32),
        compiler_params=_params("arbitrary"),
        name="suffix_sum",
    )(x)


def _gdn_kernel(q_ref, k_ref, v_ref, bz_ref, elem_ref, s0_ref, gb_ref, esel_ref, hsum_ref,
                o_ref, sout_ref, s_scr, *, nb, nh, chained):
    L = GDN_BLOCK
    t = pl.program_id(1)

    if chained:
        @pl.when(t == 0)
        def _():
            s_scr[...] = s0_ref[...]

    lane = lax.broadcasted_iota(jnp.int32, (L, LANES), 1)
    first = lane < HEAD_DIM
    ri = lax.broadcasted_iota(jnp.int32, (L, L), 0)
    ci = lax.broadcasted_iota(jnp.int32, (L, L), 1)
    incl = ci <= ri
    strict = ci < ri
    same_head = (ri < HEAD_DIM) == (ci < HEAD_DIM)
    lane2 = lax.broadcasted_iota(jnp.int32, (L, 2 * L), 1)
    first2 = jnp.bitwise_and(lane2, LANES - 1) < HEAD_DIM
    xor2 = jnp.bitwise_xor(lax.broadcasted_iota(jnp.int32, (L, 2 * L), 0), jnp.bitwise_and(lane2, L - 1))
    zero_ll = jnp.zeros((L, L), BF16)

    def halves(x, sel):
        return jnp.concatenate([jnp.where(sel, x, 0.0), jnp.where(sel, 0.0, x)], axis=0)

    def dot_heads(y, x):
        xb = x.astype(BF16)
        bd = jnp.concatenate([jnp.concatenate([xb[:, :L], zero_ll], axis=1),
                              jnp.concatenate([zero_ll, xb[:, L:]], axis=1)], axis=0)
        return jnp.dot(y.astype(BF16), bd, preferred_element_type=F32)

    npair = s_scr.shape[0]
    wbw = npair * LANES
    c = {}

    def solve_stages(blocks):
        chains = [(n, p) for n in blocks for p in range(npair)]
        ex = {}
        for n in blocks:
            elem = elem_ref[n * L:(n + 1) * L, :]
            gsum = _seg_cumsum(elem, L)
            mixed = jnp.where((lane >= nh) & (lane < 2 * nh), gsum, elem)
            ex[n] = _dot_select_exact(mixed, esel_ref[...])
        yield
        for n, p in chains:
            rows = slice(n * L, (n + 1) * L)
            cols = slice(p * LANES, (p + 1) * LANES)
            g = ex[n][:, cols]
            bt = ex[n][:, wbw + p * LANES: wbw + (p + 1) * LANES]
            kp = k_ref[rows, cols]
            qp = q_ref[rows, cols]
            g_sw = pltpu.roll(g, HEAD_DIM, 1)
            b_sw = pltpu.roll(bt, HEAD_DIM, 1)
            g_t = g.T
            a_parts, qk_parts = [], []
            for e in range(2):
                sel = first if e == 0 else jnp.logical_not(first)
                gcol = jnp.where(sel, g, g_sw)
                bcol = jnp.where(sel, bt, b_sw)
                grow = g_t[e * HEAD_DIM:e * HEAD_DIM + 1, :]
                dec = jnp.exp(jnp.where(incl, gcol - grow, NEG_INF))
                kk = _dot_nt(jnp.where(sel, kp, 0.0), kp)
                qk_parts.append(_dot_nt(jnp.where(sel, qp, 0.0), kp) * dec)
                a_parts.append(jnp.where(strict, bcol * kk * dec, 0.0))
            a_cat = jnp.concatenate(a_parts, axis=1)
            eg = jnp.exp(g)
            glast = g[L - 1:L, :]
            c[n, p] = dict(a=a_cat, qk=jnp.concatenate(qk_parts, axis=1), glast=glast, qg=qp * eg,
                           kdec=kp * jnp.exp(glast - g),
                           r=jnp.concatenate([v_ref[rows, cols] * bt, kp * bt * eg], axis=1),
                           tm1=-jnp.where(xor2 < 2, a_cat, 0.0))
        yield
        s_blk = 2
        while s_blk < L:
            pm = {}
            for key in chains:
                nmat = jnp.where((xor2 >= s_blk) & (xor2 < 2 * s_blk), c[key]["a"], 0.0)
                pm[key] = nmat + dot_heads(c[key]["tm1"], nmat)
            yield
            for key in chains:
                c[key]["tm1"] = c[key]["tm1"] - pm[key] - dot_heads(pm[key], c[key]["tm1"])
            yield
            s_blk *= 2
        for key in chains:
            r = c[key]["r"]
            c[key]["uw"] = r + _dot(c[key]["tm1"], halves(r, first2))
        yield

    def state_stages(blocks):
        pairs = range(npair)
        for n in blocks:
            rows = slice(n * L, (n + 1) * L)
            s_in = [s_scr[p] if chained else s0_ref[n, p] for p in pairs]
            ws = [_dot(jnp.concatenate([c[n, p]["uw"][:, LANES:], c[n, p]["qg"]], axis=0), s_in[p]) for p in pairs]
            yield
            u = [c[n, p]["uw"][:, :LANES] - ws[p][:L] for p in pairs]
            o = [ws[p][L:] + _dot(c[n, p]["qk"], halves(u[p], first)) for p in pairs]
            yield
            for p in pairs:
                s_new = (s_in[p] * jnp.exp(c[n, p]["glast"])
                         + jnp.where(same_head, _dot(c[n, p]["kdec"].T, u[p]), 0.0))
                if chained:
                    s_scr[p] = s_new
                else:
                    sout_ref[n, p] = s_new
            yield
            for p in pairs:
                cols = slice(p * LANES, (p + 1) * LANES)
                ms = _dot(o[p] * o[p], hsum_ref[...]) * (1.0 / HEAD_DIM)
                bz = bz_ref[rows, cols]
                o_ref[rows, cols] = o[p] * lax.rsqrt(ms + 1e-6) * gb_ref[...] * (bz * _sigmoid(bz))
            yield

    groups = [list(range(g0, min(g0 + GDN_GROUP, nb))) for g0 in range(0, nb, GDN_GROUP)]
    pending = iter(())
    for grp in groups:
        for _ in solve_stages(grp):
            next(pending, None)
        for _ in pending:
            pass
        pending = state_stages(grp)
    for _ in pending:
        pass

    if chained:
        @pl.when(t == pl.num_programs(1) - 1)
        def _():
            sout_ref[...] = s_scr[...]


def _gdn(qb, kb, vb, bz, elem, s0, lw, *, nb):
    b, t, wb = qb.shape
    npair = wb // LANES
    chained = t > GDN_BLOCK
    if not chained:
        qb, kb, vb, bz, elem = (a.reshape(1, b * t, a.shape[-1]) for a in (qb, kb, vb, bz, elem))
        nb = _pick(b, (GDN_TILE_BLOCKS, 2, 1))
    rows, total = qb.shape[:2]
    tile = nb * GDN_BLOCK
    tok = lambda w: pl.BlockSpec((None, tile, w), lambda i, j: (i, j, 0))
    if chained:
        st = pl.BlockSpec((None, npair, LANES, LANES), lambda i, j: (i, 0, 0, 0))
    else:
        st = pl.BlockSpec((nb, npair, LANES, LANES), lambda i, j: (j, 0, 0, 0))
    consts = [lw["g_b_pair"], lw["esel"], lw["hsum128"]]
    ob, s_new = pl.pallas_call(
        functools.partial(_gdn_kernel, nb=nb, nh=lw["nh"], chained=chained),
        grid=(rows, total // tile),
        in_specs=[tok(wb), tok(wb), tok(wb), tok(wb), tok(LANES), st] + [_const_spec(c.shape) for c in consts],
        out_specs=[tok(wb), st],
        out_shape=[jax.ShapeDtypeStruct((rows, total, wb), F32),
                   jax.ShapeDtypeStruct((b, npair, LANES, LANES), F32)],
        scratch_shapes=[pltpu.VMEM((npair, LANES, LANES), F32)],
        compiler_params=_params("arbitrary", "arbitrary"),
        name="gdn",
    )(qb, kb, vb, bz, elem, s0, *consts)
    return ob.reshape(b, t, wb), s_new


def _outffn_kernel(oa_ref, ob_ref, oc_ref, x_ref, ga_ref, wout_ref, gpm_ref, gpf_ref, wfi_ref, wfo_ref,
                   gpo_ref, y_ref, *, dff, nsub):
    r = x_ref.shape[0] // nsub
    rows = [slice(i * r, (i + 1) * r) for i in range(nsub)]
    cat = [jnp.concatenate([_rms(oa_ref[rs, :], ga_ref[...]), ob_ref[rs, :], oc_ref[rs, :]], axis=-1).astype(BF16)
           for rs in rows]
    m = [jnp.dot(c, wout_ref[...], preferred_element_type=F32) for c in cat]
    x1 = [x_ref[rs, :] + _rms(mi, gpm_ref[...]) for rs, mi in zip(rows, m)]
    h = [_rms(xi, gpf_ref[...]).astype(BF16) for xi in x1]
    gu = [jnp.dot(hi, wfi_ref[...], preferred_element_type=F32) for hi in h]
    a = [(g[:, :dff] * _sigmoid(g[:, :dff]) * g[:, dff:]).astype(BF16) for g in gu]
    f = [jnp.dot(ai, wfo_ref[...], preferred_element_type=F32) for ai in a]
    for rs, xi, fi in zip(rows, x1, f):
        y_ref[rs, :] = xi + _rms(fi, gpo_ref[...])


def _outffn(oa, ob, oc, x, lw, *, tm):
    n, d = x.shape
    dff = lw["w_ffn_out"].shape[0]
    tok = lambda w: pl.BlockSpec((tm, w), lambda i: (i, 0))
    consts = [lw["g_a_out"], lw["w_out"], lw["g_post_mix"], lw["g_pre_ffn"], lw["w_ffn_in"],
              lw["w_ffn_out"], lw["g_post_ffn"]]
    return pl.pallas_call(
        functools.partial(_outffn_kernel, dff=dff, nsub=2 if tm % (4 * SUBLANES) == 0 else 1),
        grid=(n // tm,),
        in_specs=[tok(oa.shape[1]), tok(ob.shape[1]), tok(oc.shape[1]), tok(d)]
                 + [_const_spec(c.shape) for c in consts],
        out_specs=tok(d),
        out_shape=jax.ShapeDtypeStruct((n, d), F32),
        compiler_params=_params("arbitrary"),
        name="outffn",
    )(oa, ob, oc, x, *consts)


def _block_ones(width):
    idx = jnp.arange(width) // HEAD_DIM
    return (idx[:, None] == idx[None, :]).astype(BF16)


def _layer_weights(l, prm, cms):
    w_in = prm["w_in"][l]
    nh = prm["b_f"].shape[1]
    wa = nh * HEAD_DIM
    wb = prm["a_log"].shape[1] * HEAD_DIM
    wc = prm["g_cv"].shape[1]
    ng = prm["w_s"].shape[1]
    assert prm["a_log"].shape[1] == nh and wa % LANES == 0 and wc % LANES == 0 and AUG * nh <= LANES
    sizes = (wa, wa, wa, nh, 3 * wb, nh, nh, wb, wc, wc)
    offs = [0]
    for sz in sizes:
        offs.append(offs[-1] + sz)
    w_in_t = w_in.T
    col = lambda i: w_in_t[offs[i]:offs[i + 1]]
    w_big = jnp.concatenate([col(4), col(8), col(9), col(0), col(1), col(2), col(7)], axis=0).astype(BF16)
    w_small = jnp.concatenate([col(3), col(5), col(6), jnp.zeros((LANES - 3 * nh, w_in.shape[0]), F32)],
                              axis=0).astype(BF16)
    zpad = jnp.zeros((LANES - 2 * nh,), F32)
    sp = jnp.zeros((SUBLANES, LANES), F32)
    sp = sp.at[0].set(jnp.concatenate([prm["b_f"][l], prm["dt_bias"][l], zpad]))
    sp = sp.at[1].set(jnp.concatenate([jnp.zeros((nh,), F32), prm["a_log"][l], zpad]))
    hl = jnp.arange(nh) * AUG
    sp = sp.at[2, (hl[:, None] + jnp.arange(3, 6)[None, :]).reshape(-1)].set(1.0)
    sp = sp.at[3, (hl[:, None] + jnp.arange(0, 3)[None, :]).reshape(-1)].set(1.0)
    pmat = jnp.zeros((3 * LANES, 2 * LANES), F32)
    for piece in range(3):
        pmat = pmat.at[piece * LANES + jnp.arange(nh), hl + piece].set(1.0)
        pmat = pmat.at[piece * LANES + jnp.arange(nh), LANES + hl + 3 + piece].set(1.0)
    row = lambda v: v.reshape(1, -1)
    ws_cat, bs_full = {}, {}
    for cm in cms:
        pos = jnp.arange(cm) // HEAD_DIM
        w = jnp.where(pos[None, :] <= pos[:, None], prm["w_s"][l][:, :cm, :cm], 0.0)
        pairs = [jnp.concatenate([w[2 * pp], w[2 * pp + 1]], axis=1) for pp in range(ng // 2)]
        kpad = max(LANES - 2 * cm, 0)
        ws_cat[cm] = jnp.pad(jnp.stack(pairs), ((0, 0), (0, 0), (0, kpad))).astype(BF16)
        bs_full[cm] = jnp.repeat(prm["b_s"][l][:, :cm].T, wc // ng, axis=1)
    src = jnp.arange(LANES)[:, None]
    dst = jnp.arange(wb)[None, :] // HEAD_DIM
    esel = jnp.concatenate([src == nh + dst, src == 2 * nh + dst], axis=1).astype(BF16)
    return dict(
        nh=nh, wa=wa, wb=wb, wc=wc,
        g_pre_mix=row(prm["g_pre_mix"][l]), w_big=w_big, w_small=w_small, sp=sp, conv_w=prm["conv_w"][l],
        g_cv=row(prm["g_cv"][l]), b_cv=row(prm["b_cv"][l]), ws_cat=ws_cat, bs_full=bs_full,
        g_c_out=row(prm["g_c_out"][l]), hsum=_block_ones(2 * LANES), hsum128=_block_ones(LANES), pmat=pmat.astype(BF16),
        g_b_pair=row(jnp.tile(prm["g_b_out"][l], LANES // HEAD_DIM)), esel=esel,
        g_a_out=row(prm["g_a_out"][l]), w_out=prm["w_out"][l].astype(BF16),
        g_post_mix=row(prm["g_post_mix"][l]), g_pre_ffn=row(prm["g_pre_ffn"][l]),
        w_ffn_in=prm["w_ffn_in"][l].astype(BF16), w_ffn_out=prm["w_ffn_out"][l].astype(BF16),
        g_post_ffn=row(prm["g_post_ffn"][l]))


def _pair_state(s):
    b, h, dk, dv = s.shape
    s = s.reshape(b, h // 2, 2, dk, dv)
    z = jnp.zeros_like(s[:, :, 0])
    top = jnp.concatenate([s[:, :, 0], z], axis=-1)
    bot = jnp.concatenate([z, s[:, :, 1]], axis=-1)
    return jnp.concatenate([top, bot], axis=-2)


def _unpair_state(sp):
    b, hp, _, _ = sp.shape
    s0 = sp[:, :, :HEAD_DIM, :HEAD_DIM]
    s1 = sp[:, :, HEAD_DIM:, HEAD_DIM:]
    return jnp.stack([s0, s1], axis=2).reshape(b, 2 * hp, HEAD_DIM, HEAD_DIM)


def _head_rows(cum, nh):
    b, t, _ = cum.shape
    return jnp.transpose(cum[:, :, :nh], (0, 2, 1)).reshape(b, nh // 2, 2, t)


def _pick(n, prefs):
    for c in prefs:
        if n % c == 0:
            return c
    return n


def _layer(x, lw, conv_prev, s0, cache, *, cm, layer, depth, kv_prev):
    b, t, d = x.shape
    nh, wb = lw["nh"], lw["wb"]
    kw1 = conv_prev.shape[1]
    conv_init = jnp.pad(conv_prev, ((0, 0), (SUBLANES - kw1, 0), (0, 0)))
    tm = _pick(t, (512, 256, 128, 64, 32, 16))
    pj = _inproj(x, lw, conv_init, tm=tm, cm=cm, layer=layer, depth=depth, kv_prev=kv_prev)

    if cache is None:
        tq = _pick(t, (ATTN_TQ, 256, 128))
        oa = _attn_prompt(pj["qaug"], pj["kaug"], pj["vab"], tq=tq, kc=min(ATTN_KC, tq), look=ATTN_LOOK)
    else:
        ck, cv, clogf_t = cache
        _, _, bs, past = clogf_t.shape
        excl = _exclusive_suffix_sum(clogf_t, layer)
        rrow = jnp.transpose(excl, (1, 0, 2)).reshape(bs, nh // 2, 2, past)
        crow = jnp.pad(_head_rows(pj["cum"], nh), ((0, 0), (0, 0), (0, 0), (0, LANES - t)))
        oa = _attn_sample(pj["qaug"], pj["kaug"], pj["vab"], pj["cum"], ck, cv, layer, rrow, crow)

    tp = -(-t // GDN_BLOCK) * GDN_BLOCK
    padt = lambda a: a if tp == t else jnp.pad(a, ((0, 0), (0, tp - t), (0, 0)))
    nb = _pick(tp // GDN_BLOCK, (GDN_TILE_BLOCKS, 2, 1))
    ob, s_new = _gdn(padt(pj["qb"]), padt(pj["kb"]), padt(pj["vb"]), padt(pj["bz"]), padt(pj["elem"]),
                     _pair_state(s0), lw, nb=nb)
    ob = ob[:, :t]

    n = b * t
    y = _outffn(oa.reshape(n, -1), ob.reshape(n, -1), pj["oc"].reshape(n, -1), x.reshape(n, d), lw,
                tm=_pick(n, (512, 256, 128, 64, 32, 16)))
    if pj["kv_time_minor"]:
        new_kv = (pj["ka"], pj["va"])
    else:
        new_kv = (pj["ka"].reshape(b, t, nh, HEAD_DIM), pj["va"].reshape(b, t, nh, HEAD_DIM))
    state = (new_kv[0], new_kv[1], pj["elem"][:, :, :nh],
             pj["ytail"][:, SUBLANES - kw1:, :], _unpair_state(s_new), pj["vn"])
    return y.reshape(b, t, d), state, pj["kv_time_minor"]


def kernel(x_prompt, x_sample, cache_a_k, cache_a_v, cache_a_logf, state_b_conv, state_b_S, g_pre_mix, w_in, b_f, conv_w, a_log, dt_bias, g_b_out, g_a_out, g_cv, b_cv, w_s, b_s, g_c_out, w_out, g_post_mix, g_pre_ffn, w_ffn_in, w_ffn_out, g_post_ffn):
    prm = dict(g_pre_mix=g_pre_mix, w_in=w_in, b_f=b_f, conv_w=conv_w, a_log=a_log, dt_bias=dt_bias,
               g_b_out=g_b_out, g_a_out=g_a_out, g_cv=g_cv, b_cv=b_cv, w_s=w_s, b_s=b_s, g_c_out=g_c_out,
               w_out=w_out, g_post_mix=g_post_mix, g_pre_ffn=g_pre_ffn, w_ffn_in=w_ffn_in,
               w_ffn_out=w_ffn_out, g_post_ffn=g_post_ffn)
    depth = w_in.shape[0]
    bp, sp_len, _ = x_prompt.shape
    n_new = x_sample.shape[1]
    cm_p = w_s.shape[2]
    assert sp_len % cm_p == 0 and sp_len % GDN_BLOCK == 0 and n_new <= HEAD_DIM and n_new % SUBLANES == 0
    kw1 = conv_w.shape[1] - 1
    nhb = a_log.shape[1]
    yp, ys = x_prompt, x_sample
    outs_p, outs_s = [], []
    cache_kt = jnp.transpose(cache_a_k, (0, 1, 3, 4, 2))
    cache_vt = jnp.transpose(cache_a_v, (0, 1, 3, 4, 2))
    clogf_t = jnp.transpose(cache_a_logf, (0, 3, 1, 2))
    for l in range(depth):
        lw = _layer_weights(l, prm, (cm_p, n_new))
        conv0 = jnp.zeros((bp, kw1, conv_w.shape[2]), F32)
        s0 = jnp.zeros((bp, nhb, HEAD_DIM, HEAD_DIM), F32)
        kv_p = (outs_p[-1][0], outs_p[-1][1]) if outs_p else None
        kv_s = (outs_s[-1][0], outs_s[-1][1]) if outs_s else None
        yp, st_p, shared_p = _layer(yp, lw, conv0, s0, None, cm=cm_p, layer=l, depth=depth, kv_prev=kv_p)
        ys, st_s, shared_s = _layer(ys, lw, state_b_conv[l], state_b_S[l], (cache_kt, cache_vt, clogf_t),
                                    cm=n_new, layer=l, depth=depth, kv_prev=kv_s)
        outs_p.append(st_p)
        outs_s.append(st_s)
    stk = lambda outs, i: jnp.stack([o[i] for o in outs], axis=0)

    def new_cache(outs, i, shared):
        if not shared:
            return stk(outs, i)
        buf = outs[-1][i]
        dp, b, _, t = buf.shape
        return jnp.transpose(buf.reshape(dp, b, -1, HEAD_DIM, t), (0, 1, 4, 2, 3))

    return (yp, ys, new_cache(outs_p, 0, shared_p), new_cache(outs_p, 1, shared_p),
            stk(outs_p, 2), stk(outs_p, 3), stk(outs_p, 4),
            new_cache(outs_s, 0, shared_s), new_cache(outs_s, 1, shared_s),
            stk(outs_s, 2), stk(outs_s, 3), stk(outs_s, 4), stk(outs_s, 5))
```

```python
import functools

import jax
import jax.numpy as jnp
from jax import lax
from jax.experimental import pallas as pl
from jax.experimental.pallas import tpu as pltpu

F32 = jnp.float32
BF16 = jnp.bfloat16

LANES = 128
SUBLANES = 8
HEAD_DIM = 64
GDN_BLOCK = 128
GDN_GROUP = 2
GDN_TILE_BLOCKS = 4
ATTN_TQ = 1024
ATTN_KC = 256
ATTN_LOOK = 2
VMEM_LIMIT = 56 * 1024 * 1024
NEG_INF = float("-inf")
LOG2E = 1.4426950408889634
AUG = 16


def _dot(a, b):
    return jnp.dot(a.astype(BF16), b.astype(BF16), preferred_element_type=F32)


def _dot_nt(a, b):
    return lax.dot_general(a.astype(BF16), b.astype(BF16), (((1,), (1,)), ((), ())),
                           preferred_element_type=F32)


def _dot_select_exact(x, sel):
    hi = x.astype(BF16)
    r1 = x - hi.astype(F32)
    mid = r1.astype(BF16)
    lo = (r1 - mid.astype(F32)).astype(BF16)
    d = lambda p: jnp.dot(p, sel, preferred_element_type=F32)
    return (d(hi) + d(mid)) + d(lo)


def _rms(x, g, eps=1e-6):
    return x * lax.rsqrt(jnp.mean(x * x, axis=-1, keepdims=True) + eps) * g


def _softplus(x):
    return jnp.maximum(x, 0.0) + jnp.log1p(jnp.exp(-jnp.abs(x)))


def _sigmoid(x):
    return 1.0 / (1.0 + jnp.exp(-x))


def _seg_cumsum(v, seg):
    row = lax.broadcasted_iota(jnp.int32, v.shape, 0)
    pos = jnp.bitwise_and(row, seg - 1)
    s = 1
    while s < seg:
        v = v + jnp.where(pos >= s, pltpu.roll(v, s, 0), 0.0)
        s *= 2
    return v


def _const_spec(shape):
    nd = len(shape)
    return pl.BlockSpec(shape, lambda *_: (0,) * nd, pipeline_mode=pl.Buffered(1))


def _params(*sem):
    return pltpu.CompilerParams(dimension_semantics=sem, vmem_limit_bytes=VMEM_LIMIT)


def _inproj_kernel(x_ref, gpre_ref, wbig_ref, wsm_ref, sp_ref, convw_ref, convinit_ref, gcv_ref,
                   bcv_ref, ws_ref, bs_ref, gco_ref, hsum_ref, pmat_ref,
                   qaug_ref, ka_ref, va_ref, kaug_ref, vab_ref, elem_ref, cum_ref, qb_ref, kb_ref,
                   vb_ref, bz_ref, oc_ref, vn_ref, ytail_ref,
                   carry_conv, carry_cum, *, tm, cm, nh, wa, wb, wc, scale, kv_time_minor, nsub, seg):
    if not seg:
        @pl.when(pl.program_id(1) == 0)
        def _():
            carry_cum[...] = jnp.zeros_like(carry_cum)
            carry_conv[...] = convinit_ref[...]

    r = tm // nsub
    o_c = 3 * wb
    o_a = o_c + 2 * wc

    def project(rs):
        h = _rms(x_ref[rs, :], gpre_ref[...]).astype(BF16)
        proj = lambda w: lax.dot_general(h, w, (((1,), (1,)), ((), ())), preferred_element_type=F32)
        return (proj(wbig_ref[:o_c, :]),
                proj(wbig_ref[o_c:o_a, :]),
                proj(wsm_ref[...]),
                proj(wbig_ref[o_a:, :]))

    def finish(rs, y, zc, zs, za, prev, cum_in):
        ka = za[:, wa:2 * wa]
        va = za[:, 2 * wa:3 * wa]
        if kv_time_minor:
            ka_ref[:, rs] = ka.T
            va_ref[:, rs] = va.T
        else:
            ka_ref[rs, :] = ka
            va_ref[rs, :] = va
        vab_ref[rs, :] = va.astype(BF16)
        bz_ref[rs, :] = za[:, 3 * wa:]

        lane = lax.broadcasted_iota(jnp.int32, (r, LANES), 1)
        zb = zs + sp_ref[0:1, :]
        soft_tail = jnp.log1p(jnp.exp(-jnp.abs(zb)))
        logf = -(jnp.maximum(-zb, 0.0) + soft_tail)
        gl = -jnp.exp(sp_ref[1:2, :]) * (jnp.maximum(zb, 0.0) + soft_tail)
        beta = _sigmoid(zs)
        elem = jnp.where(lane < nh, logf, jnp.where(lane < 2 * nh, gl, jnp.where(lane < 3 * nh, beta, 0.0)))
        elem_ref[rs, :] = elem
        cum = _seg_cumsum(elem, seg) if seg else _seg_cumsum(elem, r) + cum_in
        cum_ref[rs, :] = cum

        c2 = jnp.where(lane < nh, cum * LOG2E, 0.0)
        hi = c2.astype(BF16)
        r1 = c2 - hi.astype(F32)
        mid = r1.astype(BF16)
        lo = (r1 - mid.astype(F32)).astype(BF16)
        placed = jnp.dot(jnp.concatenate([hi, mid, lo], axis=1), pmat_ref[...], preferred_element_type=F32)
        augq = (placed[:, :LANES] + sp_ref[2:3, :]).astype(BF16)
        augk = (sp_ref[3:4, :] - placed[:, LANES:]).astype(BF16)
        qs = (za[:, :wa] * (scale * LOG2E)).astype(BF16)
        ks = ka.astype(BF16)
        qaug_ref[rs, :] = jnp.concatenate(
            [a for j in range(0, wa, LANES) for a in (qs[:, j:j + LANES], augq)], axis=1)
        kaug_ref[rs, :] = jnp.concatenate(
            [a for j in range(0, wa, LANES) for a in (ks[:, j:j + LANES], augk)], axis=1)

        row8 = lax.broadcasted_iota(jnp.int32, prev.shape, 0)
        pos = jnp.bitwise_and(lax.broadcasted_iota(jnp.int32, y.shape, 0), max(seg, 1) - 1)
        kw = convw_ref.shape[0]
        acc = y * convw_ref[kw - 1:kw, :]
        for k in range(1, kw):
            yk = pltpu.roll(y, k, 0)
            if seg:
                yk = jnp.where(pos < k, convinit_ref[k - 1, rs, :], yk)
            else:
                top = jnp.where(row8 < k, pltpu.roll(prev, k, 0), yk[0:SUBLANES])
                yk = jnp.concatenate([top, yk[SUBLANES:]], axis=0)
            acc = acc + yk * convw_ref[kw - 1 - k:kw - k, :]
        if seg:
            ytail_ref[rs, :] = y
        yc = acc * _sigmoid(acc)
        qb = yc[:, :wb]
        kb = yc[:, wb:2 * wb]
        sq = jnp.concatenate([qb * qb, kb * kb], axis=-1).astype(BF16)
        hw = hsum_ref.shape[0]
        ss = jnp.concatenate([jnp.dot(sq[:, j:j + hw], hsum_ref[...], preferred_element_type=F32)
                              for j in range(0, 2 * wb, hw)], axis=-1)
        qb_ref[rs, :] = qb * lax.rsqrt(ss[:, :wb] + 1e-6) * scale
        kb_ref[rs, :] = kb * lax.rsqrt(ss[:, wb:] + 1e-6)
        vb_ref[rs, :] = yc[:, 2 * wb:]

        u = jax.nn.gelu(zc[:, :wc])
        gv = jax.nn.gelu(zc[:, wc:])
        mu = jnp.mean(gv, axis=-1, keepdims=True)
        var = jnp.mean(jnp.square(gv - mu), axis=-1, keepdims=True)
        vn = (gv - mu) * lax.rsqrt(var + 1e-5) * gcv_ref[...] + bcv_ref[...]
        vn_ref[rs, :] = vn
        first = lax.broadcasted_iota(jnp.int32, (cm, LANES), 1) < HEAD_DIM
        kpad = ws_ref.shape[2] - 2 * cm
        rows = []
        for c in range(r // cm):
            vc = vn[c * cm:(c + 1) * cm]
            cols = []
            for pp in range(wc // LANES):
                vp = vc[:, pp * LANES:(pp + 1) * LANES]
                parts = [jnp.where(first, vp, 0.0), jnp.where(first, 0.0, vp)]
                if kpad:
                    parts.append(jnp.zeros((kpad, LANES), F32))
                cols.append(_dot(ws_ref[pp], jnp.concatenate(parts, axis=0)))
            s = jnp.concatenate(cols, axis=-1) + bs_ref[...]
            rows.append(u[c * cm:(c + 1) * cm] * s)
        oc = rows[0] if len(rows) == 1 else jnp.concatenate(rows, axis=0)
        oc_ref[rs, :] = _rms(oc, gco_ref[...])
        return y[r - SUBLANES:r], cum[r - 1:r, :]

    subs = [slice(i * r, (i + 1) * r) for i in range(nsub)]
    projected = [project(rs) for rs in subs]
    if seg:
        prev, cum_in = jnp.zeros(carry_conv.shape, F32), jnp.zeros(carry_cum.shape, F32)
    else:
        prev, cum_in = carry_conv[...], carry_cum[...]
    for rs, z in zip(subs, projected):
        prev, cum_in = finish(rs, *z, prev, cum_in)
    if not seg:
        carry_conv[...] = prev
        carry_cum[...] = cum_in
        ytail_ref[...] = prev


def _inproj_kernel_inplace(*refs, n_in, **kw):
    return _inproj_kernel(*refs[:n_in], *refs[n_in + 2:], **kw)


def _inproj(x, lw, conv_prev, *, cm, layer, depth, kv_prev):
    b0, t0, d = x.shape
    wa, wb, wc, nh = lw["wa"], lw["wb"], lw["wc"], lw["nh"]
    kw1 = conv_prev.shape[1]
    seg = 0 if t0 >= LANES else t0
    if seg:
        assert seg & (seg - 1) == 0 and seg >= SUBLANES
        x = x.reshape(1, b0 * t0, d)
        tm = _pick(b0 * t0, (512, 256, 128, 64, 32, 16))
        assert tm % seg == 0
        conv_init = jnp.stack([jnp.pad(conv_prev[:, kw1 - k:], ((0, 0), (0, t0 - k), (0, 0))) for k in range(1, kw1 + 1)])
        conv_init = conv_init.reshape(kw1, b0 * t0, -1)
    else:
        tm = _pick(t0, (512, 256, 128))
        conv_init = jnp.pad(conv_prev, ((0, 0), (SUBLANES - kw1, 0), (0, 0)))
    b, t, _ = x.shape
    nt = t // tm
    kv_time_minor = not seg
    tok = lambda w: pl.BlockSpec((None, tm, w), lambda i, j: (i, j, 0))
    per_b = lambda r, w: pl.BlockSpec((None, r, w), lambda i, j: (i, 0, 0))
    outs = [("qaug", 2 * wa, BF16), ("ka", wa, F32), ("va", wa, F32), ("kaug", 2 * wa, BF16), ("vab", wa, BF16),
            ("elem", LANES, F32), ("cum", LANES, F32), ("qb", wb, F32), ("kb", wb, F32), ("vb", wb, F32),
            ("bz", wb, F32), ("oc", wc, F32), ("vn", wc, F32)]
    out_shape = [jax.ShapeDtypeStruct((b, t, w), dt) for _, w, dt in outs]
    out_specs = [tok(w) for _, w, _ in outs]
    if kv_time_minor:
        for k in (1, 2):
            out_shape[k] = jax.ShapeDtypeStruct((depth, b, wa, t), F32)
            out_specs[k] = pl.BlockSpec((None, None, wa, tm), lambda i, j: (layer, i, 0, j))
    if seg:
        out_shape.append(jax.ShapeDtypeStruct((b, t, 3 * wb), F32))
        out_specs.append(tok(3 * wb))
        init_spec = pl.BlockSpec((kw1, tm, 3 * wb), lambda i, j: (0, j, 0))
    else:
        out_shape.append(jax.ShapeDtypeStruct((b, SUBLANES, 3 * wb), F32))
        out_specs.append(per_b(SUBLANES, 3 * wb))
        init_spec = per_b(SUBLANES, 3 * wb)
    consts = [lw["g_pre_mix"], lw["w_big"], lw["w_small"], lw["sp"], lw["conv_w"]]
    consts2 = [lw["g_cv"], lw["b_cv"], lw["ws_cat"][cm], lw["bs_full"][cm], lw["g_c_out"], lw["hsum"], lw["pmat"]]
    nsub = 2 if tm % (2 * max(cm, LANES)) == 0 else 1
    kw = dict(tm=tm, cm=cm, nh=nh, wa=wa, wb=wb, wc=wc, scale=HEAD_DIM ** -0.5, kv_time_minor=kv_time_minor,
              nsub=nsub, seg=seg)
    in_specs = ([tok(d)] + [_const_spec(c.shape) for c in consts] + [init_spec]
                + [_const_spec(c.shape) for c in consts2])
    args = [x, *consts, conv_init, *consts2]
    inplace = kv_time_minor and kv_prev is not None
    if inplace:
        kern = functools.partial(_inproj_kernel_inplace, n_in=len(args), **kw)
        aliases = {len(args): 1, len(args) + 1: 2}
        in_specs = in_specs + [pl.BlockSpec(memory_space=pl.ANY)] * 2
        args = args + list(kv_prev)
    else:
        kern = functools.partial(_inproj_kernel, **kw)
        aliases = {}
    res = pl.pallas_call(
        kern,
        grid=(b, nt),
        in_specs=in_specs,
        out_specs=out_specs,
        out_shape=out_shape,
        input_output_aliases=aliases,
        scratch_shapes=[pltpu.VMEM((SUBLANES, 3 * wb), F32), pltpu.VMEM((1, LANES), F32)],
        compiler_params=_params("arbitrary", "arbitrary"),
        name="inproj",
    )(*args)
    named = {n: r for (n, _, _), r in zip(outs, res[:-1])}
    if seg:
        named = {n: r.reshape(b0, t0, r.shape[-1]) for n, r in named.items()}
        named["conv_new"] = res[-1].reshape(b0, t0, -1)[:, t0 - kw1:, :]
    else:
        named["conv_new"] = res[-1][:, SUBLANES - kw1:, :]
    named["kv_time_minor"] = kv_time_minor
    return named


def _attn_kernel(qt_ref, k_ref, vt_ref, o_ref, *, tq, kc, look):
    p = pl.program_id(1)
    i = pl.program_id(2)
    qt = qt_ref[...]
    rowi = lax.broadcasted_iota(jnp.int32, qt.shape, 0)
    zero = jnp.zeros_like(qt)
    qts = []
    for e in range(2):
        a0 = LANES + AUG * (2 * p + e)
        keep = ((rowi >= e * HEAD_DIM) & (rowi < (e + 1) * HEAD_DIM)) | ((rowi >= a0) & (rowi < a0 + AUG))
        qts.append(jnp.where(keep, qt, zero))
    qpos = i * tq + lax.broadcasted_iota(jnp.int32, (kc, tq), 1)
    kofs = lax.broadcasted_iota(jnp.int32, (kc, tq), 0)
    ones = jnp.ones((2 * SUBLANES, kc), BF16)
    units = [(c, e) for c in range(tq // kc) for e in range(2)]

    def scores(j, c, e, masked):
        k0 = pl.multiple_of(j * tq + c * kc, kc)
        lo = c * kc if masked else 0
        s = jnp.dot(k_ref[pl.ds(k0, kc), :], qts[e][:, lo:], preferred_element_type=F32)
        return jnp.concatenate([jnp.full((kc, lo), NEG_INF, F32), s], axis=1) if lo else s

    def fold(j, c, e, s, st, masked):
        m, l, acc = st
        k0 = pl.multiple_of(j * tq + c * kc, kc)
        lo = c * kc if masked else 0
        if masked:
            s = jnp.where(k0 + kofs <= qpos, s, NEG_INF)
        m_new = jnp.maximum(m, jnp.max(s, axis=0, keepdims=True))
        alpha = jnp.exp2(m - m_new)
        pt = jnp.exp2(s - m_new).astype(BF16)
        vt = jnp.concatenate([vt_ref[e * HEAD_DIM:(e + 1) * HEAD_DIM, pl.ds(k0, kc)], ones], axis=0)
        r = jnp.dot(vt, pt[:, lo:], preferred_element_type=F32)
        if lo:
            r = jnp.concatenate([jnp.zeros((r.shape[0], lo), F32), r], axis=1)
        return m_new, alpha * l + r[HEAD_DIM:HEAD_DIM + 1], alpha * acc + r[:HEAD_DIM]

    def run(blocks, state):
        state = list(state)
        todo = [(j, c, e, masked) for j, masked in blocks for c, e in units]
        pend = {}
        for k in range(min(look, len(todo))):
            pend[k] = scores(*todo[k])
        for k, (j, c, e, masked) in enumerate(todo):
            if k + look < len(todo):
                pend[k + look] = scores(*todo[k + look])
            state[e] = fold(j, c, e, pend.pop(k), state[e], masked)
        return tuple(state)

    st0 = (jnp.full((1, tq), NEG_INF, F32), jnp.zeros((1, tq), F32), jnp.zeros((HEAD_DIM, tq), F32))
    state = lax.fori_loop(0, i // 2, lambda t, s: run([(2 * t, False), (2 * t + 1, False)], s), (st0, st0))
    state = lax.cond(i % 2 == 1,
                     lambda s: run([(i - 1, False), (i, True)], s),
                     lambda s: run([(i, True)], s), state)
    ot = jnp.concatenate([acc / l for _, l, acc in state], axis=0)
    o_ref[...] = ot.T


def _attn_prompt(qaug, kaug, vab, *, tq, kc, look):
    b, s, wa = vab.shape
    npair = wa // LANES
    qt = jnp.transpose(qaug, (0, 2, 1))
    vt = jnp.transpose(vab, (0, 2, 1))
    kern = functools.partial(_attn_kernel, tq=tq, kc=kc, look=look)
    return pl.pallas_call(
        kern,
        grid=(b, npair, s // tq),
        in_specs=[pl.BlockSpec((None, 2 * LANES, tq), lambda bi, p, i: (bi, p, i)),
                  pl.BlockSpec((None, s, 2 * LANES), lambda bi, p, i: (bi, 0, p)),
                  pl.BlockSpec((None, LANES, s), lambda bi, p, i: (bi, p, 0))],
        out_specs=pl.BlockSpec((None, tq, LANES), lambda bi, p, i: (bi, i, p)),
        out_shape=jax.ShapeDtypeStruct((b, s, wa), F32),
        compiler_params=_params("arbitrary", "arbitrary", "arbitrary"),
        name="attn_prompt",
    )(qt, kaug, vt)


def _attn_sample_kernel(q_ref, kc_ref, vc_ref, kn_ref, vn_ref, cum_ref, rrow_ref, crow_ref, o_ref, *, n):
    p = pl.program_id(1)
    q = q_ref[:, :LANES]
    lane = lax.broadcasted_iota(jnp.int32, (n, LANES), 1)
    first = lane < HEAD_DIM
    zero = jnp.zeros_like(q)
    past = kc_ref.shape[-1]
    kc = kc_ref[...].reshape(LANES, past).astype(BF16)
    vc = vc_ref[...].reshape(LANES, past).astype(BF16)
    pad = jnp.zeros((LANES - n, LANES), BF16)
    kn = jnp.concatenate([kn_ref[:, :LANES], pad], axis=0)
    vn = jnp.concatenate([vn_ref[...], pad], axis=0)
    cum = cum_ref[...]
    qm = jnp.concatenate([jnp.where(first, q, zero), jnp.where(first, zero, q)], axis=0)
    cq = jnp.concatenate([jnp.sum(jnp.where(lane == 2 * p + e, cum, 0.0), axis=-1, keepdims=True)
                          for e in range(2)], axis=0)
    top = lax.broadcasted_iota(jnp.int32, (2 * n, 1), 0) < n
    rrow = jnp.where(top, rrow_ref[0:1, :], rrow_ref[1:2, :])
    crow = jnp.where(top, crow_ref[0:1, :], crow_ref[1:2, :])
    qrow = lax.broadcasted_iota(jnp.int32, (2 * n, LANES), 0)
    causal = lax.broadcasted_iota(jnp.int32, (2 * n, LANES), 1) <= jnp.where(qrow < n, qrow, qrow - n)
    sc = _dot(qm, kc) + LOG2E * (cq + rrow)
    sn = jnp.where(causal, _dot_nt(qm, kn) + LOG2E * (cq - crow), NEG_INF)
    m = jnp.maximum(jnp.max(sc, axis=-1, keepdims=True), jnp.max(sn, axis=-1, keepdims=True))
    pc = jnp.exp2(sc - m)
    pn = jnp.exp2(sn - m)
    l = jnp.sum(pc, axis=-1, keepdims=True) + jnp.sum(pn, axis=-1, keepdims=True)
    o = (_dot_nt(pc, vc) + _dot(pn, vn)) / l
    o_ref[...] = jnp.where(first, o[:n], o[n:])


def _attn_sample(qaug, kaug, vab, cum, cache_kt, cache_vt, layer, rrow, crow):
    b, n, wa = vab.shape
    past = cache_kt.shape[-1]
    npair = wa // LANES
    new = lambda w: pl.BlockSpec((None, n, w), lambda bi, p: (bi, 0, p))
    old = lambda: pl.BlockSpec((None, None, 2, HEAD_DIM, past), lambda bi, p: (layer, bi, p, 0, 0))
    return pl.pallas_call(
        functools.partial(_attn_sample_kernel, n=n),
        grid=(b, npair),
        in_specs=[new(2 * LANES), old(), old(), new(2 * LANES), new(LANES),
                  pl.BlockSpec((None, n, LANES), lambda bi, p: (bi, 0, 0)),
                  pl.BlockSpec((None, None, 2, past), lambda bi, p: (bi, p, 0, 0)),
                  pl.BlockSpec((None, None, 2, LANES), lambda bi, p: (bi, p, 0, 0))],
        out_specs=new(LANES),
        out_shape=jax.ShapeDtypeStruct((b, n, wa), F32),
        compiler_params=_params("arbitrary", "arbitrary"),
        name="attn_sample",
    )(qaug, cache_kt, cache_vt, kaug, vab, cum, rrow, crow)


def _suffix_kernel(x_ref, o_ref):
    v = x_ref[...]
    n = v.shape[1]
    lane = lax.broadcasted_iota(jnp.int32, v.shape, 1)
    s = 1
    while s < n:
        v = v + jnp.where(lane + s < n, pltpu.roll(v, n - s, 1), 0.0)
        s *= 2
    o_ref[...] = jnp.where(lane + 1 < n, pltpu.roll(v, n - 1, 1), 0.0)


def _exclusive_suffix_sum(x, layer):
    _, h, b, p = x.shape
    return pl.pallas_call(
        _suffix_kernel,
        grid=(h,),
        in_specs=[pl.BlockSpec((None, None, b, p), lambda i: (layer, i, 0, 0))],
        out_specs=pl.BlockSpec((None, b, p), lambda i: (i, 0, 0)),
        out_shape=jax.ShapeDtypeStruct((h, b, p), F32),
        compiler_params=_params("arbitrary"),
        name="suffix_sum",
    )(x)


def _gdn_kernel(q_ref, k_ref, v_ref, bz_ref, elem_ref, s0_ref, gb_ref, esel_ref, hsum_ref,
                o_ref, sout_ref, s_scr, *, nb, nh, chained):
    L = GDN_BLOCK
    t = pl.program_id(1)

    if chained:
        @pl.when(t == 0)
        def _():
            s_scr[...] = s0_ref[...]

    lane = lax.broadcasted_iota(jnp.int32, (L, LANES), 1)
    first = lane < HEAD_DIM
    ri = lax.broadcasted_iota(jnp.int32, (L, L), 0)
    ci = lax.broadcasted_iota(jnp.int32, (L, L), 1)
    incl = ci <= ri
    strict = ci < ri
    same_head = (ri < HEAD_DIM) == (ci < HEAD_DIM)
    lane2 = lax.broadcasted_iota(jnp.int32, (L, 2 * L), 1)
    first2 = jnp.bitwise_and(lane2, LANES - 1) < HEAD_DIM
    xor2 = jnp.bitwise_xor(lax.broadcasted_iota(jnp.int32, (L, 2 * L), 0), jnp.bitwise_and(lane2, L - 1))
    zero_ll = jnp.zeros((L, L), BF16)

    def halves(x, sel):
        return jnp.concatenate([jnp.where(sel, x, 0.0), jnp.where(sel, 0.0, x)], axis=0)

    def dot_heads(y, x):
        xb = x.astype(BF16)
        bd = jnp.concatenate([jnp.concatenate([xb[:, :L], zero_ll], axis=1),
                              jnp.concatenate([zero_ll, xb[:, L:]], axis=1)], axis=0)
        return jnp.dot(y.astype(BF16), bd, preferred_element_type=F32)

    npair = s_scr.shape[0]
    wbw = npair * LANES
    c = {}

    def solve_stages(blocks):
        chains = [(n, p) for n in blocks for p in range(npair)]
        ex = {}
        for n in blocks:
            elem = elem_ref[n * L:(n + 1) * L, :]
            gsum = _seg_cumsum(elem, L)
            mixed = jnp.where((lane >= nh) & (lane < 2 * nh), gsum, elem)
            ex[n] = _dot_select_exact(mixed, esel_ref[...])
        yield
        for n, p in chains:
            rows = slice(n * L, (n + 1) * L)
            cols = slice(p * LANES, (p + 1) * LANES)
            g = ex[n][:, cols]
            bt = ex[n][:, wbw + p * LANES: wbw + (p + 1) * LANES]
            kp = k_ref[rows, cols]
            qp = q_ref[rows, cols]
            g_sw = pltpu.roll(g, HEAD_DIM, 1)
            b_sw = pltpu.roll(bt, HEAD_DIM, 1)
            g_t = g.T
            a_parts, qk_parts = [], []
            for e in range(2):
                sel = first if e == 0 else jnp.logical_not(first)
                gcol = jnp.where(sel, g, g_sw)
                bcol = jnp.where(sel, bt, b_sw)
                grow = g_t[e * HEAD_DIM:e * HEAD_DIM + 1, :]
                dec = jnp.exp(jnp.where(incl, gcol - grow, NEG_INF))
                kk = _dot_nt(jnp.where(sel, kp, 0.0), kp)
                qk_parts.append(_dot_nt(jnp.where(sel, qp, 0.0), kp) * dec)
                a_parts.append(jnp.where(strict, bcol * kk * dec, 0.0))
            a_cat = jnp.concatenate(a_parts, axis=1)
            eg = jnp.exp(g)
            glast = g[L - 1:L, :]
            c[n, p] = dict(a=a_cat, qk=jnp.concatenate(qk_parts, axis=1), glast=glast, qg=qp * eg,
                           kdec=kp * jnp.exp(glast - g),
                           r=jnp.concatenate([v_ref[rows, cols] * bt, kp * bt * eg], axis=1),
                           tm1=-jnp.where(xor2 < 2, a_cat, 0.0))
        yield
        s_blk = 2
        while s_blk < L:
            pm = {}
            for key in chains:
                nmat = jnp.where((xor2 >= s_blk) & (xor2 < 2 * s_blk), c[key]["a"], 0.0)
                pm[key] = nmat + dot_heads(c[key]["tm1"], nmat)
            yield
            for key in chains:
                c[key]["tm1"] = c[key]["tm1"] - pm[key] - dot_heads(pm[key], c[key]["tm1"])
            yield
            s_blk *= 2
        for key in chains:
            r = c[key]["r"]
            c[key]["uw"] = r + _dot(c[key]["tm1"], halves(r, first2))
        yield

    def state_stages(blocks):
        pairs = range(npair)
        for n in blocks:
            rows = slice(n * L, (n + 1) * L)
            s_in = [s_scr[p] if chained else s0_ref[n, p] for p in pairs]
            ws = [_dot(jnp.concatenate([c[n, p]["uw"][:, LANES:], c[n, p]["qg"]], axis=0), s_in[p]) for p in pairs]
            yield
            u = [c[n, p]["uw"][:, :LANES] - ws[p][:L] for p in pairs]
            o = [ws[p][L:] + _dot(c[n, p]["qk"], halves(u[p], first)) for p in pairs]
            yield
            for p in pairs:
                s_new = (s_in[p] * jnp.exp(c[n, p]["glast"])
                         + jnp.where(same_head, _dot(c[n, p]["kdec"].T, u[p]), 0.0))
                if chained:
                    s_scr[p] = s_new
                else:
                    sout_ref[n, p] = s_new
            yield
            for p in pairs:
                cols = slice(p * LANES, (p + 1) * LANES)
                ms = _dot(o[p] * o[p], hsum_ref[...]) * (1.0 / HEAD_DIM)
                bz = bz_ref[rows, cols]
                o_ref[rows, cols] = o[p] * lax.rsqrt(ms + 1e-6) * gb_ref[...] * (bz * _sigmoid(bz))
            yield

    groups = [list(range(g0, min(g0 + GDN_GROUP, nb))) for g0 in range(0, nb, GDN_GROUP)]
    pending = iter(())
    for grp in groups:
        for _ in solve_stages(grp):
            next(pending, None)
        for _ in pending:
            pass
        pending = state_stages(grp)
    for _ in pending:
        pass

    if chained:
        @pl.when(t == pl.num_programs(1) - 1)
        def _():
            sout_ref[...] = s_scr[...]


def _gdn(qb, kb, vb, bz, elem, s0, lw, *, nb):
    b, t, wb = qb.shape
    npair = wb // LANES
    chained = t > GDN_BLOCK
    if not chained:
        qb, kb, vb, bz, elem = (a.reshape(1, b * t, a.shape[-1]) for a in (qb, kb, vb, bz, elem))
        nb = _pick(b, (GDN_TILE_BLOCKS, 2, 1))
    rows, total = qb.shape[:2]
    tile = nb * GDN_BLOCK
    tok = lambda w: pl.BlockSpec((None, tile, w), lambda i, j: (i, j, 0))
    if chained:
        st = pl.BlockSpec((None, npair, LANES, LANES), lambda i, j: (i, 0, 0, 0))
    else:
        st = pl.BlockSpec((nb, npair, LANES, LANES), lambda i, j: (j, 0, 0, 0))
    consts = [lw["g_b_pair"], lw["esel"], lw["hsum128"]]
    ob, s_new = pl.pallas_call(
        functools.partial(_gdn_kernel, nb=nb, nh=lw["nh"], chained=chained),
        grid=(rows, total // tile),
        in_specs=[tok(wb), tok(wb), tok(wb), tok(wb), tok(LANES), st] + [_const_spec(c.shape) for c in consts],
        out_specs=[tok(wb), st],
        out_shape=[jax.ShapeDtypeStruct((rows, total, wb), F32),
                   jax.ShapeDtypeStruct((b, npair, LANES, LANES), F32)],
        scratch_shapes=[pltpu.VMEM((npair, LANES, LANES), F32)],
        compiler_params=_params("arbitrary", "arbitrary"),
        name="gdn",
    )(qb, kb, vb, bz, elem, s0, *consts)
    return ob.reshape(b, t, wb), s_new


def _outffn_kernel(oa_ref, ob_ref, oc_ref, x_ref, ga_ref, wout_ref, gpm_ref, gpf_ref, wfi_ref, wfo_ref,
                   gpo_ref, y_ref, *, dff, nsub):
    r = x_ref.shape[0] // nsub
    rows = [slice(i * r, (i + 1) * r) for i in range(nsub)]
    cat = [jnp.concatenate([_rms(oa_ref[rs, :], ga_ref[...]), ob_ref[rs, :], oc_ref[rs, :]], axis=-1).astype(BF16)
           for rs in rows]
    m = [jnp.dot(c, wout_ref[...], preferred_element_type=F32) for c in cat]
    x1 = [x_ref[rs, :] + _rms(mi, gpm_ref[...]) for rs, mi in zip(rows, m)]
    h = [_rms(xi, gpf_ref[...]).astype(BF16) for xi in x1]
    gu = [jnp.dot(hi, wfi_ref[...], preferred_element_type=F32) for hi in h]
    a = [(g[:, :dff] * _sigmoid(g[:, :dff]) * g[:, dff:]).astype(BF16) for g in gu]
    f = [jnp.dot(ai, wfo_ref[...], preferred_element_type=F32) for ai in a]
    for rs, xi, fi in zip(rows, x1, f):
        y_ref[rs, :] = xi + _rms(fi, gpo_ref[...])


def _outffn(oa, ob, oc, x, lw, *, tm):
    n, d = x.shape
    dff = lw["w_ffn_out"].shape[0]
    tok = lambda w: pl.BlockSpec((tm, w), lambda i: (i, 0))
    consts = [lw["g_a_out"], lw["w_out"], lw["g_post_mix"], lw["g_pre_ffn"], lw["w_ffn_in"],
              lw["w_ffn_out"], lw["g_post_ffn"]]
    return pl.pallas_call(
        functools.partial(_outffn_kernel, dff=dff, nsub=2 if tm % (4 * SUBLANES) == 0 else 1),
        grid=(n // tm,),
        in_specs=[tok(oa.shape[1]), tok(ob.shape[1]), tok(oc.shape[1]), tok(d)]
                 + [_const_spec(c.shape) for c in consts],
        out_specs=tok(d),
        out_shape=jax.ShapeDtypeStruct((n, d), F32),
        compiler_params=_params("arbitrary"),
        name="outffn",
    )(oa, ob, oc, x, *consts)


def _block_ones(width):
    idx = jnp.arange(width) // HEAD_DIM
    return (idx[:, None] == idx[None, :]).astype(BF16)


def _layer_weights(l, prm, cms):
    w_in = prm["w_in"][l]
    nh = prm["b_f"].shape[1]
    wa = nh * HEAD_DIM
    wb = prm["a_log"].shape[1] * HEAD_DIM
    wc = prm["g_cv"].shape[1]
    ng = prm["w_s"].shape[1]
    assert prm["a_log"].shape[1] == nh and wa % LANES == 0 and wc % LANES == 0 and AUG * nh <= LANES
    sizes = (wa, wa, wa, nh, 3 * wb, nh, nh, wb, wc, wc)
    offs = [0]
    for sz in sizes:
        offs.append(offs[-1] + sz)
    w_in_t = w_in.T
    col = lambda i: w_in_t[offs[i]:offs[i + 1]]
    w_big = jnp.concatenate([col(4), col(8), col(9), col(0), col(1), col(2), col(7)], axis=0).astype(BF16)
    w_small = jnp.concatenate([col(3), col(5), col(6), jnp.zeros((LANES - 3 * nh, w_in.shape[0]), F32)],
                              axis=0).astype(BF16)
    zpad = jnp.zeros((LANES - 2 * nh,), F32)
    sp = jnp.zeros((SUBLANES, LANES), F32)
    sp = sp.at[0].set(jnp.concatenate([prm["b_f"][l], prm["dt_bias"][l], zpad]))
    sp = sp.at[1].set(jnp.concatenate([jnp.zeros((nh,), F32), prm["a_log"][l], zpad]))
    hl = jnp.arange(nh) * AUG
    sp = sp.at[2, (hl[:, None] + jnp.arange(3, 6)[None, :]).reshape(-1)].set(1.0)
    sp = sp.at[3, (hl[:, None] + jnp.arange(0, 3)[None, :]).reshape(-1)].set(1.0)
    pmat = jnp.zeros((3 * LANES, 2 * LANES), F32)
    for piece in range(3):
        pmat = pmat.at[piece * LANES + jnp.arange(nh), hl + piece].set(1.0)
        pmat = pmat.at[piece * LANES + jnp.arange(nh), LANES + hl + 3 + piece].set(1.0)
    row = lambda v: v.reshape(1, -1)
    ws_cat, bs_full = {}, {}
    for cm in cms:
        pos = jnp.arange(cm) // HEAD_DIM
        w = jnp.where(pos[None, :] <= pos[:, None], prm["w_s"][l][:, :cm, :cm], 0.0)
        pairs = [jnp.concatenate([w[2 * pp], w[2 * pp + 1]], axis=1) for pp in range(ng // 2)]
        kpad = max(LANES - 2 * cm, 0)
        ws_cat[cm] = jnp.pad(jnp.stack(pairs), ((0, 0), (0, 0), (0, kpad))).astype(BF16)
        bs_full[cm] = jnp.repeat(prm["b_s"][l][:, :cm].T, wc // ng, axis=1)
    src = jnp.arange(LANES)[:, None]
    dst = jnp.arange(wb)[None, :] // HEAD_DIM
    esel = jnp.concatenate([src == nh + dst, src == 2 * nh + dst], axis=1).astype(BF16)
    return dict(
        nh=nh, wa=wa, wb=wb, wc=wc,
        g_pre_mix=row(prm["g_pre_mix"][l]), w_big=w_big, w_small=w_small, sp=sp, conv_w=prm["conv_w"][l],
        g_cv=row(prm["g_cv"][l]), b_cv=row(prm["b_cv"][l]), ws_cat=ws_cat, bs_full=bs_full,
        g_c_out=row(prm["g_c_out"][l]), hsum=_block_ones(2 * LANES), hsum128=_block_ones(LANES), pmat=pmat.astype(BF16),
        g_b_pair=row(jnp.tile(prm["g_b_out"][l], LANES // HEAD_DIM)), esel=esel,
        g_a_out=row(prm["g_a_out"][l]), w_out=prm["w_out"][l].astype(BF16),
        g_post_mix=row(prm["g_post_mix"][l]), g_pre_ffn=row(prm["g_pre_ffn"][l]),
        w_ffn_in=prm["w_ffn_in"][l].astype(BF16), w_ffn_out=prm["w_ffn_out"][l].astype(BF16),
        g_post_ffn=row(prm["g_post_ffn"][l]))


def _pair_state(s):
    b, h, dk, dv = s.shape
    s = s.reshape(b, h // 2, 2, dk, dv)
    z = jnp.zeros_like(s[:, :, 0])
    top = jnp.concatenate([s[:, :, 0], z], axis=-1)
    bot = jnp.concatenate([z, s[:, :, 1]], axis=-1)
    return jnp.concatenate([top, bot], axis=-2)


def _unpair_state(sp):
    b, hp, _, _ = sp.shape
    s0 = sp[:, :, :HEAD_DIM, :HEAD_DIM]
    s1 = sp[:, :, HEAD_DIM:, HEAD_DIM:]
    return jnp.stack([s0, s1], axis=2).reshape(b, 2 * hp, HEAD_DIM, HEAD_DIM)


def _head_rows(cum, nh):
    b, t, _ = cum.shape
    return jnp.transpose(cum[:, :, :nh], (0, 2, 1)).reshape(b, nh // 2, 2, t)


def _pick(n, prefs):
    for c in prefs:
        if n % c == 0:
            return c
    return n


def _layer(x, lw, conv_prev, s0, cache, *, cm, layer, depth, kv_prev):
    b, t, d = x.shape
    nh, wb = lw["nh"], lw["wb"]
    pj = _inproj(x, lw, conv_prev, cm=cm, layer=layer, depth=depth, kv_prev=kv_prev)

    if cache is None:
        tq = _pick(t, (ATTN_TQ, 256, 128))
        oa = _attn_prompt(pj["qaug"], pj["kaug"], pj["vab"], tq=tq, kc=min(ATTN_KC, tq), look=ATTN_LOOK)
    else:
        ck, cv, clogf_t = cache
        _, _, bs, past = clogf_t.shape
        excl = _exclusive_suffix_sum(clogf_t, layer)
        rrow = jnp.transpose(excl, (1, 0, 2)).reshape(bs, nh // 2, 2, past)
        crow = jnp.pad(_head_rows(pj["cum"], nh), ((0, 0), (0, 0), (0, 0), (0, LANES - t)))
        oa = _attn_sample(pj["qaug"], pj["kaug"], pj["vab"], pj["cum"], ck, cv, layer, rrow, crow)

    tp = -(-t // GDN_BLOCK) * GDN_BLOCK
    padt = lambda a: a if tp == t else jnp.pad(a, ((0, 0), (0, tp - t), (0, 0)))
    nb = _pick(tp // GDN_BLOCK, (GDN_TILE_BLOCKS, 2, 1))
    ob, s_new = _gdn(padt(pj["qb"]), padt(pj["kb"]), padt(pj["vb"]), padt(pj["bz"]), padt(pj["elem"]),
                     _pair_state(s0), lw, nb=nb)
    ob = ob[:, :t]

    n = b * t
    y = _outffn(oa.reshape(n, -1), ob.reshape(n, -1), pj["oc"].reshape(n, -1), x.reshape(n, d), lw,
                tm=_pick(n, (512, 256, 128, 64, 32, 16)))
    if pj["kv_time_minor"]:
        new_kv = (pj["ka"], pj["va"])
    else:
        new_kv = (pj["ka"].reshape(b, t, nh, HEAD_DIM), pj["va"].reshape(b, t, nh, HEAD_DIM))
    state = (new_kv[0], new_kv[1], pj["elem"][:, :, :nh],
             pj["conv_new"], _unpair_state(s_new), pj["vn"])
    return y.reshape(b, t, d), state, pj["kv_time_minor"]


def kernel(x_prompt, x_sample, cache_a_k, cache_a_v, cache_a_logf, state_b_conv, state_b_S, g_pre_mix, w_in, b_f, conv_w, a_log, dt_bias, g_b_out, g_a_out, g_cv, b_cv, w_s, b_s, g_c_out, w_out, g_post_mix, g_pre_ffn, w_ffn_in, w_ffn_out, g_post_ffn):
    prm = dict(g_pre_mix=g_pre_mix, w_in=w_in, b_f=b_f, conv_w=conv_w, a_log=a_log, dt_bias=dt_bias,
               g_b_out=g_b_out, g_a_out=g_a_out, g_cv=g_cv, b_cv=b_cv, w_s=w_s, b_s=b_s, g_c_out=g_c_out,
               w_out=w_out, g_post_mix=g_post_mix, g_pre_ffn=g_pre_ffn, w_ffn_in=w_ffn_in,
               w_ffn_out=w_ffn_out, g_post_ffn=g_post_ffn)
    depth = w_in.shape[0]
    bp, sp_len, _ = x_prompt.shape
    n_new = x_sample.shape[1]
    cm_p = w_s.shape[2]
    assert sp_len % cm_p == 0 and sp_len % GDN_BLOCK == 0 and n_new <= HEAD_DIM and n_new % SUBLANES == 0
    kw1 = conv_w.shape[1] - 1
    nhb = a_log.shape[1]
    yp, ys = x_prompt, x_sample
    outs_p, outs_s = [], []
    cache_kt = jnp.transpose(cache_a_k, (0, 1, 3, 4, 2))
    cache_vt = jnp.transpose(cache_a_v, (0, 1, 3, 4, 2))
    clogf_t = jnp.transpose(cache_a_logf, (0, 3, 1, 2))
    for l in range(depth):
        lw = _layer_weights(l, prm, (cm_p, n_new))
        conv0 = jnp.zeros((bp, kw1, conv_w.shape[2]), F32)
        s0 = jnp.zeros((bp, nhb, HEAD_DIM, HEAD_DIM), F32)
        kv_p = (outs_p[-1][0], outs_p[-1][1]) if outs_p else None
        kv_s = (outs_s[-1][0], outs_s[-1][1]) if outs_s else None
        yp, st_p, shared_p = _layer(yp, lw, conv0, s0, None, cm=cm_p, layer=l, depth=depth, kv_prev=kv_p)
        ys, st_s, shared_s = _layer(ys, lw, state_b_conv[l], state_b_S[l], (cache_kt, cache_vt, clogf_t),
                                    cm=n_new, layer=l, depth=depth, kv_prev=kv_s)
        outs_p.append(st_p)
        outs_s.append(st_s)
    stk = lambda outs, i: jnp.stack([o[i] for o in outs], axis=0)

    def new_cache(outs, i, shared):
        if not shared:
            return stk(outs, i)
        buf = outs[-1][i]
        dp, b, _, t = buf.shape
        return jnp.transpose(buf.reshape(dp, b, -1, HEAD_DIM, t), (0, 1, 4, 2, 3))

    return (yp, ys, new_cache(outs_p, 0, shared_p), new_cache(outs_p, 1, shared_p),
            stk(outs_p, 2), stk(outs_p, 3), stk(outs_p, 4),
            new_cache(outs_s, 0, shared_s), new_cache(outs_s, 1, shared_s),
            stk(outs_s, 2), stk(outs_s, 3), stk(outs_s, 4), stk(outs_s, 5))
```

```python
import functools

import jax
import jax.numpy as jnp
from jax import lax
from jax.experimental import pallas as pl
from jax.experimental.pallas import tpu as pltpu

F32 = jnp.float32
BF16 = jnp.bfloat16

LANES = 128
SUBLANES = 8
HEAD_DIM = 64
GDN_BLOCK = 128
GDN_GROUP = 2
GDN_TILE_BLOCKS = 4
ATTN_TQ = 1024
ATTN_KC = 256
ATTN_LOOK = 2
VMEM_LIMIT = 56 * 1024 * 1024
NEG_INF = float("-inf")
LOG2E = 1.4426950408889634
AUG = 16


def _dot(a, b):
    return jnp.dot(a.astype(BF16), b.astype(BF16), preferred_element_type=F32)


def _dot_nt(a, b):
    return lax.dot_general(a.astype(BF16), b.astype(BF16), (((1,), (1,)), ((), ())),
                           preferred_element_type=F32)


def _dot_select_exact(x, sel):
    hi = x.astype(BF16)
    r1 = x - hi.astype(F32)
    mid = r1.astype(BF16)
    lo = (r1 - mid.astype(F32)).astype(BF16)
    d = lambda p: jnp.dot(p, sel, preferred_element_type=F32)
    return (d(hi) + d(mid)) + d(lo)


def _rms(x, g, eps=1e-6):
    return x * lax.rsqrt(jnp.mean(x * x, axis=-1, keepdims=True) + eps) * g


def _sigmoid(x):
    return 1.0 / (1.0 + jnp.exp(-x))


def _seg_cumsum(v, seg):
    row = lax.broadcasted_iota(jnp.int32, v.shape, 0)
    pos = jnp.bitwise_and(row, seg - 1)
    s = 1
    while s < seg:
        v = v + jnp.where(pos >= s, pltpu.roll(v, s, 0), 0.0)
        s *= 2
    return v


def _const_spec(shape):
    nd = len(shape)
    return pl.BlockSpec(shape, lambda *_: (0,) * nd, pipeline_mode=pl.Buffered(1))


def _params(*sem):
    return pltpu.CompilerParams(dimension_semantics=sem, vmem_limit_bytes=VMEM_LIMIT)


def _inproj_kernel(x_ref, gpre_ref, wbig_ref, wsm_ref, sp_ref, convw_ref, convinit_ref, gcv_ref,
                   bcv_ref, ws_ref, bs_ref, gco_ref, hsum_ref, pmat_ref,
                   qaug_ref, ka_ref, va_ref, kaug_ref, vab_ref, elem_ref, cum_ref, qb_ref, kb_ref,
                   vb_ref, bz_ref, oc_ref, vn_ref, ytail_ref,
                   carry_conv, carry_cum, *, tm, cm, nh, wa, wb, wc, scale, kv_time_minor, nsub, seg):
    if not seg:
        @pl.when(pl.program_id(1) == 0)
        def _():
            carry_cum[...] = jnp.zeros_like(carry_cum)
            carry_conv[...] = convinit_ref[...]

    r = tm // nsub
    o_c = 3 * wb
    o_a = o_c + 2 * wc

    def project(rs):
        h = _rms(x_ref[rs, :], gpre_ref[...]).astype(BF16)
        proj = lambda w: lax.dot_general(h, w, (((1,), (1,)), ((), ())), preferred_element_type=F32)
        return (proj(wbig_ref[:o_c, :]),
                proj(wbig_ref[o_c:o_a, :]),
                proj(wsm_ref[...]),
                proj(wbig_ref[o_a:, :]))

    def finish(rs, y, zc, zs, za, prev, cum_in):
        ka = za[:, wa:2 * wa]
        va = za[:, 2 * wa:3 * wa]
        if kv_time_minor:
            ka_ref[:, rs] = ka.T
            va_ref[:, rs] = va.T
        else:
            ka_ref[rs, :] = ka
            va_ref[rs, :] = va
        vab_ref[rs, :] = va.astype(BF16)
        bz_ref[rs, :] = za[:, 3 * wa:]

        lane = lax.broadcasted_iota(jnp.int32, (r, LANES), 1)
        zb = zs + sp_ref[0:1, :]
        soft_tail = jnp.log1p(jnp.exp(-jnp.abs(zb)))
        logf = -(jnp.maximum(-zb, 0.0) + soft_tail)
        gl = -jnp.exp(sp_ref[1:2, :]) * (jnp.maximum(zb, 0.0) + soft_tail)
        beta = _sigmoid(zs)
        elem = jnp.where(lane < nh, logf, jnp.where(lane < 2 * nh, gl, jnp.where(lane < 3 * nh, beta, 0.0)))
        elem_ref[rs, :] = elem
        cum = _seg_cumsum(elem, seg) if seg else _seg_cumsum(elem, r) + cum_in
        cum_ref[rs, :] = cum

        c2 = jnp.where(lane < nh, cum * LOG2E, 0.0)
        hi = c2.astype(BF16)
        r1 = c2 - hi.astype(F32)
        mid = r1.astype(BF16)
        lo = (r1 - mid.astype(F32)).astype(BF16)
        placed = jnp.dot(jnp.concatenate([hi, mid, lo], axis=1), pmat_ref[...], preferred_element_type=F32)
        augq = (placed[:, :LANES] + sp_ref[2:3, :]).astype(BF16)
        augk = (sp_ref[3:4, :] - placed[:, LANES:]).astype(BF16)
        qs = (za[:, :wa] * (scale * LOG2E)).astype(BF16)
        ks = ka.astype(BF16)
        qaug_ref[rs, :] = jnp.concatenate(
            [a for j in range(0, wa, LANES) for a in (qs[:, j:j + LANES], augq)], axis=1)
        kaug_ref[rs, :] = jnp.concatenate(
            [a for j in range(0, wa, LANES) for a in (ks[:, j:j + LANES], augk)], axis=1)

        row8 = lax.broadcasted_iota(jnp.int32, prev.shape, 0)
        pos = jnp.bitwise_and(lax.broadcasted_iota(jnp.int32, y.shape, 0), max(seg, 1) - 1)
        kw = convw_ref.shape[0]
        acc = y * convw_ref[kw - 1:kw, :]
        for k in range(1, kw):
            yk = pltpu.roll(y, k, 0)
            if seg:
                yk = jnp.where(pos < k, convinit_ref[k - 1, rs, :], yk)
            else:
                top = jnp.where(row8 < k, pltpu.roll(prev, k, 0), yk[0:SUBLANES])
                yk = jnp.concatenate([top, yk[SUBLANES:]], axis=0)
            acc = acc + yk * convw_ref[kw - 1 - k:kw - k, :]
        if seg:
            ytail_ref[rs, :] = y
        yc = acc * _sigmoid(acc)
        qb = yc[:, :wb]
        kb = yc[:, wb:2 * wb]
        sq = jnp.concatenate([qb * qb, kb * kb], axis=-1).astype(BF16)
        hw = hsum_ref.shape[0]
        ss = jnp.concatenate([jnp.dot(sq[:, j:j + hw], hsum_ref[...], preferred_element_type=F32)
                              for j in range(0, 2 * wb, hw)], axis=-1)
        qb_ref[rs, :] = qb * lax.rsqrt(ss[:, :wb] + 1e-6) * scale
        kb_ref[rs, :] = kb * lax.rsqrt(ss[:, wb:] + 1e-6)
        vb_ref[rs, :] = yc[:, 2 * wb:]

        u = jax.nn.gelu(zc[:, :wc])
        gv = jax.nn.gelu(zc[:, wc:])
        mu = jnp.mean(gv, axis=-1, keepdims=True)
        var = jnp.mean(jnp.square(gv - mu), axis=-1, keepdims=True)
        vn = (gv - mu) * lax.rsqrt(var + 1e-5) * gcv_ref[...] + bcv_ref[...]
        vn_ref[rs, :] = vn
        first = lax.broadcasted_iota(jnp.int32, (cm, LANES), 1) < HEAD_DIM
        kpad = ws_ref.shape[2] - 2 * cm
        rows = []
        for c in range(r // cm):
            vc = vn[c * cm:(c + 1) * cm]
            cols = []
            for pp in range(wc // LANES):
                vp = vc[:, pp * LANES:(pp + 1) * LANES]
                parts = [jnp.where(first, vp, 0.0), jnp.where(first, 0.0, vp)]
                if kpad:
                    parts.append(jnp.zeros((kpad, LANES), F32))
                cols.append(_dot(ws_ref[pp], jnp.concatenate(parts, axis=0)))
            s = jnp.concatenate(cols, axis=-1) + bs_ref[...]
            rows.append(u[c * cm:(c + 1) * cm] * s)
        oc = rows[0] if len(rows) == 1 else jnp.concatenate(rows, axis=0)
        oc_ref[rs, :] = _rms(oc, gco_ref[...])
        return y[r - SUBLANES:r], cum[r - 1:r, :]

    subs = [slice(i * r, (i + 1) * r) for i in range(nsub)]
    projected = [project(rs) for rs in subs]
    if seg:
        prev, cum_in = jnp.zeros(carry_conv.shape, F32), jnp.zeros(carry_cum.shape, F32)
    else:
        prev, cum_in = carry_conv[...], carry_cum[...]
    for rs, z in zip(subs, projected):
        prev, cum_in = finish(rs, *z, prev, cum_in)
    if not seg:
        carry_conv[...] = prev
        carry_cum[...] = cum_in
        ytail_ref[...] = prev


def _inproj_kernel_inplace(*refs, n_in, **kw):
    return _inproj_kernel(*refs[:n_in], *refs[n_in + 2:], **kw)


def _inproj(x, lw, conv_prev, *, cm, layer, depth, kv_prev):
    b0, t0, d = x.shape
    wa, wb, wc, nh = lw["wa"], lw["wb"], lw["wc"], lw["nh"]
    kw1 = conv_prev.shape[1]
    seg = 0 if t0 >= LANES else t0
    if seg:
        assert seg & (seg - 1) == 0 and seg >= SUBLANES
        x = x.reshape(1, b0 * t0, d)
        tm = _pick(b0 * t0, (512, 256, 128, 64, 32, 16))
        assert tm % seg == 0
        conv_init = jnp.stack([jnp.pad(conv_prev[:, kw1 - k:], ((0, 0), (0, t0 - k), (0, 0))) for k in range(1, kw1 + 1)])
        conv_init = conv_init.reshape(kw1, b0 * t0, -1)
    else:
        tm = _pick(t0, (512, 256, 128))
        conv_init = jnp.pad(conv_prev, ((0, 0), (SUBLANES - kw1, 0), (0, 0)))
    b, t, _ = x.shape
    nt = t // tm
    kv_time_minor = not seg
    tok = lambda w: pl.BlockSpec((None, tm, w), lambda i, j: (i, j, 0))
    per_b = lambda r, w: pl.BlockSpec((None, r, w), lambda i, j: (i, 0, 0))
    outs = [("qaug", 2 * wa, BF16), ("ka", wa, F32), ("va", wa, F32), ("kaug", 2 * wa, BF16), ("vab", wa, BF16),
            ("elem", LANES, F32), ("cum", LANES, F32), ("qb", wb, F32), ("kb", wb, F32), ("vb", wb, F32),
            ("bz", wb, F32), ("oc", wc, F32), ("vn", wc, F32)]
    out_shape = [jax.ShapeDtypeStruct((b, t, w), dt) for _, w, dt in outs]
    out_specs = [tok(w) for _, w, _ in outs]
    if kv_time_minor:
        for k in (1, 2):
            out_shape[k] = jax.ShapeDtypeStruct((depth, b, wa, t), F32)
            out_specs[k] = pl.BlockSpec((None, None, wa, tm), lambda i, j: (layer, i, 0, j))
    if seg:
        out_shape.append(jax.ShapeDtypeStruct((b, t, 3 * wb), F32))
        out_specs.append(tok(3 * wb))
        init_spec = pl.BlockSpec((kw1, tm, 3 * wb), lambda i, j: (0, j, 0))
    else:
        out_shape.append(jax.ShapeDtypeStruct((b, SUBLANES, 3 * wb), F32))
        out_specs.append(per_b(SUBLANES, 3 * wb))
        init_spec = per_b(SUBLANES, 3 * wb)
    consts = [lw["g_pre_mix"], lw["w_big"], lw["w_small"], lw["sp"], lw["conv_w"]]
    consts2 = [lw["g_cv"], lw["b_cv"], lw["ws_cat"][cm], lw["bs_full"][cm], lw["g_c_out"], lw["hsum"], lw["pmat"]]
    nsub = 2 if tm % (2 * max(cm, LANES)) == 0 else 1
    kw = dict(tm=tm, cm=cm, nh=nh, wa=wa, wb=wb, wc=wc, scale=HEAD_DIM ** -0.5, kv_time_minor=kv_time_minor,
              nsub=nsub, seg=seg)
    in_specs = ([tok(d)] + [_const_spec(c.shape) for c in consts] + [init_spec]
                + [_const_spec(c.shape) for c in consts2])
    args = [x, *consts, conv_init, *consts2]
    inplace = kv_time_minor and kv_prev is not None
    if inplace:
        kern = functools.partial(_inproj_kernel_inplace, n_in=len(args), **kw)
        aliases = {len(args): 1, len(args) + 1: 2}
        in_specs = in_specs + [pl.BlockSpec(memory_space=pl.ANY)] * 2
        args = args + list(kv_prev)
    else:
        kern = functools.partial(_inproj_kernel, **kw)
        aliases = {}
    res = pl.pallas_call(
        kern,
        grid=(b, nt),
        in_specs=in_specs,
        out_specs=out_specs,
        out_shape=out_shape,
        input_output_aliases=aliases,
        scratch_shapes=[pltpu.VMEM((SUBLANES, 3 * wb), F32), pltpu.VMEM((1, LANES), F32)],
        compiler_params=_params("arbitrary", "arbitrary"),
        name="inproj",
    )(*args)
    named = {n: r for (n, _, _), r in zip(outs, res[:-1])}
    if seg:
        named = {n: r.reshape(b0, t0, r.shape[-1]) for n, r in named.items()}
        named["conv_new"] = res[-1].reshape(b0, t0, -1)[:, t0 - kw1:, :]
    else:
        named["conv_new"] = res[-1][:, SUBLANES - kw1:, :]
    named["kv_time_minor"] = kv_time_minor
    return named


def _attn_kernel(qt_ref, k_ref, vt_ref, o_ref, *, tq, kc, look):
    p = pl.program_id(1)
    i = pl.program_id(2)
    qt = qt_ref[...]
    rowi = lax.broadcasted_iota(jnp.int32, qt.shape, 0)
    zero = jnp.zeros_like(qt)
    qts = []
    for e in range(2):
        a0 = LANES + AUG * (2 * p + e)
        keep = ((rowi >= e * HEAD_DIM) & (rowi < (e + 1) * HEAD_DIM)) | ((rowi >= a0) & (rowi < a0 + AUG))
        qts.append(jnp.where(keep, qt, zero))
    qpos = i * tq + lax.broadcasted_iota(jnp.int32, (kc, tq), 1)
    kofs = lax.broadcasted_iota(jnp.int32, (kc, tq), 0)
    ones = jnp.ones((2 * SUBLANES, kc), BF16)
    units = [(c, e) for c in range(tq // kc) for e in range(2)]

    def scores(j, c, e, masked):
        k0 = pl.multiple_of(j * tq + c * kc, kc)
        lo = c * kc if masked else 0
        s = jnp.dot(k_ref[pl.ds(k0, kc), :], qts[e][:, lo:], preferred_element_type=F32)
        return jnp.concatenate([jnp.full((kc, lo), NEG_INF, F32), s], axis=1) if lo else s

    def fold(j, c, e, s, st, masked):
        m, l, acc = st
        k0 = pl.multiple_of(j * tq + c * kc, kc)
        lo = c * kc if masked else 0
        if masked:
            s = jnp.where(k0 + kofs <= qpos, s, NEG_INF)
        m_new = jnp.maximum(m, jnp.max(s, axis=0, keepdims=True))
        alpha = jnp.exp2(m - m_new)
        pt = jnp.exp2(s - m_new).astype(BF16)
        vt = jnp.concatenate([vt_ref[e * HEAD_DIM:(e + 1) * HEAD_DIM, pl.ds(k0, kc)], ones], axis=0)
        r = jnp.dot(vt, pt[:, lo:], preferred_element_type=F32)
        if lo:
            r = jnp.concatenate([jnp.zeros((r.shape[0], lo), F32), r], axis=1)
        return m_new, alpha * l + r[HEAD_DIM:HEAD_DIM + 1], alpha * acc + r[:HEAD_DIM]

    def run(blocks, state):
        state = list(state)
        todo = [(j, c, e, masked) for j, masked in blocks for c, e in units]
        pend = {}
        for k in range(min(look, len(todo))):
            pend[k] = scores(*todo[k])
        for k, (j, c, e, masked) in enumerate(todo):
            if k + look < len(todo):
                pend[k + look] = scores(*todo[k + look])
            state[e] = fold(j, c, e, pend.pop(k), state[e], masked)
        return tuple(state)

    st0 = (jnp.full((1, tq), NEG_INF, F32), jnp.zeros((1, tq), F32), jnp.zeros((HEAD_DIM, tq), F32))
    state = lax.fori_loop(0, i // 2, lambda t, s: run([(2 * t, False), (2 * t + 1, False)], s), (st0, st0))
    state = lax.cond(i % 2 == 1,
                     lambda s: run([(i - 1, False), (i, True)], s),
                     lambda s: run([(i, True)], s), state)
    ot = jnp.concatenate([acc / l for _, l, acc in state], axis=0)
    o_ref[...] = ot.T


def _attn_prompt(qaug, kaug, vab, *, tq, kc, look):
    b, s, wa = vab.shape
    npair = wa // LANES
    qt = jnp.transpose(qaug, (0, 2, 1))
    vt = jnp.transpose(vab, (0, 2, 1))
    kern = functools.partial(_attn_kernel, tq=tq, kc=kc, look=look)
    return pl.pallas_call(
        kern,
        grid=(b, npair, s // tq),
        in_specs=[pl.BlockSpec((None, 2 * LANES, tq), lambda bi, p, i: (bi, p, i)),
                  pl.BlockSpec((None, s, 2 * LANES), lambda bi, p, i: (bi, 0, p)),
                  pl.BlockSpec((None, LANES, s), lambda bi, p, i: (bi, p, 0))],
        out_specs=pl.BlockSpec((None, tq, LANES), lambda bi, p, i: (bi, i, p)),
        out_shape=jax.ShapeDtypeStruct((b, s, wa), F32),
        compiler_params=_params("arbitrary", "arbitrary", "arbitrary"),
        name="attn_prompt",
    )(qt, kaug, vt)


def _attn_sample_kernel(q_ref, kc_ref, vc_ref, kn_ref, vn_ref, cum_ref, rrow_ref, crow_ref, o_ref, *, n):
    p = pl.program_id(1)
    q = q_ref[:, :LANES]
    lane = lax.broadcasted_iota(jnp.int32, (n, LANES), 1)
    first = lane < HEAD_DIM
    zero = jnp.zeros_like(q)
    past = kc_ref.shape[-1]
    kc = kc_ref[...].reshape(LANES, past).astype(BF16)
    vc = vc_ref[...].reshape(LANES, past).astype(BF16)
    pad = jnp.zeros((LANES - n, LANES), BF16)
    kn = jnp.concatenate([kn_ref[:, :LANES], pad], axis=0)
    vn = jnp.concatenate([vn_ref[...], pad], axis=0)
    cum = cum_ref[...]
    qm = jnp.concatenate([jnp.where(first, q, zero), jnp.where(first, zero, q)], axis=0)
    cq = jnp.concatenate([jnp.sum(jnp.where(lane == 2 * p + e, cum, 0.0), axis=-1, keepdims=True)
                          for e in range(2)], axis=0)
    top = lax.broadcasted_iota(jnp.int32, (2 * n, 1), 0) < n
    rrow = jnp.where(top, rrow_ref[0:1, :], rrow_ref[1:2, :])
    crow = jnp.where(top, crow_ref[0:1, :], crow_ref[1:2, :])
    qrow = lax.broadcasted_iota(jnp.int32, (2 * n, LANES), 0)
    causal = lax.broadcasted_iota(jnp.int32, (2 * n, LANES), 1) <= jnp.where(qrow < n, qrow, qrow - n)
    sc = _dot(qm, kc) + LOG2E * (cq + rrow)
    sn = jnp.where(causal, _dot_nt(qm, kn) + LOG2E * (cq - crow), NEG_INF)
    m = jnp.maximum(jnp.max(sc, axis=-1, keepdims=True), jnp.max(sn, axis=-1, keepdims=True))
    pc = jnp.exp2(sc - m)
    pn = jnp.exp2(sn - m)
    l = jnp.sum(pc, axis=-1, keepdims=True) + jnp.sum(pn, axis=-1, keepdims=True)
    o = (_dot_nt(pc, vc) + _dot(pn, vn)) / l
    o_ref[...] = jnp.where(first, o[:n], o[n:])


def _attn_sample(qaug, kaug, vab, cum, cache_kt, cache_vt, layer, rrow, crow):
    b, n, wa = vab.shape
    past = cache_kt.shape[-1]
    npair = wa // LANES
    new = lambda w: pl.BlockSpec((None, n, w), lambda bi, p: (bi, 0, p))
    old = lambda: pl.BlockSpec((None, None, 2, HEAD_DIM, past), lambda bi, p: (layer, bi, p, 0, 0))
    return pl.pallas_call(
        functools.partial(_attn_sample_kernel, n=n),
        grid=(b, npair),
        in_specs=[new(2 * LANES), old(), old(), new(2 * LANES), new(LANES),
                  pl.BlockSpec((None, n, LANES), lambda bi, p: (bi, 0, 0)),
                  pl.BlockSpec((None, None, 2, past), lambda bi, p: (bi, p, 0, 0)),
                  pl.BlockSpec((None, None, 2, LANES), lambda bi, p: (bi, p, 0, 0))],
        out_specs=new(LANES),
        out_shape=jax.ShapeDtypeStruct((b, n, wa), F32),
        compiler_params=_params("arbitrary", "arbitrary"),
        name="attn_sample",
    )(qaug, cache_kt, cache_vt, kaug, vab, cum, rrow, crow)


def _suffix_kernel(x_ref, o_ref):
    v = x_ref[...]
    n = v.shape[1]
    lane = lax.broadcasted_iota(jnp.int32, v.shape, 1)
    s = 1
    while s < n:
        v = v + jnp.where(lane + s < n, pltpu.roll(v, n - s, 1), 0.0)
        s *= 2
    o_ref[...] = jnp.where(lane + 1 < n, pltpu.roll(v, n - 1, 1), 0.0)


def _exclusive_suffix_sum(x, layer):
    _, h, b, p = x.shape
    return pl.pallas_call(
        _suffix_kernel,
        grid=(h,),
        in_specs=[pl.BlockSpec((None, None, b, p), lambda i: (layer, i, 0, 0))],
        out_specs=pl.BlockSpec((None, b, p), lambda i: (i, 0, 0)),
        out_shape=jax.ShapeDtypeStruct((h, b, p), F32),
        compiler_params=_params("arbitrary"),
        name="suffix_sum",
    )(x)


def _gdn_kernel(q_ref, k_ref, v_ref, bz_ref, elem_ref, s0_ref, gb_ref, esel_ref, hsum_ref,
                o_ref, sout_ref, s_scr, *, nb, nh, chained):
    L = GDN_BLOCK
    t = pl.program_id(1)

    if chained:
        @pl.when(t == 0)
        def _():
            s_scr[...] = s0_ref[...]

    lane = lax.broadcasted_iota(jnp.int32, (L, LANES), 1)
    first = lane < HEAD_DIM
    ri = lax.broadcasted_iota(jnp.int32, (L, L), 0)
    ci = lax.broadcasted_iota(jnp.int32, (L, L), 1)
    incl = ci <= ri
    strict = ci < ri
    same_head = (ri < HEAD_DIM) == (ci < HEAD_DIM)
    lane2 = lax.broadcasted_iota(jnp.int32, (L, 2 * L), 1)
    first2 = jnp.bitwise_and(lane2, LANES - 1) < HEAD_DIM
    xor2 = jnp.bitwise_xor(lax.broadcasted_iota(jnp.int32, (L, 2 * L), 0), jnp.bitwise_and(lane2, L - 1))
    zero_ll = jnp.zeros((L, L), BF16)

    def halves(x, sel):
        return jnp.concatenate([jnp.where(sel, x, 0.0), jnp.where(sel, 0.0, x)], axis=0)

    def dot_heads(y, x):
        xb = x.astype(BF16)
        bd = jnp.concatenate([jnp.concatenate([xb[:, :L], zero_ll], axis=1),
                              jnp.concatenate([zero_ll, xb[:, L:]], axis=1)], axis=0)
        return jnp.dot(y.astype(BF16), bd, preferred_element_type=F32)

    npair = s_scr.shape[0]
    wbw = npair * LANES
    c = {}

    def solve_stages(blocks):
        chains = [(n, p) for n in blocks for p in range(npair)]
        ex = {}
        for n in blocks:
            elem = elem_ref[n * L:(n + 1) * L, :]
            gsum = _seg_cumsum(elem, L)
            mixed = jnp.where((lane >= nh) & (lane < 2 * nh), gsum, elem)
            ex[n] = _dot_select_exact(mixed, esel_ref[...])
        yield
        for n, p in chains:
            rows = slice(n * L, (n + 1) * L)
            cols = slice(p * LANES, (p + 1) * LANES)
            g = ex[n][:, cols]
            bt = ex[n][:, wbw + p * LANES: wbw + (p + 1) * LANES]
            kp = k_ref[rows, cols]
            qp = q_ref[rows, cols]
            g_sw = pltpu.roll(g, HEAD_DIM, 1)
            b_sw = pltpu.roll(bt, HEAD_DIM, 1)
            g_t = g.T
            a_parts, qk_parts = [], []
            for e in range(2):
                sel = first if e == 0 else jnp.logical_not(first)
                gcol = jnp.where(sel, g, g_sw)
                bcol = jnp.where(sel, bt, b_sw)
                grow = g_t[e * HEAD_DIM:e * HEAD_DIM + 1, :]
                dec = jnp.exp(jnp.where(incl, gcol - grow, NEG_INF))
                kk = _dot_nt(jnp.where(sel, kp, 0.0), kp)
                qk_parts.append(_dot_nt(jnp.where(sel, qp, 0.0), kp) * dec)
                a_parts.append(jnp.where(strict, bcol * kk * dec, 0.0))
            a_cat = jnp.concatenate(a_parts, axis=1)
            eg = jnp.exp(g)
            glast = g[L - 1:L, :]
            c[n, p] = dict(a=a_cat, qk=jnp.concatenate(qk_parts, axis=1), glast=glast, qg=qp * eg,
                           kdec=kp * jnp.exp(glast - g),
                           r=jnp.concatenate([v_ref[rows, cols] * bt, kp * bt * eg], axis=1),
                           tm1=-jnp.where(xor2 < 2, a_cat, 0.0))
        yield
        s_blk = 2
        while s_blk < L:
            pm = {}
            for key in chains:
                nmat = jnp.where((xor2 >= s_blk) & (xor2 < 2 * s_blk), c[key]["a"], 0.0)
                pm[key] = nmat + dot_heads(c[key]["tm1"], nmat)
            yield
            for key in chains:
                c[key]["tm1"] = c[key]["tm1"] - pm[key] - dot_heads(pm[key], c[key]["tm1"])
            yield
            s_blk *= 2
        for key in chains:
            r = c[key]["r"]
            c[key]["uw"] = r + _dot(c[key]["tm1"], halves(r, first2))
        yield

    def state_stages(blocks):
        pairs = range(npair)
        for n in blocks:
            rows = slice(n * L, (n + 1) * L)
            s_in = [s_scr[p] if chained else s0_ref[n, p] for p in pairs]
            ws = [_dot(jnp.concatenate([c[n, p]["uw"][:, LANES:], c[n, p]["qg"]], axis=0), s_in[p]) for p in pairs]
            yield
            u = [c[n, p]["uw"][:, :LANES] - ws[p][:L] for p in pairs]
            o = [ws[p][L:] + _dot(c[n, p]["qk"], halves(u[p], first)) for p in pairs]
            yield
            for p in pairs:
                s_new = (s_in[p] * jnp.exp(c[n, p]["glast"])
                         + jnp.where(same_head, _dot(c[n, p]["kdec"].T, u[p]), 0.0))
                if chained:
                    s_scr[p] = s_new
                else:
                    sout_ref[n, p] = s_new
            yield
            for p in pairs:
                cols = slice(p * LANES, (p + 1) * LANES)
                ms = _dot(o[p] * o[p], hsum_ref[...]) * (1.0 / HEAD_DIM)
                bz = bz_ref[rows, cols]
                o_ref[rows, cols] = o[p] * lax.rsqrt(ms + 1e-6) * gb_ref[...] * (bz * _sigmoid(bz))
            yield

    groups = [list(range(g0, min(g0 + GDN_GROUP, nb))) for g0 in range(0, nb, GDN_GROUP)]
    pending = iter(())
    for grp in groups:
        for _ in solve_stages(grp):
            next(pending, None)
        for _ in pending:
            pass
        pending = state_stages(grp)
    for _ in pending:
        pass

    if chained:
        @pl.when(t == pl.num_programs(1) - 1)
        def _():
            sout_ref[...] = s_scr[...]


def _gdn(qb, kb, vb, bz, elem, s0, lw, *, nb):
    b, t, wb = qb.shape
    npair = wb // LANES
    chained = t > GDN_BLOCK
    if not chained:
        qb, kb, vb, bz, elem = (a.reshape(1, b * t, a.shape[-1]) for a in (qb, kb, vb, bz, elem))
        nb = _pick(b, (GDN_TILE_BLOCKS, 2, 1))
    rows, total = qb.shape[:2]
    tile = nb * GDN_BLOCK
    tok = lambda w: pl.BlockSpec((None, tile, w), lambda i, j: (i, j, 0))
    if chained:
        st = pl.BlockSpec((None, npair, LANES, LANES), lambda i, j: (i, 0, 0, 0))
    else:
        st = pl.BlockSpec((nb, npair, LANES, LANES), lambda i, j: (j, 0, 0, 0))
    consts = [lw["g_b_pair"], lw["esel"], lw["hsum128"]]
    ob, s_new = pl.pallas_call(
        functools.partial(_gdn_kernel, nb=nb, nh=lw["nh"], chained=chained),
        grid=(rows, total // tile),
        in_specs=[tok(wb), tok(wb), tok(wb), tok(wb), tok(LANES), st] + [_const_spec(c.shape) for c in consts],
        out_specs=[tok(wb), st],
        out_shape=[jax.ShapeDtypeStruct((rows, total, wb), F32),
                   jax.ShapeDtypeStruct((b, npair, LANES, LANES), F32)],
        scratch_shapes=[pltpu.VMEM((npair, LANES, LANES), F32)],
        compiler_params=_params("arbitrary", "arbitrary"),
        name="gdn",
    )(qb, kb, vb, bz, elem, s0, *consts)
    return ob.reshape(b, t, wb), s_new


def _outffn_kernel(oa_ref, ob_ref, oc_ref, x_ref, ga_ref, wout_ref, gpm_ref, gpf_ref, wfi_ref, wfo_ref,
                   gpo_ref, y_ref, *, dff, nsub):
    r = x_ref.shape[0] // nsub
    rows = [slice(i * r, (i + 1) * r) for i in range(nsub)]
    cat = [jnp.concatenate([_rms(oa_ref[rs, :], ga_ref[...]), ob_ref[rs, :], oc_ref[rs, :]], axis=-1).astype(BF16)
           for rs in rows]
    m = [jnp.dot(c, wout_ref[...], preferred_element_type=F32) for c in cat]
    x1 = [x_ref[rs, :] + _rms(mi, gpm_ref[...]) for rs, mi in zip(rows, m)]
    h = [_rms(xi, gpf_ref[...]).astype(BF16) for xi in x1]
    gu = [jnp.dot(hi, wfi_ref[...], preferred_element_type=F32) for hi in h]
    a = [(g[:, :dff] * _sigmoid(g[:, :dff]) * g[:, dff:]).astype(BF16) for g in gu]
    f = [jnp.dot(ai, wfo_ref[...], preferred_element_type=F32) for ai in a]
    for rs, xi, fi in zip(rows, x1, f):
        y_ref[rs, :] = xi + _rms(fi, gpo_ref[...])


def _outffn(oa, ob, oc, x, lw, *, tm):
    n, d = x.shape
    dff = lw["w_ffn_out"].shape[0]
    tok = lambda w: pl.BlockSpec((tm, w), lambda i: (i, 0))
    consts = [lw["g_a_out"], lw["w_out"], lw["g_post_mix"], lw["g_pre_ffn"], lw["w_ffn_in"],
              lw["w_ffn_out"], lw["g_post_ffn"]]
    return pl.pallas_call(
        functools.partial(_outffn_kernel, dff=dff, nsub=2 if tm % (4 * SUBLANES) == 0 else 1),
        grid=(n // tm,),
        in_specs=[tok(oa.shape[1]), tok(ob.shape[1]), tok(oc.shape[1]), tok(d)]
                 + [_const_spec(c.shape) for c in consts],
        out_specs=tok(d),
        out_shape=jax.ShapeDtypeStruct((n, d), F32),
        compiler_params=_params("arbitrary"),
        name="outffn",
    )(oa, ob, oc, x, *consts)


def _block_ones(width):
    idx = jnp.arange(width) // HEAD_DIM
    return (idx[:, None] == idx[None, :]).astype(BF16)


def _layer_weights(l, prm, cms):
    w_in = prm["w_in"][l]
    nh = prm["b_f"].shape[1]
    wa = nh * HEAD_DIM
    wb = prm["a_log"].shape[1] * HEAD_DIM
    wc = prm["g_cv"].shape[1]
    ng = prm["w_s"].shape[1]
    assert prm["a_log"].shape[1] == nh and wa % LANES == 0 and wc % LANES == 0 and AUG * nh <= LANES
    sizes = (wa, wa, wa, nh, 3 * wb, nh, nh, wb, wc, wc)
    offs = [0]
    for sz in sizes:
        offs.append(offs[-1] + sz)
    w_in_t = w_in.T
    col = lambda i: w_in_t[offs[i]:offs[i + 1]]
    w_big = jnp.concatenate([col(4), col(8), col(9), col(0), col(1), col(2), col(7)], axis=0).astype(BF16)
    w_small = jnp.concatenate([col(3), col(5), col(6), jnp.zeros((LANES - 3 * nh, w_in.shape[0]), F32)],
                              axis=0).astype(BF16)
    zpad = jnp.zeros((LANES - 2 * nh,), F32)
    sp = jnp.zeros((SUBLANES, LANES), F32)
    sp = sp.at[0].set(jnp.concatenate([prm["b_f"][l], prm["dt_bias"][l], zpad]))
    sp = sp.at[1].set(jnp.concatenate([jnp.zeros((nh,), F32), prm["a_log"][l], zpad]))
    hl = jnp.arange(nh) * AUG
    sp = sp.at[2, (hl[:, None] + jnp.arange(3, 6)[None, :]).reshape(-1)].set(1.0)
    sp = sp.at[3, (hl[:, None] + jnp.arange(0, 3)[None, :]).reshape(-1)].set(1.0)
    pmat = jnp.zeros((3 * LANES, 2 * LANES), F32)
    for piece in range(3):
        pmat = pmat.at[piece * LANES + jnp.arange(nh), hl + piece].set(1.0)
        pmat = pmat.at[piece * LANES + jnp.arange(nh), LANES + hl + 3 + piece].set(1.0)
    row = lambda v: v.reshape(1, -1)
    ws_cat, bs_full = {}, {}
    for cm in cms:
        pos = jnp.arange(cm) // HEAD_DIM
        w = jnp.where(pos[None, :] <= pos[:, None], prm["w_s"][l][:, :cm, :cm], 0.0)
        pairs = [jnp.concatenate([w[2 * pp], w[2 * pp + 1]], axis=1) for pp in range(ng // 2)]
        kpad = max(LANES - 2 * cm, 0)
        ws_cat[cm] = jnp.pad(jnp.stack(pairs), ((0, 0), (0, 0), (0, kpad))).astype(BF16)
        bs_full[cm] = jnp.repeat(prm["b_s"][l][:, :cm].T, wc // ng, axis=1)
    src = jnp.arange(LANES)[:, None]
    dst = jnp.arange(wb)[None, :] // HEAD_DIM
    esel = jnp.concatenate([src == nh + dst, src == 2 * nh + dst], axis=1).astype(BF16)
    return dict(
        nh=nh, wa=wa, wb=wb, wc=wc,
        g_pre_mix=row(prm["g_pre_mix"][l]), w_big=w_big, w_small=w_small, sp=sp, conv_w=prm["conv_w"][l],
        g_cv=row(prm["g_cv"][l]), b_cv=row(prm["b_cv"][l]), ws_cat=ws_cat, bs_full=bs_full,
        g_c_out=row(prm["g_c_out"][l]), hsum=_block_ones(2 * LANES), hsum128=_block_ones(LANES), pmat=pmat.astype(BF16),
        g_b_pair=row(jnp.tile(prm["g_b_out"][l], LANES // HEAD_DIM)), esel=esel,
        g_a_out=row(prm["g_a_out"][l]), w_out=prm["w_out"][l].astype(BF16),
        g_post_mix=row(prm["g_post_mix"][l]), g_pre_ffn=row(prm["g_pre_ffn"][l]),
        w_ffn_in=prm["w_ffn_in"][l].astype(BF16), w_ffn_out=prm["w_ffn_out"][l].astype(BF16),
        g_post_ffn=row(prm["g_post_ffn"][l]))


def _pair_state(s):
    b, h, dk, dv = s.shape
    s = s.reshape(b, h // 2, 2, dk, dv)
    z = jnp.zeros_like(s[:, :, 0])
    top = jnp.concatenate([s[:, :, 0], z], axis=-1)
    bot = jnp.concatenate([z, s[:, :, 1]], axis=-1)
    return jnp.concatenate([top, bot], axis=-2)


def _unpair_state(sp):
    b, hp, _, _ = sp.shape
    s0 = sp[:, :, :HEAD_DIM, :HEAD_DIM]
    s1 = sp[:, :, HEAD_DIM:, HEAD_DIM:]
    return jnp.stack([s0, s1], axis=2).reshape(b, 2 * hp, HEAD_DIM, HEAD_DIM)


def _head_rows(cum, nh):
    b, t, _ = cum.shape
    return jnp.transpose(cum[:, :, :nh], (0, 2, 1)).reshape(b, nh // 2, 2, t)


def _pick(n, prefs):
    for c in prefs:
        if n % c == 0:
            return c
    return n


def _layer(x, lw, conv_prev, s0, cache, *, cm, layer, depth, kv_prev):
    b, t, d = x.shape
    nh, wb = lw["nh"], lw["wb"]
    pj = _inproj(x, lw, conv_prev, cm=cm, layer=layer, depth=depth, kv_prev=kv_prev)

    if cache is None:
        tq = _pick(t, (ATTN_TQ, 256, 128))
        oa = _attn_prompt(pj["qaug"], pj["kaug"], pj["vab"], tq=tq, kc=min(ATTN_KC, tq), look=ATTN_LOOK)
    else:
        ck, cv, clogf_t = cache
        _, _, bs, past = clogf_t.shape
        excl = _exclusive_suffix_sum(clogf_t, layer)
        rrow = jnp.transpose(excl, (1, 0, 2)).reshape(bs, nh // 2, 2, past)
        crow = jnp.pad(_head_rows(pj["cum"], nh), ((0, 0), (0, 0), (0, 0), (0, LANES - t)))
        oa = _attn_sample(pj["qaug"], pj["kaug"], pj["vab"], pj["cum"], ck, cv, layer, rrow, crow)

    tp = -(-t // GDN_BLOCK) * GDN_BLOCK
    padt = lambda a: a if tp == t else jnp.pad(a, ((0, 0), (0, tp - t), (0, 0)))
    nb = _pick(tp // GDN_BLOCK, (GDN_TILE_BLOCKS, 2, 1))
    ob, s_new = _gdn(padt(pj["qb"]), padt(pj["kb"]), padt(pj["vb"]), padt(pj["bz"]), padt(pj["elem"]),
                     _pair_state(s0), lw, nb=nb)
    ob = ob[:, :t]

    n = b * t
    y = _outffn(oa.reshape(n, -1), ob.reshape(n, -1), pj["oc"].reshape(n, -1), x.reshape(n, d), lw,
                tm=_pick(n, (512, 256, 128, 64, 32, 16)))
    if pj["kv_time_minor"]:
        new_kv = (pj["ka"], pj["va"])
    else:
        new_kv = (pj["ka"].reshape(b, t, nh, HEAD_DIM), pj["va"].reshape(b, t, nh, HEAD_DIM))
    state = (new_kv[0], new_kv[1], pj["elem"][:, :, :nh],
             pj["conv_new"], _unpair_state(s_new), pj["vn"])
    return y.reshape(b, t, d), state, pj["kv_time_minor"]


def kernel(x_prompt, x_sample, cache_a_k, cache_a_v, cache_a_logf, state_b_conv, state_b_S, g_pre_mix, w_in, b_f, conv_w, a_log, dt_bias, g_b_out, g_a_out, g_cv, b_cv, w_s, b_s, g_c_out, w_out, g_post_mix, g_pre_ffn, w_ffn_in, w_ffn_out, g_post_ffn):
    prm = dict(g_pre_mix=g_pre_mix, w_in=w_in, b_f=b_f, conv_w=conv_w, a_log=a_log, dt_bias=dt_bias,
               g_b_out=g_b_out, g_a_out=g_a_out, g_cv=g_cv, b_cv=b_cv, w_s=w_s, b_s=b_s, g_c_out=g_c_out,
               w_out=w_out, g_post_mix=g_post_mix, g_pre_ffn=g_pre_ffn, w_ffn_in=w_ffn_in,
               w_ffn_out=w_ffn_out, g_post_ffn=g_post_ffn)
    depth = w_in.shape[0]
    bp, sp_len, _ = x_prompt.shape
    n_new = x_sample.shape[1]
    cm_p = w_s.shape[2]
    assert sp_len % cm_p == 0 and sp_len % GDN_BLOCK == 0 and n_new <= HEAD_DIM and n_new % SUBLANES == 0
    kw1 = conv_w.shape[1] - 1
    nhb = a_log.shape[1]
    yp, ys = x_prompt, x_sample
    outs_p, outs_s = [], []
    cache_kt = jnp.transpose(cache_a_k, (0, 1, 3, 4, 2))
    cache_vt = jnp.transpose(cache_a_v, (0, 1, 3, 4, 2))
    clogf_t = jnp.transpose(cache_a_logf, (0, 3, 1, 2))
    for l in range(depth):
        lw = _layer_weights(l, prm, (cm_p, n_new))
        conv0 = jnp.zeros((bp, kw1, conv_w.shape[2]), F32)
        s0 = jnp.zeros((bp, nhb, HEAD_DIM, HEAD_DIM), F32)
        kv_p = (outs_p[-1][0], outs_p[-1][1]) if outs_p else None
        kv_s = (outs_s[-1][0], outs_s[-1][1]) if outs_s else None
        yp, st_p, shared_p = _layer(yp, lw, conv0, s0, None, cm=cm_p, layer=l, depth=depth, kv_prev=kv_p)
        ys, st_s, shared_s = _layer(ys, lw, state_b_conv[l], state_b_S[l], (cache_kt, cache_vt, clogf_t),
                                    cm=n_new, layer=l, depth=depth, kv_prev=kv_s)
        outs_p.append(st_p)
        outs_s.append(st_s)
    stk = lambda outs, i: jnp.stack([o[i] for o in outs], axis=0)

    def new_cache(outs, i, shared):
        if not shared:
            return stk(outs, i)
        buf = outs[-1][i]
        dp, b, _, t = buf.shape
        return jnp.transpose(buf.reshape(dp, b, -1, HEAD_DIM, t), (0, 1, 4, 2, 3))

    return (yp, ys, new_cache(outs_p, 0, shared_p), new_cache(outs_p, 1, shared_p),
            stk(outs_p, 2), stk(outs_p, 3), stk(outs_p, 4),
            new_cache(outs_s, 0, shared_s), new_cache(outs_s, 1, shared_s),
            stk(outs_s, 2), stk(outs_s, 3), stk(outs_s, 4), stk(outs_s, 5))
```

```python
import functools

import jax
import jax.numpy as jnp
from jax import lax
from jax.experimental import pallas as pl
from jax.experimental.pallas import tpu as pltpu

F32 = jnp.float32
BF16 = jnp.bfloat16

LANES = 128
SUBLANES = 8
HEAD_DIM = 64
GDN_BLOCK = 128
GDN_GROUP = 2
GDN_TILE_BLOCKS = 4
ATTN_TQ = 1024
ATTN_KC = 256
ATTN_LOOK = 1
VMEM_LIMIT = 56 * 1024 * 1024
NEG_INF = float("-inf")
LOG2E = 1.4426950408889634
AUG = 16


def _dot(a, b):
    return jnp.dot(a.astype(BF16), b.astype(BF16), preferred_element_type=F32)


def _dot_nt(a, b):
    return lax.dot_general(a.astype(BF16), b.astype(BF16), (((1,), (1,)), ((), ())),
                           preferred_element_type=F32)


def _dot_select_exact(x, sel):
    hi = x.astype(BF16)
    r1 = x - hi.astype(F32)
    mid = r1.astype(BF16)
    lo = (r1 - mid.astype(F32)).astype(BF16)
    d = lambda p: jnp.dot(p, sel, preferred_element_type=F32)
    return (d(hi) + d(mid)) + d(lo)


def _rms(x, g, eps=1e-6):
    return x * lax.rsqrt(jnp.mean(x * x, axis=-1, keepdims=True) + eps) * g


def _sigmoid(x):
    return 1.0 / (1.0 + jnp.exp(-x))


def _seg_cumsum(v, seg):
    row = lax.broadcasted_iota(jnp.int32, v.shape, 0)
    pos = jnp.bitwise_and(row, seg - 1)
    s = 1
    while s < seg:
        v = v + jnp.where(pos >= s, pltpu.roll(v, s, 0), 0.0)
        s *= 2
    return v


def _const_spec(shape):
    nd = len(shape)
    return pl.BlockSpec(shape, lambda *_: (0,) * nd, pipeline_mode=pl.Buffered(1))


def _params(*sem):
    return pltpu.CompilerParams(dimension_semantics=sem, vmem_limit_bytes=VMEM_LIMIT)


def _inproj_kernel(x_ref, gpre_ref, wbig_ref, wsm_ref, sp_ref, convw_ref, convinit_ref, gcv_ref,
                   bcv_ref, ws_ref, bs_ref, gco_ref, hsum_ref, pmat_ref,
                   qaug_ref, ka_ref, va_ref, kaug_ref, vab_ref, elem_ref, cum_ref, qb_ref, kb_ref,
                   vb_ref, bz_ref, oc_ref, vn_ref, ytail_ref,
                   carry_conv, carry_cum, *, tm, cm, nh, wa, wb, wc, scale, kv_time_minor, nsub, seg):
    if not seg:
        @pl.when(pl.program_id(1) == 0)
        def _():
            carry_cum[...] = jnp.zeros_like(carry_cum)
            carry_conv[...] = convinit_ref[...]

    r = tm // nsub
    o_c = 3 * wb
    o_a = o_c + 2 * wc

    def project(rs):
        h = _rms(x_ref[rs, :], gpre_ref[...]).astype(BF16)
        proj = lambda w: lax.dot_general(h, w, (((1,), (1,)), ((), ())), preferred_element_type=F32)
        return (proj(wbig_ref[:o_c, :]),
                proj(wbig_ref[o_c:o_a, :]),
                proj(wsm_ref[...]),
                proj(wbig_ref[o_a:, :]))

    def finish(rs, y, zc, zs, za, prev, cum_in):
        ka = za[:, wa:2 * wa]
        va = za[:, 2 * wa:3 * wa]
        if kv_time_minor:
            ka_ref[:, rs] = ka.T
            va_ref[:, rs] = va.T
        else:
            ka_ref[rs, :] = ka
            va_ref[rs, :] = va
        vab_ref[rs, :] = va.astype(BF16)
        bz_ref[rs, :] = za[:, 3 * wa:]

        lane = lax.broadcasted_iota(jnp.int32, (r, LANES), 1)
        zb = zs + sp_ref[0:1, :]
        soft_tail = jnp.log1p(jnp.exp(-jnp.abs(zb)))
        logf = -(jnp.maximum(-zb, 0.0) + soft_tail)
        gl = -jnp.exp(sp_ref[1:2, :]) * (jnp.maximum(zb, 0.0) + soft_tail)
        beta = _sigmoid(zs)
        elem = jnp.where(lane < nh, logf, jnp.where(lane < 2 * nh, gl, jnp.where(lane < 3 * nh, beta, 0.0)))
        elem_ref[rs, :] = elem
        cum = _seg_cumsum(elem, seg) if seg else _seg_cumsum(elem, r) + cum_in
        cum_ref[rs, :] = cum

        c2 = jnp.where(lane < nh, cum * LOG2E, 0.0)
        hi = c2.astype(BF16)
        r1 = c2 - hi.astype(F32)
        mid = r1.astype(BF16)
        lo = (r1 - mid.astype(F32)).astype(BF16)
        placed = jnp.dot(jnp.concatenate([hi, mid, lo], axis=1), pmat_ref[...], preferred_element_type=F32)
        augq = (placed[:, :LANES] + sp_ref[2:3, :]).astype(BF16)
        augk = (sp_ref[3:4, :] - placed[:, LANES:]).astype(BF16)
        qs = (za[:, :wa] * (scale * LOG2E)).astype(BF16)
        ks = ka.astype(BF16)
        qaug_ref[rs, :] = jnp.concatenate(
            [a for j in range(0, wa, LANES) for a in (qs[:, j:j + LANES], augq)], axis=1)
        kaug_ref[rs, :] = jnp.concatenate(
            [a for j in range(0, wa, LANES) for a in (ks[:, j:j + LANES], augk)], axis=1)

        row8 = lax.broadcasted_iota(jnp.int32, prev.shape, 0)
        pos = jnp.bitwise_and(lax.broadcasted_iota(jnp.int32, y.shape, 0), max(seg, 1) - 1)
        kw = convw_ref.shape[0]
        acc = y * convw_ref[kw - 1:kw, :]
        for k in range(1, kw):
            yk = pltpu.roll(y, k, 0)
            if seg:
                yk = jnp.where(pos < k, convinit_ref[k - 1, rs, :], yk)
            else:
                top = jnp.where(row8 < k, pltpu.roll(prev, k, 0), yk[0:SUBLANES])
                yk = jnp.concatenate([top, yk[SUBLANES:]], axis=0)
            acc = acc + yk * convw_ref[kw - 1 - k:kw - k, :]
        if seg:
            ytail_ref[rs, :] = y
        yc = acc * _sigmoid(acc)
        qb = yc[:, :wb]
        kb = yc[:, wb:2 * wb]
        sq = jnp.concatenate([qb * qb, kb * kb], axis=-1).astype(BF16)
        hw = hsum_ref.shape[0]
        ss = jnp.concatenate([jnp.dot(sq[:, j:j + hw], hsum_ref[...], preferred_element_type=F32)
                              for j in range(0, 2 * wb, hw)], axis=-1)
        qb_ref[rs, :] = qb * lax.rsqrt(ss[:, :wb] + 1e-6) * scale
        kb_ref[rs, :] = kb * lax.rsqrt(ss[:, wb:] + 1e-6)
        vb_ref[rs, :] = yc[:, 2 * wb:]

        u = jax.nn.gelu(zc[:, :wc])
        gv = jax.nn.gelu(zc[:, wc:])
        mu = jnp.mean(gv, axis=-1, keepdims=True)
        var = jnp.mean(jnp.square(gv - mu), axis=-1, keepdims=True)
        vn = (gv - mu) * lax.rsqrt(var + 1e-5) * gcv_ref[...] + bcv_ref[...]
        vn_ref[rs, :] = vn
        first = lax.broadcasted_iota(jnp.int32, (cm, LANES), 1) < HEAD_DIM
        kpad = ws_ref.shape[2] - 2 * cm
        rows = []
        for c in range(r // cm):
            vc = vn[c * cm:(c + 1) * cm]
            cols = []
            for pp in range(wc // LANES):
                vp = vc[:, pp * LANES:(pp + 1) * LANES]
                parts = [jnp.where(first, vp, 0.0), jnp.where(first, 0.0, vp)]
                if kpad:
                    parts.append(jnp.zeros((kpad, LANES), F32))
                cols.append(_dot(ws_ref[pp], jnp.concatenate(parts, axis=0)))
            s = jnp.concatenate(cols, axis=-1) + bs_ref[...]
            rows.append(u[c * cm:(c + 1) * cm] * s)
        oc = rows[0] if len(rows) == 1 else jnp.concatenate(rows, axis=0)
        oc_ref[rs, :] = _rms(oc, gco_ref[...])
        return y[r - SUBLANES:r], cum[r - 1:r, :]

    subs = [slice(i * r, (i + 1) * r) for i in range(nsub)]
    projected = [project(rs) for rs in subs]
    if seg:
        prev, cum_in = jnp.zeros(carry_conv.shape, F32), jnp.zeros(carry_cum.shape, F32)
    else:
        prev, cum_in = carry_conv[...], carry_cum[...]
    for rs, z in zip(subs, projected):
        prev, cum_in = finish(rs, *z, prev, cum_in)
    if not seg:
        carry_conv[...] = prev
        carry_cum[...] = cum_in
        ytail_ref[...] = prev


def _inproj_kernel_inplace(*refs, n_in, **kw):
    return _inproj_kernel(*refs[:n_in], *refs[n_in + 2:], **kw)


def _inproj(x, lw, conv_prev, *, cm, layer, depth, kv_prev):
    b0, t0, d = x.shape
    wa, wb, wc, nh = lw["wa"], lw["wb"], lw["wc"], lw["nh"]
    kw1 = conv_prev.shape[1]
    seg = 0 if t0 >= LANES else t0
    if seg:
        assert seg & (seg - 1) == 0 and seg >= SUBLANES
        x = x.reshape(1, b0 * t0, d)
        tm = _pick(b0 * t0, (512, 256, 128, 64, 32, 16))
        assert tm % seg == 0
        conv_init = jnp.stack([jnp.pad(conv_prev[:, kw1 - k:], ((0, 0), (0, t0 - k), (0, 0))) for k in range(1, kw1 + 1)])
        conv_init = conv_init.reshape(kw1, b0 * t0, -1)
    else:
        tm = _pick(t0, (512, 256, 128))
        conv_init = jnp.pad(conv_prev, ((0, 0), (SUBLANES - kw1, 0), (0, 0)))
    b, t, _ = x.shape
    nt = t // tm
    kv_time_minor = not seg
    tok = lambda w: pl.BlockSpec((None, tm, w), lambda i, j: (i, j, 0))
    per_b = lambda r, w: pl.BlockSpec((None, r, w), lambda i, j: (i, 0, 0))
    outs = [("qaug", 2 * wa, BF16), ("ka", wa, F32), ("va", wa, F32), ("kaug", 2 * wa, BF16), ("vab", wa, BF16),
            ("elem", LANES, F32), ("cum", LANES, F32), ("qb", wb, F32), ("kb", wb, F32), ("vb", wb, F32),
            ("bz", wb, F32), ("oc", wc, F32), ("vn", wc, F32)]
    out_shape = [jax.ShapeDtypeStruct((b, t, w), dt) for _, w, dt in outs]
    out_specs = [tok(w) for _, w, _ in outs]
    if kv_time_minor:
        for k in (1, 2):
            out_shape[k] = jax.ShapeDtypeStruct((depth, b, wa, t), F32)
            out_specs[k] = pl.BlockSpec((None, None, wa, tm), lambda i, j: (layer, i, 0, j))
    if seg:
        out_shape.append(jax.ShapeDtypeStruct((b, t, 3 * wb), F32))
        out_specs.append(tok(3 * wb))
        init_spec = pl.BlockSpec((kw1, tm, 3 * wb), lambda i, j: (0, j, 0))
    else:
        out_shape.append(jax.ShapeDtypeStruct((b, SUBLANES, 3 * wb), F32))
        out_specs.append(per_b(SUBLANES, 3 * wb))
        init_spec = per_b(SUBLANES, 3 * wb)
    consts = [lw["g_pre_mix"], lw["w_big"], lw["w_small"], lw["sp"], lw["conv_w"]]
    consts2 = [lw["g_cv"], lw["b_cv"], lw["ws_cat"][cm], lw["bs_full"][cm], lw["g_c_out"], lw["hsum"], lw["pmat"]]
    nsub = 2 if tm % (2 * max(cm, LANES)) == 0 else 1
    kw = dict(tm=tm, cm=cm, nh=nh, wa=wa, wb=wb, wc=wc, scale=HEAD_DIM ** -0.5, kv_time_minor=kv_time_minor,
              nsub=nsub, seg=seg)
    in_specs = ([tok(d)] + [_const_spec(c.shape) for c in consts] + [init_spec]
                + [_const_spec(c.shape) for c in consts2])
    args = [x, *consts, conv_init, *consts2]
    inplace = kv_time_minor and kv_prev is not None
    if inplace:
        kern = functools.partial(_inproj_kernel_inplace, n_in=len(args), **kw)
        aliases = {len(args): 1, len(args) + 1: 2}
        in_specs = in_specs + [pl.BlockSpec(memory_space=pl.ANY)] * 2
        args = args + list(kv_prev)
    else:
        kern = functools.partial(_inproj_kernel, **kw)
        aliases = {}
    res = pl.pallas_call(
        kern,
        grid=(b, nt),
        in_specs=in_specs,
        out_specs=out_specs,
        out_shape=out_shape,
        input_output_aliases=aliases,
        scratch_shapes=[pltpu.VMEM((SUBLANES, 3 * wb), F32), pltpu.VMEM((1, LANES), F32)],
        compiler_params=_params("arbitrary", "arbitrary"),
        name="inproj",
    )(*args)
    named = {n: r for (n, _, _), r in zip(outs, res[:-1])}
    if seg:
        named = {n: r.reshape(b0, t0, r.shape[-1]) for n, r in named.items()}
        named["conv_new"] = res[-1].reshape(b0, t0, -1)[:, t0 - kw1:, :]
    else:
        named["conv_new"] = res[-1][:, SUBLANES - kw1:, :]
    named["kv_time_minor"] = kv_time_minor
    return named


def _attn_kernel(qt_ref, k_ref, vt_ref, o_ref, *, tq, kc, look):
    p = pl.program_id(1)
    i = pl.program_id(2)
    qt = qt_ref[...]
    rowi = lax.broadcasted_iota(jnp.int32, qt.shape, 0)
    zero = jnp.zeros_like(qt)
    qts = []
    for e in range(2):
        a0 = LANES + AUG * (2 * p + e)
        keep = ((rowi >= e * HEAD_DIM) & (rowi < (e + 1) * HEAD_DIM)) | ((rowi >= a0) & (rowi < a0 + AUG))
        qts.append(jnp.where(keep, qt, zero))
    qpos = i * tq + lax.broadcasted_iota(jnp.int32, (kc, tq), 1)
    kofs = lax.broadcasted_iota(jnp.int32, (kc, tq), 0)
    ones = jnp.ones((2 * SUBLANES, kc), BF16)
    units = [(c, e) for c in range(tq // kc) for e in range(2)]

    def scores(j, c, e, masked):
        k0 = pl.multiple_of(j * tq + c * kc, kc)
        lo = c * kc if masked else 0
        s = jnp.dot(k_ref[pl.ds(k0, kc), :], qts[e][:, lo:], preferred_element_type=F32)
        return jnp.concatenate([jnp.full((kc, lo), NEG_INF, F32), s], axis=1) if lo else s

    def fold(j, c, e, s, st, masked):
        m, l, acc = st
        k0 = pl.multiple_of(j * tq + c * kc, kc)
        lo = c * kc if masked else 0
        if masked:
            s = jnp.where(k0 + kofs <= qpos, s, NEG_INF)
        m_new = jnp.maximum(m, jnp.max(s, axis=0, keepdims=True))
        alpha = jnp.exp2(m - m_new)
        pt = jnp.exp2(s - m_new).astype(BF16)
        vt = jnp.concatenate([vt_ref[e * HEAD_DIM:(e + 1) * HEAD_DIM, pl.ds(k0, kc)], ones], axis=0)
        r = jnp.dot(vt, pt[:, lo:], preferred_element_type=F32)
        if lo:
            r = jnp.concatenate([jnp.zeros((r.shape[0], lo), F32), r], axis=1)
        return m_new, alpha * l + r[HEAD_DIM:HEAD_DIM + 1], alpha * acc + r[:HEAD_DIM]

    def run(blocks, state):
        state = list(state)
        todo = [(j, c, e, masked) for j, masked in blocks for c, e in units]
        pend = {}
        for k in range(min(look, len(todo))):
            pend[k] = scores(*todo[k])
        for k, (j, c, e, masked) in enumerate(todo):
            if k + look < len(todo):
                pend[k + look] = scores(*todo[k + look])
            state[e] = fold(j, c, e, pend.pop(k), state[e], masked)
        return tuple(state)

    st0 = (jnp.full((1, tq), NEG_INF, F32), jnp.zeros((1, tq), F32), jnp.zeros((HEAD_DIM, tq), F32))
    state = lax.fori_loop(0, i // 2, lambda t, s: run([(2 * t, False), (2 * t + 1, False)], s), (st0, st0))
    state = lax.cond(i % 2 == 1,
                     lambda s: run([(i - 1, False), (i, True)], s),
                     lambda s: run([(i, True)], s), state)
    ot = jnp.concatenate([acc / l for _, l, acc in state], axis=0)
    o_ref[...] = ot.T


def _attn_prompt(qaug, kaug, vab, *, tq, kc, look):
    b, s, wa = vab.shape
    npair = wa // LANES
    qt = jnp.transpose(qaug, (0, 2, 1))
    vt = jnp.transpose(vab, (0, 2, 1))
    kern = functools.partial(_attn_kernel, tq=tq, kc=kc, look=look)
    return pl.pallas_call(
        kern,
        grid=(b, npair, s // tq),
        in_specs=[pl.BlockSpec((None, 2 * LANES, tq), lambda bi, p, i: (bi, p, i)),
                  pl.BlockSpec((None, s, 2 * LANES), lambda bi, p, i: (bi, 0, p)),
                  pl.BlockSpec((None, LANES, s), lambda bi, p, i: (bi, p, 0))],
        out_specs=pl.BlockSpec((None, tq, LANES), lambda bi, p, i: (bi, i, p)),
        out_shape=jax.ShapeDtypeStruct((b, s, wa), F32),
        compiler_params=_params("arbitrary", "arbitrary", "arbitrary"),
        name="attn_prompt",
    )(qt, kaug, vt)


def _attn_sample_kernel(q_ref, kc_ref, vc_ref, kn_ref, vn_ref, cum_ref, rrow_ref, crow_ref, o_ref, *, n):
    p = pl.program_id(1)
    q = q_ref[:, :LANES]
    lane = lax.broadcasted_iota(jnp.int32, (n, LANES), 1)
    first = lane < HEAD_DIM
    zero = jnp.zeros_like(q)
    past = kc_ref.shape[-1]
    kc = kc_ref[...].reshape(LANES, past).astype(BF16)
    vc = vc_ref[...].reshape(LANES, past).astype(BF16)
    pad = jnp.zeros((LANES - n, LANES), BF16)
    kn = jnp.concatenate([kn_ref[:, :LANES], pad], axis=0)
    vn = jnp.concatenate([vn_ref[...], pad], axis=0)
    cum = cum_ref[...]
    qm = jnp.concatenate([jnp.where(first, q, zero), jnp.where(first, zero, q)], axis=0)
    cq = jnp.concatenate([jnp.sum(jnp.where(lane == 2 * p + e, cum, 0.0), axis=-1, keepdims=True)
                          for e in range(2)], axis=0)
    top = lax.broadcasted_iota(jnp.int32, (2 * n, 1), 0) < n
    rrow = jnp.where(top, rrow_ref[0:1, :], rrow_ref[1:2, :])
    crow = jnp.where(top, crow_ref[0:1, :], crow_ref[1:2, :])
    qrow = lax.broadcasted_iota(jnp.int32, (2 * n, LANES), 0)
    causal = lax.broadcasted_iota(jnp.int32, (2 * n, LANES), 1) <= jnp.where(qrow < n, qrow, qrow - n)
    sc = _dot(qm, kc) + LOG2E * (cq + rrow)
    sn = jnp.where(causal, _dot_nt(qm, kn) + LOG2E * (cq - crow), NEG_INF)
    m = jnp.maximum(jnp.max(sc, axis=-1, keepdims=True), jnp.max(sn, axis=-1, keepdims=True))
    pc = jnp.exp2(sc - m)
    pn = jnp.exp2(sn - m)
    l = jnp.sum(pc, axis=-1, keepdims=True) + jnp.sum(pn, axis=-1, keepdims=True)
    o = (_dot_nt(pc, vc) + _dot(pn, vn)) / l
    o_ref[...] = jnp.where(first, o[:n], o[n:])


def _attn_sample(qaug, kaug, vab, cum, cache_kt, cache_vt, layer, rrow, crow):
    b, n, wa = vab.shape
    past = cache_kt.shape[-1]
    npair = wa // LANES
    new = lambda w: pl.BlockSpec((None, n, w), lambda bi, p: (bi, 0, p))
    old = lambda: pl.BlockSpec((None, None, 2, HEAD_DIM, past), lambda bi, p: (layer, bi, p, 0, 0))
    return pl.pallas_call(
        functools.partial(_attn_sample_kernel, n=n),
        grid=(b, npair),
        in_specs=[new(2 * LANES), old(), old(), new(2 * LANES), new(LANES),
                  pl.BlockSpec((None, n, LANES), lambda bi, p: (bi, 0, 0)),
                  pl.BlockSpec((None, None, 2, past), lambda bi, p: (bi, p, 0, 0)),
                  pl.BlockSpec((None, None, 2, LANES), lambda bi, p: (bi, p, 0, 0))],
        out_specs=new(LANES),
        out_shape=jax.ShapeDtypeStruct((b, n, wa), F32),
        compiler_params=_params("arbitrary", "arbitrary"),
        name="attn_sample",
    )(qaug, cache_kt, cache_vt, kaug, vab, cum, rrow, crow)


def _suffix_kernel(x_ref, o_ref):
    v = x_ref[...]
    n = v.shape[1]
    lane = lax.broadcasted_iota(jnp.int32, v.shape, 1)
    s = 1
    while s < n:
        v = v + jnp.where(lane + s < n, pltpu.roll(v, n - s, 1), 0.0)
        s *= 2
    o_ref[...] = jnp.where(lane + 1 < n, pltpu.roll(v, n - 1, 1), 0.0)


def _exclusive_suffix_sum(x, layer):
    _, h, b, p = x.shape
    return pl.pallas_call(
        _suffix_kernel,
        grid=(h,),
        in_specs=[pl.BlockSpec((None, None, b, p), lambda i: (layer, i, 0, 0))],
        out_specs=pl.BlockSpec((None, b, p), lambda i: (i, 0, 0)),
        out_shape=jax.ShapeDtypeStruct((h, b, p), F32),
        compiler_params=_params("arbitrary"),
        name="suffix_sum",
    )(x)


def _gdn_kernel(q_ref, k_ref, v_ref, bz_ref, elem_ref, s0_ref, gb_ref, esel_ref, hsum_ref,
                o_ref, sout_ref, s_scr, *, nb, nh, chained):
    L = GDN_BLOCK
    t = pl.program_id(1)

    if chained:
        @pl.when(t == 0)
        def _():
            s_scr[...] = s0_ref[...]

    lane = lax.broadcasted_iota(jnp.int32, (L, LANES), 1)
    first = lane < HEAD_DIM
    ri = lax.broadcasted_iota(jnp.int32, (L, L), 0)
    ci = lax.broadcasted_iota(jnp.int32, (L, L), 1)
    incl = ci <= ri
    strict = ci < ri
    same_head = (ri < HEAD_DIM) == (ci < HEAD_DIM)
    lane2 = lax.broadcasted_iota(jnp.int32, (L, 2 * L), 1)
    first2 = jnp.bitwise_and(lane2, LANES - 1) < HEAD_DIM
    xor2 = jnp.bitwise_xor(lax.broadcasted_iota(jnp.int32, (L, 2 * L), 0), jnp.bitwise_and(lane2, L - 1))
    zero_ll = jnp.zeros((L, L), BF16)

    def halves(x, sel):
        return jnp.concatenate([jnp.where(sel, x, 0.0), jnp.where(sel, 0.0, x)], axis=0)

    def dot_heads(y, x):
        xb = x.astype(BF16)
        bd = jnp.concatenate([jnp.concatenate([xb[:, :L], zero_ll], axis=1),
                              jnp.concatenate([zero_ll, xb[:, L:]], axis=1)], axis=0)
        return jnp.dot(y.astype(BF16), bd, preferred_element_type=F32)

    npair = s_scr.shape[0]
    wbw = npair * LANES
    c = {}

    def solve_stages(blocks):
        chains = [(n, p) for n in blocks for p in range(npair)]
        ex = {}
        for n in blocks:
            elem = elem_ref[n * L:(n + 1) * L, :]
            gsum = _seg_cumsum(elem, L)
            mixed = jnp.where((lane >= nh) & (lane < 2 * nh), gsum, elem)
            ex[n] = _dot_select_exact(mixed, esel_ref[...])
        yield
        for n, p in chains:
            rows = slice(n * L, (n + 1) * L)
            cols = slice(p * LANES, (p + 1) * LANES)
            g = ex[n][:, cols]
            bt = ex[n][:, wbw + p * LANES: wbw + (p + 1) * LANES]
            kp = k_ref[rows, cols]
            qp = q_ref[rows, cols]
            g_sw = pltpu.roll(g, HEAD_DIM, 1)
            b_sw = pltpu.roll(bt, HEAD_DIM, 1)
            g_t = g.T
            a_parts, qk_parts = [], []
            for e in range(2):
                sel = first if e == 0 else jnp.logical_not(first)
                gcol = jnp.where(sel, g, g_sw)
                bcol = jnp.where(sel, bt, b_sw)
                grow = g_t[e * HEAD_DIM:e * HEAD_DIM + 1, :]
                dec = jnp.exp(jnp.where(incl, gcol - grow, NEG_INF))
                kk = _dot_nt(jnp.where(sel, kp, 0.0), kp)
                qk_parts.append(_dot_nt(jnp.where(sel, qp, 0.0), kp) * dec)
                a_parts.append(jnp.where(strict, bcol * kk * dec, 0.0))
            a_cat = jnp.concatenate(a_parts, axis=1)
            eg = jnp.exp(g)
            glast = g[L - 1:L, :]
            c[n, p] = dict(a=a_cat, qk=jnp.concatenate(qk_parts, axis=1), glast=glast, qg=qp * eg,
                           kdec=kp * jnp.exp(glast - g),
                           r=jnp.concatenate([v_ref[rows, cols] * bt, kp * bt * eg], axis=1),
                           tm1=-jnp.where(xor2 < 2, a_cat, 0.0))
        yield
        s_blk = 2
        while s_blk < L:
            pm = {}
            for key in chains:
                nmat = jnp.where((xor2 >= s_blk) & (xor2 < 2 * s_blk), c[key]["a"], 0.0)
                pm[key] = nmat + dot_heads(c[key]["tm1"], nmat)
            yield
            for key in chains:
                c[key]["tm1"] = c[key]["tm1"] - pm[key] - dot_heads(pm[key], c[key]["tm1"])
            yield
            s_blk *= 2
        for key in chains:
            r = c[key]["r"]
            c[key]["uw"] = r + _dot(c[key]["tm1"], halves(r, first2))
        yield

    def state_stages(blocks):
        pairs = range(npair)
        for n in blocks:
            rows = slice(n * L, (n + 1) * L)
            s_in = [s_scr[p] if chained else s0_ref[n, p] for p in pairs]
            ws = [_dot(jnp.concatenate([c[n, p]["uw"][:, LANES:], c[n, p]["qg"]], axis=0), s_in[p]) for p in pairs]
            yield
            u = [c[n, p]["uw"][:, :LANES] - ws[p][:L] for p in pairs]
            o = [ws[p][L:] + _dot(c[n, p]["qk"], halves(u[p], first)) for p in pairs]
            yield
            for p in pairs:
                s_new = (s_in[p] * jnp.exp(c[n, p]["glast"])
                         + jnp.where(same_head, _dot(c[n, p]["kdec"].T, u[p]), 0.0))
                if chained:
                    s_scr[p] = s_new
                else:
                    sout_ref[n, p] = s_new
            yield
            for p in pairs:
                cols = slice(p * LANES, (p + 1) * LANES)
                ms = _dot(o[p] * o[p], hsum_ref[...]) * (1.0 / HEAD_DIM)
                bz = bz_ref[rows, cols]
                o_ref[rows, cols] = o[p] * lax.rsqrt(ms + 1e-6) * gb_ref[...] * (bz * _sigmoid(bz))
            yield

    groups = [list(range(g0, min(g0 + GDN_GROUP, nb))) for g0 in range(0, nb, GDN_GROUP)]
    pending = iter(())
    for grp in groups:
        for _ in solve_stages(grp):
            next(pending, None)
        for _ in pending:
            pass
        pending = state_stages(grp)
    for _ in pending:
        pass

    if chained:
        @pl.when(t == pl.num_programs(1) - 1)
        def _():
            sout_ref[...] = s_scr[...]


def _gdn(qb, kb, vb, bz, elem, s0, lw, *, nb):
    b, t, wb = qb.shape
    npair = wb // LANES
    chained = t > GDN_BLOCK
    if not chained:
        qb, kb, vb, bz, elem = (a.reshape(1, b * t, a.shape[-1]) for a in (qb, kb, vb, bz, elem))
        nb = _pick(b, (GDN_TILE_BLOCKS, 2, 1))
    rows, total = qb.shape[:2]
    tile = nb * GDN_BLOCK
    tok = lambda w: pl.BlockSpec((None, tile, w), lambda i, j: (i, j, 0))
    if chained:
        st = pl.BlockSpec((None, npair, LANES, LANES), lambda i, j: (i, 0, 0, 0))
    else:
        st = pl.BlockSpec((nb, npair, LANES, LANES), lambda i, j: (j, 0, 0, 0))
    consts = [lw["g_b_pair"], lw["esel"], lw["hsum128"]]
    ob, s_new = pl.pallas_call(
        functools.partial(_gdn_kernel, nb=nb, nh=lw["nh"], chained=chained),
        grid=(rows, total // tile),
        in_specs=[tok(wb), tok(wb), tok(wb), tok(wb), tok(LANES), st] + [_const_spec(c.shape) for c in consts],
        out_specs=[tok(wb), st],
        out_shape=[jax.ShapeDtypeStruct((rows, total, wb), F32),
                   jax.ShapeDtypeStruct((b, npair, LANES, LANES), F32)],
        scratch_shapes=[pltpu.VMEM((npair, LANES, LANES), F32)],
        compiler_params=_params("arbitrary", "arbitrary"),
        name="gdn",
    )(qb, kb, vb, bz, elem, s0, *consts)
    return ob.reshape(b, t, wb), s_new


def _outffn_kernel(oa_ref, ob_ref, oc_ref, x_ref, ga_ref, wout_ref, gpm_ref, gpf_ref, wfi_ref, wfo_ref,
                   gpo_ref, y_ref, *, dff, nsub):
    r = x_ref.shape[0] // nsub
    rows = [slice(i * r, (i + 1) * r) for i in range(nsub)]
    cat = [jnp.concatenate([_rms(oa_ref[rs, :], ga_ref[...]), ob_ref[rs, :], oc_ref[rs, :]], axis=-1).astype(BF16)
           for rs in rows]
    m = [jnp.dot(c, wout_ref[...], preferred_element_type=F32) for c in cat]
    x1 = [x_ref[rs, :] + _rms(mi, gpm_ref[...]) for rs, mi in zip(rows, m)]
    h = [_rms(xi, gpf_ref[...]).astype(BF16) for xi in x1]
    gu = [jnp.dot(hi, wfi_ref[...], preferred_element_type=F32) for hi in h]
    a = [(g[:, :dff] * _sigmoid(g[:, :dff]) * g[:, dff:]).astype(BF16) for g in gu]
    f = [jnp.dot(ai, wfo_ref[...], preferred_element_type=F32) for ai in a]
    for rs, xi, fi in zip(rows, x1, f):
        y_ref[rs, :] = xi + _rms(fi, gpo_ref[...])


def _outffn(oa, ob, oc, x, lw, *, tm):
    n, d = x.shape
    dff = lw["w_ffn_out"].shape[0]
    tok = lambda w: pl.BlockSpec((tm, w), lambda i: (i, 0))
    consts = [lw["g_a_out"], lw["w_out"], lw["g_post_mix"], lw["g_pre_ffn"], lw["w_ffn_in"],
              lw["w_ffn_out"], lw["g_post_ffn"]]
    return pl.pallas_call(
        functools.partial(_outffn_kernel, dff=dff, nsub=2 if tm % (4 * SUBLANES) == 0 else 1),
        grid=(n // tm,),
        in_specs=[tok(oa.shape[1]), tok(ob.shape[1]), tok(oc.shape[1]), tok(d)]
                 + [_const_spec(c.shape) for c in consts],
        out_specs=tok(d),
        out_shape=jax.ShapeDtypeStruct((n, d), F32),
        compiler_params=_params("arbitrary"),
        name="outffn",
    )(oa, ob, oc, x, *consts)


def _block_ones(width):
    idx = jnp.arange(width) // HEAD_DIM
    return (idx[:, None] == idx[None, :]).astype(BF16)


def _layer_weights(l, prm, cms):
    w_in = prm["w_in"][l]
    nh = prm["b_f"].shape[1]
    wa = nh * HEAD_DIM
    wb = prm["a_log"].shape[1] * HEAD_DIM
    wc = prm["g_cv"].shape[1]
    ng = prm["w_s"].shape[1]
    assert prm["a_log"].shape[1] == nh and wa % LANES == 0 and wc % LANES == 0 and AUG * nh <= LANES
    sizes = (wa, wa, wa, nh, 3 * wb, nh, nh, wb, wc, wc)
    offs = [0]
    for sz in sizes:
        offs.append(offs[-1] + sz)
    w_in_t = w_in.T
    col = lambda i: w_in_t[offs[i]:offs[i + 1]]
    w_big = jnp.concatenate([col(4), col(8), col(9), col(0), col(1), col(2), col(7)], axis=0).astype(BF16)
    w_small = jnp.concatenate([col(3), col(5), col(6), jnp.zeros((LANES - 3 * nh, w_in.shape[0]), F32)],
                              axis=0).astype(BF16)
    zpad = jnp.zeros((LANES - 2 * nh,), F32)
    sp = jnp.zeros((SUBLANES, LANES), F32)
    sp = sp.at[0].set(jnp.concatenate([prm["b_f"][l], prm["dt_bias"][l], zpad]))
    sp = sp.at[1].set(jnp.concatenate([jnp.zeros((nh,), F32), prm["a_log"][l], zpad]))
    hl = jnp.arange(nh) * AUG
    sp = sp.at[2, (hl[:, None] + jnp.arange(3, 6)[None, :]).reshape(-1)].set(1.0)
    sp = sp.at[3, (hl[:, None] + jnp.arange(0, 3)[None, :]).reshape(-1)].set(1.0)
    pmat = jnp.zeros((3 * LANES, 2 * LANES), F32)
    for piece in range(3):
        pmat = pmat.at[piece * LANES + jnp.arange(nh), hl + piece].set(1.0)
        pmat = pmat.at[piece * LANES + jnp.arange(nh), LANES + hl + 3 + piece].set(1.0)
    row = lambda v: v.reshape(1, -1)
    ws_cat, bs_full = {}, {}
    for cm in cms:
        pos = jnp.arange(cm) // HEAD_DIM
        w = jnp.where(pos[None, :] <= pos[:, None], prm["w_s"][l][:, :cm, :cm], 0.0)
        pairs = [jnp.concatenate([w[2 * pp], w[2 * pp + 1]], axis=1) for pp in range(ng // 2)]
        kpad = max(LANES - 2 * cm, 0)
        ws_cat[cm] = jnp.pad(jnp.stack(pairs), ((0, 0), (0, 0), (0, kpad))).astype(BF16)
        bs_full[cm] = jnp.repeat(prm["b_s"][l][:, :cm].T, wc // ng, axis=1)
    src = jnp.arange(LANES)[:, None]
    dst = jnp.arange(wb)[None, :] // HEAD_DIM
    esel = jnp.concatenate([src == nh + dst, src == 2 * nh + dst], axis=1).astype(BF16)
    return dict(
        nh=nh, wa=wa, wb=wb, wc=wc,
        g_pre_mix=row(prm["g_pre_mix"][l]), w_big=w_big, w_small=w_small, sp=sp, conv_w=prm["conv_w"][l],
        g_cv=row(prm["g_cv"][l]), b_cv=row(prm["b_cv"][l]), ws_cat=ws_cat, bs_full=bs_full,
        g_c_out=row(prm["g_c_out"][l]), hsum=_block_ones(2 * LANES), hsum128=_block_ones(LANES), pmat=pmat.astype(BF16),
        g_b_pair=row(jnp.tile(prm["g_b_out"][l], LANES // HEAD_DIM)), esel=esel,
        g_a_out=row(prm["g_a_out"][l]), w_out=prm["w_out"][l].astype(BF16),
        g_post_mix=row(prm["g_post_mix"][l]), g_pre_ffn=row(prm["g_pre_ffn"][l]),
        w_ffn_in=prm["w_ffn_in"][l].astype(BF16), w_ffn_out=prm["w_ffn_out"][l].astype(BF16),
        g_post_ffn=row(prm["g_post_ffn"][l]))


def _pair_state(s):
    b, h, dk, dv = s.shape
    s = s.reshape(b, h // 2, 2, dk, dv)
    z = jnp.zeros_like(s[:, :, 0])
    top = jnp.concatenate([s[:, :, 0], z], axis=-1)
    bot = jnp.concatenate([z, s[:, :, 1]], axis=-1)
    return jnp.concatenate([top, bot], axis=-2)


def _unpair_state(sp):
    b, hp, _, _ = sp.shape
    s0 = sp[:, :, :HEAD_DIM, :HEAD_DIM]
    s1 = sp[:, :, HEAD_DIM:, HEAD_DIM:]
    return jnp.stack([s0, s1], axis=2).reshape(b, 2 * hp, HEAD_DIM, HEAD_DIM)


def _head_rows(cum, nh):
    b, t, _ = cum.shape
    return jnp.transpose(cum[:, :, :nh], (0, 2, 1)).reshape(b, nh // 2, 2, t)


def _pick(n, prefs):
    for c in prefs:
        if n % c == 0:
            return c
    return n


def _layer(x, lw, conv_prev, s0, cache, *, cm, layer, depth, kv_prev):
    b, t, d = x.shape
    nh, wb = lw["nh"], lw["wb"]
    pj = _inproj(x, lw, conv_prev, cm=cm, layer=layer, depth=depth, kv_prev=kv_prev)

    if cache is None:
        tq = _pick(t, (ATTN_TQ, 256, 128))
        oa = _attn_prompt(pj["qaug"], pj["kaug"], pj["vab"], tq=tq, kc=min(ATTN_KC, tq), look=ATTN_LOOK)
    else:
        ck, cv, clogf_t = cache
        _, _, bs, past = clogf_t.shape
        excl = _exclusive_suffix_sum(clogf_t, layer)
        rrow = jnp.transpose(excl, (1, 0, 2)).reshape(bs, nh // 2, 2, past)
        crow = jnp.pad(_head_rows(pj["cum"], nh), ((0, 0), (0, 0), (0, 0), (0, LANES - t)))
        oa = _attn_sample(pj["qaug"], pj["kaug"], pj["vab"], pj["cum"], ck, cv, layer, rrow, crow)

    tp = -(-t // GDN_BLOCK) * GDN_BLOCK
    padt = lambda a: a if tp == t else jnp.pad(a, ((0, 0), (0, tp - t), (0, 0)))
    nb = _pick(tp // GDN_BLOCK, (GDN_TILE_BLOCKS, 2, 1))
    ob, s_new = _gdn(padt(pj["qb"]), padt(pj["kb"]), padt(pj["vb"]), padt(pj["bz"]), padt(pj["elem"]),
                     _pair_state(s0), lw, nb=nb)
    ob = ob[:, :t]

    n = b * t
    y = _outffn(oa.reshape(n, -1), ob.reshape(n, -1), pj["oc"].reshape(n, -1), x.reshape(n, d), lw,
                tm=_pick(n, (512, 256, 128, 64, 32, 16)))
    if pj["kv_time_minor"]:
        new_kv = (pj["ka"], pj["va"])
    else:
        new_kv = (pj["ka"].reshape(b, t, nh, HEAD_DIM), pj["va"].reshape(b, t, nh, HEAD_DIM))
    state = (new_kv[0], new_kv[1], pj["elem"][:, :, :nh],
             pj["conv_new"], _unpair_state(s_new), pj["vn"])
    return y.reshape(b, t, d), state, pj["kv_time_minor"]


def kernel(x_prompt, x_sample, cache_a_k, cache_a_v, cache_a_logf, state_b_conv, state_b_S, g_pre_mix, w_in, b_f, conv_w, a_log, dt_bias, g_b_out, g_a_out, g_cv, b_cv, w_s, b_s, g_c_out, w_out, g_post_mix, g_pre_ffn, w_ffn_in, w_ffn_out, g_post_ffn):
    prm = dict(g_pre_mix=g_pre_mix, w_in=w_in, b_f=b_f, conv_w=conv_w, a_log=a_log, dt_bias=dt_bias,
               g_b_out=g_b_out, g_a_out=g_a_out, g_cv=g_cv, b_cv=b_cv, w_s=w_s, b_s=b_s, g_c_out=g_c_out,
               w_out=w_out, g_post_mix=g_post_mix, g_pre_ffn=g_pre_ffn, w_ffn_in=w_ffn_in,
               w_ffn_out=w_ffn_out, g_post_ffn=g_post_ffn)
    depth = w_in.shape[0]
    bp, sp_len, _ = x_prompt.shape
    n_new = x_sample.shape[1]
    cm_p = w_s.shape[2]
    assert sp_len % cm_p == 0 and sp_len % GDN_BLOCK == 0 and n_new <= HEAD_DIM and n_new % SUBLANES == 0
    kw1 = conv_w.shape[1] - 1
    nhb = a_log.shape[1]
    yp, ys = x_prompt, x_sample
    outs_p, outs_s = [], []
    cache_kt = jnp.transpose(cache_a_k, (0, 1, 3, 4, 2))
    cache_vt = jnp.transpose(cache_a_v, (0, 1, 3, 4, 2))
    clogf_t = jnp.transpose(cache_a_logf, (0, 3, 1, 2))
    for l in range(depth):
        lw = _layer_weights(l, prm, (cm_p, n_new))
        conv0 = jnp.zeros((bp, kw1, conv_w.shape[2]), F32)
        s0 = jnp.zeros((bp, nhb, HEAD_DIM, HEAD_DIM), F32)
        kv_p = (outs_p[-1][0], outs_p[-1][1]) if outs_p else None
        kv_s = (outs_s[-1][0], outs_s[-1][1]) if outs_s else None
        yp, st_p, shared_p = _layer(yp, lw, conv0, s0, None, cm=cm_p, layer=l, depth=depth, kv_prev=kv_p)
        ys, st_s, shared_s = _layer(ys, lw, state_b_conv[l], state_b_S[l], (cache_kt, cache_vt, clogf_t),
                                    cm=n_new, layer=l, depth=depth, kv_prev=kv_s)
        outs_p.append(st_p)
        outs_s.append(st_s)
    stk = lambda outs, i: jnp.stack([o[i] for o in outs], axis=0)

    def new_cache(outs, i, shared):
        if not shared:
            return stk(outs, i)
        buf = outs[-1][i]
        dp, b, _, t = buf.shape
        return jnp.transpose(buf.reshape(dp, b, -1, HEAD_DIM, t), (0, 1, 4, 2, 3))

    return (yp, ys, new_cache(outs_p, 0, shared_p), new_cache(outs_p, 1, shared_p),
            stk(outs_p, 2), stk(outs_p, 3), stk(outs_p, 4),
            new_cache(outs_s, 0, shared_s), new_cache(outs_s, 1, shared_s),
            stk(outs_s, 2), stk(outs_s, 3), stk(outs_s, 4), stk(outs_s, 5))
```

```python
import functools

import jax
import jax.numpy as jnp
from jax import lax
from jax.experimental import pallas as pl
from jax.experimental.pallas import tpu as pltpu

F32 = jnp.float32
BF16 = jnp.bfloat16

LANES = 128
SUBLANES = 8
HEAD_DIM = 64
GDN_BLOCK = 128
GDN_GROUP = 2
GDN_TILE_BLOCKS = 4
ATTN_TQ = 1024
ATTN_KC = 256
ATTN_LOOK = 2
ATTN_QSPLIT = 2
VMEM_LIMIT = 56 * 1024 * 1024
NEG_INF = float("-inf")
LOG2E = 1.4426950408889634
AUG = 16


def _dot(a, b):
    return jnp.dot(a.astype(BF16), b.astype(BF16), preferred_element_type=F32)


def _dot_nt(a, b):
    return lax.dot_general(a.astype(BF16), b.astype(BF16), (((1,), (1,)), ((), ())),
                           preferred_element_type=F32)


def _dot_select_exact(x, sel):
    hi = x.astype(BF16)
    r1 = x - hi.astype(F32)
    mid = r1.astype(BF16)
    lo = (r1 - mid.astype(F32)).astype(BF16)
    d = lambda p: jnp.dot(p, sel, preferred_element_type=F32)
    return (d(hi) + d(mid)) + d(lo)


def _rms(x, g, eps=1e-6):
    return x * lax.rsqrt(jnp.mean(x * x, axis=-1, keepdims=True) + eps) * g


def _sigmoid(x):
    return 1.0 / (1.0 + jnp.exp(-x))


def _seg_cumsum(v, seg):
    row = lax.broadcasted_iota(jnp.int32, v.shape, 0)
    pos = jnp.bitwise_and(row, seg - 1)
    s = 1
    while s < seg:
        v = v + jnp.where(pos >= s, pltpu.roll(v, s, 0), 0.0)
        s *= 2
    return v


def _const_spec(shape):
    nd = len(shape)
    return pl.BlockSpec(shape, lambda *_: (0,) * nd, pipeline_mode=pl.Buffered(1))


def _params(*sem):
    return pltpu.CompilerParams(dimension_semantics=sem, vmem_limit_bytes=VMEM_LIMIT)


def _inproj_kernel(x_ref, gpre_ref, wbig_ref, wsm_ref, sp_ref, convw_ref, convinit_ref, gcv_ref,
                   bcv_ref, ws_ref, bs_ref, gco_ref, hsum_ref, pmat_ref,
                   qaug_ref, ka_ref, va_ref, kaug_ref, vab_ref, elem_ref, cum_ref, qb_ref, kb_ref,
                   vb_ref, bz_ref, oc_ref, vn_ref, ytail_ref,
                   carry_conv, carry_cum, *, tm, cm, nh, wa, wb, wc, scale, kv_time_minor, nsub, seg):
    if not seg:
        @pl.when(pl.program_id(1) == 0)
        def _():
            carry_cum[...] = jnp.zeros_like(carry_cum)
            carry_conv[...] = convinit_ref[...]

    r = tm // nsub
    o_c = 3 * wb
    o_a = o_c + 2 * wc

    def project(rs):
        h = _rms(x_ref[rs, :], gpre_ref[...]).astype(BF16)
        proj = lambda w: lax.dot_general(h, w, (((1,), (1,)), ((), ())), preferred_element_type=F32)
        return (proj(wbig_ref[:o_c, :]),
                proj(wbig_ref[o_c:o_a, :]),
                proj(wsm_ref[...]),
                proj(wbig_ref[o_a:, :]))

    def finish(rs, y, zc, zs, za, prev, cum_in):
        ka = za[:, wa:2 * wa]
        va = za[:, 2 * wa:3 * wa]
        if kv_time_minor:
            ka_ref[:, rs] = ka.T
            va_ref[:, rs] = va.T
        else:
            ka_ref[rs, :] = ka
            va_ref[rs, :] = va
        vab_ref[rs, :] = va.astype(BF16)
        bz_ref[rs, :] = za[:, 3 * wa:]

        lane = lax.broadcasted_iota(jnp.int32, (r, LANES), 1)
        zb = zs + sp_ref[0:1, :]
        soft_tail = jnp.log1p(jnp.exp(-jnp.abs(zb)))
        logf = -(jnp.maximum(-zb, 0.0) + soft_tail)
        gl = -jnp.exp(sp_ref[1:2, :]) * (jnp.maximum(zb, 0.0) + soft_tail)
        beta = _sigmoid(zs)
        elem = jnp.where(lane < nh, logf, jnp.where(lane < 2 * nh, gl, jnp.where(lane < 3 * nh, beta, 0.0)))
        elem_ref[rs, :] = elem
        cum = _seg_cumsum(elem, seg) if seg else _seg_cumsum(elem, r) + cum_in
        cum_ref[rs, :] = cum

        c2 = jnp.where(lane < nh, cum * LOG2E, 0.0)
        hi = c2.astype(BF16)
        r1 = c2 - hi.astype(F32)
        mid = r1.astype(BF16)
        lo = (r1 - mid.astype(F32)).astype(BF16)
        placed = jnp.dot(jnp.concatenate([hi, mid, lo], axis=1), pmat_ref[...], preferred_element_type=F32)
        augq = (placed[:, :LANES] + sp_ref[2:3, :]).astype(BF16)
        augk = (sp_ref[3:4, :] - placed[:, LANES:]).astype(BF16)
        qs = (za[:, :wa] * (scale * LOG2E)).astype(BF16)
        ks = ka.astype(BF16)
        qaug_ref[rs, :] = jnp.concatenate(
            [a for j in range(0, wa, LANES) for a in (qs[:, j:j + LANES], augq)], axis=1)
        kaug_ref[rs, :] = jnp.concatenate(
            [a for j in range(0, wa, LANES) for a in (ks[:, j:j + LANES], augk)], axis=1)

        row8 = lax.broadcasted_iota(jnp.int32, prev.shape, 0)
        pos = jnp.bitwise_and(lax.broadcasted_iota(jnp.int32, y.shape, 0), max(seg, 1) - 1)
        kw = convw_ref.shape[0]
        acc = y * convw_ref[kw - 1:kw, :]
        for k in range(1, kw):
            yk = pltpu.roll(y, k, 0)
            if seg:
                yk = jnp.where(pos < k, convinit_ref[k - 1, rs, :], yk)
            else:
                top = jnp.where(row8 < k, pltpu.roll(prev, k, 0), yk[0:SUBLANES])
                yk = jnp.concatenate([top, yk[SUBLANES:]], axis=0)
            acc = acc + yk * convw_ref[kw - 1 - k:kw - k, :]
        if seg:
            ytail_ref[rs, :] = y
        yc = acc * _sigmoid(acc)
        qb = yc[:, :wb]
        kb = yc[:, wb:2 * wb]
        sq = jnp.concatenate([qb * qb, kb * kb], axis=-1).astype(BF16)
        hw = hsum_ref.shape[0]
        ss = jnp.concatenate([jnp.dot(sq[:, j:j + hw], hsum_ref[...], preferred_element_type=F32)
                              for j in range(0, 2 * wb, hw)], axis=-1)
        qb_ref[rs, :] = qb * lax.rsqrt(ss[:, :wb] + 1e-6) * scale
        kb_ref[rs, :] = kb * lax.rsqrt(ss[:, wb:] + 1e-6)
        vb_ref[rs, :] = yc[:, 2 * wb:]

        u = jax.nn.gelu(zc[:, :wc])
        gv = jax.nn.gelu(zc[:, wc:])
        mu = jnp.mean(gv, axis=-1, keepdims=True)
        var = jnp.mean(jnp.square(gv - mu), axis=-1, keepdims=True)
        vn = (gv - mu) * lax.rsqrt(var + 1e-5) * gcv_ref[...] + bcv_ref[...]
        vn_ref[rs, :] = vn
        first = lax.broadcasted_iota(jnp.int32, (cm, LANES), 1) < HEAD_DIM
        kpad = ws_ref.shape[2] - 2 * cm
        rows = []
        for c in range(r // cm):
            vc = vn[c * cm:(c + 1) * cm]
            cols = []
            for pp in range(wc // LANES):
                vp = vc[:, pp * LANES:(pp + 1) * LANES]
                parts = [jnp.where(first, vp, 0.0), jnp.where(first, 0.0, vp)]
                if kpad:
                    parts.append(jnp.zeros((kpad, LANES), F32))
                cols.append(_dot(ws_ref[pp], jnp.concatenate(parts, axis=0)))
            s = jnp.concatenate(cols, axis=-1) + bs_ref[...]
            rows.append(u[c * cm:(c + 1) * cm] * s)
        oc = rows[0] if len(rows) == 1 else jnp.concatenate(rows, axis=0)
        oc_ref[rs, :] = _rms(oc, gco_ref[...])
        return y[r - SUBLANES:r], cum[r - 1:r, :]

    subs = [slice(i * r, (i + 1) * r) for i in range(nsub)]
    projected = [project(rs) for rs in subs]
    if seg:
        prev, cum_in = jnp.zeros(carry_conv.shape, F32), jnp.zeros(carry_cum.shape, F32)
    else:
        prev, cum_in = carry_conv[...], carry_cum[...]
    for rs, z in zip(subs, projected):
        prev, cum_in = finish(rs, *z, prev, cum_in)
    if not seg:
        carry_conv[...] = prev
        carry_cum[...] = cum_in
        ytail_ref[...] = prev


def _inproj_kernel_inplace(*refs, n_in, **kw):
    return _inproj_kernel(*refs[:n_in], *refs[n_in + 2:], **kw)


def _inproj(x, lw, conv_prev, *, cm, layer, depth, kv_prev):
    b0, t0, d = x.shape
    wa, wb, wc, nh = lw["wa"], lw["wb"], lw["wc"], lw["nh"]
    kw1 = conv_prev.shape[1]
    seg = 0 if t0 >= LANES else t0
    if seg:
        assert seg & (seg - 1) == 0 and seg >= SUBLANES
        x = x.reshape(1, b0 * t0, d)
        tm = _pick(b0 * t0, (512, 256, 128, 64, 32, 16))
        assert tm % seg == 0
        conv_init = jnp.stack([jnp.pad(conv_prev[:, kw1 - k:], ((0, 0), (0, t0 - k), (0, 0))) for k in range(1, kw1 + 1)])
        conv_init = conv_init.reshape(kw1, b0 * t0, -1)
    else:
        tm = _pick(t0, (512, 256, 128))
        conv_init = jnp.pad(conv_prev, ((0, 0), (SUBLANES - kw1, 0), (0, 0)))
    b, t, _ = x.shape
    nt = t // tm
    kv_time_minor = not seg
    tok = lambda w: pl.BlockSpec((None, tm, w), lambda i, j: (i, j, 0))
    per_b = lambda r, w: pl.BlockSpec((None, r, w), lambda i, j: (i, 0, 0))
    outs = [("qaug", 2 * wa, BF16), ("ka", wa, F32), ("va", wa, F32), ("kaug", 2 * wa, BF16), ("vab", wa, BF16),
            ("elem", LANES, F32), ("cum", LANES, F32), ("qb", wb, F32), ("kb", wb, F32), ("vb", wb, F32),
            ("bz", wb, F32), ("oc", wc, F32), ("vn", wc, F32)]
    out_shape = [jax.ShapeDtypeStruct((b, t, w), dt) for _, w, dt in outs]
    out_specs = [tok(w) for _, w, _ in outs]
    if kv_time_minor:
        for k in (1, 2):
            out_shape[k] = jax.ShapeDtypeStruct((depth, b, wa, t), F32)
            out_specs[k] = pl.BlockSpec((None, None, wa, tm), lambda i, j: (layer, i, 0, j))
    if seg:
        out_shape.append(jax.ShapeDtypeStruct((b, t, 3 * wb), F32))
        out_specs.append(tok(3 * wb))
        init_spec = pl.BlockSpec((kw1, tm, 3 * wb), lambda i, j: (0, j, 0))
    else:
        out_shape.append(jax.ShapeDtypeStruct((b, SUBLANES, 3 * wb), F32))
        out_specs.append(per_b(SUBLANES, 3 * wb))
        init_spec = per_b(SUBLANES, 3 * wb)
    consts = [lw["g_pre_mix"], lw["w_big"], lw["w_small"], lw["sp"], lw["conv_w"]]
    consts2 = [lw["g_cv"], lw["b_cv"], lw["ws_cat"][cm], lw["bs_full"][cm], lw["g_c_out"], lw["hsum"], lw["pmat"]]
    nsub = 2 if tm % (2 * max(cm, LANES)) == 0 else 1
    kw = dict(tm=tm, cm=cm, nh=nh, wa=wa, wb=wb, wc=wc, scale=HEAD_DIM ** -0.5, kv_time_minor=kv_time_minor,
              nsub=nsub, seg=seg)
    in_specs = ([tok(d)] + [_const_spec(c.shape) for c in consts] + [init_spec]
                + [_const_spec(c.shape) for c in consts2])
    args = [x, *consts, conv_init, *consts2]
    inplace = kv_time_minor and kv_prev is not None
    if inplace:
        kern = functools.partial(_inproj_kernel_inplace, n_in=len(args), **kw)
        aliases = {len(args): 1, len(args) + 1: 2}
        in_specs = in_specs + [pl.BlockSpec(memory_space=pl.ANY)] * 2
        args = args + list(kv_prev)
    else:
        kern = functools.partial(_inproj_kernel, **kw)
        aliases = {}
    res = pl.pallas_call(
        kern,
        grid=(b, nt),
        in_specs=in_specs,
        out_specs=out_specs,
        out_shape=out_shape,
        input_output_aliases=aliases,
        scratch_shapes=[pltpu.VMEM((SUBLANES, 3 * wb), F32), pltpu.VMEM((1, LANES), F32)],
        compiler_params=_params("arbitrary", "arbitrary"),
        name="inproj",
    )(*args)
    named = {n: r for (n, _, _), r in zip(outs, res[:-1])}
    if seg:
        named = {n: r.reshape(b0, t0, r.shape[-1]) for n, r in named.items()}
        named["conv_new"] = res[-1].reshape(b0, t0, -1)[:, t0 - kw1:, :]
    else:
        named["conv_new"] = res[-1][:, SUBLANES - kw1:, :]
    named["kv_time_minor"] = kv_time_minor
    return named


def _attn_kernel(qt_ref, k_ref, vt_ref, o_ref, *, tq, kc, look, qsplit):
    p = pl.program_id(1)
    i = pl.program_id(2)
    qt = qt_ref[...]
    rowi = lax.broadcasted_iota(jnp.int32, qt.shape, 0)
    zero = jnp.zeros_like(qt)
    qts = []
    for e in range(2):
        a0 = LANES + AUG * (2 * p + e)
        keep = ((rowi >= e * HEAD_DIM) & (rowi < (e + 1) * HEAD_DIM)) | ((rowi >= a0) & (rowi < a0 + AUG))
        qts.append(jnp.where(keep, qt, zero))
    wq = tq // qsplit
    kofs = lax.broadcasted_iota(jnp.int32, (kc, wq), 0)
    qofs = lax.broadcasted_iota(jnp.int32, (kc, wq), 1)
    ones = jnp.ones((2 * SUBLANES, kc), BF16)
    units = [(c, e, h) for c in range(tq // kc) for e in range(2) for h in range(qsplit)]
    slots = [(e, h) for e in range(2) for h in range(qsplit)]

    def trim(c, h, masked):
        lo = max(c * kc - h * wq, 0) if masked else 0
        return None if lo >= wq else lo

    def scores(j, c, e, h, masked):
        k0 = pl.multiple_of(j * tq + c * kc, kc)
        lo = trim(c, h, masked)
        s = jnp.dot(k_ref[pl.ds(k0, kc), :], qts[e][:, h * wq + lo:(h + 1) * wq], preferred_element_type=F32)
        return jnp.concatenate([jnp.full((kc, lo), NEG_INF, F32), s], axis=1) if lo else s

    def fold(j, c, e, h, s, st, masked):
        m, l, acc = st
        k0 = pl.multiple_of(j * tq + c * kc, kc)
        lo = trim(c, h, masked)
        if masked:
            s = jnp.where(c * kc + kofs <= h * wq + qofs, s, NEG_INF)
        m_new = jnp.maximum(m, jnp.max(s, axis=0, keepdims=True))
        alpha = jnp.exp2(m - m_new)
        pt = jnp.exp2(s - m_new).astype(BF16)
        vt = jnp.concatenate([vt_ref[e * HEAD_DIM:(e + 1) * HEAD_DIM, pl.ds(k0, kc)], ones], axis=0)
        r = jnp.dot(vt, pt[:, lo:], preferred_element_type=F32)
        if lo:
            r = jnp.concatenate([jnp.zeros((r.shape[0], lo), F32), r], axis=1)
        return m_new, alpha * l + r[HEAD_DIM:HEAD_DIM + 1], alpha * acc + r[:HEAD_DIM]

    def run(blocks, state):
        state = dict(zip(slots, state))
        todo = [(j, c, e, h, masked) for j, masked in blocks for c, e, h in units
                if trim(c, h, masked) is not None]
        pend = {}
        for k in range(min(look, len(todo))):
            pend[k] = scores(*todo[k])
        for k, (j, c, e, h, masked) in enumerate(todo):
            if k + look < len(todo):
                pend[k + look] = scores(*todo[k + look])
            state[e, h] = fold(j, c, e, h, pend.pop(k), state[e, h], masked)
        return tuple(state[sl] for sl in slots)

    st0 = (jnp.full((1, wq), NEG_INF, F32), jnp.zeros((1, wq), F32), jnp.zeros((HEAD_DIM, wq), F32))
    state = lax.fori_loop(0, i // 2, lambda t, s: run([(2 * t, False), (2 * t + 1, False)], s),
                          (st0,) * len(slots))
    state = lax.cond(i % 2 == 1,
                     lambda s: run([(i - 1, False), (i, True)], s),
                     lambda s: run([(i, True)], s), state)
    done = dict(zip(slots, state))
    ot = jnp.concatenate([jnp.concatenate([done[e, h][2] / done[e, h][1] for h in range(qsplit)], axis=1)
                          for e in range(2)], axis=0)
    o_ref[...] = ot.T


def _attn_prompt(qaug, kaug, vab, *, tq, kc, look, qsplit):
    b, s, wa = vab.shape
    npair = wa // LANES
    qt = jnp.transpose(qaug, (0, 2, 1))
    vt = jnp.transpose(vab, (0, 2, 1))
    kern = functools.partial(_attn_kernel, tq=tq, kc=kc, look=look, qsplit=qsplit)
    return pl.pallas_call(
        kern,
        grid=(b, npair, s // tq),
        in_specs=[pl.BlockSpec((None, 2 * LANES, tq), lambda bi, p, i: (bi, p, i)),
                  pl.BlockSpec((None, s, 2 * LANES), lambda bi, p, i: (bi, 0, p)),
                  pl.BlockSpec((None, LANES, s), lambda bi, p, i: (bi, p, 0))],
        out_specs=pl.BlockSpec((None, tq, LANES), lambda bi, p, i: (bi, i, p)),
        out_shape=jax.ShapeDtypeStruct((b, s, wa), F32),
        compiler_params=_params("arbitrary", "arbitrary", "arbitrary"),
        name="attn_prompt",
    )(qt, kaug, vt)


def _attn_sample_kernel(q_ref, kc_ref, vc_ref, kn_ref, vn_ref, cum_ref, rrow_ref, crow_ref, o_ref, *, n):
    p = pl.program_id(1)
    q = q_ref[:, :LANES]
    lane = lax.broadcasted_iota(jnp.int32, (n, LANES), 1)
    first = lane < HEAD_DIM
    zero = jnp.zeros_like(q)
    past = kc_ref.shape[-1]
    kc = kc_ref[...].reshape(LANES, past).astype(BF16)
    vc = vc_ref[...].reshape(LANES, past).astype(BF16)
    pad = jnp.zeros((LANES - n, LANES), BF16)
    kn = jnp.concatenate([kn_ref[:, :LANES], pad], axis=0)
    vn = jnp.concatenate([vn_ref[...], pad], axis=0)
    cum = cum_ref[...]
    qm = jnp.concatenate([jnp.where(first, q, zero), jnp.where(first, zero, q)], axis=0)
    cq = jnp.concatenate([jnp.sum(jnp.where(lane == 2 * p + e, cum, 0.0), axis=-1, keepdims=True)
                          for e in range(2)], axis=0)
    top = lax.broadcasted_iota(jnp.int32, (2 * n, 1), 0) < n
    rrow = jnp.where(top, rrow_ref[0:1, :], rrow_ref[1:2, :])
    crow = jnp.where(top, crow_ref[0:1, :], crow_ref[1:2, :])
    qrow = lax.broadcasted_iota(jnp.int32, (2 * n, LANES), 0)
    causal = lax.broadcasted_iota(jnp.int32, (2 * n, LANES), 1) <= jnp.where(qrow < n, qrow, qrow - n)
    sc = _dot(qm, kc) + LOG2E * (cq + rrow)
    sn = jnp.where(causal, _dot_nt(qm, kn) + LOG2E * (cq - crow), NEG_INF)
    m = jnp.maximum(jnp.max(sc, axis=-1, keepdims=True), jnp.max(sn, axis=-1, keepdims=True))
    pc = jnp.exp2(sc - m)
    pn = jnp.exp2(sn - m)
    l = jnp.sum(pc, axis=-1, keepdims=True) + jnp.sum(pn, axis=-1, keepdims=True)
    o = (_dot_nt(pc, vc) + _dot(pn, vn)) / l
    o_ref[...] = jnp.where(first, o[:n], o[n:])


def _attn_sample(qaug, kaug, vab, cum, cache_kt, cache_vt, layer, rrow, crow):
    b, n, wa = vab.shape
    past = cache_kt.shape[-1]
    npair = wa // LANES
    new = lambda w: pl.BlockSpec((None, n, w), lambda bi, p: (bi, 0, p))
    old = lambda: pl.BlockSpec((None, None, 2, HEAD_DIM, past), lambda bi, p: (layer, bi, p, 0, 0))
    return pl.pallas_call(
        functools.partial(_attn_sample_kernel, n=n),
        grid=(b, npair),
        in_specs=[new(2 * LANES), old(), old(), new(2 * LANES), new(LANES),
                  pl.BlockSpec((None, n, LANES), lambda bi, p: (bi, 0, 0)),
                  pl.BlockSpec((None, None, 2, past), lambda bi, p: (bi, p, 0, 0)),
                  pl.BlockSpec((None, None, 2, LANES), lambda bi, p: (bi, p, 0, 0))],
        out_specs=new(LANES),
        out_shape=jax.ShapeDtypeStruct((b, n, wa), F32),
        compiler_params=_params("arbitrary", "arbitrary"),
        name="attn_sample",
    )(qaug, cache_kt, cache_vt, kaug, vab, cum, rrow, crow)


def _suffix_kernel(x_ref, o_ref):
    v = x_ref[...]
    n = v.shape[1]
    lane = lax.broadcasted_iota(jnp.int32, v.shape, 1)
    s = 1
    while s < n:
        v = v + jnp.where(lane + s < n, pltpu.roll(v, n - s, 1), 0.0)
        s *= 2
    o_ref[...] = jnp.where(lane + 1 < n, pltpu.roll(v, n - 1, 1), 0.0)


def _exclusive_suffix_sum(x, layer):
    _, h, b, p = x.shape
    return pl.pallas_call(
        _suffix_kernel,
        grid=(h,),
        in_specs=[pl.BlockSpec((None, None, b, p), lambda i: (layer, i, 0, 0))],
        out_specs=pl.BlockSpec((None, b, p), lambda i: (i, 0, 0)),
        out_shape=jax.ShapeDtypeStruct((h, b, p), F32),
        compiler_params=_params("arbitrary"),
        name="suffix_sum",
    )(x)


def _gdn_kernel(q_ref, k_ref, v_ref, bz_ref, elem_ref, s0_ref, gb_ref, esel_ref, hsum_ref,
                o_ref, sout_ref, s_scr, *, nb, nh, chained):
    L = GDN_BLOCK
    t = pl.program_id(1)

    if chained:
        @pl.when(t == 0)
        def _():
            s_scr[...] = s0_ref[...]

    lane = lax.broadcasted_iota(jnp.int32, (L, LANES), 1)
    first = lane < HEAD_DIM
    ri = lax.broadcasted_iota(jnp.int32, (L, L), 0)
    ci = lax.broadcasted_iota(jnp.int32, (L, L), 1)
    incl = ci <= ri
    strict = ci < ri
    same_head = (ri < HEAD_DIM) == (ci < HEAD_DIM)
    lane2 = lax.broadcasted_iota(jnp.int32, (L, 2 * L), 1)
    first2 = jnp.bitwise_and(lane2, LANES - 1) < HEAD_DIM
    xor2 = jnp.bitwise_xor(lax.broadcasted_iota(jnp.int32, (L, 2 * L), 0), jnp.bitwise_and(lane2, L - 1))
    zero_ll = jnp.zeros((L, L), BF16)

    def halves(x, sel):
        return jnp.concatenate([jnp.where(sel, x, 0.0), jnp.where(sel, 0.0, x)], axis=0)

    def dot_heads(y, x):
        xb = x.astype(BF16)
        bd = jnp.concatenate([jnp.concatenate([xb[:, :L], zero_ll], axis=1),
                              jnp.concatenate([zero_ll, xb[:, L:]], axis=1)], axis=0)
        return jnp.dot(y.astype(BF16), bd, preferred_element_type=F32)

    npair = s_scr.shape[0]
    wbw = npair * LANES
    c = {}

    def solve_stages(blocks):
        chains = [(n, p) for n in blocks for p in range(npair)]
        ex = {}
        for n in blocks:
            elem = elem_ref[n * L:(n + 1) * L, :]
            gsum = _seg_cumsum(elem, L)
            mixed = jnp.where((lane >= nh) & (lane < 2 * nh), gsum, elem)
            ex[n] = _dot_select_exact(mixed, esel_ref[...])
        yield
        for n, p in chains:
            rows = slice(n * L, (n + 1) * L)
            cols = slice(p * LANES, (p + 1) * LANES)
            g = ex[n][:, cols]
            bt = ex[n][:, wbw + p * LANES: wbw + (p + 1) * LANES]
            kp = k_ref[rows, cols]
            qp = q_ref[rows, cols]
            g_sw = pltpu.roll(g, HEAD_DIM, 1)
            b_sw = pltpu.roll(bt, HEAD_DIM, 1)
            g_t = g.T
            a_parts, qk_parts = [], []
            for e in range(2):
                sel = first if e == 0 else jnp.logical_not(first)
                gcol = jnp.where(sel, g, g_sw)
                bcol = jnp.where(sel, bt, b_sw)
                grow = g_t[e * HEAD_DIM:e * HEAD_DIM + 1, :]
                dec = jnp.exp(jnp.where(incl, gcol - grow, NEG_INF))
                kk = _dot_nt(jnp.where(sel, kp, 0.0), kp)
                qk_parts.append(_dot_nt(jnp.where(sel, qp, 0.0), kp) * dec)
                a_parts.append(jnp.where(strict, bcol * kk * dec, 0.0))
            a_cat = jnp.concatenate(a_parts, axis=1)
            eg = jnp.exp(g)
            glast = g[L - 1:L, :]
            c[n, p] = dict(a=a_cat, qk=jnp.concatenate(qk_parts, axis=1), glast=glast, qg=qp * eg,
                           kdec=kp * jnp.exp(glast - g),
                           r=jnp.concatenate([v_ref[rows, cols] * bt, kp * bt * eg], axis=1),
                           tm1=-jnp.where(xor2 < 2, a_cat, 0.0))
        yield
        s_blk = 2
        while s_blk < L:
            pm = {}
            for key in chains:
                nmat = jnp.where((xor2 >= s_blk) & (xor2 < 2 * s_blk), c[key]["a"], 0.0)
                pm[key] = nmat + dot_heads(c[key]["tm1"], nmat)
            yield
            for key in chains:
                c[key]["tm1"] = c[key]["tm1"] - pm[key] - dot_heads(pm[key], c[key]["tm1"])
            yield
            s_blk *= 2
        for key in chains:
            r = c[key]["r"]
            c[key]["uw"] = r + _dot(c[key]["tm1"], halves(r, first2))
        yield

    def state_stages(blocks):
        pairs = range(npair)
        for n in blocks:
            rows = slice(n * L, (n + 1) * L)
            s_in = [s_scr[p] if chained else s0_ref[n, p] for p in pairs]
            ws = [_dot(jnp.concatenate([c[n, p]["uw"][:, LANES:], c[n, p]["qg"]], axis=0), s_in[p]) for p in pairs]
            yield
            u = [c[n, p]["uw"][:, :LANES] - ws[p][:L] for p in pairs]
            o = [ws[p][L:] + _dot(c[n, p]["qk"], halves(u[p], first)) for p in pairs]
            yield
            for p in pairs:
                s_new = (s_in[p] * jnp.exp(c[n, p]["glast"])
                         + jnp.where(same_head, _dot(c[n, p]["kdec"].T, u[p]), 0.0))
                if chained:
                    s_scr[p] = s_new
                else:
                    sout_ref[n, p] = s_new
            yield
            for p in pairs:
                cols = slice(p * LANES, (p + 1) * LANES)
                ms = _dot(o[p] * o[p], hsum_ref[...]) * (1.0 / HEAD_DIM)
                bz = bz_ref[rows, cols]
                o_ref[rows, cols] = o[p] * lax.rsqrt(ms + 1e-6) * gb_ref[...] * (bz * _sigmoid(bz))
            yield

    groups = [list(range(g0, min(g0 + GDN_GROUP, nb))) for g0 in range(0, nb, GDN_GROUP)]
    pending = iter(())
    for grp in groups:
        for _ in solve_stages(grp):
            next(pending, None)
        for _ in pending:
            pass
        pending = state_stages(grp)
    for _ in pending:
        pass

    if chained:
        @pl.when(t == pl.num_programs(1) - 1)
        def _():
            sout_ref[...] = s_scr[...]


def _gdn(qb, kb, vb, bz, elem, s0, lw, *, nb):
    b, t, wb = qb.shape
    npair = wb // LANES
    chained = t > GDN_BLOCK
    if not chained:
        qb, kb, vb, bz, elem = (a.reshape(1, b * t, a.shape[-1]) for a in (qb, kb, vb, bz, elem))
        nb = _pick(b, (GDN_TILE_BLOCKS, 2, 1))
    rows, total = qb.shape[:2]
    tile = nb * GDN_BLOCK
    tok = lambda w: pl.BlockSpec((None, tile, w), lambda i, j: (i, j, 0))
    if chained:
        st = pl.BlockSpec((None, npair, LANES, LANES), lambda i, j: (i, 0, 0, 0))
    else:
        st = pl.BlockSpec((nb, npair, LANES, LANES), lambda i, j: (j, 0, 0, 0))
    consts = [lw["g_b_pair"], lw["esel"], lw["hsum128"]]
    ob, s_new = pl.pallas_call(
        functools.partial(_gdn_kernel, nb=nb, nh=lw["nh"], chained=chained),
        grid=(rows, total // tile),
        in_specs=[tok(wb), tok(wb), tok(wb), tok(wb), tok(LANES), st] + [_const_spec(c.shape) for c in consts],
        out_specs=[tok(wb), st],
        out_shape=[jax.ShapeDtypeStruct((rows, total, wb), F32),
                   jax.ShapeDtypeStruct((b, npair, LANES, LANES), F32)],
        scratch_shapes=[pltpu.VMEM((npair, LANES, LANES), F32)],
        compiler_params=_params("arbitrary", "arbitrary"),
        name="gdn",
    )(qb, kb, vb, bz, elem, s0, *consts)
    return ob.reshape(b, t, wb), s_new


def _outffn_kernel(oa_ref, ob_ref, oc_ref, x_ref, ga_ref, wout_ref, gpm_ref, gpf_ref, wfi_ref, wfo_ref,
                   gpo_ref, y_ref, *, dff, nsub):
    r = x_ref.shape[0] // nsub
    rows = [slice(i * r, (i + 1) * r) for i in range(nsub)]
    cat = [jnp.concatenate([_rms(oa_ref[rs, :], ga_ref[...]), ob_ref[rs, :], oc_ref[rs, :]], axis=-1).astype(BF16)
           for rs in rows]
    m = [jnp.dot(c, wout_ref[...], preferred_element_type=F32) for c in cat]
    x1 = [x_ref[rs, :] + _rms(mi, gpm_ref[...]) for rs, mi in zip(rows, m)]
    h = [_rms(xi, gpf_ref[...]).astype(BF16) for xi in x1]
    gu = [jnp.dot(hi, wfi_ref[...], preferred_element_type=F32) for hi in h]
    a = [(g[:, :dff] * _sigmoid(g[:, :dff]) * g[:, dff:]).astype(BF16) for g in gu]
    f = [jnp.dot(ai, wfo_ref[...], preferred_element_type=F32) for ai in a]
    for rs, xi, fi in zip(rows, x1, f):
        y_ref[rs, :] = xi + _rms(fi, gpo_ref[...])


def _outffn(oa, ob, oc, x, lw, *, tm):
    n, d = x.shape
    dff = lw["w_ffn_out"].shape[0]
    tok = lambda w: pl.BlockSpec((tm, w), lambda i: (i, 0))
    consts = [lw["g_a_out"], lw["w_out"], lw["g_post_mix"], lw["g_pre_ffn"], lw["w_ffn_in"],
              lw["w_ffn_out"], lw["g_post_ffn"]]
    return pl.pallas_call(
        functools.partial(_outffn_kernel, dff=dff, nsub=2 if tm % (4 * SUBLANES) == 0 else 1),
        grid=(n // tm,),
        in_specs=[tok(oa.shape[1]), tok(ob.shape[1]), tok(oc.shape[1]), tok(d)]
                 + [_const_spec(c.shape) for c in consts],
        out_specs=tok(d),
        out_shape=jax.ShapeDtypeStruct((n, d), F32),
        compiler_params=_params("arbitrary"),
        name="outffn",
    )(oa, ob, oc, x, *consts)


def _block_ones(width):
    idx = jnp.arange(width) // HEAD_DIM
    return (idx[:, None] == idx[None, :]).astype(BF16)


def _layer_weights(l, prm, cms):
    w_in = prm["w_in"][l]
    nh = prm["b_f"].shape[1]
    wa = nh * HEAD_DIM
    wb = prm["a_log"].shape[1] * HEAD_DIM
    wc = prm["g_cv"].shape[1]
    ng = prm["w_s"].shape[1]
    assert prm["a_log"].shape[1] == nh and wa % LANES == 0 and wc % LANES == 0 and AUG * nh <= LANES
    sizes = (wa, wa, wa, nh, 3 * wb, nh, nh, wb, wc, wc)
    offs = [0]
    for sz in sizes:
        offs.append(offs[-1] + sz)
    w_in_t = w_in.T
    col = lambda i: w_in_t[offs[i]:offs[i + 1]]
    w_big = jnp.concatenate([col(4), col(8), col(9), col(0), col(1), col(2), col(7)], axis=0).astype(BF16)
    w_small = jnp.concatenate([col(3), col(5), col(6), jnp.zeros((LANES - 3 * nh, w_in.shape[0]), F32)],
                              axis=0).astype(BF16)
    zpad = jnp.zeros((LANES - 2 * nh,), F32)
    sp = jnp.zeros((SUBLANES, LANES), F32)
    sp = sp.at[0].set(jnp.concatenate([prm["b_f"][l], prm["dt_bias"][l], zpad]))
    sp = sp.at[1].set(jnp.concatenate([jnp.zeros((nh,), F32), prm["a_log"][l], zpad]))
    hl = jnp.arange(nh) * AUG
    sp = sp.at[2, (hl[:, None] + jnp.arange(3, 6)[None, :]).reshape(-1)].set(1.0)
    sp = sp.at[3, (hl[:, None] + jnp.arange(0, 3)[None, :]).reshape(-1)].set(1.0)
    pmat = jnp.zeros((3 * LANES, 2 * LANES), F32)
    for piece in range(3):
        pmat = pmat.at[piece * LANES + jnp.arange(nh), hl + piece].set(1.0)
        pmat = pmat.at[piece * LANES + jnp.arange(nh), LANES + hl + 3 + piece].set(1.0)
    row = lambda v: v.reshape(1, -1)
    ws_cat, bs_full = {}, {}
    for cm in cms:
        pos = jnp.arange(cm) // HEAD_DIM
        w = jnp.where(pos[None, :] <= pos[:, None], prm["w_s"][l][:, :cm, :cm], 0.0)
        pairs = [jnp.concatenate([w[2 * pp], w[2 * pp + 1]], axis=1) for pp in range(ng // 2)]
        kpad = max(LANES - 2 * cm, 0)
        ws_cat[cm] = jnp.pad(jnp.stack(pairs), ((0, 0), (0, 0), (0, kpad))).astype(BF16)
        bs_full[cm] = jnp.repeat(prm["b_s"][l][:, :cm].T, wc // ng, axis=1)
    src = jnp.arange(LANES)[:, None]
    dst = jnp.arange(wb)[None, :] // HEAD_DIM
    esel = jnp.concatenate([src == nh + dst, src == 2 * nh + dst], axis=1).astype(BF16)
    return dict(
        nh=nh, wa=wa, wb=wb, wc=wc,
        g_pre_mix=row(prm["g_pre_mix"][l]), w_big=w_big, w_small=w_small, sp=sp, conv_w=prm["conv_w"][l],
        g_cv=row(prm["g_cv"][l]), b_cv=row(prm["b_cv"][l]), ws_cat=ws_cat, bs_full=bs_full,
        g_c_out=row(prm["g_c_out"][l]), hsum=_block_ones(2 * LANES), hsum128=_block_ones(LANES), pmat=pmat.astype(BF16),
        g_b_pair=row(jnp.tile(prm["g_b_out"][l], LANES // HEAD_DIM)), esel=esel,
        g_a_out=row(prm["g_a_out"][l]), w_out=prm["w_out"][l].astype(BF16),
        g_post_mix=row(prm["g_post_mix"][l]), g_pre_ffn=row(prm["g_pre_ffn"][l]),
        w_ffn_in=prm["w_ffn_in"][l].astype(BF16), w_ffn_out=prm["w_ffn_out"][l].astype(BF16),
        g_post_ffn=row(prm["g_post_ffn"][l]))


def _pair_state(s):
    b, h, dk, dv = s.shape
    s = s.reshape(b, h // 2, 2, dk, dv)
    z = jnp.zeros_like(s[:, :, 0])
    top = jnp.concatenate([s[:, :, 0], z], axis=-1)
    bot = jnp.concatenate([z, s[:, :, 1]], axis=-1)
    return jnp.concatenate([top, bot], axis=-2)


def _unpair_state(sp):
    b, hp, _, _ = sp.shape
    s0 = sp[:, :, :HEAD_DIM, :HEAD_DIM]
    s1 = sp[:, :, HEAD_DIM:, HEAD_DIM:]
    return jnp.stack([s0, s1], axis=2).reshape(b, 2 * hp, HEAD_DIM, HEAD_DIM)


def _head_rows(cum, nh):
    b, t, _ = cum.shape
    return jnp.transpose(cum[:, :, :nh], (0, 2, 1)).reshape(b, nh // 2, 2, t)


def _pick(n, prefs):
    for c in prefs:
        if n % c == 0:
            return c
    return n


def _layer(x, lw, conv_prev, s0, cache, *, cm, layer, depth, kv_prev):
    b, t, d = x.shape
    nh, wb = lw["nh"], lw["wb"]
    pj = _inproj(x, lw, conv_prev, cm=cm, layer=layer, depth=depth, kv_prev=kv_prev)

    if cache is None:
        tq = _pick(t, (ATTN_TQ, 256, 128))
        kc = min(ATTN_KC, tq)
        qsplit = ATTN_QSPLIT if tq % (ATTN_QSPLIT * kc) == 0 else 1
        oa = _attn_prompt(pj["qaug"], pj["kaug"], pj["vab"], tq=tq, kc=kc, look=ATTN_LOOK, qsplit=qsplit)
    else:
        ck, cv, clogf_t = cache
        _, _, bs, past = clogf_t.shape
        excl = _exclusive_suffix_sum(clogf_t, layer)
        rrow = jnp.transpose(excl, (1, 0, 2)).reshape(bs, nh // 2, 2, past)
        crow = jnp.pad(_head_rows(pj["cum"], nh), ((0, 0), (0, 0), (0, 0), (0, LANES - t)))
        oa = _attn_sample(pj["qaug"], pj["kaug"], pj["vab"], pj["cum"], ck, cv, layer, rrow, crow)

    tp = -(-t // GDN_BLOCK) * GDN_BLOCK
    padt = lambda a: a if tp == t else jnp.pad(a, ((0, 0), (0, tp - t), (0, 0)))
    nb = _pick(tp // GDN_BLOCK, (GDN_TILE_BLOCKS, 2, 1))
    ob, s_new = _gdn(padt(pj["qb"]), padt(pj["kb"]), padt(pj["vb"]), padt(pj["bz"]), padt(pj["elem"]),
                     _pair_state(s0), lw, nb=nb)
    ob = ob[:, :t]

    n = b * t
    y = _outffn(oa.reshape(n, -1), ob.reshape(n, -1), pj["oc"].reshape(n, -1), x.reshape(n, d), lw,
                tm=_pick(n, (512, 256, 128, 64, 32, 16)))
    if pj["kv_time_minor"]:
        new_kv = (pj["ka"], pj["va"])
    else:
        new_kv = (pj["ka"].reshape(b, t, nh, HEAD_DIM), pj["va"].reshape(b, t, nh, HEAD_DIM))
    state = (new_kv[0], new_kv[1], pj["elem"][:, :, :nh],
             pj["conv_new"], _unpair_state(s_new), pj["vn"])
    return y.reshape(b, t, d), state, pj["kv_time_minor"]


def kernel(x_prompt, x_sample, cache_a_k, cache_a_v, cache_a_logf, state_b_conv, state_b_S, g_pre_mix, w_in, b_f, conv_w, a_log, dt_bias, g_b_out, g_a_out, g_cv, b_cv, w_s, b_s, g_c_out, w_out, g_post_mix, g_pre_ffn, w_ffn_in, w_ffn_out, g_post_ffn):
    prm = dict(g_pre_mix=g_pre_mix, w_in=w_in, b_f=b_f, conv_w=conv_w, a_log=a_log, dt_bias=dt_bias,
               g_b_out=g_b_out, g_a_out=g_a_out, g_cv=g_cv, b_cv=b_cv, w_s=w_s, b_s=b_s, g_c_out=g_c_out,
               w_out=w_out, g_post_mix=g_post_mix, g_pre_ffn=g_pre_ffn, w_ffn_in=w_ffn_in,
               w_ffn_out=w_ffn_out, g_post_ffn=g_post_ffn)
    depth = w_in.shape[0]
    bp, sp_len, _ = x_prompt.shape
    n_new = x_sample.shape[1]
    cm_p = w_s.shape[2]
    assert sp_len % cm_p == 0 and sp_len % GDN_BLOCK == 0 and n_new <= HEAD_DIM and n_new % SUBLANES == 0
    kw1 = conv_w.shape[1] - 1
    nhb = a_log.shape[1]
    yp, ys = x_prompt, x_sample
    outs_p, outs_s = [], []
    cache_kt = jnp.transpose(cache_a_k, (0, 1, 3, 4, 2))
    cache_vt = jnp.transpose(cache_a_v, (0, 1, 3, 4, 2))
    clogf_t = jnp.transpose(cache_a_logf, (0, 3, 1, 2))
    for l in range(depth):
        lw = _layer_weights(l, prm, (cm_p, n_new))
        conv0 = jnp.zeros((bp, kw1, conv_w.shape[2]), F32)
        s0 = jnp.zeros((bp, nhb, HEAD_DIM, HEAD_DIM), F32)
        kv_p = (outs_p[-1][0], outs_p[-1][1]) if outs_p else None
        kv_s = (outs_s[-1][0], outs_s[-1][1]) if outs_s else None
        yp, st_p, shared_p = _layer(yp, lw, conv0, s0, None, cm=cm_p, layer=l, depth=depth, kv_prev=kv_p)
        ys, st_s, shared_s = _layer(ys, lw, state_b_conv[l], state_b_S[l], (cache_kt, cache_vt, clogf_t),
                                    cm=n_new, layer=l, depth=depth, kv_prev=kv_s)
        outs_p.append(st_p)
        outs_s.append(st_s)
    stk = lambda outs, i: jnp.stack([o[i] for o in outs], axis=0)

    def new_cache(outs, i, shared):
        if not shared:
            return stk(outs, i)
        buf = outs[-1][i]
        dp, b, _, t = buf.shape
        return jnp.transpose(buf.reshape(dp, b, -1, HEAD_DIM, t), (0, 1, 4, 2, 3))

    return (yp, ys, new_cache(outs_p, 0, shared_p), new_cache(outs_p, 1, shared_p),
            stk(outs_p, 2), stk(outs_p, 3), stk(outs_p, 4),
            new_cache(outs_s, 0, shared_s), new_cache(outs_s, 1, shared_s),
            stk(outs_s, 2), stk(outs_s, 3), stk(outs_s, 4), stk(outs_s, 5))
```

```python
import functools

import jax
import jax.numpy as jnp
from jax import lax
from jax.experimental import pallas as pl
from jax.experimental.pallas import tpu as pltpu

F32 = jnp.float32
BF16 = jnp.bfloat16

LANES = 128
SUBLANES = 8
HEAD_DIM = 64
GDN_BLOCK = 128
GDN_GROUP = 2
GDN_TILE_BLOCKS = 4
ATTN_TQ = 1024
ATTN_KC = 256
ATTN_LOOK = 4
ATTN_QSPLIT = 4
VMEM_LIMIT = 56 * 1024 * 1024
NEG_INF = float("-inf")
LOG2E = 1.4426950408889634
AUG = 16


def _dot(a, b):
    return jnp.dot(a.astype(BF16), b.astype(BF16), preferred_element_type=F32)


def _dot_nt(a, b):
    return lax.dot_general(a.astype(BF16), b.astype(BF16), (((1,), (1,)), ((), ())),
                           preferred_element_type=F32)


def _dot_select_exact(x, sel):
    hi = x.astype(BF16)
    r1 = x - hi.astype(F32)
    mid = r1.astype(BF16)
    lo = (r1 - mid.astype(F32)).astype(BF16)
    d = lambda p: jnp.dot(p, sel, preferred_element_type=F32)
    return (d(hi) + d(mid)) + d(lo)


def _rms(x, g, eps=1e-6):
    return x * lax.rsqrt(jnp.mean(x * x, axis=-1, keepdims=True) + eps) * g


def _sigmoid(x):
    return 1.0 / (1.0 + jnp.exp(-x))


def _seg_cumsum(v, seg):
    row = lax.broadcasted_iota(jnp.int32, v.shape, 0)
    pos = jnp.bitwise_and(row, seg - 1)
    s = 1
    while s < seg:
        v = v + jnp.where(pos >= s, pltpu.roll(v, s, 0), 0.0)
        s *= 2
    return v


def _const_spec(shape):
    nd = len(shape)
    return pl.BlockSpec(shape, lambda *_: (0,) * nd, pipeline_mode=pl.Buffered(1))


def _params(*sem):
    return pltpu.CompilerParams(dimension_semantics=sem, vmem_limit_bytes=VMEM_LIMIT)


def _inproj_kernel(x_ref, gpre_ref, wbig_ref, wsm_ref, sp_ref, convw_ref, convinit_ref, gcv_ref,
                   bcv_ref, ws_ref, bs_ref, gco_ref, hsum_ref, pmat_ref,
                   qaug_ref, ka_ref, va_ref, kaug_ref, vab_ref, elem_ref, cum_ref, qb_ref, kb_ref,
                   vb_ref, bz_ref, oc_ref, vn_ref, ytail_ref,
                   carry_conv, carry_cum, *, tm, cm, nh, wa, wb, wc, scale, kv_time_minor, nsub, seg):
    if not seg:
        @pl.when(pl.program_id(1) == 0)
        def _():
            carry_cum[...] = jnp.zeros_like(carry_cum)
            carry_conv[...] = convinit_ref[...]

    r = tm // nsub
    o_c = 3 * wb
    o_a = o_c + 2 * wc

    def project(rs):
        h = _rms(x_ref[rs, :], gpre_ref[...]).astype(BF16)
        proj = lambda w: lax.dot_general(h, w, (((1,), (1,)), ((), ())), preferred_element_type=F32)
        return (proj(wbig_ref[:o_c, :]),
                proj(wbig_ref[o_c:o_a, :]),
                proj(wsm_ref[...]),
                proj(wbig_ref[o_a:, :]))

    def finish(rs, y, zc, zs, za, prev, cum_in):
        ka = za[:, wa:2 * wa]
        va = za[:, 2 * wa:3 * wa]
        if kv_time_minor:
            ka_ref[:, rs] = ka.T
            va_ref[:, rs] = va.T
        else:
            ka_ref[rs, :] = ka
            va_ref[rs, :] = va
        vab_ref[rs, :] = va.astype(BF16)
        bz_ref[rs, :] = za[:, 3 * wa:]

        lane = lax.broadcasted_iota(jnp.int32, (r, LANES), 1)
        zb = zs + sp_ref[0:1, :]
        soft_tail = jnp.log1p(jnp.exp(-jnp.abs(zb)))
        logf = -(jnp.maximum(-zb, 0.0) + soft_tail)
        gl = -jnp.exp(sp_ref[1:2, :]) * (jnp.maximum(zb, 0.0) + soft_tail)
        beta = _sigmoid(zs)
        elem = jnp.where(lane < nh, logf, jnp.where(lane < 2 * nh, gl, jnp.where(lane < 3 * nh, beta, 0.0)))
        elem_ref[rs, :] = elem
        cum = _seg_cumsum(elem, seg) if seg else _seg_cumsum(elem, r) + cum_in
        cum_ref[rs, :] = cum

        c2 = jnp.where(lane < nh, cum * LOG2E, 0.0)
        hi = c2.astype(BF16)
        r1 = c2 - hi.astype(F32)
        mid = r1.astype(BF16)
        lo = (r1 - mid.astype(F32)).astype(BF16)
        placed = jnp.dot(jnp.concatenate([hi, mid, lo], axis=1), pmat_ref[...], preferred_element_type=F32)
        augq = (placed[:, :LANES] + sp_ref[2:3, :]).astype(BF16)
        augk = (sp_ref[3:4, :] - placed[:, LANES:]).astype(BF16)
        qs = (za[:, :wa] * (scale * LOG2E)).astype(BF16)
        ks = ka.astype(BF16)
        qaug_ref[rs, :] = jnp.concatenate(
            [a for j in range(0, wa, LANES) for a in (qs[:, j:j + LANES], augq)], axis=1)
        kaug_ref[rs, :] = jnp.concatenate(
            [a for j in range(0, wa, LANES) for a in (ks[:, j:j + LANES], augk)], axis=1)

        row8 = lax.broadcasted_iota(jnp.int32, prev.shape, 0)
        pos = jnp.bitwise_and(lax.broadcasted_iota(jnp.int32, y.shape, 0), max(seg, 1) - 1)
        kw = convw_ref.shape[0]
        acc = y * convw_ref[kw - 1:kw, :]
        for k in range(1, kw):
            yk = pltpu.roll(y, k, 0)
            if seg:
                yk = jnp.where(pos < k, convinit_ref[k - 1, rs, :], yk)
            else:
                top = jnp.where(row8 < k, pltpu.roll(prev, k, 0), yk[0:SUBLANES])
                yk = jnp.concatenate([top, yk[SUBLANES:]], axis=0)
            acc = acc + yk * convw_ref[kw - 1 - k:kw - k, :]
        if seg:
            ytail_ref[rs, :] = y
        yc = acc * _sigmoid(acc)
        qb = yc[:, :wb]
        kb = yc[:, wb:2 * wb]
        sq = jnp.concatenate([qb * qb, kb * kb], axis=-1).astype(BF16)
        hw = hsum_ref.shape[0]
        ss = jnp.concatenate([jnp.dot(sq[:, j:j + hw], hsum_ref[...], preferred_element_type=F32)
                              for j in range(0, 2 * wb, hw)], axis=-1)
        qb_ref[rs, :] = qb * lax.rsqrt(ss[:, :wb] + 1e-6) * scale
        kb_ref[rs, :] = kb * lax.rsqrt(ss[:, wb:] + 1e-6)
        vb_ref[rs, :] = yc[:, 2 * wb:]

        u = jax.nn.gelu(zc[:, :wc])
        gv = jax.nn.gelu(zc[:, wc:])
        mu = jnp.mean(gv, axis=-1, keepdims=True)
        var = jnp.mean(jnp.square(gv - mu), axis=-1, keepdims=True)
        vn = (gv - mu) * lax.rsqrt(var + 1e-5) * gcv_ref[...] + bcv_ref[...]
        vn_ref[rs, :] = vn
        first = lax.broadcasted_iota(jnp.int32, (cm, LANES), 1) < HEAD_DIM
        kpad = ws_ref.shape[2] - 2 * cm
        rows = []
        for c in range(r // cm):
            vc = vn[c * cm:(c + 1) * cm]
            cols = []
            for pp in range(wc // LANES):
                vp = vc[:, pp * LANES:(pp + 1) * LANES]
                parts = [jnp.where(first, vp, 0.0), jnp.where(first, 0.0, vp)]
                if kpad:
                    parts.append(jnp.zeros((kpad, LANES), F32))
                cols.append(_dot(ws_ref[pp], jnp.concatenate(parts, axis=0)))
            s = jnp.concatenate(cols, axis=-1) + bs_ref[...]
            rows.append(u[c * cm:(c + 1) * cm] * s)
        oc = rows[0] if len(rows) == 1 else jnp.concatenate(rows, axis=0)
        oc_ref[rs, :] = _rms(oc, gco_ref[...])
        return y[r - SUBLANES:r], cum[r - 1:r, :]

    subs = [slice(i * r, (i + 1) * r) for i in range(nsub)]
    projected = [project(rs) for rs in subs]
    if seg:
        prev, cum_in = jnp.zeros(carry_conv.shape, F32), jnp.zeros(carry_cum.shape, F32)
    else:
        prev, cum_in = carry_conv[...], carry_cum[...]
    for rs, z in zip(subs, projected):
        prev, cum_in = finish(rs, *z, prev, cum_in)
    if not seg:
        carry_conv[...] = prev
        carry_cum[...] = cum_in
        ytail_ref[...] = prev


def _inproj_kernel_inplace(*refs, n_in, **kw):
    return _inproj_kernel(*refs[:n_in], *refs[n_in + 2:], **kw)


def _inproj(x, lw, conv_prev, *, cm, layer, depth, kv_prev):
    b0, t0, d = x.shape
    wa, wb, wc, nh = lw["wa"], lw["wb"], lw["wc"], lw["nh"]
    kw1 = conv_prev.shape[1]
    seg = 0 if t0 >= LANES else t0
    if seg:
        assert seg & (seg - 1) == 0 and seg >= SUBLANES
        x = x.reshape(1, b0 * t0, d)
        tm = _pick(b0 * t0, (512, 256, 128, 64, 32, 16))
        assert tm % seg == 0
        conv_init = jnp.stack([jnp.pad(conv_prev[:, kw1 - k:], ((0, 0), (0, t0 - k), (0, 0))) for k in range(1, kw1 + 1)])
        conv_init = conv_init.reshape(kw1, b0 * t0, -1)
    else:
        tm = _pick(t0, (512, 256, 128))
        conv_init = jnp.pad(conv_prev, ((0, 0), (SUBLANES - kw1, 0), (0, 0)))
    b, t, _ = x.shape
    nt = t // tm
    kv_time_minor = not seg
    tok = lambda w: pl.BlockSpec((None, tm, w), lambda i, j: (i, j, 0))
    per_b = lambda r, w: pl.BlockSpec((None, r, w), lambda i, j: (i, 0, 0))
    outs = [("qaug", 2 * wa, BF16), ("ka", wa, F32), ("va", wa, F32), ("kaug", 2 * wa, BF16), ("vab", wa, BF16),
            ("elem", LANES, F32), ("cum", LANES, F32), ("qb", wb, F32), ("kb", wb, F32), ("vb", wb, F32),
            ("bz", wb, F32), ("oc", wc, F32), ("vn", wc, F32)]
    out_shape = [jax.ShapeDtypeStruct((b, t, w), dt) for _, w, dt in outs]
    out_specs = [tok(w) for _, w, _ in outs]
    if kv_time_minor:
        for k in (1, 2):
            out_shape[k] = jax.ShapeDtypeStruct((depth, b, wa, t), F32)
            out_specs[k] = pl.BlockSpec((None, None, wa, tm), lambda i, j: (layer, i, 0, j))
    if seg:
        out_shape.append(jax.ShapeDtypeStruct((b, t, 3 * wb), F32))
        out_specs.append(tok(3 * wb))
        init_spec = pl.BlockSpec((kw1, tm, 3 * wb), lambda i, j: (0, j, 0))
    else:
        out_shape.append(jax.ShapeDtypeStruct((b, SUBLANES, 3 * wb), F32))
        out_specs.append(per_b(SUBLANES, 3 * wb))
        init_spec = per_b(SUBLANES, 3 * wb)
    consts = [lw["g_pre_mix"], lw["w_big"], lw["w_small"], lw["sp"], lw["conv_w"]]
    consts2 = [lw["g_cv"], lw["b_cv"], lw["ws_cat"][cm], lw["bs_full"][cm], lw["g_c_out"], lw["hsum"], lw["pmat"]]
    nsub = 2 if tm % (2 * max(cm, LANES)) == 0 else 1
    kw = dict(tm=tm, cm=cm, nh=nh, wa=wa, wb=wb, wc=wc, scale=HEAD_DIM ** -0.5, kv_time_minor=kv_time_minor,
              nsub=nsub, seg=seg)
    in_specs = ([tok(d)] + [_const_spec(c.shape) for c in consts] + [init_spec]
                + [_const_spec(c.shape) for c in consts2])
    args = [x, *consts, conv_init, *consts2]
    inplace = kv_time_minor and kv_prev is not None
    if inplace:
        kern = functools.partial(_inproj_kernel_inplace, n_in=len(args), **kw)
        aliases = {len(args): 1, len(args) + 1: 2}
        in_specs = in_specs + [pl.BlockSpec(memory_space=pl.ANY)] * 2
        args = args + list(kv_prev)
    else:
        kern = functools.partial(_inproj_kernel, **kw)
        aliases = {}
    res = pl.pallas_call(
        kern,
        grid=(b, nt),
        in_specs=in_specs,
        out_specs=out_specs,
        out_shape=out_shape,
        input_output_aliases=aliases,
        scratch_shapes=[pltpu.VMEM((SUBLANES, 3 * wb), F32), pltpu.VMEM((1, LANES), F32)],
        compiler_params=_params("arbitrary", "arbitrary"),
        name="inproj",
    )(*args)
    named = {n: r for (n, _, _), r in zip(outs, res[:-1])}
    if seg:
        named = {n: r.reshape(b0, t0, r.shape[-1]) for n, r in named.items()}
        named["conv_new"] = res[-1].reshape(b0, t0, -1)[:, t0 - kw1:, :]
    else:
        named["conv_new"] = res[-1][:, SUBLANES - kw1:, :]
    named["kv_time_minor"] = kv_time_minor
    return named


def _attn_kernel(qt_ref, k_ref, vt_ref, o_ref, *, tq, kc, look, qsplit):
    p = pl.program_id(1)
    i = pl.program_id(2)
    qt = qt_ref[...]
    rowi = lax.broadcasted_iota(jnp.int32, qt.shape, 0)
    zero = jnp.zeros_like(qt)
    qts = []
    for e in range(2):
        a0 = LANES + AUG * (2 * p + e)
        keep = ((rowi >= e * HEAD_DIM) & (rowi < (e + 1) * HEAD_DIM)) | ((rowi >= a0) & (rowi < a0 + AUG))
        qts.append(jnp.where(keep, qt, zero))
    wq = tq // qsplit
    kofs = lax.broadcasted_iota(jnp.int32, (kc, wq), 0)
    qofs = lax.broadcasted_iota(jnp.int32, (kc, wq), 1)
    ones = jnp.ones((2 * SUBLANES, kc), BF16)
    units = [(c, e, h) for c in range(tq // kc) for e in range(2) for h in range(qsplit)]
    slots = [(e, h) for e in range(2) for h in range(qsplit)]

    def trim(c, h, masked):
        lo = max(c * kc - h * wq, 0) if masked else 0
        return None if lo >= wq else lo

    def scores(j, c, e, h, masked):
        k0 = pl.multiple_of(j * tq + c * kc, kc)
        lo = trim(c, h, masked)
        s = jnp.dot(k_ref[pl.ds(k0, kc), :], qts[e][:, h * wq + lo:(h + 1) * wq], preferred_element_type=F32)
        return jnp.concatenate([jnp.full((kc, lo), NEG_INF, F32), s], axis=1) if lo else s

    def fold(j, c, e, h, s, st, masked):
        m, l, acc = st
        k0 = pl.multiple_of(j * tq + c * kc, kc)
        lo = trim(c, h, masked)
        if masked:
            s = jnp.where(c * kc + kofs <= h * wq + qofs, s, NEG_INF)
        m_new = jnp.maximum(m, jnp.max(s, axis=0, keepdims=True))
        alpha = jnp.exp2(m - m_new)
        pt = jnp.exp2(s - m_new).astype(BF16)
        vt = jnp.concatenate([vt_ref[e * HEAD_DIM:(e + 1) * HEAD_DIM, pl.ds(k0, kc)], ones], axis=0)
        r = jnp.dot(vt, pt[:, lo:], preferred_element_type=F32)
        if lo:
            r = jnp.concatenate([jnp.zeros((r.shape[0], lo), F32), r], axis=1)
        return m_new, alpha * l + r[HEAD_DIM:HEAD_DIM + 1], alpha * acc + r[:HEAD_DIM]

    def run(blocks, state):
        state = dict(zip(slots, state))
        todo = [(j, c, e, h, masked) for j, masked in blocks for c, e, h in units
                if trim(c, h, masked) is not None]
        pend = {}
        for k in range(min(look, len(todo))):
            pend[k] = scores(*todo[k])
        for k, (j, c, e, h, masked) in enumerate(todo):
            if k + look < len(todo):
                pend[k + look] = scores(*todo[k + look])
            state[e, h] = fold(j, c, e, h, pend.pop(k), state[e, h], masked)
        return tuple(state[sl] for sl in slots)

    st0 = (jnp.full((1, wq), NEG_INF, F32), jnp.zeros((1, wq), F32), jnp.zeros((HEAD_DIM, wq), F32))
    state = lax.fori_loop(0, i // 2, lambda t, s: run([(2 * t, False), (2 * t + 1, False)], s),
                          (st0,) * len(slots))
    state = lax.cond(i % 2 == 1,
                     lambda s: run([(i - 1, False), (i, True)], s),
                     lambda s: run([(i, True)], s), state)
    done = dict(zip(slots, state))
    ot = jnp.concatenate([jnp.concatenate([done[e, h][2] / done[e, h][1] for h in range(qsplit)], axis=1)
                          for e in range(2)], axis=0)
    o_ref[...] = ot.T


def _attn_prompt(qaug, kaug, vab, *, tq, kc, look, qsplit):
    b, s, wa = vab.shape
    npair = wa // LANES
    qt = jnp.transpose(qaug, (0, 2, 1))
    vt = jnp.transpose(vab, (0, 2, 1))
    kern = functools.partial(_attn_kernel, tq=tq, kc=kc, look=look, qsplit=qsplit)
    return pl.pallas_call(
        kern,
        grid=(b, npair, s // tq),
        in_specs=[pl.BlockSpec((None, 2 * LANES, tq), lambda bi, p, i: (bi, p, i)),
                  pl.BlockSpec((None, s, 2 * LANES), lambda bi, p, i: (bi, 0, p)),
                  pl.BlockSpec((None, LANES, s), lambda bi, p, i: (bi, p, 0))],
        out_specs=pl.BlockSpec((None, tq, LANES), lambda bi, p, i: (bi, i, p)),
        out_shape=jax.ShapeDtypeStruct((b, s, wa), F32),
        compiler_params=_params("arbitrary", "arbitrary", "arbitrary"),
        name="attn_prompt",
    )(qt, kaug, vt)


def _attn_sample_kernel(q_ref, kc_ref, vc_ref, kn_ref, vn_ref, cum_ref, rrow_ref, crow_ref, o_ref, *, n):
    p = pl.program_id(1)
    q = q_ref[:, :LANES]
    lane = lax.broadcasted_iota(jnp.int32, (n, LANES), 1)
    first = lane < HEAD_DIM
    zero = jnp.zeros_like(q)
    past = kc_ref.shape[-1]
    kc = kc_ref[...].reshape(LANES, past).astype(BF16)
    vc = vc_ref[...].reshape(LANES, past).astype(BF16)
    pad = jnp.zeros((LANES - n, LANES), BF16)
    kn = jnp.concatenate([kn_ref[:, :LANES], pad], axis=0)
    vn = jnp.concatenate([vn_ref[...], pad], axis=0)
    cum = cum_ref[...]
    qm = jnp.concatenate([jnp.where(first, q, zero), jnp.where(first, zero, q)], axis=0)
    cq = jnp.concatenate([jnp.sum(jnp.where(lane == 2 * p + e, cum, 0.0), axis=-1, keepdims=True)
                          for e in range(2)], axis=0)
    top = lax.broadcasted_iota(jnp.int32, (2 * n, 1), 0) < n
    rrow = jnp.where(top, rrow_ref[0:1, :], rrow_ref[1:2, :])
    crow = jnp.where(top, crow_ref[0:1, :], crow_ref[1:2, :])
    qrow = lax.broadcasted_iota(jnp.int32, (2 * n, LANES), 0)
    causal = lax.broadcasted_iota(jnp.int32, (2 * n, LANES), 1) <= jnp.where(qrow < n, qrow, qrow - n)
    sc = _dot(qm, kc) + LOG2E * (cq + rrow)
    sn = jnp.where(causal, _dot_nt(qm, kn) + LOG2E * (cq - crow), NEG_INF)
    m = jnp.maximum(jnp.max(sc, axis=-1, keepdims=True), jnp.max(sn, axis=-1, keepdims=True))
    pc = jnp.exp2(sc - m)
    pn = jnp.exp2(sn - m)
    l = jnp.sum(pc, axis=-1, keepdims=True) + jnp.sum(pn, axis=-1, keepdims=True)
    o = (_dot_nt(pc, vc) + _dot(pn, vn)) / l
    o_ref[...] = jnp.where(first, o[:n], o[n:])


def _attn_sample(qaug, kaug, vab, cum, cache_kt, cache_vt, layer, rrow, crow):
    b, n, wa = vab.shape
    past = cache_kt.shape[-1]
    npair = wa // LANES
    new = lambda w: pl.BlockSpec((None, n, w), lambda bi, p: (bi, 0, p))
    old = lambda: pl.BlockSpec((None, None, 2, HEAD_DIM, past), lambda bi, p: (layer, bi, p, 0, 0))
    return pl.pallas_call(
        functools.partial(_attn_sample_kernel, n=n),
        grid=(b, npair),
        in_specs=[new(2 * LANES), old(), old(), new(2 * LANES), new(LANES),
                  pl.BlockSpec((None, n, LANES), lambda bi, p: (bi, 0, 0)),
                  pl.BlockSpec((None, None, 2, past), lambda bi, p: (bi, p, 0, 0)),
                  pl.BlockSpec((None, None, 2, LANES), lambda bi, p: (bi, p, 0, 0))],
        out_specs=new(LANES),
        out_shape=jax.ShapeDtypeStruct((b, n, wa), F32),
        compiler_params=_params("arbitrary", "arbitrary"),
        name="attn_sample",
    )(qaug, cache_kt, cache_vt, kaug, vab, cum, rrow, crow)


def _suffix_kernel(x_ref, o_ref):
    v = x_ref[...]
    n = v.shape[1]
    lane = lax.broadcasted_iota(jnp.int32, v.shape, 1)
    s = 1
    while s < n:
        v = v + jnp.where(lane + s < n, pltpu.roll(v, n - s, 1), 0.0)
        s *= 2
    o_ref[...] = jnp.where(lane + 1 < n, pltpu.roll(v, n - 1, 1), 0.0)


def _exclusive_suffix_sum(x, layer):
    _, h, b, p = x.shape
    return pl.pallas_call(
        _suffix_kernel,
        grid=(h,),
        in_specs=[pl.BlockSpec((None, None, b, p), lambda i: (layer, i, 0, 0))],
        out_specs=pl.BlockSpec((None, b, p), lambda i: (i, 0, 0)),
        out_shape=jax.ShapeDtypeStruct((h, b, p), F32),
        compiler_params=_params("arbitrary"),
        name="suffix_sum",
    )(x)


def _gdn_kernel(q_ref, k_ref, v_ref, bz_ref, elem_ref, s0_ref, gb_ref, esel_ref, hsum_ref,
                o_ref, sout_ref, s_scr, *, nb, nh, chained):
    L = GDN_BLOCK
    t = pl.program_id(1)

    if chained:
        @pl.when(t == 0)
        def _():
            s_scr[...] = s0_ref[...]

    lane = lax.broadcasted_iota(jnp.int32, (L, LANES), 1)
    first = lane < HEAD_DIM
    ri = lax.broadcasted_iota(jnp.int32, (L, L), 0)
    ci = lax.broadcasted_iota(jnp.int32, (L, L), 1)
    incl = ci <= ri
    strict = ci < ri
    same_head = (ri < HEAD_DIM) == (ci < HEAD_DIM)
    lane2 = lax.broadcasted_iota(jnp.int32, (L, 2 * L), 1)
    first2 = jnp.bitwise_and(lane2, LANES - 1) < HEAD_DIM
    xor2 = jnp.bitwise_xor(lax.broadcasted_iota(jnp.int32, (L, 2 * L), 0), jnp.bitwise_and(lane2, L - 1))
    zero_ll = jnp.zeros((L, L), BF16)

    def halves(x, sel):
        return jnp.concatenate([jnp.where(sel, x, 0.0), jnp.where(sel, 0.0, x)], axis=0)

    def dot_heads(y, x):
        xb = x.astype(BF16)
        bd = jnp.concatenate([jnp.concatenate([xb[:, :L], zero_ll], axis=1),
                              jnp.concatenate([zero_ll, xb[:, L:]], axis=1)], axis=0)
        return jnp.dot(y.astype(BF16), bd, preferred_element_type=F32)

    npair = s_scr.shape[0]
    wbw = npair * LANES
    c = {}

    def solve_stages(blocks):
        chains = [(n, p) for n in blocks for p in range(npair)]
        ex = {}
        for n in blocks:
            elem = elem_ref[n * L:(n + 1) * L, :]
            gsum = _seg_cumsum(elem, L)
            mixed = jnp.where((lane >= nh) & (lane < 2 * nh), gsum, elem)
            ex[n] = _dot_select_exact(mixed, esel_ref[...])
        yield
        for n, p in chains:
            rows = slice(n * L, (n + 1) * L)
            cols = slice(p * LANES, (p + 1) * LANES)
            g = ex[n][:, cols]
            bt = ex[n][:, wbw + p * LANES: wbw + (p + 1) * LANES]
            kp = k_ref[rows, cols]
            qp = q_ref[rows, cols]
            g_sw = pltpu.roll(g, HEAD_DIM, 1)
            b_sw = pltpu.roll(bt, HEAD_DIM, 1)
            g_t = g.T
            a_parts, qk_parts = [], []
            for e in range(2):
                sel = first if e == 0 else jnp.logical_not(first)
                gcol = jnp.where(sel, g, g_sw)
                bcol = jnp.where(sel, bt, b_sw)
                grow = g_t[e * HEAD_DIM:e * HEAD_DIM + 1, :]
                dec = jnp.exp(jnp.where(incl, gcol - grow, NEG_INF))
                kk = _dot_nt(jnp.where(sel, kp, 0.0), kp)
                qk_parts.append(_dot_nt(jnp.where(sel, qp, 0.0), kp) * dec)
                a_parts.append(jnp.where(strict, bcol * kk * dec, 0.0))
            a_cat = jnp.concatenate(a_parts, axis=1)
            eg = jnp.exp(g)
            glast = g[L - 1:L, :]
            c[n, p] = dict(a=a_cat, qk=jnp.concatenate(qk_parts, axis=1), glast=glast, qg=qp * eg,
                           kdec=kp * jnp.exp(glast - g),
                           r=jnp.concatenate([v_ref[rows, cols] * bt, kp * bt * eg], axis=1),
                           tm1=-jnp.where(xor2 < 2, a_cat, 0.0))
        yield
        s_blk = 2
        while s_blk < L:
            pm = {}
            for key in chains:
                nmat = jnp.where((xor2 >= s_blk) & (xor2 < 2 * s_blk), c[key]["a"], 0.0)
                pm[key] = nmat + dot_heads(c[key]["tm1"], nmat)
            yield
            for key in chains:
                c[key]["tm1"] = c[key]["tm1"] - pm[key] - dot_heads(pm[key], c[key]["tm1"])
            yield
            s_blk *= 2
        for key in chains:
            r = c[key]["r"]
            c[key]["uw"] = r + _dot(c[key]["tm1"], halves(r, first2))
        yield

    def state_stages(blocks):
        pairs = range(npair)
        for n in blocks:
            rows = slice(n * L, (n + 1) * L)
            s_in = [s_scr[p] if chained else s0_ref[n, p] for p in pairs]
            ws = [_dot(jnp.concatenate([c[n, p]["uw"][:, LANES:], c[n, p]["qg"]], axis=0), s_in[p]) for p in pairs]
            yield
            u = [c[n, p]["uw"][:, :LANES] - ws[p][:L] for p in pairs]
            o = [ws[p][L:] + _dot(c[n, p]["qk"], halves(u[p], first)) for p in pairs]
            yield
            for p in pairs:
                s_new = (s_in[p] * jnp.exp(c[n, p]["glast"])
                         + jnp.where(same_head, _dot(c[n, p]["kdec"].T, u[p]), 0.0))
                if chained:
                    s_scr[p] = s_new
                else:
                    sout_ref[n, p] = s_new
            yield
            for p in pairs:
                cols = slice(p * LANES, (p + 1) * LANES)
                ms = _dot(o[p] * o[p], hsum_ref[...]) * (1.0 / HEAD_DIM)
                bz = bz_ref[rows, cols]
                o_ref[rows, cols] = o[p] * lax.rsqrt(ms + 1e-6) * gb_ref[...] * (bz * _sigmoid(bz))
            yield

    groups = [list(range(g0, min(g0 + GDN_GROUP, nb))) for g0 in range(0, nb, GDN_GROUP)]
    pending = iter(())
    for grp in groups:
        for _ in solve_stages(grp):
            next(pending, None)
        for _ in pending:
            pass
        pending = state_stages(grp)
    for _ in pending:
        pass

    if chained:
        @pl.when(t == pl.num_programs(1) - 1)
        def _():
            sout_ref[...] = s_scr[...]


def _gdn(qb, kb, vb, bz, elem, s0, lw, *, nb):
    b, t, wb = qb.shape
    npair = wb // LANES
    chained = t > GDN_BLOCK
    if not chained:
        qb, kb, vb, bz, elem = (a.reshape(1, b * t, a.shape[-1]) for a in (qb, kb, vb, bz, elem))
        nb = _pick(b, (GDN_TILE_BLOCKS, 2, 1))
    rows, total = qb.shape[:2]
    tile = nb * GDN_BLOCK
    tok = lambda w: pl.BlockSpec((None, tile, w), lambda i, j: (i, j, 0))
    if chained:
        st = pl.BlockSpec((None, npair, LANES, LANES), lambda i, j: (i, 0, 0, 0))
    else:
        st = pl.BlockSpec((nb, npair, LANES, LANES), lambda i, j: (j, 0, 0, 0))
    consts = [lw["g_b_pair"], lw["esel"], lw["hsum128"]]
    ob, s_new = pl.pallas_call(
        functools.partial(_gdn_kernel, nb=nb, nh=lw["nh"], chained=chained),
        grid=(rows, total // tile),
        in_specs=[tok(wb), tok(wb), tok(wb), tok(wb), tok(LANES), st] + [_const_spec(c.shape) for c in consts],
        out_specs=[tok(wb), st],
        out_shape=[jax.ShapeDtypeStruct((rows, total, wb), F32),
                   jax.ShapeDtypeStruct((b, npair, LANES, LANES), F32)],
        scratch_shapes=[pltpu.VMEM((npair, LANES, LANES), F32)],
        compiler_params=_params("arbitrary", "arbitrary"),
        name="gdn",
    )(qb, kb, vb, bz, elem, s0, *consts)
    return ob.reshape(b, t, wb), s_new


def _outffn_kernel(oa_ref, ob_ref, oc_ref, x_ref, ga_ref, wout_ref, gpm_ref, gpf_ref, wfi_ref, wfo_ref,
                   gpo_ref, y_ref, *, dff, nsub):
    r = x_ref.shape[0] // nsub
    rows = [slice(i * r, (i + 1) * r) for i in range(nsub)]
    cat = [jnp.concatenate([_rms(oa_ref[rs, :], ga_ref[...]), ob_ref[rs, :], oc_ref[rs, :]], axis=-1).astype(BF16)
           for rs in rows]
    m = [jnp.dot(c, wout_ref[...], preferred_element_type=F32) for c in cat]
    x1 = [x_ref[rs, :] + _rms(mi, gpm_ref[...]) for rs, mi in zip(rows, m)]
    h = [_rms(xi, gpf_ref[...]).astype(BF16) for xi in x1]
    gu = [jnp.dot(hi, wfi_ref[...], preferred_element_type=F32) for hi in h]
    a = [(g[:, :dff] * _sigmoid(g[:, :dff]) * g[:, dff:]).astype(BF16) for g in gu]
    f = [jnp.dot(ai, wfo_ref[...], preferred_element_type=F32) for ai in a]
    for rs, xi, fi in zip(rows, x1, f):
        y_ref[rs, :] = xi + _rms(fi, gpo_ref[...])


def _outffn(oa, ob, oc, x, lw, *, tm):
    n, d = x.shape
    dff = lw["w_ffn_out"].shape[0]
    tok = lambda w: pl.BlockSpec((tm, w), lambda i: (i, 0))
    consts = [lw["g_a_out"], lw["w_out"], lw["g_post_mix"], lw["g_pre_ffn"], lw["w_ffn_in"],
              lw["w_ffn_out"], lw["g_post_ffn"]]
    return pl.pallas_call(
        functools.partial(_outffn_kernel, dff=dff, nsub=2 if tm % (4 * SUBLANES) == 0 else 1),
        grid=(n // tm,),
        in_specs=[tok(oa.shape[1]), tok(ob.shape[1]), tok(oc.shape[1]), tok(d)]
                 + [_const_spec(c.shape) for c in consts],
        out_specs=tok(d),
        out_shape=jax.ShapeDtypeStruct((n, d), F32),
        compiler_params=_params("arbitrary"),
        name="outffn",
    )(oa, ob, oc, x, *consts)


def _block_ones(width):
    idx = jnp.arange(width) // HEAD_DIM
    return (idx[:, None] == idx[None, :]).astype(BF16)


def _layer_weights(l, prm, cms):
    w_in = prm["w_in"][l]
    nh = prm["b_f"].shape[1]
    wa = nh * HEAD_DIM
    wb = prm["a_log"].shape[1] * HEAD_DIM
    wc = prm["g_cv"].shape[1]
    ng = prm["w_s"].shape[1]
    assert prm["a_log"].shape[1] == nh and wa % LANES == 0 and wc % LANES == 0 and AUG * nh <= LANES
    sizes = (wa, wa, wa, nh, 3 * wb, nh, nh, wb, wc, wc)
    offs = [0]
    for sz in sizes:
        offs.append(offs[-1] + sz)
    w_in_t = w_in.T
    col = lambda i: w_in_t[offs[i]:offs[i + 1]]
    w_big = jnp.concatenate([col(4), col(8), col(9), col(0), col(1), col(2), col(7)], axis=0).astype(BF16)
    w_small = jnp.concatenate([col(3), col(5), col(6), jnp.zeros((LANES - 3 * nh, w_in.shape[0]), F32)],
                              axis=0).astype(BF16)
    zpad = jnp.zeros((LANES - 2 * nh,), F32)
    sp = jnp.zeros((SUBLANES, LANES), F32)
    sp = sp.at[0].set(jnp.concatenate([prm["b_f"][l], prm["dt_bias"][l], zpad]))
    sp = sp.at[1].set(jnp.concatenate([jnp.zeros((nh,), F32), prm["a_log"][l], zpad]))
    hl = jnp.arange(nh) * AUG
    sp = sp.at[2, (hl[:, None] + jnp.arange(3, 6)[None, :]).reshape(-1)].set(1.0)
    sp = sp.at[3, (hl[:, None] + jnp.arange(0, 3)[None, :]).reshape(-1)].set(1.0)
    pmat = jnp.zeros((3 * LANES, 2 * LANES), F32)
    for piece in range(3):
        pmat = pmat.at[piece * LANES + jnp.arange(nh), hl + piece].set(1.0)
        pmat = pmat.at[piece * LANES + jnp.arange(nh), LANES + hl + 3 + piece].set(1.0)
    row = lambda v: v.reshape(1, -1)
    ws_cat, bs_full = {}, {}
    for cm in cms:
        pos = jnp.arange(cm) // HEAD_DIM
        w = jnp.where(pos[None, :] <= pos[:, None], prm["w_s"][l][:, :cm, :cm], 0.0)
        pairs = [jnp.concatenate([w[2 * pp], w[2 * pp + 1]], axis=1) for pp in range(ng // 2)]
        kpad = max(LANES - 2 * cm, 0)
        ws_cat[cm] = jnp.pad(jnp.stack(pairs), ((0, 0), (0, 0), (0, kpad))).astype(BF16)
        bs_full[cm] = jnp.repeat(prm["b_s"][l][:, :cm].T, wc // ng, axis=1)
    src = jnp.arange(LANES)[:, None]
    dst = jnp.arange(wb)[None, :] // HEAD_DIM
    esel = jnp.concatenate([src == nh + dst, src == 2 * nh + dst], axis=1).astype(BF16)
    return dict(
        nh=nh, wa=wa, wb=wb, wc=wc,
        g_pre_mix=row(prm["g_pre_mix"][l]), w_big=w_big, w_small=w_small, sp=sp, conv_w=prm["conv_w"][l],
        g_cv=row(prm["g_cv"][l]), b_cv=row(prm["b_cv"][l]), ws_cat=ws_cat, bs_full=bs_full,
        g_c_out=row(prm["g_c_out"][l]), hsum=_block_ones(2 * LANES), hsum128=_block_ones(LANES), pmat=pmat.astype(BF16),
        g_b_pair=row(jnp.tile(prm["g_b_out"][l], LANES // HEAD_DIM)), esel=esel,
        g_a_out=row(prm["g_a_out"][l]), w_out=prm["w_out"][l].astype(BF16),
        g_post_mix=row(prm["g_post_mix"][l]), g_pre_ffn=row(prm["g_pre_ffn"][l]),
        w_ffn_in=prm["w_ffn_in"][l].astype(BF16), w_ffn_out=prm["w_ffn_out"][l].astype(BF16),
        g_post_ffn=row(prm["g_post_ffn"][l]))


def _pair_state(s):
    b, h, dk, dv = s.shape
    s = s.reshape(b, h // 2, 2, dk, dv)
    z = jnp.zeros_like(s[:, :, 0])
    top = jnp.concatenate([s[:, :, 0], z], axis=-1)
    bot = jnp.concatenate([z, s[:, :, 1]], axis=-1)
    return jnp.concatenate([top, bot], axis=-2)


def _unpair_state(sp):
    b, hp, _, _ = sp.shape
    s0 = sp[:, :, :HEAD_DIM, :HEAD_DIM]
    s1 = sp[:, :, HEAD_DIM:, HEAD_DIM:]
    return jnp.stack([s0, s1], axis=2).reshape(b, 2 * hp, HEAD_DIM, HEAD_DIM)


def _head_rows(cum, nh):
    b, t, _ = cum.shape
    return jnp.transpose(cum[:, :, :nh], (0, 2, 1)).reshape(b, nh // 2, 2, t)


def _pick(n, prefs):
    for c in prefs:
        if n % c == 0:
            return c
    return n


def _layer(x, lw, conv_prev, s0, cache, *, cm, layer, depth, kv_prev):
    b, t, d = x.shape
    nh, wb = lw["nh"], lw["wb"]
    pj = _inproj(x, lw, conv_prev, cm=cm, layer=layer, depth=depth, kv_prev=kv_prev)

    if cache is None:
        tq = _pick(t, (ATTN_TQ, 256, 128))
        kc = min(ATTN_KC, tq)
        qsplit = ATTN_QSPLIT if tq % (ATTN_QSPLIT * kc) == 0 else 1
        oa = _attn_prompt(pj["qaug"], pj["kaug"], pj["vab"], tq=tq, kc=kc, look=ATTN_LOOK, qsplit=qsplit)
    else:
        ck, cv, clogf_t = cache
        _, _, bs, past = clogf_t.shape
        excl = _exclusive_suffix_sum(clogf_t, layer)
        rrow = jnp.transpose(excl, (1, 0, 2)).reshape(bs, nh // 2, 2, past)
        crow = jnp.pad(_head_rows(pj["cum"], nh), ((0, 0), (0, 0), (0, 0), (0, LANES - t)))
        oa = _attn_sample(pj["qaug"], pj["kaug"], pj["vab"], pj["cum"], ck, cv, layer, rrow, crow)

    tp = -(-t // GDN_BLOCK) * GDN_BLOCK
    padt = lambda a: a if tp == t else jnp.pad(a, ((0, 0), (0, tp - t), (0, 0)))
    nb = _pick(tp // GDN_BLOCK, (GDN_TILE_BLOCKS, 2, 1))
    ob, s_new = _gdn(padt(pj["qb"]), padt(pj["kb"]), padt(pj["vb"]), padt(pj["bz"]), padt(pj["elem"]),
                     _pair_state(s0), lw, nb=nb)
    ob = ob[:, :t]

    n = b * t
    y = _outffn(oa.reshape(n, -1), ob.reshape(n, -1), pj["oc"].reshape(n, -1), x.reshape(n, d), lw,
                tm=_pick(n, (512, 256, 128, 64, 32, 16)))
    if pj["kv_time_minor"]:
        new_kv = (pj["ka"], pj["va"])
    else:
        new_kv = (pj["ka"].reshape(b, t, nh, HEAD_DIM), pj["va"].reshape(b, t, nh, HEAD_DIM))
    state = (new_kv[0], new_kv[1], pj["elem"][:, :, :nh],
             pj["conv_new"], _unpair_state(s_new), pj["vn"])
    return y.reshape(b, t, d), state, pj["kv_time_minor"]


def kernel(x_prompt, x_sample, cache_a_k, cache_a_v, cache_a_logf, state_b_conv, state_b_S, g_pre_mix, w_in, b_f, conv_w, a_log, dt_bias, g_b_out, g_a_out, g_cv, b_cv, w_s, b_s, g_c_out, w_out, g_post_mix, g_pre_ffn, w_ffn_in, w_ffn_out, g_post_ffn):
    prm = dict(g_pre_mix=g_pre_mix, w_in=w_in, b_f=b_f, conv_w=conv_w, a_log=a_log, dt_bias=dt_bias,
               g_b_out=g_b_out, g_a_out=g_a_out, g_cv=g_cv, b_cv=b_cv, w_s=w_s, b_s=b_s, g_c_out=g_c_out,
               w_out=w_out, g_post_mix=g_post_mix, g_pre_ffn=g_pre_ffn, w_ffn_in=w_ffn_in,
               w_ffn_out=w_ffn_out, g_post_ffn=g_post_ffn)
    depth = w_in.shape[0]
    bp, sp_len, _ = x_prompt.shape
    n_new = x_sample.shape[1]
    cm_p = w_s.shape[2]
    assert sp_len % cm_p == 0 and sp_len % GDN_BLOCK == 0 and n_new <= HEAD_DIM and n_new % SUBLANES == 0
    kw1 = conv_w.shape[1] - 1
    nhb = a_log.shape[1]
    yp, ys = x_prompt, x_sample
    outs_p, outs_s = [], []
    cache_kt = jnp.transpose(cache_a_k, (0, 1, 3, 4, 2))
    cache_vt = jnp.transpose(cache_a_v, (0, 1, 3, 4, 2))
    clogf_t = jnp.transpose(cache_a_logf, (0, 3, 1, 2))
    for l in range(depth):
        lw = _layer_weights(l, prm, (cm_p, n_new))
        conv0 = jnp.zeros((bp, kw1, conv_w.shape[2]), F32)
        s0 = jnp.zeros((bp, nhb, HEAD_DIM, HEAD_DIM), F32)
        kv_p = (outs_p[-1][0], outs_p[-1][1]) if outs_p else None
        kv_s = (outs_s[-1][0], outs_s[-1][1]) if outs_s else None
        yp, st_p, shared_p = _layer(yp, lw, conv0, s0, None, cm=cm_p, layer=l, depth=depth, kv_prev=kv_p)
        ys, st_s, shared_s = _layer(ys, lw, state_b_conv[l], state_b_S[l], (cache_kt, cache_vt, clogf_t),
                                    cm=n_new, layer=l, depth=depth, kv_prev=kv_s)
        outs_p.append(st_p)
        outs_s.append(st_s)
    stk = lambda outs, i: jnp.stack([o[i] for o in outs], axis=0)

    def new_cache(outs, i, shared):
        if not shared:
            return stk(outs, i)
        buf = outs[-1][i]
        dp, b, _, t = buf.shape
        return jnp.transpose(buf.reshape(dp, b, -1, HEAD_DIM, t), (0, 1, 4, 2, 3))

    return (yp, ys, new_cache(outs_p, 0, shared_p), new_cache(outs_p, 1, shared_p),
            stk(outs_p, 2), stk(outs_p, 3), stk(outs_p, 4),
            new_cache(outs_s, 0, shared_s), new_cache(outs_s, 1, shared_s),
            stk(outs_s, 2), stk(outs_s, 3), stk(outs_s, 4), stk(outs_s, 5))
```

```python
import functools

import jax
import jax.numpy as jnp
from jax import lax
from jax.experimental import pallas as pl
from jax.experimental.pallas import tpu as pltpu

F32 = jnp.float32
BF16 = jnp.bfloat16

LANES = 128
SUBLANES = 8
HEAD_DIM = 64
GDN_BLOCK = 128
GDN_GROUP = 2
GDN_TILE_BLOCKS = 4
ATTN_TQ = 1024
ATTN_KC = 256
ATTN_LOOK = 6
ATTN_QSPLIT = 4
VMEM_LIMIT = 56 * 1024 * 1024
NEG_INF = float("-inf")
LOG2E = 1.4426950408889634
AUG = 16


def _dot(a, b):
    return jnp.dot(a.astype(BF16), b.astype(BF16), preferred_element_type=F32)


def _dot_nt(a, b):
    return lax.dot_general(a.astype(BF16), b.astype(BF16), (((1,), (1,)), ((), ())),
                           preferred_element_type=F32)


def _dot_select_exact(x, sel):
    hi = x.astype(BF16)
    r1 = x - hi.astype(F32)
    mid = r1.astype(BF16)
    lo = (r1 - mid.astype(F32)).astype(BF16)
    d = lambda p: jnp.dot(p, sel, preferred_element_type=F32)
    return (d(hi) + d(mid)) + d(lo)


def _rms(x, g, eps=1e-6):
    return x * lax.rsqrt(jnp.mean(x * x, axis=-1, keepdims=True) + eps) * g


def _sigmoid(x):
    return 1.0 / (1.0 + jnp.exp(-x))


def _seg_cumsum(v, seg):
    row = lax.broadcasted_iota(jnp.int32, v.shape, 0)
    pos = jnp.bitwise_and(row, seg - 1)
    s = 1
    while s < seg:
        v = v + jnp.where(pos >= s, pltpu.roll(v, s, 0), 0.0)
        s *= 2
    return v


def _const_spec(shape):
    nd = len(shape)
    return pl.BlockSpec(shape, lambda *_: (0,) * nd, pipeline_mode=pl.Buffered(1))


def _params(*sem):
    return pltpu.CompilerParams(dimension_semantics=sem, vmem_limit_bytes=VMEM_LIMIT)


def _inproj_kernel(x_ref, gpre_ref, wbig_ref, wsm_ref, sp_ref, convw_ref, convinit_ref, gcv_ref,
                   bcv_ref, ws_ref, bs_ref, gco_ref, hsum_ref, pmat_ref,
                   qaug_ref, ka_ref, va_ref, kaug_ref, vab_ref, elem_ref, cum_ref, qb_ref, kb_ref,
                   vb_ref, bz_ref, oc_ref, vn_ref, ytail_ref,
                   carry_conv, carry_cum, *, tm, cm, nh, wa, wb, wc, scale, kv_time_minor, nsub, seg):
    if not seg:
        @pl.when(pl.program_id(1) == 0)
        def _():
            carry_cum[...] = jnp.zeros_like(carry_cum)
            carry_conv[...] = convinit_ref[...]

    r = tm // nsub
    o_c = 3 * wb
    o_a = o_c + 2 * wc

    def project(rs):
        h = _rms(x_ref[rs, :], gpre_ref[...]).astype(BF16)
        proj = lambda w: lax.dot_general(h, w, (((1,), (1,)), ((), ())), preferred_element_type=F32)
        return (proj(wbig_ref[:o_c, :]),
                proj(wbig_ref[o_c:o_a, :]),
                proj(wsm_ref[...]),
                proj(wbig_ref[o_a:, :]))

    def finish(rs, y, zc, zs, za, prev, cum_in):
        ka = za[:, wa:2 * wa]
        va = za[:, 2 * wa:3 * wa]
        if kv_time_minor:
            ka_ref[:, rs] = ka.T
            va_ref[:, rs] = va.T
        else:
            ka_ref[rs, :] = ka
            va_ref[rs, :] = va
        vab_ref[rs, :] = va.astype(BF16)
        bz_ref[rs, :] = za[:, 3 * wa:]

        lane = lax.broadcasted_iota(jnp.int32, (r, LANES), 1)
        zb = zs + sp_ref[0:1, :]
        soft_tail = jnp.log1p(jnp.exp(-jnp.abs(zb)))
        logf = -(jnp.maximum(-zb, 0.0) + soft_tail)
        gl = -jnp.exp(sp_ref[1:2, :]) * (jnp.maximum(zb, 0.0) + soft_tail)
        beta = _sigmoid(zs)
        elem = jnp.where(lane < nh, logf, jnp.where(lane < 2 * nh, gl, jnp.where(lane < 3 * nh, beta, 0.0)))
        elem_ref[rs, :] = elem
        cum = _seg_cumsum(elem, seg) if seg else _seg_cumsum(elem, r) + cum_in
        cum_ref[rs, :] = cum

        c2 = jnp.where(lane < nh, cum * LOG2E, 0.0)
        hi = c2.astype(BF16)
        r1 = c2 - hi.astype(F32)
        mid = r1.astype(BF16)
        lo = (r1 - mid.astype(F32)).astype(BF16)
        placed = jnp.dot(jnp.concatenate([hi, mid, lo], axis=1), pmat_ref[...], preferred_element_type=F32)
        augq = (placed[:, :LANES] + sp_ref[2:3, :]).astype(BF16)
        augk = (sp_ref[3:4, :] - placed[:, LANES:]).astype(BF16)
        qs = (za[:, :wa] * (scale * LOG2E)).astype(BF16)
        ks = ka.astype(BF16)
        qaug_ref[rs, :] = jnp.concatenate(
            [a for j in range(0, wa, LANES) for a in (qs[:, j:j + LANES], augq)], axis=1)
        kaug_ref[rs, :] = jnp.concatenate(
            [a for j in range(0, wa, LANES) for a in (ks[:, j:j + LANES], augk)], axis=1)

        row8 = lax.broadcasted_iota(jnp.int32, prev.shape, 0)
        pos = jnp.bitwise_and(lax.broadcasted_iota(jnp.int32, y.shape, 0), max(seg, 1) - 1)
        kw = convw_ref.shape[0]
        acc = y * convw_ref[kw - 1:kw, :]
        for k in range(1, kw):
            yk = pltpu.roll(y, k, 0)
            if seg:
                yk = jnp.where(pos < k, convinit_ref[k - 1, rs, :], yk)
            else:
                top = jnp.where(row8 < k, pltpu.roll(prev, k, 0), yk[0:SUBLANES])
                yk = jnp.concatenate([top, yk[SUBLANES:]], axis=0)
            acc = acc + yk * convw_ref[kw - 1 - k:kw - k, :]
        if seg:
            ytail_ref[rs, :] = y
        yc = acc * _sigmoid(acc)
        qb = yc[:, :wb]
        kb = yc[:, wb:2 * wb]
        sq = jnp.concatenate([qb * qb, kb * kb], axis=-1).astype(BF16)
        hw = hsum_ref.shape[0]
        ss = jnp.concatenate([jnp.dot(sq[:, j:j + hw], hsum_ref[...], preferred_element_type=F32)
                              for j in range(0, 2 * wb, hw)], axis=-1)
        qb_ref[rs, :] = qb * lax.rsqrt(ss[:, :wb] + 1e-6) * scale
        kb_ref[rs, :] = kb * lax.rsqrt(ss[:, wb:] + 1e-6)
        vb_ref[rs, :] = yc[:, 2 * wb:]

        u = jax.nn.gelu(zc[:, :wc])
        gv = jax.nn.gelu(zc[:, wc:])
        mu = jnp.mean(gv, axis=-1, keepdims=True)
        var = jnp.mean(jnp.square(gv - mu), axis=-1, keepdims=True)
        vn = (gv - mu) * lax.rsqrt(var + 1e-5) * gcv_ref[...] + bcv_ref[...]
        vn_ref[rs, :] = vn
        first = lax.broadcasted_iota(jnp.int32, (cm, LANES), 1) < HEAD_DIM
        kpad = ws_ref.shape[2] - 2 * cm
        rows = []
        for c in range(r // cm):
            vc = vn[c * cm:(c + 1) * cm]
            cols = []
            for pp in range(wc // LANES):
                vp = vc[:, pp * LANES:(pp + 1) * LANES]
                parts = [jnp.where(first, vp, 0.0), jnp.where(first, 0.0, vp)]
                if kpad:
                    parts.append(jnp.zeros((kpad, LANES), F32))
                cols.append(_dot(ws_ref[pp], jnp.concatenate(parts, axis=0)))
            s = jnp.concatenate(cols, axis=-1) + bs_ref[...]
            rows.append(u[c * cm:(c + 1) * cm] * s)
        oc = rows[0] if len(rows) == 1 else jnp.concatenate(rows, axis=0)
        oc_ref[rs, :] = _rms(oc, gco_ref[...])
        return y[r - SUBLANES:r], cum[r - 1:r, :]

    subs = [slice(i * r, (i + 1) * r) for i in range(nsub)]
    projected = [project(rs) for rs in subs]
    if seg:
        prev, cum_in = jnp.zeros(carry_conv.shape, F32), jnp.zeros(carry_cum.shape, F32)
    else:
        prev, cum_in = carry_conv[...], carry_cum[...]
    for rs, z in zip(subs, projected):
        prev, cum_in = finish(rs, *z, prev, cum_in)
    if not seg:
        carry_conv[...] = prev
        carry_cum[...] = cum_in
        ytail_ref[...] = prev


def _inproj_kernel_inplace(*refs, n_in, **kw):
    return _inproj_kernel(*refs[:n_in], *refs[n_in + 2:], **kw)


def _inproj(x, lw, conv_prev, *, cm, layer, depth, kv_prev):
    b0, t0, d = x.shape
    wa, wb, wc, nh = lw["wa"], lw["wb"], lw["wc"], lw["nh"]
    kw1 = conv_prev.shape[1]
    seg = 0 if t0 >= LANES else t0
    if seg:
        assert seg & (seg - 1) == 0 and seg >= SUBLANES
        x = x.reshape(1, b0 * t0, d)
        tm = _pick(b0 * t0, (512, 256, 128, 64, 32, 16))
        assert tm % seg == 0
        conv_init = jnp.stack([jnp.pad(conv_prev[:, kw1 - k:], ((0, 0), (0, t0 - k), (0, 0))) for k in range(1, kw1 + 1)])
        conv_init = conv_init.reshape(kw1, b0 * t0, -1)
    else:
        tm = _pick(t0, (512, 256, 128))
        conv_init = jnp.pad(conv_prev, ((0, 0), (SUBLANES - kw1, 0), (0, 0)))
    b, t, _ = x.shape
    nt = t // tm
    kv_time_minor = not seg
    tok = lambda w: pl.BlockSpec((None, tm, w), lambda i, j: (i, j, 0))
    per_b = lambda r, w: pl.BlockSpec((None, r, w), lambda i, j: (i, 0, 0))
    outs = [("qaug", 2 * wa, BF16), ("ka", wa, F32), ("va", wa, F32), ("kaug", 2 * wa, BF16), ("vab", wa, BF16),
            ("elem", LANES, F32), ("cum", LANES, F32), ("qb", wb, F32), ("kb", wb, F32), ("vb", wb, F32),
            ("bz", wb, F32), ("oc", wc, F32), ("vn", wc, F32)]
    out_shape = [jax.ShapeDtypeStruct((b, t, w), dt) for _, w, dt in outs]
    out_specs = [tok(w) for _, w, _ in outs]
    if kv_time_minor:
        for k in (1, 2):
            out_shape[k] = jax.ShapeDtypeStruct((depth, b, wa, t), F32)
            out_specs[k] = pl.BlockSpec((None, None, wa, tm), lambda i, j: (layer, i, 0, j))
    if seg:
        out_shape.append(jax.ShapeDtypeStruct((b, t, 3 * wb), F32))
        out_specs.append(tok(3 * wb))
        init_spec = pl.BlockSpec((kw1, tm, 3 * wb), lambda i, j: (0, j, 0))
    else:
        out_shape.append(jax.ShapeDtypeStruct((b, SUBLANES, 3 * wb), F32))
        out_specs.append(per_b(SUBLANES, 3 * wb))
        init_spec = per_b(SUBLANES, 3 * wb)
    consts = [lw["g_pre_mix"], lw["w_big"], lw["w_small"], lw["sp"], lw["conv_w"]]
    consts2 = [lw["g_cv"], lw["b_cv"], lw["ws_cat"][cm], lw["bs_full"][cm], lw["g_c_out"], lw["hsum"], lw["pmat"]]
    nsub = 2 if tm % (2 * max(cm, LANES)) == 0 else 1
    kw = dict(tm=tm, cm=cm, nh=nh, wa=wa, wb=wb, wc=wc, scale=HEAD_DIM ** -0.5, kv_time_minor=kv_time_minor,
              nsub=nsub, seg=seg)
    in_specs = ([tok(d)] + [_const_spec(c.shape) for c in consts] + [init_spec]
                + [_const_spec(c.shape) for c in consts2])
    args = [x, *consts, conv_init, *consts2]
    inplace = kv_time_minor and kv_prev is not None
    if inplace:
        kern = functools.partial(_inproj_kernel_inplace, n_in=len(args), **kw)
        aliases = {len(args): 1, len(args) + 1: 2}
        in_specs = in_specs + [pl.BlockSpec(memory_space=pl.ANY)] * 2
        args = args + list(kv_prev)
    else:
        kern = functools.partial(_inproj_kernel, **kw)
        aliases = {}
    res = pl.pallas_call(
        kern,
        grid=(b, nt),
        in_specs=in_specs,
        out_specs=out_specs,
        out_shape=out_shape,
        input_output_aliases=aliases,
        scratch_shapes=[pltpu.VMEM((SUBLANES, 3 * wb), F32), pltpu.VMEM((1, LANES), F32)],
        compiler_params=_params("arbitrary", "arbitrary"),
        name="inproj",
    )(*args)
    named = {n: r for (n, _, _), r in zip(outs, res[:-1])}
    if seg:
        named = {n: r.reshape(b0, t0, r.shape[-1]) for n, r in named.items()}
        named["conv_new"] = res[-1].reshape(b0, t0, -1)[:, t0 - kw1:, :]
    else:
        named["conv_new"] = res[-1][:, SUBLANES - kw1:, :]
    named["kv_time_minor"] = kv_time_minor
    return named


def _attn_kernel(qt_ref, k_ref, vt_ref, o_ref, *, tq, kc, look, qsplit):
    p = pl.program_id(1)
    i = pl.program_id(2)
    qt = qt_ref[...]
    rowi = lax.broadcasted_iota(jnp.int32, qt.shape, 0)
    zero = jnp.zeros_like(qt)
    qts = []
    for e in range(2):
        a0 = LANES + AUG * (2 * p + e)
        keep = ((rowi >= e * HEAD_DIM) & (rowi < (e + 1) * HEAD_DIM)) | ((rowi >= a0) & (rowi < a0 + AUG))
        qts.append(jnp.where(keep, qt, zero))
    wq = tq // qsplit
    kofs = lax.broadcasted_iota(jnp.int32, (kc, wq), 0)
    qofs = lax.broadcasted_iota(jnp.int32, (kc, wq), 1)
    ones = jnp.ones((2 * SUBLANES, kc), BF16)
    units = [(c, e, h) for c in range(tq // kc) for e in range(2) for h in range(qsplit)]
    slots = [(e, h) for e in range(2) for h in range(qsplit)]

    def trim(c, h, masked):
        lo = max(c * kc - h * wq, 0) if masked else 0
        return None if lo >= wq else lo

    def scores(j, c, e, h, masked):
        k0 = pl.multiple_of(j * tq + c * kc, kc)
        lo = trim(c, h, masked)
        s = jnp.dot(k_ref[pl.ds(k0, kc), :], qts[e][:, h * wq + lo:(h + 1) * wq], preferred_element_type=F32)
        return jnp.concatenate([jnp.full((kc, lo), NEG_INF, F32), s], axis=1) if lo else s

    def fold(j, c, e, h, s, st, masked):
        m, l, acc = st
        k0 = pl.multiple_of(j * tq + c * kc, kc)
        lo = trim(c, h, masked)
        if masked:
            s = jnp.where(c * kc + kofs <= h * wq + qofs, s, NEG_INF)
        m_new = jnp.maximum(m, jnp.max(s, axis=0, keepdims=True))
        alpha = jnp.exp2(m - m_new)
        pt = jnp.exp2(s - m_new).astype(BF16)
        vt = jnp.concatenate([vt_ref[e * HEAD_DIM:(e + 1) * HEAD_DIM, pl.ds(k0, kc)], ones], axis=0)
        r = jnp.dot(vt, pt[:, lo:], preferred_element_type=F32)
        if lo:
            r = jnp.concatenate([jnp.zeros((r.shape[0], lo), F32), r], axis=1)
        return m_new, alpha * l + r[HEAD_DIM:HEAD_DIM + 1], alpha * acc + r[:HEAD_DIM]

    def run(blocks, state):
        state = dict(zip(slots, state))
        todo = [(j, c, e, h, masked) for j, masked in blocks for c, e, h in units
                if trim(c, h, masked) is not None]
        pend = {}
        for k in range(min(look, len(todo))):
            pend[k] = scores(*todo[k])
        for k, (j, c, e, h, masked) in enumerate(todo):
            if k + look < len(todo):
                pend[k + look] = scores(*todo[k + look])
            state[e, h] = fold(j, c, e, h, pend.pop(k), state[e, h], masked)
        return tuple(state[sl] for sl in slots)

    st0 = (jnp.full((1, wq), NEG_INF, F32), jnp.zeros((1, wq), F32), jnp.zeros((HEAD_DIM, wq), F32))
    state = lax.fori_loop(0, i // 2, lambda t, s: run([(2 * t, False), (2 * t + 1, False)], s),
                          (st0,) * len(slots))
    state = lax.cond(i % 2 == 1,
                     lambda s: run([(i - 1, False), (i, True)], s),
                     lambda s: run([(i, True)], s), state)
    done = dict(zip(slots, state))
    ot = jnp.concatenate([jnp.concatenate([done[e, h][2] / done[e, h][1] for h in range(qsplit)], axis=1)
                          for e in range(2)], axis=0)
    o_ref[...] = ot.T


def _attn_prompt(qaug, kaug, vab, *, tq, kc, look, qsplit):
    b, s, wa = vab.shape
    npair = wa // LANES
    qt = jnp.transpose(qaug, (0, 2, 1))
    vt = jnp.transpose(vab, (0, 2, 1))
    kern = functools.partial(_attn_kernel, tq=tq, kc=kc, look=look, qsplit=qsplit)
    return pl.pallas_call(
        kern,
        grid=(b, npair, s // tq),
        in_specs=[pl.BlockSpec((None, 2 * LANES, tq), lambda bi, p, i: (bi, p, i)),
                  pl.BlockSpec((None, s, 2 * LANES), lambda bi, p, i: (bi, 0, p)),
                  pl.BlockSpec((None, LANES, s), lambda bi, p, i: (bi, p, 0))],
        out_specs=pl.BlockSpec((None, tq, LANES), lambda bi, p, i: (bi, i, p)),
        out_shape=jax.ShapeDtypeStruct((b, s, wa), F32),
        compiler_params=_params("arbitrary", "arbitrary", "arbitrary"),
        name="attn_prompt",
    )(qt, kaug, vt)


def _attn_sample_kernel(q_ref, kc_ref, vc_ref, kn_ref, vn_ref, cum_ref, rrow_ref, crow_ref, o_ref, *, n):
    p = pl.program_id(1)
    q = q_ref[:, :LANES]
    lane = lax.broadcasted_iota(jnp.int32, (n, LANES), 1)
    first = lane < HEAD_DIM
    zero = jnp.zeros_like(q)
    past = kc_ref.shape[-1]
    kc = kc_ref[...].reshape(LANES, past).astype(BF16)
    vc = vc_ref[...].reshape(LANES, past).astype(BF16)
    pad = jnp.zeros((LANES - n, LANES), BF16)
    kn = jnp.concatenate([kn_ref[:, :LANES], pad], axis=0)
    vn = jnp.concatenate([vn_ref[...], pad], axis=0)
    cum = cum_ref[...]
    qm = jnp.concatenate([jnp.where(first, q, zero), jnp.where(first, zero, q)], axis=0)
    cq = jnp.concatenate([jnp.sum(jnp.where(lane == 2 * p + e, cum, 0.0), axis=-1, keepdims=True)
                          for e in range(2)], axis=0)
    top = lax.broadcasted_iota(jnp.int32, (2 * n, 1), 0) < n
    rrow = jnp.where(top, rrow_ref[0:1, :], rrow_ref[1:2, :])
    crow = jnp.where(top, crow_ref[0:1, :], crow_ref[1:2, :])
    qrow = lax.broadcasted_iota(jnp.int32, (2 * n, LANES), 0)
    causal = lax.broadcasted_iota(jnp.int32, (2 * n, LANES), 1) <= jnp.where(qrow < n, qrow, qrow - n)
    sc = _dot(qm, kc) + LOG2E * (cq + rrow)
    sn = jnp.where(causal, _dot_nt(qm, kn) + LOG2E * (cq - crow), NEG_INF)
    m = jnp.maximum(jnp.max(sc, axis=-1, keepdims=True), jnp.max(sn, axis=-1, keepdims=True))
    pc = jnp.exp2(sc - m)
    pn = jnp.exp2(sn - m)
    l = jnp.sum(pc, axis=-1, keepdims=True) + jnp.sum(pn, axis=-1, keepdims=True)
    o = (_dot_nt(pc, vc) + _dot(pn, vn)) / l
    o_ref[...] = jnp.where(first, o[:n], o[n:])


def _attn_sample(qaug, kaug, vab, cum, cache_kt, cache_vt, layer, rrow, crow):
    b, n, wa = vab.shape
    past = cache_kt.shape[-1]
    npair = wa // LANES
    new = lambda w: pl.BlockSpec((None, n, w), lambda bi, p: (bi, 0, p))
    old = lambda: pl.BlockSpec((None, None, 2, HEAD_DIM, past), lambda bi, p: (layer, bi, p, 0, 0))
    return pl.pallas_call(
        functools.partial(_attn_sample_kernel, n=n),
        grid=(b, npair),
        in_specs=[new(2 * LANES), old(), old(), new(2 * LANES), new(LANES),
                  pl.BlockSpec((None, n, LANES), lambda bi, p: (bi, 0, 0)),
                  pl.BlockSpec((None, None, 2, past), lambda bi, p: (bi, p, 0, 0)),
                  pl.BlockSpec((None, None, 2, LANES), lambda bi, p: (bi, p, 0, 0))],
        out_specs=new(LANES),
        out_shape=jax.ShapeDtypeStruct((b, n, wa), F32),
        compiler_params=_params("arbitrary", "arbitrary"),
        name="attn_sample",
    )(qaug, cache_kt, cache_vt, kaug, vab, cum, rrow, crow)


def _suffix_kernel(x_ref, o_ref):
    v = x_ref[...]
    n = v.shape[1]
    lane = lax.broadcasted_iota(jnp.int32, v.shape, 1)
    s = 1
    while s < n:
        v = v + jnp.where(lane + s < n, pltpu.roll(v, n - s, 1), 0.0)
        s *= 2
    o_ref[...] = jnp.where(lane + 1 < n, pltpu.roll(v, n - 1, 1), 0.0)


def _exclusive_suffix_sum(x, layer):
    _, h, b, p = x.shape
    return pl.pallas_call(
        _suffix_kernel,
        grid=(h,),
        in_specs=[pl.BlockSpec((None, None, b, p), lambda i: (layer, i, 0, 0))],
        out_specs=pl.BlockSpec((None, b, p), lambda i: (i, 0, 0)),
        out_shape=jax.ShapeDtypeStruct((h, b, p), F32),
        compiler_params=_params("arbitrary"),
        name="suffix_sum",
    )(x)


def _gdn_kernel(q_ref, k_ref, v_ref, bz_ref, elem_ref, s0_ref, gb_ref, esel_ref, hsum_ref,
                o_ref, sout_ref, s_scr, *, nb, nh, chained):
    L = GDN_BLOCK
    t = pl.program_id(1)

    if chained:
        @pl.when(t == 0)
        def _():
            s_scr[...] = s0_ref[...]

    lane = lax.broadcasted_iota(jnp.int32, (L, LANES), 1)
    first = lane < HEAD_DIM
    ri = lax.broadcasted_iota(jnp.int32, (L, L), 0)
    ci = lax.broadcasted_iota(jnp.int32, (L, L), 1)
    incl = ci <= ri
    strict = ci < ri
    same_head = (ri < HEAD_DIM) == (ci < HEAD_DIM)
    lane2 = lax.broadcasted_iota(jnp.int32, (L, 2 * L), 1)
    first2 = jnp.bitwise_and(lane2, LANES - 1) < HEAD_DIM
    xor2 = jnp.bitwise_xor(lax.broadcasted_iota(jnp.int32, (L, 2 * L), 0), jnp.bitwise_and(lane2, L - 1))
    zero_ll = jnp.zeros((L, L), BF16)

    def halves(x, sel):
        return jnp.concatenate([jnp.where(sel, x, 0.0), jnp.where(sel, 0.0, x)], axis=0)

    def dot_heads(y, x):
        xb = x.astype(BF16)
        bd = jnp.concatenate([jnp.concatenate([xb[:, :L], zero_ll], axis=1),
                              jnp.concatenate([zero_ll, xb[:, L:]], axis=1)], axis=0)
        return jnp.dot(y.astype(BF16), bd, preferred_element_type=F32)

    npair = s_scr.shape[0]
    wbw = npair * LANES
    c = {}

    def solve_stages(blocks):
        chains = [(n, p) for n in blocks for p in range(npair)]
        ex = {}
        for n in blocks:
            elem = elem_ref[n * L:(n + 1) * L, :]
            gsum = _seg_cumsum(elem, L)
            mixed = jnp.where((lane >= nh) & (lane < 2 * nh), gsum, elem)
            ex[n] = _dot_select_exact(mixed, esel_ref[...])
        yield
        for n, p in chains:
            rows = slice(n * L, (n + 1) * L)
            cols = slice(p * LANES, (p + 1) * LANES)
            g = ex[n][:, cols]
            bt = ex[n][:, wbw + p * LANES: wbw + (p + 1) * LANES]
            kp = k_ref[rows, cols]
            qp = q_ref[rows, cols]
            g_sw = pltpu.roll(g, HEAD_DIM, 1)
            b_sw = pltpu.roll(bt, HEAD_DIM, 1)
            g_t = g.T
            a_parts, qk_parts = [], []
            for e in range(2):
                sel = first if e == 0 else jnp.logical_not(first)
                gcol = jnp.where(sel, g, g_sw)
                bcol = jnp.where(sel, bt, b_sw)
                grow = g_t[e * HEAD_DIM:e * HEAD_DIM + 1, :]
                dec = jnp.exp(jnp.where(incl, gcol - grow, NEG_INF))
                kk = _dot_nt(jnp.where(sel, kp, 0.0), kp)
                qk_parts.append(_dot_nt(jnp.where(sel, qp, 0.0), kp) * dec)
                a_parts.append(jnp.where(strict, bcol * kk * dec, 0.0))
            a_cat = jnp.concatenate(a_parts, axis=1)
            eg = jnp.exp(g)
            glast = g[L - 1:L, :]
            c[n, p] = dict(a=a_cat, qk=jnp.concatenate(qk_parts, axis=1), glast=glast, qg=qp * eg,
                           kdec=kp * jnp.exp(glast - g),
                           r=jnp.concatenate([v_ref[rows, cols] * bt, kp * bt * eg], axis=1),
                           tm1=-jnp.where(xor2 < 2, a_cat, 0.0))
        yield
        s_blk = 2
        while s_blk < L:
            pm = {}
            for key in chains:
                nmat = jnp.where((xor2 >= s_blk) & (xor2 < 2 * s_blk), c[key]["a"], 0.0)
                pm[key] = nmat + dot_heads(c[key]["tm1"], nmat)
            yield
            for key in chains:
                c[key]["tm1"] = c[key]["tm1"] - pm[key] - dot_heads(pm[key], c[key]["tm1"])
            yield
            s_blk *= 2
        for key in chains:
            r = c[key]["r"]
            c[key]["uw"] = r + _dot(c[key]["tm1"], halves(r, first2))
        yield

    def state_stages(blocks):
        pairs = range(npair)
        for n in blocks:
            rows = slice(n * L, (n + 1) * L)
            s_in = [s_scr[p] if chained else s0_ref[n, p] for p in pairs]
            ws = [_dot(jnp.concatenate([c[n, p]["uw"][:, LANES:], c[n, p]["qg"]], axis=0), s_in[p]) for p in pairs]
            yield
            u = [c[n, p]["uw"][:, :LANES] - ws[p][:L] for p in pairs]
            o = [ws[p][L:] + _dot(c[n, p]["qk"], halves(u[p], first)) for p in pairs]
            yield
            for p in pairs:
                s_new = (s_in[p] * jnp.exp(c[n, p]["glast"])
                         + jnp.where(same_head, _dot(c[n, p]["kdec"].T, u[p]), 0.0))
                if chained:
                    s_scr[p] = s_new
                else:
                    sout_ref[n, p] = s_new
            yield
            for p in pairs:
                cols = slice(p * LANES, (p + 1) * LANES)
                ms = _dot(o[p] * o[p], hsum_ref[...]) * (1.0 / HEAD_DIM)
                bz = bz_ref[rows, cols]
                o_ref[rows, cols] = o[p] * lax.rsqrt(ms + 1e-6) * gb_ref[...] * (bz * _sigmoid(bz))
            yield

    groups = [list(range(g0, min(g0 + GDN_GROUP, nb))) for g0 in range(0, nb, GDN_GROUP)]
    pending = iter(())
    for grp in groups:
        for _ in solve_stages(grp):
            next(pending, None)
        for _ in pending:
            pass
        pending = state_stages(grp)
    for _ in pending:
        pass

    if chained:
        @pl.when(t == pl.num_programs(1) - 1)
        def _():
            sout_ref[...] = s_scr[...]


def _gdn(qb, kb, vb, bz, elem, s0, lw, *, nb):
    b, t, wb = qb.shape
    npair = wb // LANES
    chained = t > GDN_BLOCK
    if not chained:
        qb, kb, vb, bz, elem = (a.reshape(1, b * t, a.shape[-1]) for a in (qb, kb, vb, bz, elem))
        nb = _pick(b, (GDN_TILE_BLOCKS, 2, 1))
    rows, total = qb.shape[:2]
    tile = nb * GDN_BLOCK
    tok = lambda w: pl.BlockSpec((None, tile, w), lambda i, j: (i, j, 0))
    if chained:
        st = pl.BlockSpec((None, npair, LANES, LANES), lambda i, j: (i, 0, 0, 0))
    else:
        st = pl.BlockSpec((nb, npair, LANES, LANES), lambda i, j: (j, 0, 0, 0))
    consts = [lw["g_b_pair"], lw["esel"], lw["hsum128"]]
    ob, s_new = pl.pallas_call(
        functools.partial(_gdn_kernel, nb=nb, nh=lw["nh"], chained=chained),
        grid=(rows, total // tile),
        in_specs=[tok(wb), tok(wb), tok(wb), tok(wb), tok(LANES), st] + [_const_spec(c.shape) for c in consts],
        out_specs=[tok(wb), st],
        out_shape=[jax.ShapeDtypeStruct((rows, total, wb), F32),
                   jax.ShapeDtypeStruct((b, npair, LANES, LANES), F32)],
        scratch_shapes=[pltpu.VMEM((npair, LANES, LANES), F32)],
        compiler_params=_params("arbitrary", "arbitrary"),
        name="gdn",
    )(qb, kb, vb, bz, elem, s0, *consts)
    return ob.reshape(b, t, wb), s_new


def _outffn_kernel(oa_ref, ob_ref, oc_ref, x_ref, ga_ref, wout_ref, gpm_ref, gpf_ref, wfi_ref, wfo_ref,
                   gpo_ref, y_ref, *, dff, nsub):
    r = x_ref.shape[0] // nsub
    rows = [slice(i * r, (i + 1) * r) for i in range(nsub)]
    cat = [jnp.concatenate([_rms(oa_ref[rs, :], ga_ref[...]), ob_ref[rs, :], oc_ref[rs, :]], axis=-1).astype(BF16)
           for rs in rows]
    m = [jnp.dot(c, wout_ref[...], preferred_element_type=F32) for c in cat]
    x1 = [x_ref[rs, :] + _rms(mi, gpm_ref[...]) for rs, mi in zip(rows, m)]
    h = [_rms(xi, gpf_ref[...]).astype(BF16) for xi in x1]
    gu = [jnp.dot(hi, wfi_ref[...], preferred_element_type=F32) for hi in h]
    a = [(g[:, :dff] * _sigmoid(g[:, :dff]) * g[:, dff:]).astype(BF16) for g in gu]
    f = [jnp.dot(ai, wfo_ref[...], preferred_element_type=F32) for ai in a]
    for rs, xi, fi in zip(rows, x1, f):
        y_ref[rs, :] = xi + _rms(fi, gpo_ref[...])


def _outffn(oa, ob, oc, x, lw, *, tm):
    n, d = x.shape
    dff = lw["w_ffn_out"].shape[0]
    tok = lambda w: pl.BlockSpec((tm, w), lambda i: (i, 0))
    consts = [lw["g_a_out"], lw["w_out"], lw["g_post_mix"], lw["g_pre_ffn"], lw["w_ffn_in"],
              lw["w_ffn_out"], lw["g_post_ffn"]]
    return pl.pallas_call(
        functools.partial(_outffn_kernel, dff=dff, nsub=2 if tm % (4 * SUBLANES) == 0 else 1),
        grid=(n // tm,),
        in_specs=[tok(oa.shape[1]), tok(ob.shape[1]), tok(oc.shape[1]), tok(d)]
                 + [_const_spec(c.shape) for c in consts],
        out_specs=tok(d),
        out_shape=jax.ShapeDtypeStruct((n, d), F32),
        compiler_params=_params("arbitrary"),
        name="outffn",
    )(oa, ob, oc, x, *consts)


def _block_ones(width):
    idx = jnp.arange(width) // HEAD_DIM
    return (idx[:, None] == idx[None, :]).astype(BF16)


def _layer_weights(l, prm, cms):
    w_in = prm["w_in"][l]
    nh = prm["b_f"].shape[1]
    wa = nh * HEAD_DIM
    wb = prm["a_log"].shape[1] * HEAD_DIM
    wc = prm["g_cv"].shape[1]
    ng = prm["w_s"].shape[1]
    assert prm["a_log"].shape[1] == nh and wa % LANES == 0 and wc % LANES == 0 and AUG * nh <= LANES
    sizes = (wa, wa, wa, nh, 3 * wb, nh, nh, wb, wc, wc)
    offs = [0]
    for sz in sizes:
        offs.append(offs[-1] + sz)
    w_in_t = w_in.T
    col = lambda i: w_in_t[offs[i]:offs[i + 1]]
    w_big = jnp.concatenate([col(4), col(8), col(9), col(0), col(1), col(2), col(7)], axis=0).astype(BF16)
    w_small = jnp.concatenate([col(3), col(5), col(6), jnp.zeros((LANES - 3 * nh, w_in.shape[0]), F32)],
                              axis=0).astype(BF16)
    zpad = jnp.zeros((LANES - 2 * nh,), F32)
    sp = jnp.zeros((SUBLANES, LANES), F32)
    sp = sp.at[0].set(jnp.concatenate([prm["b_f"][l], prm["dt_bias"][l], zpad]))
    sp = sp.at[1].set(jnp.concatenate([jnp.zeros((nh,), F32), prm["a_log"][l], zpad]))
    hl = jnp.arange(nh) * AUG
    sp = sp.at[2, (hl[:, None] + jnp.arange(3, 6)[None, :]).reshape(-1)].set(1.0)
    sp = sp.at[3, (hl[:, None] + jnp.arange(0, 3)[None, :]).reshape(-1)].set(1.0)
    pmat = jnp.zeros((3 * LANES, 2 * LANES), F32)
    for piece in range(3):
        pmat = pmat.at[piece * LANES + jnp.arange(nh), hl + piece].set(1.0)
        pmat = pmat.at[piece * LANES + jnp.arange(nh), LANES + hl + 3 + piece].set(1.0)
    row = lambda v: v.reshape(1, -1)
    ws_cat, bs_full = {}, {}
    for cm in cms:
        pos = jnp.arange(cm) // HEAD_DIM
        w = jnp.where(pos[None, :] <= pos[:, None], prm["w_s"][l][:, :cm, :cm], 0.0)
        pairs = [jnp.concatenate([w[2 * pp], w[2 * pp + 1]], axis=1) for pp in range(ng // 2)]
        kpad = max(LANES - 2 * cm, 0)
        ws_cat[cm] = jnp.pad(jnp.stack(pairs), ((0, 0), (0, 0), (0, kpad))).astype(BF16)
        bs_full[cm] = jnp.repeat(prm["b_s"][l][:, :cm].T, wc // ng, axis=1)
    src = jnp.arange(LANES)[:, None]
    dst = jnp.arange(wb)[None, :] // HEAD_DIM
    esel = jnp.concatenate([src == nh + dst, src == 2 * nh + dst], axis=1).astype(BF16)
    return dict(
        nh=nh, wa=wa, wb=wb, wc=wc,
        g_pre_mix=row(prm["g_pre_mix"][l]), w_big=w_big, w_small=w_small, sp=sp, conv_w=prm["conv_w"][l],
        g_cv=row(prm["g_cv"][l]), b_cv=row(prm["b_cv"][l]), ws_cat=ws_cat, bs_full=bs_full,
        g_c_out=row(prm["g_c_out"][l]), hsum=_block_ones(2 * LANES), hsum128=_block_ones(LANES), pmat=pmat.astype(BF16),
        g_b_pair=row(jnp.tile(prm["g_b_out"][l], LANES // HEAD_DIM)), esel=esel,
        g_a_out=row(prm["g_a_out"][l]), w_out=prm["w_out"][l].astype(BF16),
        g_post_mix=row(prm["g_post_mix"][l]), g_pre_ffn=row(prm["g_pre_ffn"][l]),
        w_ffn_in=prm["w_ffn_in"][l].astype(BF16), w_ffn_out=prm["w_ffn_out"][l].astype(BF16),
        g_post_ffn=row(prm["g_post_ffn"][l]))


def _pair_state(s):
    b, h, dk, dv = s.shape
    s = s.reshape(b, h // 2, 2, dk, dv)
    z = jnp.zeros_like(s[:, :, 0])
    top = jnp.concatenate([s[:, :, 0], z], axis=-1)
    bot = jnp.concatenate([z, s[:, :, 1]], axis=-1)
    return jnp.concatenate([top, bot], axis=-2)


def _unpair_state(sp):
    b, hp, _, _ = sp.shape
    s0 = sp[:, :, :HEAD_DIM, :HEAD_DIM]
    s1 = sp[:, :, HEAD_DIM:, HEAD_DIM:]
    return jnp.stack([s0, s1], axis=2).reshape(b, 2 * hp, HEAD_DIM, HEAD_DIM)


def _head_rows(cum, nh):
    b, t, _ = cum.shape
    return jnp.transpose(cum[:, :, :nh], (0, 2, 1)).reshape(b, nh // 2, 2, t)


def _pick(n, prefs):
    for c in prefs:
        if n % c == 0:
            return c
    return n


def _layer(x, lw, conv_prev, s0, cache, *, cm, layer, depth, kv_prev):
    b, t, d = x.shape
    nh, wb = lw["nh"], lw["wb"]
    pj = _inproj(x, lw, conv_prev, cm=cm, layer=layer, depth=depth, kv_prev=kv_prev)

    if cache is None:
        tq = _pick(t, (ATTN_TQ, 256, 128))
        kc = min(ATTN_KC, tq)
        qsplit = ATTN_QSPLIT if tq % (ATTN_QSPLIT * kc) == 0 else 1
        oa = _attn_prompt(pj["qaug"], pj["kaug"], pj["vab"], tq=tq, kc=kc, look=ATTN_LOOK, qsplit=qsplit)
    else:
        ck, cv, clogf_t = cache
        _, _, bs, past = clogf_t.shape
        excl = _exclusive_suffix_sum(clogf_t, layer)
        rrow = jnp.transpose(excl, (1, 0, 2)).reshape(bs, nh // 2, 2, past)
        crow = jnp.pad(_head_rows(pj["cum"], nh), ((0, 0), (0, 0), (0, 0), (0, LANES - t)))
        oa = _attn_sample(pj["qaug"], pj["kaug"], pj["vab"], pj["cum"], ck, cv, layer, rrow, crow)

    tp = -(-t // GDN_BLOCK) * GDN_BLOCK
    padt = lambda a: a if tp == t else jnp.pad(a, ((0, 0), (0, tp - t), (0, 0)))
    nb = _pick(tp // GDN_BLOCK, (GDN_TILE_BLOCKS, 2, 1))
    ob, s_new = _gdn(padt(pj["qb"]), padt(pj["kb"]), padt(pj["vb"]), padt(pj["bz"]), padt(pj["elem"]),
                     _pair_state(s0), lw, nb=nb)
    ob = ob[:, :t]

    n = b * t
    y = _outffn(oa.reshape(n, -1), ob.reshape(n, -1), pj["oc"].reshape(n, -1), x.reshape(n, d), lw,
                tm=_pick(n, (512, 256, 128, 64, 32, 16)))
    if pj["kv_time_minor"]:
        new_kv = (pj["ka"], pj["va"])
    else:
        new_kv = (pj["ka"].reshape(b, t, nh, HEAD_DIM), pj["va"].reshape(b, t, nh, HEAD_DIM))
    state = (new_kv[0], new_kv[1], pj["elem"][:, :, :nh],
             pj["conv_new"], _unpair_state(s_new), pj["vn"])
    return y.reshape(b, t, d), state, pj["kv_time_minor"]


def kernel(x_prompt, x_sample, cache_a_k, cache_a_v, cache_a_logf, state_b_conv, state_b_S, g_pre_mix, w_in, b_f, conv_w, a_log, dt_bias, g_b_out, g_a_out, g_cv, b_cv, w_s, b_s, g_c_out, w_out, g_post_mix, g_pre_ffn, w_ffn_in, w_ffn_out, g_post_ffn):
    prm = dict(g_pre_mix=g_pre_mix, w_in=w_in, b_f=b_f, conv_w=conv_w, a_log=a_log, dt_bias=dt_bias,
               g_b_out=g_b_out, g_a_out=g_a_out, g_cv=g_cv, b_cv=b_cv, w_s=w_s, b_s=b_s, g_c_out=g_c_out,
               w_out=w_out, g_post_mix=g_post_mix, g_pre_ffn=g_pre_ffn, w_ffn_in=w_ffn_in,
               w_ffn_out=w_ffn_out, g_post_ffn=g_post_ffn)
    depth = w_in.shape[0]
    bp, sp_len, _ = x_prompt.shape
    n_new = x_sample.shape[1]
    cm_p = w_s.shape[2]
    assert sp_len % cm_p == 0 and sp_len % GDN_BLOCK == 0 and n_new <= HEAD_DIM and n_new % SUBLANES == 0
    kw1 = conv_w.shape[1] - 1
    nhb = a_log.shape[1]
    yp, ys = x_prompt, x_sample
    outs_p, outs_s = [], []
    cache_kt = jnp.transpose(cache_a_k, (0, 1, 3, 4, 2))
    cache_vt = jnp.transpose(cache_a_v, (0, 1, 3, 4, 2))
    clogf_t = jnp.transpose(cache_a_logf, (0, 3, 1, 2))
    for l in range(depth):
        lw = _layer_weights(l, prm, (cm_p, n_new))
        conv0 = jnp.zeros((bp, kw1, conv_w.shape[2]), F32)
        s0 = jnp.zeros((bp, nhb, HEAD_DIM, HEAD_DIM), F32)
        kv_p = (outs_p[-1][0], outs_p[-1][1]) if outs_p else None
        kv_s = (outs_s[-1][0], outs_s[-1][1]) if outs_s else None
        yp, st_p, shared_p = _layer(yp, lw, conv0, s0, None, cm=cm_p, layer=l, depth=depth, kv_prev=kv_p)
        ys, st_s, shared_s = _layer(ys, lw, state_b_conv[l], state_b_S[l], (cache_kt, cache_vt, clogf_t),
                                    cm=n_new, layer=l, depth=depth, kv_prev=kv_s)
        outs_p.append(st_p)
        outs_s.append(st_s)
    stk = lambda outs, i: jnp.stack([o[i] for o in outs], axis=0)

    def new_cache(outs, i, shared):
        if not shared:
            return stk(outs, i)
        buf = outs[-1][i]
        dp, b, _, t = buf.shape
        return jnp.transpose(buf.reshape(dp, b, -1, HEAD_DIM, t), (0, 1, 4, 2, 3))

    return (yp, ys, new_cache(outs_p, 0, shared_p), new_cache(outs_p, 1, shared_p),
            stk(outs_p, 2), stk(outs_p, 3), stk(outs_p, 4),
            new_cache(outs_s, 0, shared_s), new_cache(outs_s, 1, shared_s),
            stk(outs_s, 2), stk(outs_s, 3), stk(outs_s, 4), stk(outs_s, 5))
```

```python
import functools

import jax
import jax.numpy as jnp
from jax import lax
from jax.experimental import pallas as pl
from jax.experimental.pallas import tpu as pltpu

F32 = jnp.float32
BF16 = jnp.bfloat16

LANES = 128
SUBLANES = 8
HEAD_DIM = 64
GDN_BLOCK = 128
GDN_GROUP = 2
GDN_TILE_BLOCKS = 4
ATTN_TQ = 1024
ATTN_KC = 256
ATTN_LOOK = 8
ATTN_QSPLIT = 4
VMEM_LIMIT = 56 * 1024 * 1024
NEG_INF = float("-inf")
LOG2E = 1.4426950408889634
AUG = 16


def _dot(a, b):
    return jnp.dot(a.astype(BF16), b.astype(BF16), preferred_element_type=F32)


def _dot_nt(a, b):
    return lax.dot_general(a.astype(BF16), b.astype(BF16), (((1,), (1,)), ((), ())),
                           preferred_element_type=F32)


def _dot_select_exact(x, sel):
    hi = x.astype(BF16)
    r1 = x - hi.astype(F32)
    mid = r1.astype(BF16)
    lo = (r1 - mid.astype(F32)).astype(BF16)
    d = lambda p: jnp.dot(p, sel, preferred_element_type=F32)
    return (d(hi) + d(mid)) + d(lo)


def _rms(x, g, eps=1e-6):
    return x * lax.rsqrt(jnp.mean(x * x, axis=-1, keepdims=True) + eps) * g


def _sigmoid(x):
    return 1.0 / (1.0 + jnp.exp(-x))


def _seg_cumsum(v, seg):
    row = lax.broadcasted_iota(jnp.int32, v.shape, 0)
    pos = jnp.bitwise_and(row, seg - 1)
    s = 1
    while s < seg:
        v = v + jnp.where(pos >= s, pltpu.roll(v, s, 0), 0.0)
        s *= 2
    return v


def _const_spec(shape):
    nd = len(shape)
    return pl.BlockSpec(shape, lambda *_: (0,) * nd, pipeline_mode=pl.Buffered(1))


def _params(*sem):
    return pltpu.CompilerParams(dimension_semantics=sem, vmem_limit_bytes=VMEM_LIMIT)


def _inproj_kernel(x_ref, gpre_ref, wbig_ref, wsm_ref, sp_ref, convw_ref, convinit_ref, gcv_ref,
                   bcv_ref, ws_ref, bs_ref, gco_ref, hsum_ref, pmat_ref,
                   qaug_ref, ka_ref, va_ref, kaug_ref, vab_ref, elem_ref, cum_ref, qb_ref, kb_ref,
                   vb_ref, bz_ref, oc_ref, vn_ref, ytail_ref,
                   carry_conv, carry_cum, *, tm, cm, nh, wa, wb, wc, scale, kv_time_minor, nsub, seg):
    if not seg:
        @pl.when(pl.program_id(1) == 0)
        def _():
            carry_cum[...] = jnp.zeros_like(carry_cum)
            carry_conv[...] = convinit_ref[...]

    r = tm // nsub
    o_c = 3 * wb
    o_a = o_c + 2 * wc

    def project(rs):
        h = _rms(x_ref[rs, :], gpre_ref[...]).astype(BF16)
        proj = lambda w: lax.dot_general(h, w, (((1,), (1,)), ((), ())), preferred_element_type=F32)
        return (proj(wbig_ref[:o_c, :]),
                proj(wbig_ref[o_c:o_a, :]),
                proj(wsm_ref[...]),
                proj(wbig_ref[o_a:, :]))

    def finish(rs, y, zc, zs, za, prev, cum_in):
        ka = za[:, wa:2 * wa]
        va = za[:, 2 * wa:3 * wa]
        if kv_time_minor:
            ka_ref[:, rs] = ka.T
            va_ref[:, rs] = va.T
        else:
            ka_ref[rs, :] = ka
            va_ref[rs, :] = va
        vab_ref[rs, :] = va.astype(BF16)
        bz_ref[rs, :] = za[:, 3 * wa:]

        lane = lax.broadcasted_iota(jnp.int32, (r, LANES), 1)
        zb = zs + sp_ref[0:1, :]
        soft_tail = jnp.log1p(jnp.exp(-jnp.abs(zb)))
        logf = -(jnp.maximum(-zb, 0.0) + soft_tail)
        gl = -jnp.exp(sp_ref[1:2, :]) * (jnp.maximum(zb, 0.0) + soft_tail)
        beta = _sigmoid(zs)
        elem = jnp.where(lane < nh, logf, jnp.where(lane < 2 * nh, gl, jnp.where(lane < 3 * nh, beta, 0.0)))
        elem_ref[rs, :] = elem
        cum = _seg_cumsum(elem, seg) if seg else _seg_cumsum(elem, r) + cum_in
        cum_ref[rs, :] = cum

        c2 = jnp.where(lane < nh, cum * LOG2E, 0.0)
        hi = c2.astype(BF16)
        r1 = c2 - hi.astype(F32)
        mid = r1.astype(BF16)
        lo = (r1 - mid.astype(F32)).astype(BF16)
        placed = jnp.dot(jnp.concatenate([hi, mid, lo], axis=1), pmat_ref[...], preferred_element_type=F32)
        augq = (placed[:, :LANES] + sp_ref[2:3, :]).astype(BF16)
        augk = (sp_ref[3:4, :] - placed[:, LANES:]).astype(BF16)
        qs = (za[:, :wa] * (scale * LOG2E)).astype(BF16)
        ks = ka.astype(BF16)
        qaug_ref[rs, :] = jnp.concatenate(
            [a for j in range(0, wa, LANES) for a in (qs[:, j:j + LANES], augq)], axis=1)
        kaug_ref[rs, :] = jnp.concatenate(
            [a for j in range(0, wa, LANES) for a in (ks[:, j:j + LANES], augk)], axis=1)

        row8 = lax.broadcasted_iota(jnp.int32, prev.shape, 0)
        pos = jnp.bitwise_and(lax.broadcasted_iota(jnp.int32, y.shape, 0), max(seg, 1) - 1)
        kw = convw_ref.shape[0]
        acc = y * convw_ref[kw - 1:kw, :]
        for k in range(1, kw):
            yk = pltpu.roll(y, k, 0)
            if seg:
                yk = jnp.where(pos < k, convinit_ref[k - 1, rs, :], yk)
            else:
                top = jnp.where(row8 < k, pltpu.roll(prev, k, 0), yk[0:SUBLANES])
                yk = jnp.concatenate([top, yk[SUBLANES:]], axis=0)
            acc = acc + yk * convw_ref[kw - 1 - k:kw - k, :]
        if seg:
            ytail_ref[rs, :] = y
        yc = acc * _sigmoid(acc)
        qb = yc[:, :wb]
        kb = yc[:, wb:2 * wb]
        sq = jnp.concatenate([qb * qb, kb * kb], axis=-1).astype(BF16)
        hw = hsum_ref.shape[0]
        ss = jnp.concatenate([jnp.dot(sq[:, j:j + hw], hsum_ref[...], preferred_element_type=F32)
                              for j in range(0, 2 * wb, hw)], axis=-1)
        qb_ref[rs, :] = qb * lax.rsqrt(ss[:, :wb] + 1e-6) * scale
        kb_ref[rs, :] = kb * lax.rsqrt(ss[:, wb:] + 1e-6)
        vb_ref[rs, :] = yc[:, 2 * wb:]

        u = jax.nn.gelu(zc[:, :wc])
        gv = jax.nn.gelu(zc[:, wc:])
        mu = jnp.mean(gv, axis=-1, keepdims=True)
        var = jnp.mean(jnp.square(gv - mu), axis=-1, keepdims=True)
        vn = (gv - mu) * lax.rsqrt(var + 1e-5) * gcv_ref[...] + bcv_ref[...]
        vn_ref[rs, :] = vn
        first = lax.broadcasted_iota(jnp.int32, (cm, LANES), 1) < HEAD_DIM
        kpad = ws_ref.shape[2] - 2 * cm
        rows = []
        for c in range(r // cm):
            vc = vn[c * cm:(c + 1) * cm]
            cols = []
            for pp in range(wc // LANES):
                vp = vc[:, pp * LANES:(pp + 1) * LANES]
                parts = [jnp.where(first, vp, 0.0), jnp.where(first, 0.0, vp)]
                if kpad:
                    parts.append(jnp.zeros((kpad, LANES), F32))
                cols.append(_dot(ws_ref[pp], jnp.concatenate(parts, axis=0)))
            s = jnp.concatenate(cols, axis=-1) + bs_ref[...]
            rows.append(u[c * cm:(c + 1) * cm] * s)
        oc = rows[0] if len(rows) == 1 else jnp.concatenate(rows, axis=0)
        oc_ref[rs, :] = _rms(oc, gco_ref[...])
        return y[r - SUBLANES:r], cum[r - 1:r, :]

    subs = [slice(i * r, (i + 1) * r) for i in range(nsub)]
    projected = [project(rs) for rs in subs]
    if seg:
        prev, cum_in = jnp.zeros(carry_conv.shape, F32), jnp.zeros(carry_cum.shape, F32)
    else:
        prev, cum_in = carry_conv[...], carry_cum[...]
    for rs, z in zip(subs, projected):
        prev, cum_in = finish(rs, *z, prev, cum_in)
    if not seg:
        carry_conv[...] = prev
        carry_cum[...] = cum_in
        ytail_ref[...] = prev


def _inproj_kernel_inplace(*refs, n_in, **kw):
    return _inproj_kernel(*refs[:n_in], *refs[n_in + 2:], **kw)


def _inproj(x, lw, conv_prev, *, cm, layer, depth, kv_prev):
    b0, t0, d = x.shape
    wa, wb, wc, nh = lw["wa"], lw["wb"], lw["wc"], lw["nh"]
    kw1 = conv_prev.shape[1]
    seg = 0 if t0 >= LANES else t0
    if seg:
        assert seg & (seg - 1) == 0 and seg >= SUBLANES
        x = x.reshape(1, b0 * t0, d)
        tm = _pick(b0 * t0, (512, 256, 128, 64, 32, 16))
        assert tm % seg == 0
        conv_init = jnp.stack([jnp.pad(conv_prev[:, kw1 - k:], ((0, 0), (0, t0 - k), (0, 0))) for k in range(1, kw1 + 1)])
        conv_init = conv_init.reshape(kw1, b0 * t0, -1)
    else:
        tm = _pick(t0, (512, 256, 128))
        conv_init = jnp.pad(conv_prev, ((0, 0), (SUBLANES - kw1, 0), (0, 0)))
    b, t, _ = x.shape
    nt = t // tm
    kv_time_minor = not seg
    tok = lambda w: pl.BlockSpec((None, tm, w), lambda i, j: (i, j, 0))
    per_b = lambda r, w: pl.BlockSpec((None, r, w), lambda i, j: (i, 0, 0))
    outs = [("qaug", 2 * wa, BF16), ("ka", wa, F32), ("va", wa, F32), ("kaug", 2 * wa, BF16), ("vab", wa, BF16),
            ("elem", LANES, F32), ("cum", LANES, F32), ("qb", wb, F32), ("kb", wb, F32), ("vb", wb, F32),
            ("bz", wb, F32), ("oc", wc, F32), ("vn", wc, F32)]
    out_shape = [jax.ShapeDtypeStruct((b, t, w), dt) for _, w, dt in outs]
    out_specs = [tok(w) for _, w, _ in outs]
    if kv_time_minor:
        for k in (1, 2):
            out_shape[k] = jax.ShapeDtypeStruct((depth, b, wa, t), F32)
            out_specs[k] = pl.BlockSpec((None, None, wa, tm), lambda i, j: (layer, i, 0, j))
    if seg:
        out_shape.append(jax.ShapeDtypeStruct((b, t, 3 * wb), F32))
        out_specs.append(tok(3 * wb))
        init_spec = pl.BlockSpec((kw1, tm, 3 * wb), lambda i, j: (0, j, 0))
    else:
        out_shape.append(jax.ShapeDtypeStruct((b, SUBLANES, 3 * wb), F32))
        out_specs.append(per_b(SUBLANES, 3 * wb))
        init_spec = per_b(SUBLANES, 3 * wb)
    consts = [lw["g_pre_mix"], lw["w_big"], lw["w_small"], lw["sp"], lw["conv_w"]]
    consts2 = [lw["g_cv"], lw["b_cv"], lw["ws_cat"][cm], lw["bs_full"][cm], lw["g_c_out"], lw["hsum"], lw["pmat"]]
    nsub = 2 if tm % (2 * max(cm, LANES)) == 0 else 1
    kw = dict(tm=tm, cm=cm, nh=nh, wa=wa, wb=wb, wc=wc, scale=HEAD_DIM ** -0.5, kv_time_minor=kv_time_minor,
              nsub=nsub, seg=seg)
    in_specs = ([tok(d)] + [_const_spec(c.shape) for c in consts] + [init_spec]
                + [_const_spec(c.shape) for c in consts2])
    args = [x, *consts, conv_init, *consts2]
    inplace = kv_time_minor and kv_prev is not None
    if inplace:
        kern = functools.partial(_inproj_kernel_inplace, n_in=len(args), **kw)
        aliases = {len(args): 1, len(args) + 1: 2}
        in_specs = in_specs + [pl.BlockSpec(memory_space=pl.ANY)] * 2
        args = args + list(kv_prev)
    else:
        kern = functools.partial(_inproj_kernel, **kw)
        aliases = {}
    res = pl.pallas_call(
        kern,
        grid=(b, nt),
        in_specs=in_specs,
        out_specs=out_specs,
        out_shape=out_shape,
        input_output_aliases=aliases,
        scratch_shapes=[pltpu.VMEM((SUBLANES, 3 * wb), F32), pltpu.VMEM((1, LANES), F32)],
        compiler_params=_params("arbitrary", "arbitrary"),
        name="inproj",
    )(*args)
    named = {n: r for (n, _, _), r in zip(outs, res[:-1])}
    if seg:
        named = {n: r.reshape(b0, t0, r.shape[-1]) for n, r in named.items()}
        named["conv_new"] = res[-1].reshape(b0, t0, -1)[:, t0 - kw1:, :]
    else:
        named["conv_new"] = res[-1][:, SUBLANES - kw1:, :]
    named["kv_time_minor"] = kv_time_minor
    return named


def _attn_kernel(qt_ref, k_ref, vt_ref, o_ref, *, tq, kc, look, qsplit):
    p = pl.program_id(1)
    i = pl.program_id(2)
    qt = qt_ref[...]
    rowi = lax.broadcasted_iota(jnp.int32, qt.shape, 0)
    zero = jnp.zeros_like(qt)
    qts = []
    for e in range(2):
        a0 = LANES + AUG * (2 * p + e)
        keep = ((rowi >= e * HEAD_DIM) & (rowi < (e + 1) * HEAD_DIM)) | ((rowi >= a0) & (rowi < a0 + AUG))
        qts.append(jnp.where(keep, qt, zero))
    wq = tq // qsplit
    kofs = lax.broadcasted_iota(jnp.int32, (kc, wq), 0)
    qofs = lax.broadcasted_iota(jnp.int32, (kc, wq), 1)
    ones = jnp.ones((2 * SUBLANES, kc), BF16)
    units = [(c, e, h) for c in range(tq // kc) for e in range(2) for h in range(qsplit)]
    slots = [(e, h) for e in range(2) for h in range(qsplit)]

    def trim(c, h, masked):
        lo = max(c * kc - h * wq, 0) if masked else 0
        return None if lo >= wq else lo

    def scores(j, c, e, h, masked):
        k0 = pl.multiple_of(j * tq + c * kc, kc)
        lo = trim(c, h, masked)
        s = jnp.dot(k_ref[pl.ds(k0, kc), :], qts[e][:, h * wq + lo:(h + 1) * wq], preferred_element_type=F32)
        return jnp.concatenate([jnp.full((kc, lo), NEG_INF, F32), s], axis=1) if lo else s

    def fold(j, c, e, h, s, st, masked):
        m, l, acc = st
        k0 = pl.multiple_of(j * tq + c * kc, kc)
        lo = trim(c, h, masked)
        if masked:
            s = jnp.where(c * kc + kofs <= h * wq + qofs, s, NEG_INF)
        m_new = jnp.maximum(m, jnp.max(s, axis=0, keepdims=True))
        alpha = jnp.exp2(m - m_new)
        pt = jnp.exp2(s - m_new).astype(BF16)
        vt = jnp.concatenate([vt_ref[e * HEAD_DIM:(e + 1) * HEAD_DIM, pl.ds(k0, kc)], ones], axis=0)
        r = jnp.dot(vt, pt[:, lo:], preferred_element_type=F32)
        if lo:
            r = jnp.concatenate([jnp.zeros((r.shape[0], lo), F32), r], axis=1)
        return m_new, alpha * l + r[HEAD_DIM:HEAD_DIM + 1], alpha * acc + r[:HEAD_DIM]

    def run(blocks, state):
        state = dict(zip(slots, state))
        todo = [(j, c, e, h, masked) for j, masked in blocks for c, e, h in units
                if trim(c, h, masked) is not None]
        pend = {}
        for k in range(min(look, len(todo))):
            pend[k] = scores(*todo[k])
        for k, (j, c, e, h, masked) in enumerate(todo):
            if k + look < len(todo):
                pend[k + look] = scores(*todo[k + look])
            state[e, h] = fold(j, c, e, h, pend.pop(k), state[e, h], masked)
        return tuple(state[sl] for sl in slots)

    st0 = (jnp.full((1, wq), NEG_INF, F32), jnp.zeros((1, wq), F32), jnp.zeros((HEAD_DIM, wq), F32))
    state = lax.fori_loop(0, i // 2, lambda t, s: run([(2 * t, False), (2 * t + 1, False)], s),
                          (st0,) * len(slots))
    state = lax.cond(i % 2 == 1,
                     lambda s: run([(i - 1, False), (i, True)], s),
                     lambda s: run([(i, True)], s), state)
    done = dict(zip(slots, state))
    ot = jnp.concatenate([jnp.concatenate([done[e, h][2] / done[e, h][1] for h in range(qsplit)], axis=1)
                          for e in range(2)], axis=0)
    o_ref[...] = ot.T


def _attn_prompt(qaug, kaug, vab, *, tq, kc, look, qsplit):
    b, s, wa = vab.shape
    npair = wa // LANES
    qt = jnp.transpose(qaug, (0, 2, 1))
    vt = jnp.transpose(vab, (0, 2, 1))
    kern = functools.partial(_attn_kernel, tq=tq, kc=kc, look=look, qsplit=qsplit)
    return pl.pallas_call(
        kern,
        grid=(b, npair, s // tq),
        in_specs=[pl.BlockSpec((None, 2 * LANES, tq), lambda bi, p, i: (bi, p, i)),
                  pl.BlockSpec((None, s, 2 * LANES), lambda bi, p, i: (bi, 0, p)),
                  pl.BlockSpec((None, LANES, s), lambda bi, p, i: (bi, p, 0))],
        out_specs=pl.BlockSpec((None, tq, LANES), lambda bi, p, i: (bi, i, p)),
        out_shape=jax.ShapeDtypeStruct((b, s, wa), F32),
        compiler_params=_params("arbitrary", "arbitrary", "arbitrary"),
        name="attn_prompt",
    )(qt, kaug, vt)


def _attn_sample_kernel(q_ref, kc_ref, vc_ref, kn_ref, vn_ref, cum_ref, rrow_ref, crow_ref, o_ref, *, n):
    p = pl.program_id(1)
    q = q_ref[:, :LANES]
    lane = lax.broadcasted_iota(jnp.int32, (n, LANES), 1)
    first = lane < HEAD_DIM
    zero = jnp.zeros_like(q)
    past = kc_ref.shape[-1]
    kc = kc_ref[...].reshape(LANES, past).astype(BF16)
    vc = vc_ref[...].reshape(LANES, past).astype(BF16)
    pad = jnp.zeros((LANES - n, LANES), BF16)
    kn = jnp.concatenate([kn_ref[:, :LANES], pad], axis=0)
    vn = jnp.concatenate([vn_ref[...], pad], axis=0)
    cum = cum_ref[...]
    qm = jnp.concatenate([jnp.where(first, q, zero), jnp.where(first, zero, q)], axis=0)
    cq = jnp.concatenate([jnp.sum(jnp.where(lane == 2 * p + e, cum, 0.0), axis=-1, keepdims=True)
                          for e in range(2)], axis=0)
    top = lax.broadcasted_iota(jnp.int32, (2 * n, 1), 0) < n
    rrow = jnp.where(top, rrow_ref[0:1, :], rrow_ref[1:2, :])
    crow = jnp.where(top, crow_ref[0:1, :], crow_ref[1:2, :])
    qrow = lax.broadcasted_iota(jnp.int32, (2 * n, LANES), 0)
    causal = lax.broadcasted_iota(jnp.int32, (2 * n, LANES), 1) <= jnp.where(qrow < n, qrow, qrow - n)
    sc = _dot(qm, kc) + LOG2E * (cq + rrow)
    sn = jnp.where(causal, _dot_nt(qm, kn) + LOG2E * (cq - crow), NEG_INF)
    m = jnp.maximum(jnp.max(sc, axis=-1, keepdims=True), jnp.max(sn, axis=-1, keepdims=True))
    pc = jnp.exp2(sc - m)
    pn = jnp.exp2(sn - m)
    l = jnp.sum(pc, axis=-1, keepdims=True) + jnp.sum(pn, axis=-1, keepdims=True)
    o = (_dot_nt(pc, vc) + _dot(pn, vn)) / l
    o_ref[...] = jnp.where(first, o[:n], o[n:])


def _attn_sample(qaug, kaug, vab, cum, cache_kt, cache_vt, layer, rrow, crow):
    b, n, wa = vab.shape
    past = cache_kt.shape[-1]
    npair = wa // LANES
    new = lambda w: pl.BlockSpec((None, n, w), lambda bi, p: (bi, 0, p))
    old = lambda: pl.BlockSpec((None, None, 2, HEAD_DIM, past), lambda bi, p: (layer, bi, p, 0, 0))
    return pl.pallas_call(
        functools.partial(_attn_sample_kernel, n=n),
        grid=(b, npair),
        in_specs=[new(2 * LANES), old(), old(), new(2 * LANES), new(LANES),
                  pl.BlockSpec((None, n, LANES), lambda bi, p: (bi, 0, 0)),
                  pl.BlockSpec((None, None, 2, past), lambda bi, p: (bi, p, 0, 0)),
                  pl.BlockSpec((None, None, 2, LANES), lambda bi, p: (bi, p, 0, 0))],
        out_specs=new(LANES),
        out_shape=jax.ShapeDtypeStruct((b, n, wa), F32),
        compiler_params=_params("arbitrary", "arbitrary"),
        name="attn_sample",
    )(qaug, cache_kt, cache_vt, kaug, vab, cum, rrow, crow)


def _suffix_kernel(x_ref, o_ref):
    v = x_ref[...]
    n = v.shape[1]
    lane = lax.broadcasted_iota(jnp.int32, v.shape, 1)
    s = 1
    while s < n:
        v = v + jnp.where(lane + s < n, pltpu.roll(v, n - s, 1), 0.0)
        s *= 2
    o_ref[...] = jnp.where(lane + 1 < n, pltpu.roll(v, n - 1, 1), 0.0)


def _exclusive_suffix_sum(x, layer):
    _, h, b, p = x.shape
    return pl.pallas_call(
        _suffix_kernel,
        grid=(h,),
        in_specs=[pl.BlockSpec((None, None, b, p), lambda i: (layer, i, 0, 0))],
        out_specs=pl.BlockSpec((None, b, p), lambda i: (i, 0, 0)),
        out_shape=jax.ShapeDtypeStruct((h, b, p), F32),
        compiler_params=_params("arbitrary"),
        name="suffix_sum",
    )(x)


def _gdn_kernel(q_ref, k_ref, v_ref, bz_ref, elem_ref, s0_ref, gb_ref, esel_ref, hsum_ref,
                o_ref, sout_ref, s_scr, *, nb, nh, chained):
    L = GDN_BLOCK
    t = pl.program_id(1)

    if chained:
        @pl.when(t == 0)
        def _():
            s_scr[...] = s0_ref[...]

    lane = lax.broadcasted_iota(jnp.int32, (L, LANES), 1)
    first = lane < HEAD_DIM
    ri = lax.broadcasted_iota(jnp.int32, (L, L), 0)
    ci = lax.broadcasted_iota(jnp.int32, (L, L), 1)
    incl = ci <= ri
    strict = ci < ri
    same_head = (ri < HEAD_DIM) == (ci < HEAD_DIM)
    lane2 = lax.broadcasted_iota(jnp.int32, (L, 2 * L), 1)
    first2 = jnp.bitwise_and(lane2, LANES - 1) < HEAD_DIM
    xor2 = jnp.bitwise_xor(lax.broadcasted_iota(jnp.int32, (L, 2 * L), 0), jnp.bitwise_and(lane2, L - 1))
    zero_ll = jnp.zeros((L, L), BF16)

    def halves(x, sel):
        return jnp.concatenate([jnp.where(sel, x, 0.0), jnp.where(sel, 0.0, x)], axis=0)

    def dot_heads(y, x):
        xb = x.astype(BF16)
        bd = jnp.concatenate([jnp.concatenate([xb[:, :L], zero_ll], axis=1),
                              jnp.concatenate([zero_ll, xb[:, L:]], axis=1)], axis=0)
        return jnp.dot(y.astype(BF16), bd, preferred_element_type=F32)

    npair = s_scr.shape[0]
    wbw = npair * LANES
    c = {}

    def solve_stages(blocks):
        chains = [(n, p) for n in blocks for p in range(npair)]
        ex = {}
        for n in blocks:
            elem = elem_ref[n * L:(n + 1) * L, :]
            gsum = _seg_cumsum(elem, L)
            mixed = jnp.where((lane >= nh) & (lane < 2 * nh), gsum, elem)
            ex[n] = _dot_select_exact(mixed, esel_ref[...])
        yield
        for n, p in chains:
            rows = slice(n * L, (n + 1) * L)
            cols = slice(p * LANES, (p + 1) * LANES)
            g = ex[n][:, cols]
            bt = ex[n][:, wbw + p * LANES: wbw + (p + 1) * LANES]
            kp = k_ref[rows, cols]
            qp = q_ref[rows, cols]
            g_sw = pltpu.roll(g, HEAD_DIM, 1)
            b_sw = pltpu.roll(bt, HEAD_DIM, 1)
            g_t = g.T
            a_parts, qk_parts = [], []
            for e in range(2):
                sel = first if e == 0 else jnp.logical_not(first)
                gcol = jnp.where(sel, g, g_sw)
                bcol = jnp.where(sel, bt, b_sw)
                grow = g_t[e * HEAD_DIM:e * HEAD_DIM + 1, :]
                dec = jnp.exp(jnp.where(incl, gcol - grow, NEG_INF))
                kk = _dot_nt(jnp.where(sel, kp, 0.0), kp)
                qk_parts.append(_dot_nt(jnp.where(sel, qp, 0.0), kp) * dec)
                a_parts.append(jnp.where(strict, bcol * kk * dec, 0.0))
            a_cat = jnp.concatenate(a_parts, axis=1)
            eg = jnp.exp(g)
            glast = g[L - 1:L, :]
            c[n, p] = dict(a=a_cat, qk=jnp.concatenate(qk_parts, axis=1), glast=glast, qg=qp * eg,
                           kdec=kp * jnp.exp(glast - g),
                           r=jnp.concatenate([v_ref[rows, cols] * bt, kp * bt * eg], axis=1),
                           tm1=-jnp.where(xor2 < 2, a_cat, 0.0))
        yield
        s_blk = 2
        while s_blk < L:
            pm = {}
            for key in chains:
                nmat = jnp.where((xor2 >= s_blk) & (xor2 < 2 * s_blk), c[key]["a"], 0.0)
                pm[key] = nmat + dot_heads(c[key]["tm1"], nmat)
            yield
            for key in chains:
                c[key]["tm1"] = c[key]["tm1"] - pm[key] - dot_heads(pm[key], c[key]["tm1"])
            yield
            s_blk *= 2
        for key in chains:
            r = c[key]["r"]
            c[key]["uw"] = r + _dot(c[key]["tm1"], halves(r, first2))
        yield

    def state_stages(blocks):
        pairs = range(npair)
        for n in blocks:
            rows = slice(n * L, (n + 1) * L)
            s_in = [s_scr[p] if chained else s0_ref[n, p] for p in pairs]
            ws = [_dot(jnp.concatenate([c[n, p]["uw"][:, LANES:], c[n, p]["qg"]], axis=0), s_in[p]) for p in pairs]
            yield
            u = [c[n, p]["uw"][:, :LANES] - ws[p][:L] for p in pairs]
            o = [ws[p][L:] + _dot(c[n, p]["qk"], halves(u[p], first)) for p in pairs]
            yield
            for p in pairs:
                s_new = (s_in[p] * jnp.exp(c[n, p]["glast"])
                         + jnp.where(same_head, _dot(c[n, p]["kdec"].T, u[p]), 0.0))
                if chained:
                    s_scr[p] = s_new
                else:
                    sout_ref[n, p] = s_new
            yield
            for p in pairs:
                cols = slice(p * LANES, (p + 1) * LANES)
                ms = _dot(o[p] * o[p], hsum_ref[...]) * (1.0 / HEAD_DIM)
                bz = bz_ref[rows, cols]
                o_ref[rows, cols] = o[p] * lax.rsqrt(ms + 1e-6) * gb_ref[...] * (bz * _sigmoid(bz))
            yield

    groups = [list(range(g0, min(g0 + GDN_GROUP, nb))) for g0 in range(0, nb, GDN_GROUP)]
    pending = iter(())
    for grp in groups:
        for _ in solve_stages(grp):
            next(pending, None)
        for _ in pending:
            pass
        pending = state_stages(grp)
    for _ in pending:
        pass

    if chained:
        @pl.when(t == pl.num_programs(1) - 1)
        def _():
            sout_ref[...] = s_scr[...]


def _gdn(qb, kb, vb, bz, elem, s0, lw, *, nb):
    b, t, wb = qb.shape
    npair = wb // LANES
    chained = t > GDN_BLOCK
    if not chained:
        qb, kb, vb, bz, elem = (a.reshape(1, b * t, a.shape[-1]) for a in (qb, kb, vb, bz, elem))
        nb = _pick(b, (GDN_TILE_BLOCKS, 2, 1))
    rows, total = qb.shape[:2]
    tile = nb * GDN_BLOCK
    tok = lambda w: pl.BlockSpec((None, tile, w), lambda i, j: (i, j, 0))
    if chained:
        st = pl.BlockSpec((None, npair, LANES, LANES), lambda i, j: (i, 0, 0, 0))
    else:
        st = pl.BlockSpec((nb, npair, LANES, LANES), lambda i, j: (j, 0, 0, 0))
    consts = [lw["g_b_pair"], lw["esel"], lw["hsum128"]]
    ob, s_new = pl.pallas_call(
        functools.partial(_gdn_kernel, nb=nb, nh=lw["nh"], chained=chained),
        grid=(rows, total // tile),
        in_specs=[tok(wb), tok(wb), tok(wb), tok(wb), tok(LANES), st] + [_const_spec(c.shape) for c in consts],
        out_specs=[tok(wb), st],
        out_shape=[jax.ShapeDtypeStruct((rows, total, wb), F32),
                   jax.ShapeDtypeStruct((b, npair, LANES, LANES), F32)],
        scratch_shapes=[pltpu.VMEM((npair, LANES, LANES), F32)],
        compiler_params=_params("arbitrary", "arbitrary"),
        name="gdn",
    )(qb, kb, vb, bz, elem, s0, *consts)
    return ob.reshape(b, t, wb), s_new


def _outffn_kernel(oa_ref, ob_ref, oc_ref, x_ref, ga_ref, wout_ref, gpm_ref, gpf_ref, wfi_ref, wfo_ref,
                   gpo_ref, y_ref, *, dff, nsub):
    r = x_ref.shape[0] // nsub
    rows = [slice(i * r, (i + 1) * r) for i in range(nsub)]
    cat = [jnp.concatenate([_rms(oa_ref[rs, :], ga_ref[...]), ob_ref[rs, :], oc_ref[rs, :]], axis=-1).astype(BF16)
           for rs in rows]
    m = [jnp.dot(c, wout_ref[...], preferred_element_type=F32) for c in cat]
    x1 = [x_ref[rs, :] + _rms(mi, gpm_ref[...]) for rs, mi in zip(rows, m)]
    h = [_rms(xi, gpf_ref[...]).astype(BF16) for xi in x1]
    gu = [jnp.dot(hi, wfi_ref[...], preferred_element_type=F32) for hi in h]
    a = [(g[:, :dff] * _sigmoid(g[:, :dff]) * g[:, dff:]).astype(BF16) for g in gu]
    f = [jnp.dot(ai, wfo_ref[...], preferred_element_type=F32) for ai in a]
    for rs, xi, fi in zip(rows, x1, f):
        y_ref[rs, :] = xi + _rms(fi, gpo_ref[...])


def _outffn(oa, ob, oc, x, lw, *, tm):
    n, d = x.shape
    dff = lw["w_ffn_out"].shape[0]
    tok = lambda w: pl.BlockSpec((tm, w), lambda i: (i, 0))
    consts = [lw["g_a_out"], lw["w_out"], lw["g_post_mix"], lw["g_pre_ffn"], lw["w_ffn_in"],
              lw["w_ffn_out"], lw["g_post_ffn"]]
    return pl.pallas_call(
        functools.partial(_outffn_kernel, dff=dff, nsub=2 if tm % (4 * SUBLANES) == 0 else 1),
        grid=(n // tm,),
        in_specs=[tok(oa.shape[1]), tok(ob.shape[1]), tok(oc.shape[1]), tok(d)]
                 + [_const_spec(c.shape) for c in consts],
        out_specs=tok(d),
        out_shape=jax.ShapeDtypeStruct((n, d), F32),
        compiler_params=_params("arbitrary"),
        name="outffn",
    )(oa, ob, oc, x, *consts)


def _block_ones(width):
    idx = jnp.arange(width) // HEAD_DIM
    return (idx[:, None] == idx[None, :]).astype(BF16)


def _layer_weights(l, prm, cms):
    w_in = prm["w_in"][l]
    nh = prm["b_f"].shape[1]
    wa = nh * HEAD_DIM
    wb = prm["a_log"].shape[1] * HEAD_DIM
    wc = prm["g_cv"].shape[1]
    ng = prm["w_s"].shape[1]
    assert prm["a_log"].shape[1] == nh and wa % LANES == 0 and wc % LANES == 0 and AUG * nh <= LANES
    sizes = (wa, wa, wa, nh, 3 * wb, nh, nh, wb, wc, wc)
    offs = [0]
    for sz in sizes:
        offs.append(offs[-1] + sz)
    w_in_t = w_in.T
    col = lambda i: w_in_t[offs[i]:offs[i + 1]]
    w_big = jnp.concatenate([col(4), col(8), col(9), col(0), col(1), col(2), col(7)], axis=0).astype(BF16)
    w_small = jnp.concatenate([col(3), col(5), col(6), jnp.zeros((LANES - 3 * nh, w_in.shape[0]), F32)],
                              axis=0).astype(BF16)
    zpad = jnp.zeros((LANES - 2 * nh,), F32)
    sp = jnp.zeros((SUBLANES, LANES), F32)
    sp = sp.at[0].set(jnp.concatenate([prm["b_f"][l], prm["dt_bias"][l], zpad]))
    sp = sp.at[1].set(jnp.concatenate([jnp.zeros((nh,), F32), prm["a_log"][l], zpad]))
    hl = jnp.arange(nh) * AUG
    sp = sp.at[2, (hl[:, None] + jnp.arange(3, 6)[None, :]).reshape(-1)].set(1.0)
    sp = sp.at[3, (hl[:, None] + jnp.arange(0, 3)[None, :]).reshape(-1)].set(1.0)
    pmat = jnp.zeros((3 * LANES, 2 * LANES), F32)
    for piece in range(3):
        pmat = pmat.at[piece * LANES + jnp.arange(nh), hl + piece].set(1.0)
        pmat = pmat.at[piece * LANES + jnp.arange(nh), LANES + hl + 3 + piece].set(1.0)
    row = lambda v: v.reshape(1, -1)
    ws_cat, bs_full = {}, {}
    for cm in cms:
        pos = jnp.arange(cm) // HEAD_DIM
        w = jnp.where(pos[None, :] <= pos[:, None], prm["w_s"][l][:, :cm, :cm], 0.0)
        pairs = [jnp.concatenate([w[2 * pp], w[2 * pp + 1]], axis=1) for pp in range(ng // 2)]
        kpad = max(LANES - 2 * cm, 0)
        ws_cat[cm] = jnp.pad(jnp.stack(pairs), ((0, 0), (0, 0), (0, kpad))).astype(BF16)
        bs_full[cm] = jnp.repeat(prm["b_s"][l][:, :cm].T, wc // ng, axis=1)
    src = jnp.arange(LANES)[:, None]
    dst = jnp.arange(wb)[None, :] // HEAD_DIM
    esel = jnp.concatenate([src == nh + dst, src == 2 * nh + dst], axis=1).astype(BF16)
    return dict(
        nh=nh, wa=wa, wb=wb, wc=wc,
        g_pre_mix=row(prm["g_pre_mix"][l]), w_big=w_big, w_small=w_small, sp=sp, conv_w=prm["conv_w"][l],
        g_cv=row(prm["g_cv"][l]), b_cv=row(prm["b_cv"][l]), ws_cat=ws_cat, bs_full=bs_full,
        g_c_out=row(prm["g_c_out"][l]), hsum=_block_ones(2 * LANES), hsum128=_block_ones(LANES), pmat=pmat.astype(BF16),
        g_b_pair=row(jnp.tile(prm["g_b_out"][l], LANES // HEAD_DIM)), esel=esel,
        g_a_out=row(prm["g_a_out"][l]), w_out=prm["w_out"][l].astype(BF16),
        g_post_mix=row(prm["g_post_mix"][l]), g_pre_ffn=row(prm["g_pre_ffn"][l]),
        w_ffn_in=prm["w_ffn_in"][l].astype(BF16), w_ffn_out=prm["w_ffn_out"][l].astype(BF16),
        g_post_ffn=row(prm["g_post_ffn"][l]))


def _pair_state(s):
    b, h, dk, dv = s.shape
    s = s.reshape(b, h // 2, 2, dk, dv)
    z = jnp.zeros_like(s[:, :, 0])
    top = jnp.concatenate([s[:, :, 0], z], axis=-1)
    bot = jnp.concatenate([z, s[:, :, 1]], axis=-1)
    return jnp.concatenate([top, bot], axis=-2)


def _unpair_state(sp):
    b, hp, _, _ = sp.shape
    s0 = sp[:, :, :HEAD_DIM, :HEAD_DIM]
    s1 = sp[:, :, HEAD_DIM:, HEAD_DIM:]
    return jnp.stack([s0, s1], axis=2).reshape(b, 2 * hp, HEAD_DIM, HEAD_DIM)


def _head_rows(cum, nh):
    b, t, _ = cum.shape
    return jnp.transpose(cum[:, :, :nh], (0, 2, 1)).reshape(b, nh // 2, 2, t)


def _pick(n, prefs):
    for c in prefs:
        if n % c == 0:
            return c
    return n


def _layer(x, lw, conv_prev, s0, cache, *, cm, layer, depth, kv_prev):
    b, t, d = x.shape
    nh, wb = lw["nh"], lw["wb"]
    pj = _inproj(x, lw, conv_prev, cm=cm, layer=layer, depth=depth, kv_prev=kv_prev)

    if cache is None:
        tq = _pick(t, (ATTN_TQ, 256, 128))
        kc = min(ATTN_KC, tq)
        qsplit = ATTN_QSPLIT if tq % (ATTN_QSPLIT * kc) == 0 else 1
        oa = _attn_prompt(pj["qaug"], pj["kaug"], pj["vab"], tq=tq, kc=kc, look=ATTN_LOOK, qsplit=qsplit)
    else:
        ck, cv, clogf_t = cache
        _, _, bs, past = clogf_t.shape
        excl = _exclusive_suffix_sum(clogf_t, layer)
        rrow = jnp.transpose(excl, (1, 0, 2)).reshape(bs, nh // 2, 2, past)
        crow = jnp.pad(_head_rows(pj["cum"], nh), ((0, 0), (0, 0), (0, 0), (0, LANES - t)))
        oa = _attn_sample(pj["qaug"], pj["kaug"], pj["vab"], pj["cum"], ck, cv, layer, rrow, crow)

    tp = -(-t // GDN_BLOCK) * GDN_BLOCK
    padt = lambda a: a if tp == t else jnp.pad(a, ((0, 0), (0, tp - t), (0, 0)))
    nb = _pick(tp // GDN_BLOCK, (GDN_TILE_BLOCKS, 2, 1))
    ob, s_new = _gdn(padt(pj["qb"]), padt(pj["kb"]), padt(pj["vb"]), padt(pj["bz"]), padt(pj["elem"]),
                     _pair_state(s0), lw, nb=nb)
    ob = ob[:, :t]

    n = b * t
    y = _outffn(oa.reshape(n, -1), ob.reshape(n, -1), pj["oc"].reshape(n, -1), x.reshape(n, d), lw,
                tm=_pick(n, (512, 256, 128, 64, 32, 16)))
    if pj["kv_time_minor"]:
        new_kv = (pj["ka"], pj["va"])
    else:
        new_kv = (pj["ka"].reshape(b, t, nh, HEAD_DIM), pj["va"].reshape(b, t, nh, HEAD_DIM))
    state = (new_kv[0], new_kv[1], pj["elem"][:, :, :nh],
             pj["conv_new"], _unpair_state(s_new), pj["vn"])
    return y.reshape(b, t, d), state, pj["kv_time_minor"]


def kernel(x_prompt, x_sample, cache_a_k, cache_a_v, cache_a_logf, state_b_conv, state_b_S, g_pre_mix, w_in, b_f, conv_w, a_log, dt_bias, g_b_out, g_a_out, g_cv, b_cv, w_s, b_s, g_c_out, w_out, g_post_mix, g_pre_ffn, w_ffn_in, w_ffn_out, g_post_ffn):
    prm = dict(g_pre_mix=g_pre_mix, w_in=w_in, b_f=b_f, conv_w=conv_w, a_log=a_log, dt_bias=dt_bias,
               g_b_out=g_b_out, g_a_out=g_a_out, g_cv=g_cv, b_cv=b_cv, w_s=w_s, b_s=b_s, g_c_out=g_c_out,
               w_out=w_out, g_post_mix=g_post_mix, g_pre_ffn=g_pre_ffn, w_ffn_in=w_ffn_in,
               w_ffn_out=w_ffn_out, g_post_ffn=g_post_ffn)
    depth = w_in.shape[0]
    bp, sp_len, _ = x_prompt.shape
    n_new = x_sample.shape[1]
    cm_p = w_s.shape[2]
    assert sp_len % cm_p == 0 and sp_len % GDN_BLOCK == 0 and n_new <= HEAD_DIM and n_new % SUBLANES == 0
    kw1 = conv_w.shape[1] - 1
    nhb = a_log.shape[1]
    yp, ys = x_prompt, x_sample
    outs_p, outs_s = [], []
    cache_kt = jnp.transpose(cache_a_k, (0, 1, 3, 4, 2))
    cache_vt = jnp.transpose(cache_a_v, (0, 1, 3, 4, 2))
    clogf_t = jnp.transpose(cache_a_logf, (0, 3, 1, 2))
    for l in range(depth):
        lw = _layer_weights(l, prm, (cm_p, n_new))
        conv0 = jnp.zeros((bp, kw1, conv_w.shape[2]), F32)
        s0 = jnp.zeros((bp, nhb, HEAD_DIM, HEAD_DIM), F32)
        kv_p = (outs_p[-1][0], outs_p[-1][1]) if outs_p else None
        kv_s = (outs_s[-1][0], outs_s[-1][1]) if outs_s else None
        yp, st_p, shared_p = _layer(yp, lw, conv0, s0, None, cm=cm_p, layer=l, depth=depth, kv_prev=kv_p)
        ys, st_s, shared_s = _layer(ys, lw, state_b_conv[l], state_b_S[l], (cache_kt, cache_vt, clogf_t),
                                    cm=n_new, layer=l, depth=depth, kv_prev=kv_s)
        outs_p.append(st_p)
        outs_s.append(st_s)
    stk = lambda outs, i: jnp.stack([o[i] for o in outs], axis=0)

    def new_cache(outs, i, shared):
        if not shared:
            return stk(outs, i)
        buf = outs[-1][i]
        dp, b, _, t = buf.shape
        return jnp.transpose(buf.reshape(dp, b, -1, HEAD_DIM, t), (0, 1, 4, 2, 3))

    return (yp, ys, new_cache(outs_p, 0, shared_p), new_cache(outs_p, 1, shared_p),
            stk(outs_p, 2), stk(outs_p, 3), stk(outs_p, 4),
            new_cache(outs_s, 0, shared_s), new_cache(outs_s, 1, shared_s),
            stk(outs_s, 2), stk(outs_s, 3), stk(outs_s, 4), stk(outs_s, 5))
```

```python
import functools

import jax
import jax.numpy as jnp
from jax import lax
from jax.experimental import pallas as pl
from jax.experimental.pallas import tpu as pltpu

F32 = jnp.float32
BF16 = jnp.bfloat16

LANES = 128
SUBLANES = 8
HEAD_DIM = 64
GDN_BLOCK = 128
GDN_GROUP = 2
GDN_TILE_BLOCKS = 4
ATTN_TQ = 1024
ATTN_KC = 256
ATTN_LOOK = 12
ATTN_QSPLIT = 4
VMEM_LIMIT = 56 * 1024 * 1024
NEG_INF = float("-inf")
LOG2E = 1.4426950408889634
AUG = 16


def _dot(a, b):
    return jnp.dot(a.astype(BF16), b.astype(BF16), preferred_element_type=F32)


def _dot_nt(a, b):
    return lax.dot_general(a.astype(BF16), b.astype(BF16), (((1,), (1,)), ((), ())),
                           preferred_element_type=F32)


def _dot_select_exact(x, sel):
    hi = x.astype(BF16)
    r1 = x - hi.astype(F32)
    mid = r1.astype(BF16)
    lo = (r1 - mid.astype(F32)).astype(BF16)
    d = lambda p: jnp.dot(p, sel, preferred_element_type=F32)
    return (d(hi) + d(mid)) + d(lo)


def _rms(x, g, eps=1e-6):
    return x * lax.rsqrt(jnp.mean(x * x, axis=-1, keepdims=True) + eps) * g


def _sigmoid(x):
    return 1.0 / (1.0 + jnp.exp(-x))


def _seg_cumsum(v, seg):
    row = lax.broadcasted_iota(jnp.int32, v.shape, 0)
    pos = jnp.bitwise_and(row, seg - 1)
    s = 1
    while s < seg:
        v = v + jnp.where(pos >= s, pltpu.roll(v, s, 0), 0.0)
        s *= 2
    return v


def _const_spec(shape):
    nd = len(shape)
    return pl.BlockSpec(shape, lambda *_: (0,) * nd, pipeline_mode=pl.Buffered(1))


def _params(*sem):
    return pltpu.CompilerParams(dimension_semantics=sem, vmem_limit_bytes=VMEM_LIMIT)


def _inproj_kernel(x_ref, gpre_ref, wbig_ref, wsm_ref, sp_ref, convw_ref, convinit_ref, gcv_ref,
                   bcv_ref, ws_ref, bs_ref, gco_ref, hsum_ref, pmat_ref,
                   qaug_ref, ka_ref, va_ref, kaug_ref, vab_ref, elem_ref, cum_ref, qb_ref, kb_ref,
                   vb_ref, bz_ref, oc_ref, vn_ref, ytail_ref,
                   carry_conv, carry_cum, *, tm, cm, nh, wa, wb, wc, scale, kv_time_minor, nsub, seg):
    if not seg:
        @pl.when(pl.program_id(1) == 0)
        def _():
            carry_cum[...] = jnp.zeros_like(carry_cum)
            carry_conv[...] = convinit_ref[...]

    r = tm // nsub
    o_c = 3 * wb
    o_a = o_c + 2 * wc

    def project(rs):
        h = _rms(x_ref[rs, :], gpre_ref[...]).astype(BF16)
        proj = lambda w: lax.dot_general(h, w, (((1,), (1,)), ((), ())), preferred_element_type=F32)
        return (proj(wbig_ref[:o_c, :]),
                proj(wbig_ref[o_c:o_a, :]),
                proj(wsm_ref[...]),
                proj(wbig_ref[o_a:, :]))

    def finish(rs, y, zc, zs, za, prev, cum_in):
        ka = za[:, wa:2 * wa]
        va = za[:, 2 * wa:3 * wa]
        if kv_time_minor:
            ka_ref[:, rs] = ka.T
            va_ref[:, rs] = va.T
        else:
            ka_ref[rs, :] = ka
            va_ref[rs, :] = va
        vab_ref[rs, :] = va.astype(BF16)
        bz_ref[rs, :] = za[:, 3 * wa:]

        lane = lax.broadcasted_iota(jnp.int32, (r, LANES), 1)
        zb = zs + sp_ref[0:1, :]
        soft_tail = jnp.log1p(jnp.exp(-jnp.abs(zb)))
        logf = -(jnp.maximum(-zb, 0.0) + soft_tail)
        gl = -jnp.exp(sp_ref[1:2, :]) * (jnp.maximum(zb, 0.0) + soft_tail)
        beta = _sigmoid(zs)
        elem = jnp.where(lane < nh, logf, jnp.where(lane < 2 * nh, gl, jnp.where(lane < 3 * nh, beta, 0.0)))
        elem_ref[rs, :] = elem
        cum = _seg_cumsum(elem, seg) if seg else _seg_cumsum(elem, r) + cum_in
        cum_ref[rs, :] = cum

        c2 = jnp.where(lane < nh, cum * LOG2E, 0.0)
        hi = c2.astype(BF16)
        r1 = c2 - hi.astype(F32)
        mid = r1.astype(BF16)
        lo = (r1 - mid.astype(F32)).astype(BF16)
        placed = jnp.dot(jnp.concatenate([hi, mid, lo], axis=1), pmat_ref[...], preferred_element_type=F32)
        augq = (placed[:, :LANES] + sp_ref[2:3, :]).astype(BF16)
        augk = (sp_ref[3:4, :] - placed[:, LANES:]).astype(BF16)
        qs = (za[:, :wa] * (scale * LOG2E)).astype(BF16)
        ks = ka.astype(BF16)
        qaug_ref[rs, :] = jnp.concatenate(
            [a for j in range(0, wa, LANES) for a in (qs[:, j:j + LANES], augq)], axis=1)
        kaug_ref[rs, :] = jnp.concatenate(
            [a for j in range(0, wa, LANES) for a in (ks[:, j:j + LANES], augk)], axis=1)

        row8 = lax.broadcasted_iota(jnp.int32, prev.shape, 0)
        pos = jnp.bitwise_and(lax.broadcasted_iota(jnp.int32, y.shape, 0), max(seg, 1) - 1)
        kw = convw_ref.shape[0]
        acc = y * convw_ref[kw - 1:kw, :]
        for k in range(1, kw):
            yk = pltpu.roll(y, k, 0)
            if seg:
                yk = jnp.where(pos < k, convinit_ref[k - 1, rs, :], yk)
            else:
                top = jnp.where(row8 < k, pltpu.roll(prev, k, 0), yk[0:SUBLANES])
                yk = jnp.concatenate([top, yk[SUBLANES:]], axis=0)
            acc = acc + yk * convw_ref[kw - 1 - k:kw - k, :]
        if seg:
            ytail_ref[rs, :] = y
        yc = acc * _sigmoid(acc)
        qb = yc[:, :wb]
        kb = yc[:, wb:2 * wb]
        sq = jnp.concatenate([qb * qb, kb * kb], axis=-1).astype(BF16)
        hw = hsum_ref.shape[0]
        ss = jnp.concatenate([jnp.dot(sq[:, j:j + hw], hsum_ref[...], preferred_element_type=F32)
                              for j in range(0, 2 * wb, hw)], axis=-1)
        qb_ref[rs, :] = qb * lax.rsqrt(ss[:, :wb] + 1e-6) * scale
        kb_ref[rs, :] = kb * lax.rsqrt(ss[:, wb:] + 1e-6)
        vb_ref[rs, :] = yc[:, 2 * wb:]

        u = jax.nn.gelu(zc[:, :wc])
        gv = jax.nn.gelu(zc[:, wc:])
        mu = jnp.mean(gv, axis=-1, keepdims=True)
        var = jnp.mean(jnp.square(gv - mu), axis=-1, keepdims=True)
        vn = (gv - mu) * lax.rsqrt(var + 1e-5) * gcv_ref[...] + bcv_ref[...]
        vn_ref[rs, :] = vn
        first = lax.broadcasted_iota(jnp.int32, (cm, LANES), 1) < HEAD_DIM
        kpad = ws_ref.shape[2] - 2 * cm
        rows = []
        for c in range(r // cm):
            vc = vn[c * cm:(c + 1) * cm]
            cols = []
            for pp in range(wc // LANES):
                vp = vc[:, pp * LANES:(pp + 1) * LANES]
                parts = [jnp.where(first, vp, 0.0), jnp.where(first, 0.0, vp)]
                if kpad:
                    parts.append(jnp.zeros((kpad, LANES), F32))
                cols.append(_dot(ws_ref[pp], jnp.concatenate(parts, axis=0)))
            s = jnp.concatenate(cols, axis=-1) + bs_ref[...]
            rows.append(u[c * cm:(c + 1) * cm] * s)
        oc = rows[0] if len(rows) == 1 else jnp.concatenate(rows, axis=0)
        oc_ref[rs, :] = _rms(oc, gco_ref[...])
        return y[r - SUBLANES:r], cum[r - 1:r, :]

    subs = [slice(i * r, (i + 1) * r) for i in range(nsub)]
    projected = [project(rs) for rs in subs]
    if seg:
        prev, cum_in = jnp.zeros(carry_conv.shape, F32), jnp.zeros(carry_cum.shape, F32)
    else:
        prev, cum_in = carry_conv[...], carry_cum[...]
    for rs, z in zip(subs, projected):
        prev, cum_in = finish(rs, *z, prev, cum_in)
    if not seg:
        carry_conv[...] = prev
        carry_cum[...] = cum_in
        ytail_ref[...] = prev


def _inproj_kernel_inplace(*refs, n_in, **kw):
    return _inproj_kernel(*refs[:n_in], *refs[n_in + 2:], **kw)


def _inproj(x, lw, conv_prev, *, cm, layer, depth, kv_prev):
    b0, t0, d = x.shape
    wa, wb, wc, nh = lw["wa"], lw["wb"], lw["wc"], lw["nh"]
    kw1 = conv_prev.shape[1]
    seg = 0 if t0 >= LANES else t0
    if seg:
        assert seg & (seg - 1) == 0 and seg >= SUBLANES
        x = x.reshape(1, b0 * t0, d)
        tm = _pick(b0 * t0, (512, 256, 128, 64, 32, 16))
        assert tm % seg == 0
        conv_init = jnp.stack([jnp.pad(conv_prev[:, kw1 - k:], ((0, 0), (0, t0 - k), (0, 0))) for k in range(1, kw1 + 1)])
        conv_init = conv_init.reshape(kw1, b0 * t0, -1)
    else:
        tm = _pick(t0, (512, 256, 128))
        conv_init = jnp.pad(conv_prev, ((0, 0), (SUBLANES - kw1, 0), (0, 0)))
    b, t, _ = x.shape
    nt = t // tm
    kv_time_minor = not seg
    tok = lambda w: pl.BlockSpec((None, tm, w), lambda i, j: (i, j, 0))
    per_b = lambda r, w: pl.BlockSpec((None, r, w), lambda i, j: (i, 0, 0))
    outs = [("qaug", 2 * wa, BF16), ("ka", wa, F32), ("va", wa, F32), ("kaug", 2 * wa, BF16), ("vab", wa, BF16),
            ("elem", LANES, F32), ("cum", LANES, F32), ("qb", wb, F32), ("kb", wb, F32), ("vb", wb, F32),
            ("bz", wb, F32), ("oc", wc, F32), ("vn", wc, F32)]
    out_shape = [jax.ShapeDtypeStruct((b, t, w), dt) for _, w, dt in outs]
    out_specs = [tok(w) for _, w, _ in outs]
    if kv_time_minor:
        for k in (1, 2):
            out_shape[k] = jax.ShapeDtypeStruct((depth, b, wa, t), F32)
            out_specs[k] = pl.BlockSpec((None, None, wa, tm), lambda i, j: (layer, i, 0, j))
    if seg:
        out_shape.append(jax.ShapeDtypeStruct((b, t, 3 * wb), F32))
        out_specs.append(tok(3 * wb))
        init_spec = pl.BlockSpec((kw1, tm, 3 * wb), lambda i, j: (0, j, 0))
    else:
        out_shape.append(jax.ShapeDtypeStruct((b, SUBLANES, 3 * wb), F32))
        out_specs.append(per_b(SUBLANES, 3 * wb))
        init_spec = per_b(SUBLANES, 3 * wb)
    consts = [lw["g_pre_mix"], lw["w_big"], lw["w_small"], lw["sp"], lw["conv_w"]]
    consts2 = [lw["g_cv"], lw["b_cv"], lw["ws_cat"][cm], lw["bs_full"][cm], lw["g_c_out"], lw["hsum"], lw["pmat"]]
    nsub = 2 if tm % (2 * max(cm, LANES)) == 0 else 1
    kw = dict(tm=tm, cm=cm, nh=nh, wa=wa, wb=wb, wc=wc, scale=HEAD_DIM ** -0.5, kv_time_minor=kv_time_minor,
              nsub=nsub, seg=seg)
    in_specs = ([tok(d)] + [_const_spec(c.shape) for c in consts] + [init_spec]
                + [_const_spec(c.shape) for c in consts2])
    args = [x, *consts, conv_init, *consts2]
    inplace = kv_time_minor and kv_prev is not None
    if inplace:
        kern = functools.partial(_inproj_kernel_inplace, n_in=len(args), **kw)
        aliases = {len(args): 1, len(args) + 1: 2}
        in_specs = in_specs + [pl.BlockSpec(memory_space=pl.ANY)] * 2
        args = args + list(kv_prev)
    else:
        kern = functools.partial(_inproj_kernel, **kw)
        aliases = {}
    res = pl.pallas_call(
        kern,
        grid=(b, nt),
        in_specs=in_specs,
        out_specs=out_specs,
        out_shape=out_shape,
        input_output_aliases=aliases,
        scratch_shapes=[pltpu.VMEM((SUBLANES, 3 * wb), F32), pltpu.VMEM((1, LANES), F32)],
        compiler_params=_params("arbitrary", "arbitrary"),
        name="inproj",
    )(*args)
    named = {n: r for (n, _, _), r in zip(outs, res[:-1])}
    if seg:
        named = {n: r.reshape(b0, t0, r.shape[-1]) for n, r in named.items()}
        named["conv_new"] = res[-1].reshape(b0, t0, -1)[:, t0 - kw1:, :]
    else:
        named["conv_new"] = res[-1][:, SUBLANES - kw1:, :]
    named["kv_time_minor"] = kv_time_minor
    return named


def _attn_kernel(qt_ref, k_ref, vt_ref, o_ref, *, tq, kc, look, qsplit):
    p = pl.program_id(1)
    i = pl.program_id(2)
    qt = qt_ref[...]
    rowi = lax.broadcasted_iota(jnp.int32, qt.shape, 0)
    zero = jnp.zeros_like(qt)
    qts = []
    for e in range(2):
        a0 = LANES + AUG * (2 * p + e)
        keep = ((rowi >= e * HEAD_DIM) & (rowi < (e + 1) * HEAD_DIM)) | ((rowi >= a0) & (rowi < a0 + AUG))
        qts.append(jnp.where(keep, qt, zero))
    wq = tq // qsplit
    kofs = lax.broadcasted_iota(jnp.int32, (kc, wq), 0)
    qofs = lax.broadcasted_iota(jnp.int32, (kc, wq), 1)
    ones = jnp.ones((2 * SUBLANES, kc), BF16)
    units = [(c, e, h) for c in range(tq // kc) for e in range(2) for h in range(qsplit)]
    slots = [(e, h) for e in range(2) for h in range(qsplit)]

    def trim(c, h, masked):
        lo = max(c * kc - h * wq, 0) if masked else 0
        return None if lo >= wq else lo

    def scores(j, c, e, h, masked):
        k0 = pl.multiple_of(j * tq + c * kc, kc)
        lo = trim(c, h, masked)
        s = jnp.dot(k_ref[pl.ds(k0, kc), :], qts[e][:, h * wq + lo:(h + 1) * wq], preferred_element_type=F32)
        return jnp.concatenate([jnp.full((kc, lo), NEG_INF, F32), s], axis=1) if lo else s

    def fold(j, c, e, h, s, st, masked):
        m, l, acc = st
        k0 = pl.multiple_of(j * tq + c * kc, kc)
        lo = trim(c, h, masked)
        if masked:
            s = jnp.where(c * kc + kofs <= h * wq + qofs, s, NEG_INF)
        m_new = jnp.maximum(m, jnp.max(s, axis=0, keepdims=True))
        alpha = jnp.exp2(m - m_new)
        pt = jnp.exp2(s - m_new).astype(BF16)
        vt = jnp.concatenate([vt_ref[e * HEAD_DIM:(e + 1) * HEAD_DIM, pl.ds(k0, kc)], ones], axis=0)
        r = jnp.dot(vt, pt[:, lo:], preferred_element_type=F32)
        if lo:
            r = jnp.concatenate([jnp.zeros((r.shape[0], lo), F32), r], axis=1)
        return m_new, alpha * l + r[HEAD_DIM:HEAD_DIM + 1], alpha * acc + r[:HEAD_DIM]

    def run(blocks, state):
        state = dict(zip(slots, state))
        todo = [(j, c, e, h, masked) for j, masked in blocks for c, e, h in units
                if trim(c, h, masked) is not None]
        pend = {}
        for k in range(min(look, len(todo))):
            pend[k] = scores(*todo[k])
        for k, (j, c, e, h, masked) in enumerate(todo):
            if k + look < len(todo):
                pend[k + look] = scores(*todo[k + look])
            state[e, h] = fold(j, c, e, h, pend.pop(k), state[e, h], masked)
        return tuple(state[sl] for sl in slots)

    st0 = (jnp.full((1, wq), NEG_INF, F32), jnp.zeros((1, wq), F32), jnp.zeros((HEAD_DIM, wq), F32))
    state = lax.fori_loop(0, i // 2, lambda t, s: run([(2 * t, False), (2 * t + 1, False)], s),
                          (st0,) * len(slots))
    state = lax.cond(i % 2 == 1,
                     lambda s: run([(i - 1, False), (i, True)], s),
                     lambda s: run([(i, True)], s), state)
    done = dict(zip(slots, state))
    ot = jnp.concatenate([jnp.concatenate([done[e, h][2] / done[e, h][1] for h in range(qsplit)], axis=1)
                          for e in range(2)], axis=0)
    o_ref[...] = ot.T


def _attn_prompt(qaug, kaug, vab, *, tq, kc, look, qsplit):
    b, s, wa = vab.shape
    npair = wa // LANES
    qt = jnp.transpose(qaug, (0, 2, 1))
    vt = jnp.transpose(vab, (0, 2, 1))
    kern = functools.partial(_attn_kernel, tq=tq, kc=kc, look=look, qsplit=qsplit)
    return pl.pallas_call(
        kern,
        grid=(b, npair, s // tq),
        in_specs=[pl.BlockSpec((None, 2 * LANES, tq), lambda bi, p, i: (bi, p, i)),
                  pl.BlockSpec((None, s, 2 * LANES), lambda bi, p, i: (bi, 0, p)),
                  pl.BlockSpec((None, LANES, s), lambda bi, p, i: (bi, p, 0))],
        out_specs=pl.BlockSpec((None, tq, LANES), lambda bi, p, i: (bi, i, p)),
        out_shape=jax.ShapeDtypeStruct((b, s, wa), F32),
        compiler_params=_params("arbitrary", "arbitrary", "arbitrary"),
        name="attn_prompt",
    )(qt, kaug, vt)


def _attn_sample_kernel(q_ref, kc_ref, vc_ref, kn_ref, vn_ref, cum_ref, rrow_ref, crow_ref, o_ref, *, n):
    p = pl.program_id(1)
    q = q_ref[:, :LANES]
    lane = lax.broadcasted_iota(jnp.int32, (n, LANES), 1)
    first = lane < HEAD_DIM
    zero = jnp.zeros_like(q)
    past = kc_ref.shape[-1]
    kc = kc_ref[...].reshape(LANES, past).astype(BF16)
    vc = vc_ref[...].reshape(LANES, past).astype(BF16)
    pad = jnp.zeros((LANES - n, LANES), BF16)
    kn = jnp.concatenate([kn_ref[:, :LANES], pad], axis=0)
    vn = jnp.concatenate([vn_ref[...], pad], axis=0)
    cum = cum_ref[...]
    qm = jnp.concatenate([jnp.where(first, q, zero), jnp.where(first, zero, q)], axis=0)
    cq = jnp.concatenate([jnp.sum(jnp.where(lane == 2 * p + e, cum, 0.0), axis=-1, keepdims=True)
                          for e in range(2)], axis=0)
    top = lax.broadcasted_iota(jnp.int32, (2 * n, 1), 0) < n
    rrow = jnp.where(top, rrow_ref[0:1, :], rrow_ref[1:2, :])
    crow = jnp.where(top, crow_ref[0:1, :], crow_ref[1:2, :])
    qrow = lax.broadcasted_iota(jnp.int32, (2 * n, LANES), 0)
    causal = lax.broadcasted_iota(jnp.int32, (2 * n, LANES), 1) <= jnp.where(qrow < n, qrow, qrow - n)
    sc = _dot(qm, kc) + LOG2E * (cq + rrow)
    sn = jnp.where(causal, _dot_nt(qm, kn) + LOG2E * (cq - crow), NEG_INF)
    m = jnp.maximum(jnp.max(sc, axis=-1, keepdims=True), jnp.max(sn, axis=-1, keepdims=True))
    pc = jnp.exp2(sc - m)
    pn = jnp.exp2(sn - m)
    l = jnp.sum(pc, axis=-1, keepdims=True) + jnp.sum(pn, axis=-1, keepdims=True)
    o = (_dot_nt(pc, vc) + _dot(pn, vn)) / l
    o_ref[...] = jnp.where(first, o[:n], o[n:])


def _attn_sample(qaug, kaug, vab, cum, cache_kt, cache_vt, layer, rrow, crow):
    b, n, wa = vab.shape
    past = cache_kt.shape[-1]
    npair = wa // LANES
    new = lambda w: pl.BlockSpec((None, n, w), lambda bi, p: (bi, 0, p))
    old = lambda: pl.BlockSpec((None, None, 2, HEAD_DIM, past), lambda bi, p: (layer, bi, p, 0, 0))
    return pl.pallas_call(
        functools.partial(_attn_sample_kernel, n=n),
        grid=(b, npair),
        in_specs=[new(2 * LANES), old(), old(), new(2 * LANES), new(LANES),
                  pl.BlockSpec((None, n, LANES), lambda bi, p: (bi, 0, 0)),
                  pl.BlockSpec((None, None, 2, past), lambda bi, p: (bi, p, 0, 0)),
                  pl.BlockSpec((None, None, 2, LANES), lambda bi, p: (bi, p, 0, 0))],
        out_specs=new(LANES),
        out_shape=jax.ShapeDtypeStruct((b, n, wa), F32),
        compiler_params=_params("arbitrary", "arbitrary"),
        name="attn_sample",
    )(qaug, cache_kt, cache_vt, kaug, vab, cum, rrow, crow)


def _suffix_kernel(x_ref, o_ref):
    v = x_ref[...]
    n = v.shape[1]
    lane = lax.broadcasted_iota(jnp.int32, v.shape, 1)
    s = 1
    while s < n:
        v = v + jnp.where(lane + s < n, pltpu.roll(v, n - s, 1), 0.0)
        s *= 2
    o_ref[...] = jnp.where(lane + 1 < n, pltpu.roll(v, n - 1, 1), 0.0)


def _exclusive_suffix_sum(x, layer):
    _, h, b, p = x.shape
    return pl.pallas_call(
        _suffix_kernel,
        grid=(h,),
        in_specs=[pl.BlockSpec((None, None, b, p), lambda i: (layer, i, 0, 0))],
        out_specs=pl.BlockSpec((None, b, p), lambda i: (i, 0, 0)),
        out_shape=jax.ShapeDtypeStruct((h, b, p), F32),
        compiler_params=_params("arbitrary"),
        name="suffix_sum",
    )(x)


def _gdn_kernel(q_ref, k_ref, v_ref, bz_ref, elem_ref, s0_ref, gb_ref, esel_ref, hsum_ref,
                o_ref, sout_ref, s_scr, *, nb, nh, chained):
    L = GDN_BLOCK
    t = pl.program_id(1)

    if chained:
        @pl.when(t == 0)
        def _():
            s_scr[...] = s0_ref[...]

    lane = lax.broadcasted_iota(jnp.int32, (L, LANES), 1)
    first = lane < HEAD_DIM
    ri = lax.broadcasted_iota(jnp.int32, (L, L), 0)
    ci = lax.broadcasted_iota(jnp.int32, (L, L), 1)
    incl = ci <= ri
    strict = ci < ri
    same_head = (ri < HEAD_DIM) == (ci < HEAD_DIM)
    lane2 = lax.broadcasted_iota(jnp.int32, (L, 2 * L), 1)
    first2 = jnp.bitwise_and(lane2, LANES - 1) < HEAD_DIM
    xor2 = jnp.bitwise_xor(lax.broadcasted_iota(jnp.int32, (L, 2 * L), 0), jnp.bitwise_and(lane2, L - 1))
    zero_ll = jnp.zeros((L, L), BF16)

    def halves(x, sel):
        return jnp.concatenate([jnp.where(sel, x, 0.0), jnp.where(sel, 0.0, x)], axis=0)

    def dot_heads(y, x):
        xb = x.astype(BF16)
        bd = jnp.concatenate([jnp.concatenate([xb[:, :L], zero_ll], axis=1),
                              jnp.concatenate([zero_ll, xb[:, L:]], axis=1)], axis=0)
        return jnp.dot(y.astype(BF16), bd, preferred_element_type=F32)

    npair = s_scr.shape[0]
    wbw = npair * LANES
    c = {}

    def solve_stages(blocks):
        chains = [(n, p) for n in blocks for p in range(npair)]
        ex = {}
        for n in blocks:
            elem = elem_ref[n * L:(n + 1) * L, :]
            gsum = _seg_cumsum(elem, L)
            mixed = jnp.where((lane >= nh) & (lane < 2 * nh), gsum, elem)
            ex[n] = _dot_select_exact(mixed, esel_ref[...])
        yield
        for n, p in chains:
            rows = slice(n * L, (n + 1) * L)
            cols = slice(p * LANES, (p + 1) * LANES)
            g = ex[n][:, cols]
            bt = ex[n][:, wbw + p * LANES: wbw + (p + 1) * LANES]
            kp = k_ref[rows, cols]
            qp = q_ref[rows, cols]
            g_sw = pltpu.roll(g, HEAD_DIM, 1)
            b_sw = pltpu.roll(bt, HEAD_DIM, 1)
            g_t = g.T
            a_parts, qk_parts = [], []
            for e in range(2):
                sel = first if e == 0 else jnp.logical_not(first)
                gcol = jnp.where(sel, g, g_sw)
                bcol = jnp.where(sel, bt, b_sw)
                grow = g_t[e * HEAD_DIM:e * HEAD_DIM + 1, :]
                dec = jnp.exp(jnp.where(incl, gcol - grow, NEG_INF))
                kk = _dot_nt(jnp.where(sel, kp, 0.0), kp)
                qk_parts.append(_dot_nt(jnp.where(sel, qp, 0.0), kp) * dec)
                a_parts.append(jnp.where(strict, bcol * kk * dec, 0.0))
            a_cat = jnp.concatenate(a_parts, axis=1)
            eg = jnp.exp(g)
            glast = g[L - 1:L, :]
            c[n, p] = dict(a=a_cat, qk=jnp.concatenate(qk_parts, axis=1), glast=glast, qg=qp * eg,
                           kdec=kp * jnp.exp(glast - g),
                           r=jnp.concatenate([v_ref[rows, cols] * bt, kp * bt * eg], axis=1),
                           tm1=-jnp.where(xor2 < 2, a_cat, 0.0))
        yield
        s_blk = 2
        while s_blk < L:
            pm = {}
            for key in chains:
                nmat = jnp.where((xor2 >= s_blk) & (xor2 < 2 * s_blk), c[key]["a"], 0.0)
                pm[key] = nmat + dot_heads(c[key]["tm1"], nmat)
            yield
            for key in chains:
                c[key]["tm1"] = c[key]["tm1"] - pm[key] - dot_heads(pm[key], c[key]["tm1"])
            yield
            s_blk *= 2
        for key in chains:
            r = c[key]["r"]
            c[key]["uw"] = r + _dot(c[key]["tm1"], halves(r, first2))
        yield

    def state_stages(blocks):
        pairs = range(npair)
        for n in blocks:
            rows = slice(n * L, (n + 1) * L)
            s_in = [s_scr[p] if chained else s0_ref[n, p] for p in pairs]
            ws = [_dot(jnp.concatenate([c[n, p]["uw"][:, LANES:], c[n, p]["qg"]], axis=0), s_in[p]) for p in pairs]
            yield
            u = [c[n, p]["uw"][:, :LANES] - ws[p][:L] for p in pairs]
            o = [ws[p][L:] + _dot(c[n, p]["qk"], halves(u[p], first)) for p in pairs]
            yield
            for p in pairs:
                s_new = (s_in[p] * jnp.exp(c[n, p]["glast"])
                         + jnp.where(same_head, _dot(c[n, p]["kdec"].T, u[p]), 0.0))
                if chained:
                    s_scr[p] = s_new
                else:
                    sout_ref[n, p] = s_new
            yield
            for p in pairs:
                cols = slice(p * LANES, (p + 1) * LANES)
                ms = _dot(o[p] * o[p], hsum_ref[...]) * (1.0 / HEAD_DIM)
                bz = bz_ref[rows, cols]
                o_ref[rows, cols] = o[p] * lax.rsqrt(ms + 1e-6) * gb_ref[...] * (bz * _sigmoid(bz))
            yield

    groups = [list(range(g0, min(g0 + GDN_GROUP, nb))) for g0 in range(0, nb, GDN_GROUP)]
    pending = iter(())
    for grp in groups:
        for _ in solve_stages(grp):
            next(pending, None)
        for _ in pending:
            pass
        pending = state_stages(grp)
    for _ in pending:
        pass

    if chained:
        @pl.when(t == pl.num_programs(1) - 1)
        def _():
            sout_ref[...] = s_scr[...]


def _gdn(qb, kb, vb, bz, elem, s0, lw, *, nb):
    b, t, wb = qb.shape
    npair = wb // LANES
    chained = t > GDN_BLOCK
    if not chained:
        qb, kb, vb, bz, elem = (a.reshape(1, b * t, a.shape[-1]) for a in (qb, kb, vb, bz, elem))
        nb = _pick(b, (GDN_TILE_BLOCKS, 2, 1))
    rows, total = qb.shape[:2]
    tile = nb * GDN_BLOCK
    tok = lambda w: pl.BlockSpec((None, tile, w), lambda i, j: (i, j, 0))
    if chained:
        st = pl.BlockSpec((None, npair, LANES, LANES), lambda i, j: (i, 0, 0, 0))
    else:
        st = pl.BlockSpec((nb, npair, LANES, LANES), lambda i, j: (j, 0, 0, 0))
    consts = [lw["g_b_pair"], lw["esel"], lw["hsum128"]]
    ob, s_new = pl.pallas_call(
        functools.partial(_gdn_kernel, nb=nb, nh=lw["nh"], chained=chained),
        grid=(rows, total // tile),
        in_specs=[tok(wb), tok(wb), tok(wb), tok(wb), tok(LANES), st] + [_const_spec(c.shape) for c in consts],
        out_specs=[tok(wb), st],
        out_shape=[jax.ShapeDtypeStruct((rows, total, wb), F32),
                   jax.ShapeDtypeStruct((b, npair, LANES, LANES), F32)],
        scratch_shapes=[pltpu.VMEM((npair, LANES, LANES), F32)],
        compiler_params=_params("arbitrary", "arbitrary"),
        name="gdn",
    )(qb, kb, vb, bz, elem, s0, *consts)
    return ob.reshape(b, t, wb), s_new


def _outffn_kernel(oa_ref, ob_ref, oc_ref, x_ref, ga_ref, wout_ref, gpm_ref, gpf_ref, wfi_ref, wfo_ref,
                   gpo_ref, y_ref, *, dff, nsub):
    r = x_ref.shape[0] // nsub
    rows = [slice(i * r, (i + 1) * r) for i in range(nsub)]
    cat = [jnp.concatenate([_rms(oa_ref[rs, :], ga_ref[...]), ob_ref[rs, :], oc_ref[rs, :]], axis=-1).astype(BF16)
           for rs in rows]
    m = [jnp.dot(c, wout_ref[...], preferred_element_type=F32) for c in cat]
    x1 = [x_ref[rs, :] + _rms(mi, gpm_ref[...]) for rs, mi in zip(rows, m)]
    h = [_rms(xi, gpf_ref[...]).astype(BF16) for xi in x1]
    gu = [jnp.dot(hi, wfi_ref[...], preferred_element_type=F32) for hi in h]
    a = [(g[:, :dff] * _sigmoid(g[:, :dff]) * g[:, dff:]).astype(BF16) for g in gu]
    f = [jnp.dot(ai, wfo_ref[...], preferred_element_type=F32) for ai in a]
    for rs, xi, fi in zip(rows, x1, f):
        y_ref[rs, :] = xi + _rms(fi, gpo_ref[...])


def _outffn(oa, ob, oc, x, lw, *, tm):
    n, d = x.shape
    dff = lw["w_ffn_out"].shape[0]
    tok = lambda w: pl.BlockSpec((tm, w), lambda i: (i, 0))
    consts = [lw["g_a_out"], lw["w_out"], lw["g_post_mix"], lw["g_pre_ffn"], lw["w_ffn_in"],
              lw["w_ffn_out"], lw["g_post_ffn"]]
    return pl.pallas_call(
        functools.partial(_outffn_kernel, dff=dff, nsub=2 if tm % (4 * SUBLANES) == 0 else 1),
        grid=(n // tm,),
        in_specs=[tok(oa.shape[1]), tok(ob.shape[1]), tok(oc.shape[1]), tok(d)]
                 + [_const_spec(c.shape) for c in consts],
        out_specs=tok(d),
        out_shape=jax.ShapeDtypeStruct((n, d), F32),
        compiler_params=_params("arbitrary"),
        name="outffn",
    )(oa, ob, oc, x, *consts)


def _block_ones(width):
    idx = jnp.arange(width) // HEAD_DIM
    return (idx[:, None] == idx[None, :]).astype(BF16)


def _layer_weights(l, prm, cms):
    w_in = prm["w_in"][l]
    nh = prm["b_f"].shape[1]
    wa = nh * HEAD_DIM
    wb = prm["a_log"].shape[1] * HEAD_DIM
    wc = prm["g_cv"].shape[1]
    ng = prm["w_s"].shape[1]
    assert prm["a_log"].shape[1] == nh and wa % LANES == 0 and wc % LANES == 0 and AUG * nh <= LANES
    sizes = (wa, wa, wa, nh, 3 * wb, nh, nh, wb, wc, wc)
    offs = [0]
    for sz in sizes:
        offs.append(offs[-1] + sz)
    w_in_t = w_in.T
    col = lambda i: w_in_t[offs[i]:offs[i + 1]]
    w_big = jnp.concatenate([col(4), col(8), col(9), col(0), col(1), col(2), col(7)], axis=0).astype(BF16)
    w_small = jnp.concatenate([col(3), col(5), col(6), jnp.zeros((LANES - 3 * nh, w_in.shape[0]), F32)],
                              axis=0).astype(BF16)
    zpad = jnp.zeros((LANES - 2 * nh,), F32)
    sp = jnp.zeros((SUBLANES, LANES), F32)
    sp = sp.at[0].set(jnp.concatenate([prm["b_f"][l], prm["dt_bias"][l], zpad]))
    sp = sp.at[1].set(jnp.concatenate([jnp.zeros((nh,), F32), prm["a_log"][l], zpad]))
    hl = jnp.arange(nh) * AUG
    sp = sp.at[2, (hl[:, None] + jnp.arange(3, 6)[None, :]).reshape(-1)].set(1.0)
    sp = sp.at[3, (hl[:, None] + jnp.arange(0, 3)[None, :]).reshape(-1)].set(1.0)
    pmat = jnp.zeros((3 * LANES, 2 * LANES), F32)
    for piece in range(3):
        pmat = pmat.at[piece * LANES + jnp.arange(nh), hl + piece].set(1.0)
        pmat = pmat.at[piece * LANES + jnp.arange(nh), LANES + hl + 3 + piece].set(1.0)
    row = lambda v: v.reshape(1, -1)
    ws_cat, bs_full = {}, {}
    for cm in cms:
        pos = jnp.arange(cm) // HEAD_DIM
        w = jnp.where(pos[None, :] <= pos[:, None], prm["w_s"][l][:, :cm, :cm], 0.0)
        pairs = [jnp.concatenate([w[2 * pp], w[2 * pp + 1]], axis=1) for pp in range(ng // 2)]
        kpad = max(LANES - 2 * cm, 0)
        ws_cat[cm] = jnp.pad(jnp.stack(pairs), ((0, 0), (0, 0), (0, kpad))).astype(BF16)
        bs_full[cm] = jnp.repeat(prm["b_s"][l][:, :cm].T, wc // ng, axis=1)
    src = jnp.arange(LANES)[:, None]
    dst = jnp.arange(wb)[None, :] // HEAD_DIM
    esel = jnp.concatenate([src == nh + dst, src == 2 * nh + dst], axis=1).astype(BF16)
    return dict(
        nh=nh, wa=wa, wb=wb, wc=wc,
        g_pre_mix=row(prm["g_pre_mix"][l]), w_big=w_big, w_small=w_small, sp=sp, conv_w=prm["conv_w"][l],
        g_cv=row(prm["g_cv"][l]), b_cv=row(prm["b_cv"][l]), ws_cat=ws_cat, bs_full=bs_full,
        g_c_out=row(prm["g_c_out"][l]), hsum=_block_ones(2 * LANES), hsum128=_block_ones(LANES), pmat=pmat.astype(BF16),
        g_b_pair=row(jnp.tile(prm["g_b_out"][l], LANES // HEAD_DIM)), esel=esel,
        g_a_out=row(prm["g_a_out"][l]), w_out=prm["w_out"][l].astype(BF16),
        g_post_mix=row(prm["g_post_mix"][l]), g_pre_ffn=row(prm["g_pre_ffn"][l]),
        w_ffn_in=prm["w_ffn_in"][l].astype(BF16), w_ffn_out=prm["w_ffn_out"][l].astype(BF16),
        g_post_ffn=row(prm["g_post_ffn"][l]))


def _pair_state(s):
    b, h, dk, dv = s.shape
    s = s.reshape(b, h // 2, 2, dk, dv)
    z = jnp.zeros_like(s[:, :, 0])
    top = jnp.concatenate([s[:, :, 0], z], axis=-1)
    bot = jnp.concatenate([z, s[:, :, 1]], axis=-1)
    return jnp.concatenate([top, bot], axis=-2)


def _unpair_state(sp):
    b, hp, _, _ = sp.shape
    s0 = sp[:, :, :HEAD_DIM, :HEAD_DIM]
    s1 = sp[:, :, HEAD_DIM:, HEAD_DIM:]
    return jnp.stack([s0, s1], axis=2).reshape(b, 2 * hp, HEAD_DIM, HEAD_DIM)


def _head_rows(cum, nh):
    b, t, _ = cum.shape
    return jnp.transpose(cum[:, :, :nh], (0, 2, 1)).reshape(b, nh // 2, 2, t)


def _pick(n, prefs):
    for c in prefs:
        if n % c == 0:
            return c
    return n


def _layer(x, lw, conv_prev, s0, cache, *, cm, layer, depth, kv_prev):
    b, t, d = x.shape
    nh, wb = lw["nh"], lw["wb"]
    pj = _inproj(x, lw, conv_prev, cm=cm, layer=layer, depth=depth, kv_prev=kv_prev)

    if cache is None:
        tq = _pick(t, (ATTN_TQ, 256, 128))
        kc = min(ATTN_KC, tq)
        qsplit = ATTN_QSPLIT if tq % (ATTN_QSPLIT * kc) == 0 else 1
        oa = _attn_prompt(pj["qaug"], pj["kaug"], pj["vab"], tq=tq, kc=kc, look=ATTN_LOOK, qsplit=qsplit)
    else:
        ck, cv, clogf_t = cache
        _, _, bs, past = clogf_t.shape
        excl = _exclusive_suffix_sum(clogf_t, layer)
        rrow = jnp.transpose(excl, (1, 0, 2)).reshape(bs, nh // 2, 2, past)
        crow = jnp.pad(_head_rows(pj["cum"], nh), ((0, 0), (0, 0), (0, 0), (0, LANES - t)))
        oa = _attn_sample(pj["qaug"], pj["kaug"], pj["vab"], pj["cum"], ck, cv, layer, rrow, crow)

    tp = -(-t // GDN_BLOCK) * GDN_BLOCK
    padt = lambda a: a if tp == t else jnp.pad(a, ((0, 0), (0, tp - t), (0, 0)))
    nb = _pick(tp // GDN_BLOCK, (GDN_TILE_BLOCKS, 2, 1))
    ob, s_new = _gdn(padt(pj["qb"]), padt(pj["kb"]), padt(pj["vb"]), padt(pj["bz"]), padt(pj["elem"]),
                     _pair_state(s0), lw, nb=nb)
    ob = ob[:, :t]

    n = b * t
    y = _outffn(oa.reshape(n, -1), ob.reshape(n, -1), pj["oc"].reshape(n, -1), x.reshape(n, d), lw,
                tm=_pick(n, (512, 256, 128, 64, 32, 16)))
    if pj["kv_time_minor"]:
        new_kv = (pj["ka"], pj["va"])
    else:
        new_kv = (pj["ka"].reshape(b, t, nh, HEAD_DIM), pj["va"].reshape(b, t, nh, HEAD_DIM))
    state = (new_kv[0], new_kv[1], pj["elem"][:, :, :nh],
             pj["conv_new"], _unpair_state(s_new), pj["vn"])
    return y.reshape(b, t, d), state, pj["kv_time_minor"]


def kernel(x_prompt, x_sample, cache_a_k, cache_a_v, cache_a_logf, state_b_conv, state_b_S, g_pre_mix, w_in, b_f, conv_w, a_log, dt_bias, g_b_out, g_a_out, g_cv, b_cv, w_s, b_s, g_c_out, w_out, g_post_mix, g_pre_ffn, w_ffn_in, w_ffn_out, g_post_ffn):
    prm = dict(g_pre_mix=g_pre_mix, w_in=w_in, b_f=b_f, conv_w=conv_w, a_log=a_log, dt_bias=dt_bias,
               g_b_out=g_b_out, g_a_out=g_a_out, g_cv=g_cv, b_cv=b_cv, w_s=w_s, b_s=b_s, g_c_out=g_c_out,
               w_out=w_out, g_post_mix=g_post_mix, g_pre_ffn=g_pre_ffn, w_ffn_in=w_ffn_in,
               w_ffn_out=w_ffn_out, g_post_ffn=g_post_ffn)
    depth = w_in.shape[0]
    bp, sp_len, _ = x_prompt.shape
    n_new = x_sample.shape[1]
    cm_p = w_s.shape[2]
    assert sp_len % cm_p == 0 and sp_len % GDN_BLOCK == 0 and n_new <= HEAD_DIM and n_new % SUBLANES == 0
    kw1 = conv_w.shape[1] - 1
    nhb = a_log.shape[1]
    yp, ys = x_prompt, x_sample
    outs_p, outs_s = [], []
    cache_kt = jnp.transpose(cache_a_k, (0, 1, 3, 4, 2))
    cache_vt = jnp.transpose(cache_a_v, (0, 1, 3, 4, 2))
    clogf_t = jnp.transpose(cache_a_logf, (0, 3, 1, 2))
    for l in range(depth):
        lw = _layer_weights(l, prm, (cm_p, n_new))
        conv0 = jnp.zeros((bp, kw1, conv_w.shape[2]), F32)
        s0 = jnp.zeros((bp, nhb, HEAD_DIM, HEAD_DIM), F32)
        kv_p = (outs_p[-1][0], outs_p[-1][1]) if outs_p else None
        kv_s = (outs_s[-1][0], outs_s[-1][1]) if outs_s else None
        yp, st_p, shared_p = _layer(yp, lw, conv0, s0, None, cm=cm_p, layer=l, depth=depth, kv_prev=kv_p)
        ys, st_s, shared_s = _layer(ys, lw, state_b_conv[l], state_b_S[l], (cache_kt, cache_vt, clogf_t),
                                    cm=n_new, layer=l, depth=depth, kv_prev=kv_s)
        outs_p.append(st_p)
        outs_s.append(st_s)
    stk = lambda outs, i: jnp.stack([o[i] for o in outs], axis=0)

    def new_cache(outs, i, shared):
        if not shared:
            return stk(outs, i)
        buf = outs[-1][i]
        dp, b, _, t = buf.shape
        return jnp.transpose(buf.reshape(dp, b, -1, HEAD_DIM, t), (0, 1, 4, 2, 3))

    return (yp, ys, new_cache(outs_p, 0, shared_p), new_cache(outs_p, 1, shared_p),
            stk(outs_p, 2), stk(outs_p, 3), stk(outs_p, 4),
            new_cache(outs_s, 0, shared_s), new_cache(outs_s, 1, shared_s),
            stk(outs_s, 2), stk(outs_s, 3), stk(outs_s, 4), stk(outs_s, 5))
```

```python
import functools

import jax
import jax.numpy as jnp
from jax import lax
from jax.experimental import pallas as pl
from jax.experimental.pallas import tpu as pltpu

F32 = jnp.float32
BF16 = jnp.bfloat16

LANES = 128
SUBLANES = 8
HEAD_DIM = 64
GDN_BLOCK = 128
GDN_GROUP = 2
GDN_TILE_BLOCKS = 4
FFN_CHUNKS = 2
ATTN_TQ = 1024
ATTN_KC = 256
ATTN_LOOK = 8
ATTN_QSPLIT = 4
VMEM_LIMIT = 56 * 1024 * 1024
NEG_INF = float("-inf")
LOG2E = 1.4426950408889634
AUG = 16


def _dot(a, b):
    return jnp.dot(a.astype(BF16), b.astype(BF16), preferred_element_type=F32)


def _dot_nt(a, b):
    return lax.dot_general(a.astype(BF16), b.astype(BF16), (((1,), (1,)), ((), ())),
                           preferred_element_type=F32)


def _dot_select_exact(x, sel):
    hi = x.astype(BF16)
    r1 = x - hi.astype(F32)
    mid = r1.astype(BF16)
    lo = (r1 - mid.astype(F32)).astype(BF16)
    d = lambda p: jnp.dot(p, sel, preferred_element_type=F32)
    return (d(hi) + d(mid)) + d(lo)


def _rms(x, g, eps=1e-6):
    return x * lax.rsqrt(jnp.mean(x * x, axis=-1, keepdims=True) + eps) * g


def _sigmoid(x):
    return 1.0 / (1.0 + jnp.exp(-x))


def _seg_cumsum(v, seg):
    row = lax.broadcasted_iota(jnp.int32, v.shape, 0)
    pos = jnp.bitwise_and(row, seg - 1)
    s = 1
    while s < seg:
        v = v + jnp.where(pos >= s, pltpu.roll(v, s, 0), 0.0)
        s *= 2
    return v


def _const_spec(shape):
    nd = len(shape)
    return pl.BlockSpec(shape, lambda *_: (0,) * nd, pipeline_mode=pl.Buffered(1))


def _params(*sem):
    return pltpu.CompilerParams(dimension_semantics=sem, vmem_limit_bytes=VMEM_LIMIT)


def _inproj_kernel(x_ref, gpre_ref, wbig_ref, wsm_ref, sp_ref, convw_ref, convinit_ref, gcv_ref,
                   bcv_ref, ws_ref, bs_ref, gco_ref, hsum_ref, pmat_ref,
                   qaug_ref, ka_ref, va_ref, kaug_ref, vab_ref, elem_ref, cum_ref, qb_ref, kb_ref,
                   vb_ref, bz_ref, oc_ref, vn_ref, ytail_ref,
                   carry_conv, carry_cum, *, tm, cm, nh, wa, wb, wc, scale, kv_time_minor, nsub, seg):
    if not seg:
        @pl.when(pl.program_id(1) == 0)
        def _():
            carry_cum[...] = jnp.zeros_like(carry_cum)
            carry_conv[...] = convinit_ref[...]

    r = tm // nsub
    o_c = 3 * wb
    o_a = o_c + 2 * wc

    def project(rs):
        h = _rms(x_ref[rs, :], gpre_ref[...]).astype(BF16)
        proj = lambda w: lax.dot_general(h, w, (((1,), (1,)), ((), ())), preferred_element_type=F32)
        return (proj(wbig_ref[:o_c, :]),
                proj(wbig_ref[o_c:o_a, :]),
                proj(wsm_ref[...]),
                proj(wbig_ref[o_a:, :]))

    def finish(rs, y, zc, zs, za, prev, cum_in):
        ka = za[:, wa:2 * wa]
        va = za[:, 2 * wa:3 * wa]
        if kv_time_minor:
            ka_ref[:, rs] = ka.T
            va_ref[:, rs] = va.T
        else:
            ka_ref[rs, :] = ka
            va_ref[rs, :] = va
        vab_ref[rs, :] = va.astype(BF16)
        bz_ref[rs, :] = za[:, 3 * wa:]

        lane = lax.broadcasted_iota(jnp.int32, (r, LANES), 1)
        zb = zs + sp_ref[0:1, :]
        soft_tail = jnp.log1p(jnp.exp(-jnp.abs(zb)))
        logf = -(jnp.maximum(-zb, 0.0) + soft_tail)
        gl = -jnp.exp(sp_ref[1:2, :]) * (jnp.maximum(zb, 0.0) + soft_tail)
        beta = _sigmoid(zs)
        elem = jnp.where(lane < nh, logf, jnp.where(lane < 2 * nh, gl, jnp.where(lane < 3 * nh, beta, 0.0)))
        elem_ref[rs, :] = elem
        cum = _seg_cumsum(elem, seg) if seg else _seg_cumsum(elem, r) + cum_in
        cum_ref[rs, :] = cum

        c2 = jnp.where(lane < nh, cum * LOG2E, 0.0)
        hi = c2.astype(BF16)
        r1 = c2 - hi.astype(F32)
        mid = r1.astype(BF16)
        lo = (r1 - mid.astype(F32)).astype(BF16)
        placed = jnp.dot(jnp.concatenate([hi, mid, lo], axis=1), pmat_ref[...], preferred_element_type=F32)
        augq = (placed[:, :LANES] + sp_ref[2:3, :]).astype(BF16)
        augk = (sp_ref[3:4, :] - placed[:, LANES:]).astype(BF16)
        qs = (za[:, :wa] * (scale * LOG2E)).astype(BF16)
        ks = ka.astype(BF16)
        qaug_ref[rs, :] = jnp.concatenate(
            [a for j in range(0, wa, LANES) for a in (qs[:, j:j + LANES], augq)], axis=1)
        kaug_ref[rs, :] = jnp.concatenate(
            [a for j in range(0, wa, LANES) for a in (ks[:, j:j + LANES], augk)], axis=1)

        row8 = lax.broadcasted_iota(jnp.int32, prev.shape, 0)
        pos = jnp.bitwise_and(lax.broadcasted_iota(jnp.int32, y.shape, 0), max(seg, 1) - 1)
        kw = convw_ref.shape[0]
        acc = y * convw_ref[kw - 1:kw, :]
        for k in range(1, kw):
            yk = pltpu.roll(y, k, 0)
            if seg:
                yk = jnp.where(pos < k, convinit_ref[k - 1, rs, :], yk)
            else:
                top = jnp.where(row8 < k, pltpu.roll(prev, k, 0), yk[0:SUBLANES])
                yk = jnp.concatenate([top, yk[SUBLANES:]], axis=0)
            acc = acc + yk * convw_ref[kw - 1 - k:kw - k, :]
        if seg:
            ytail_ref[rs, :] = y
        yc = acc * _sigmoid(acc)
        qb = yc[:, :wb]
        kb = yc[:, wb:2 * wb]
        sq = jnp.concatenate([qb * qb, kb * kb], axis=-1).astype(BF16)
        hw = hsum_ref.shape[0]
        ss = jnp.concatenate([jnp.dot(sq[:, j:j + hw], hsum_ref[...], preferred_element_type=F32)
                              for j in range(0, 2 * wb, hw)], axis=-1)
        qb_ref[rs, :] = qb * lax.rsqrt(ss[:, :wb] + 1e-6) * scale
        kb_ref[rs, :] = kb * lax.rsqrt(ss[:, wb:] + 1e-6)
        vb_ref[rs, :] = yc[:, 2 * wb:]

        u = jax.nn.gelu(zc[:, :wc])
        gv = jax.nn.gelu(zc[:, wc:])
        mu = jnp.mean(gv, axis=-1, keepdims=True)
        var = jnp.mean(jnp.square(gv - mu), axis=-1, keepdims=True)
        vn = (gv - mu) * lax.rsqrt(var + 1e-5) * gcv_ref[...] + bcv_ref[...]
        vn_ref[rs, :] = vn
        first = lax.broadcasted_iota(jnp.int32, (cm, LANES), 1) < HEAD_DIM
        kpad = ws_ref.shape[2] - 2 * cm
        rows = []
        for c in range(r // cm):
            vc = vn[c * cm:(c + 1) * cm]
            cols = []
            for pp in range(wc // LANES):
                vp = vc[:, pp * LANES:(pp + 1) * LANES]
                parts = [jnp.where(first, vp, 0.0), jnp.where(first, 0.0, vp)]
                if kpad:
                    parts.append(jnp.zeros((kpad, LANES), F32))
                cols.append(_dot(ws_ref[pp], jnp.concatenate(parts, axis=0)))
            s = jnp.concatenate(cols, axis=-1) + bs_ref[...]
            rows.append(u[c * cm:(c + 1) * cm] * s)
        oc = rows[0] if len(rows) == 1 else jnp.concatenate(rows, axis=0)
        oc_ref[rs, :] = _rms(oc, gco_ref[...])
        return y[r - SUBLANES:r], cum[r - 1:r, :]

    subs = [slice(i * r, (i + 1) * r) for i in range(nsub)]
    projected = [project(rs) for rs in subs]
    if seg:
        prev, cum_in = jnp.zeros(carry_conv.shape, F32), jnp.zeros(carry_cum.shape, F32)
    else:
        prev, cum_in = carry_conv[...], carry_cum[...]
    for rs, z in zip(subs, projected):
        prev, cum_in = finish(rs, *z, prev, cum_in)
    if not seg:
        carry_conv[...] = prev
        carry_cum[...] = cum_in
        ytail_ref[...] = prev


def _inproj_kernel_inplace(*refs, n_in, **kw):
    return _inproj_kernel(*refs[:n_in], *refs[n_in + 2:], **kw)


def _inproj(x, lw, conv_prev, *, cm, layer, depth, kv_prev):
    b0, t0, d = x.shape
    wa, wb, wc, nh = lw["wa"], lw["wb"], lw["wc"], lw["nh"]
    kw1 = conv_prev.shape[1]
    seg = 0 if t0 >= LANES else t0
    if seg:
        assert seg & (seg - 1) == 0 and seg >= SUBLANES
        x = x.reshape(1, b0 * t0, d)
        tm = _pick(b0 * t0, (512, 256, 128, 64, 32, 16))
        assert tm % seg == 0
        conv_init = jnp.stack([jnp.pad(conv_prev[:, kw1 - k:], ((0, 0), (0, t0 - k), (0, 0))) for k in range(1, kw1 + 1)])
        conv_init = conv_init.reshape(kw1, b0 * t0, -1)
    else:
        tm = _pick(t0, (512, 256, 128))
        conv_init = jnp.pad(conv_prev, ((0, 0), (SUBLANES - kw1, 0), (0, 0)))
    b, t, _ = x.shape
    nt = t // tm
    kv_time_minor = not seg
    tok = lambda w: pl.BlockSpec((None, tm, w), lambda i, j: (i, j, 0))
    per_b = lambda r, w: pl.BlockSpec((None, r, w), lambda i, j: (i, 0, 0))
    outs = [("qaug", 2 * wa, BF16), ("ka", wa, F32), ("va", wa, F32), ("kaug", 2 * wa, BF16), ("vab", wa, BF16),
            ("elem", LANES, F32), ("cum", LANES, F32), ("qb", wb, F32), ("kb", wb, F32), ("vb", wb, F32),
            ("bz", wb, F32), ("oc", wc, F32), ("vn", wc, F32)]
    out_shape = [jax.ShapeDtypeStruct((b, t, w), dt) for _, w, dt in outs]
    out_specs = [tok(w) for _, w, _ in outs]
    if kv_time_minor:
        for k in (1, 2):
            out_shape[k] = jax.ShapeDtypeStruct((depth, b, wa, t), F32)
            out_specs[k] = pl.BlockSpec((None, None, wa, tm), lambda i, j: (layer, i, 0, j))
    if seg:
        out_shape.append(jax.ShapeDtypeStruct((b, t, 3 * wb), F32))
        out_specs.append(tok(3 * wb))
        init_spec = pl.BlockSpec((kw1, tm, 3 * wb), lambda i, j: (0, j, 0))
    else:
        out_shape.append(jax.ShapeDtypeStruct((b, SUBLANES, 3 * wb), F32))
        out_specs.append(per_b(SUBLANES, 3 * wb))
        init_spec = per_b(SUBLANES, 3 * wb)
    consts = [lw["g_pre_mix"], lw["w_big"], lw["w_small"], lw["sp"], lw["conv_w"]]
    consts2 = [lw["g_cv"], lw["b_cv"], lw["ws_cat"][cm], lw["bs_full"][cm], lw["g_c_out"], lw["hsum"], lw["pmat"]]
    nsub = 2 if tm % (2 * max(cm, LANES)) == 0 else 1
    kw = dict(tm=tm, cm=cm, nh=nh, wa=wa, wb=wb, wc=wc, scale=HEAD_DIM ** -0.5, kv_time_minor=kv_time_minor,
              nsub=nsub, seg=seg)
    in_specs = ([tok(d)] + [_const_spec(c.shape) for c in consts] + [init_spec]
                + [_const_spec(c.shape) for c in consts2])
    args = [x, *consts, conv_init, *consts2]
    inplace = kv_time_minor and kv_prev is not None
    if inplace:
        kern = functools.partial(_inproj_kernel_inplace, n_in=len(args), **kw)
        aliases = {len(args): 1, len(args) + 1: 2}
        in_specs = in_specs + [pl.BlockSpec(memory_space=pl.ANY)] * 2
        args = args + list(kv_prev)
    else:
        kern = functools.partial(_inproj_kernel, **kw)
        aliases = {}
    res = pl.pallas_call(
        kern,
        grid=(b, nt),
        in_specs=in_specs,
        out_specs=out_specs,
        out_shape=out_shape,
        input_output_aliases=aliases,
        scratch_shapes=[pltpu.VMEM((SUBLANES, 3 * wb), F32), pltpu.VMEM((1, LANES), F32)],
        compiler_params=_params("arbitrary", "arbitrary"),
        name="inproj",
    )(*args)
    named = {n: r for (n, _, _), r in zip(outs, res[:-1])}
    if seg:
        named = {n: r.reshape(b0, t0, r.shape[-1]) for n, r in named.items()}
        named["conv_new"] = res[-1].reshape(b0, t0, -1)[:, t0 - kw1:, :]
    else:
        named["conv_new"] = res[-1][:, SUBLANES - kw1:, :]
    named["kv_time_minor"] = kv_time_minor
    return named


def _attn_kernel(qt_ref, k_ref, vt_ref, o_ref, *, tq, kc, look, qsplit):
    p = pl.program_id(1)
    i = pl.program_id(2)
    qt = qt_ref[...]
    rowi = lax.broadcasted_iota(jnp.int32, qt.shape, 0)
    zero = jnp.zeros_like(qt)
    qts = []
    for e in range(2):
        a0 = LANES + AUG * (2 * p + e)
        keep = ((rowi >= e * HEAD_DIM) & (rowi < (e + 1) * HEAD_DIM)) | ((rowi >= a0) & (rowi < a0 + AUG))
        qts.append(jnp.where(keep, qt, zero))
    wq = tq // qsplit
    kofs = lax.broadcasted_iota(jnp.int32, (kc, wq), 0)
    qofs = lax.broadcasted_iota(jnp.int32, (kc, wq), 1)
    ones = jnp.ones((2 * SUBLANES, kc), BF16)
    units = [(c, e, h) for c in range(tq // kc) for e in range(2) for h in range(qsplit)]
    slots = [(e, h) for e in range(2) for h in range(qsplit)]

    def trim(c, h, masked):
        lo = max(c * kc - h * wq, 0) if masked else 0
        return None if lo >= wq else lo

    def scores(j, c, e, h, masked):
        k0 = pl.multiple_of(j * tq + c * kc, kc)
        lo = trim(c, h, masked)
        s = jnp.dot(k_ref[pl.ds(k0, kc), :], qts[e][:, h * wq + lo:(h + 1) * wq], preferred_element_type=F32)
        return jnp.concatenate([jnp.full((kc, lo), NEG_INF, F32), s], axis=1) if lo else s

    def fold(j, c, e, h, s, st, masked):
        m, l, acc = st
        k0 = pl.multiple_of(j * tq + c * kc, kc)
        lo = trim(c, h, masked)
        if masked:
            s = jnp.where(c * kc + kofs <= h * wq + qofs, s, NEG_INF)
        m_new = jnp.maximum(m, jnp.max(s, axis=0, keepdims=True))
        alpha = jnp.exp2(m - m_new)
        pt = jnp.exp2(s - m_new).astype(BF16)
        vt = jnp.concatenate([vt_ref[e * HEAD_DIM:(e + 1) * HEAD_DIM, pl.ds(k0, kc)], ones], axis=0)
        r = jnp.dot(vt, pt[:, lo:], preferred_element_type=F32)
        if lo:
            r = jnp.concatenate([jnp.zeros((r.shape[0], lo), F32), r], axis=1)
        return m_new, alpha * l + r[HEAD_DIM:HEAD_DIM + 1], alpha * acc + r[:HEAD_DIM]

    def run(blocks, state):
        state = dict(zip(slots, state))
        todo = [(j, c, e, h, masked) for j, masked in blocks for c, e, h in units
                if trim(c, h, masked) is not None]
        pend = {}
        for k in range(min(look, len(todo))):
            pend[k] = scores(*todo[k])
        for k, (j, c, e, h, masked) in enumerate(todo):
            if k + look < len(todo):
                pend[k + look] = scores(*todo[k + look])
            state[e, h] = fold(j, c, e, h, pend.pop(k), state[e, h], masked)
        return tuple(state[sl] for sl in slots)

    st0 = (jnp.full((1, wq), NEG_INF, F32), jnp.zeros((1, wq), F32), jnp.zeros((HEAD_DIM, wq), F32))
    state = lax.fori_loop(0, i // 2, lambda t, s: run([(2 * t, False), (2 * t + 1, False)], s),
                          (st0,) * len(slots))
    state = lax.cond(i % 2 == 1,
                     lambda s: run([(i - 1, False), (i, True)], s),
                     lambda s: run([(i, True)], s), state)
    done = dict(zip(slots, state))
    ot = jnp.concatenate([jnp.concatenate([done[e, h][2] / done[e, h][1] for h in range(qsplit)], axis=1)
                          for e in range(2)], axis=0)
    o_ref[...] = ot.T


def _attn_prompt(qaug, kaug, vab, *, tq, kc, look, qsplit):
    b, s, wa = vab.shape
    npair = wa // LANES
    qt = jnp.transpose(qaug, (0, 2, 1))
    vt = jnp.transpose(vab, (0, 2, 1))
    kern = functools.partial(_attn_kernel, tq=tq, kc=kc, look=look, qsplit=qsplit)
    return pl.pallas_call(
        kern,
        grid=(b, npair, s // tq),
        in_specs=[pl.BlockSpec((None, 2 * LANES, tq), lambda bi, p, i: (bi, p, i)),
                  pl.BlockSpec((None, s, 2 * LANES), lambda bi, p, i: (bi, 0, p)),
                  pl.BlockSpec((None, LANES, s), lambda bi, p, i: (bi, p, 0))],
        out_specs=pl.BlockSpec((None, tq, LANES), lambda bi, p, i: (bi, i, p)),
        out_shape=jax.ShapeDtypeStruct((b, s, wa), F32),
        compiler_params=_params("arbitrary", "arbitrary", "arbitrary"),
        name="attn_prompt",
    )(qt, kaug, vt)


def _attn_sample_kernel(q_ref, kc_ref, vc_ref, kn_ref, vn_ref, cum_ref, rrow_ref, crow_ref, o_ref, *, n):
    p = pl.program_id(1)
    q = q_ref[:, :LANES]
    lane = lax.broadcasted_iota(jnp.int32, (n, LANES), 1)
    first = lane < HEAD_DIM
    zero = jnp.zeros_like(q)
    past = kc_ref.shape[-1]
    kc = kc_ref[...].reshape(LANES, past).astype(BF16)
    vc = vc_ref[...].reshape(LANES, past).astype(BF16)
    pad = jnp.zeros((LANES - n, LANES), BF16)
    kn = jnp.concatenate([kn_ref[:, :LANES], pad], axis=0)
    vn = jnp.concatenate([vn_ref[...], pad], axis=0)
    cum = cum_ref[...]
    qm = jnp.concatenate([jnp.where(first, q, zero), jnp.where(first, zero, q)], axis=0)
    cq = jnp.concatenate([jnp.sum(jnp.where(lane == 2 * p + e, cum, 0.0), axis=-1, keepdims=True)
                          for e in range(2)], axis=0)
    top = lax.broadcasted_iota(jnp.int32, (2 * n, 1), 0) < n
    rrow = jnp.where(top, rrow_ref[0:1, :], rrow_ref[1:2, :])
    crow = jnp.where(top, crow_ref[0:1, :], crow_ref[1:2, :])
    qrow = lax.broadcasted_iota(jnp.int32, (2 * n, LANES), 0)
    causal = lax.broadcasted_iota(jnp.int32, (2 * n, LANES), 1) <= jnp.where(qrow < n, qrow, qrow - n)
    sc = _dot(qm, kc) + LOG2E * (cq + rrow)
    sn = jnp.where(causal, _dot_nt(qm, kn) + LOG2E * (cq - crow), NEG_INF)
    m = jnp.maximum(jnp.max(sc, axis=-1, keepdims=True), jnp.max(sn, axis=-1, keepdims=True))
    pc = jnp.exp2(sc - m)
    pn = jnp.exp2(sn - m)
    l = jnp.sum(pc, axis=-1, keepdims=True) + jnp.sum(pn, axis=-1, keepdims=True)
    o = (_dot_nt(pc, vc) + _dot(pn, vn)) / l
    o_ref[...] = jnp.where(first, o[:n], o[n:])


def _attn_sample(qaug, kaug, vab, cum, cache_kt, cache_vt, layer, rrow, crow):
    b, n, wa = vab.shape
    past = cache_kt.shape[-1]
    npair = wa // LANES
    new = lambda w: pl.BlockSpec((None, n, w), lambda bi, p: (bi, 0, p))
    old = lambda: pl.BlockSpec((None, None, 2, HEAD_DIM, past), lambda bi, p: (layer, bi, p, 0, 0))
    return pl.pallas_call(
        functools.partial(_attn_sample_kernel, n=n),
        grid=(b, npair),
        in_specs=[new(2 * LANES), old(), old(), new(2 * LANES), new(LANES),
                  pl.BlockSpec((None, n, LANES), lambda bi, p: (bi, 0, 0)),
                  pl.BlockSpec((None, None, 2, past), lambda bi, p: (bi, p, 0, 0)),
                  pl.BlockSpec((None, None, 2, LANES), lambda bi, p: (bi, p, 0, 0))],
        out_specs=new(LANES),
        out_shape=jax.ShapeDtypeStruct((b, n, wa), F32),
        compiler_params=_params("arbitrary", "arbitrary"),
        name="attn_sample",
    )(qaug, cache_kt, cache_vt, kaug, vab, cum, rrow, crow)


def _suffix_kernel(x_ref, o_ref):
    v = x_ref[...]
    n = v.shape[1]
    lane = lax.broadcasted_iota(jnp.int32, v.shape, 1)
    s = 1
    while s < n:
        v = v + jnp.where(lane + s < n, pltpu.roll(v, n - s, 1), 0.0)
        s *= 2
    o_ref[...] = jnp.where(lane + 1 < n, pltpu.roll(v, n - 1, 1), 0.0)


def _exclusive_suffix_sum(x, layer):
    _, h, b, p = x.shape
    return pl.pallas_call(
        _suffix_kernel,
        grid=(h,),
        in_specs=[pl.BlockSpec((None, None, b, p), lambda i: (layer, i, 0, 0))],
        out_specs=pl.BlockSpec((None, b, p), lambda i: (i, 0, 0)),
        out_shape=jax.ShapeDtypeStruct((h, b, p), F32),
        compiler_params=_params("arbitrary"),
        name="suffix_sum",
    )(x)


def _gdn_kernel(q_ref, k_ref, v_ref, bz_ref, elem_ref, s0_ref, gb_ref, esel_ref, hsum_ref,
                o_ref, sout_ref, s_scr, *, nb, nh, chained):
    L = GDN_BLOCK
    t = pl.program_id(1)

    if chained:
        @pl.when(t == 0)
        def _():
            s_scr[...] = s0_ref[...]

    lane = lax.broadcasted_iota(jnp.int32, (L, LANES), 1)
    first = lane < HEAD_DIM
    ri = lax.broadcasted_iota(jnp.int32, (L, L), 0)
    ci = lax.broadcasted_iota(jnp.int32, (L, L), 1)
    incl = ci <= ri
    strict = ci < ri
    same_head = (ri < HEAD_DIM) == (ci < HEAD_DIM)
    lane2 = lax.broadcasted_iota(jnp.int32, (L, 2 * L), 1)
    first2 = jnp.bitwise_and(lane2, LANES - 1) < HEAD_DIM
    xor2 = jnp.bitwise_xor(lax.broadcasted_iota(jnp.int32, (L, 2 * L), 0), jnp.bitwise_and(lane2, L - 1))
    zero_ll = jnp.zeros((L, L), BF16)

    def halves(x, sel):
        return jnp.concatenate([jnp.where(sel, x, 0.0), jnp.where(sel, 0.0, x)], axis=0)

    def dot_heads(y, x):
        xb = x.astype(BF16)
        bd = jnp.concatenate([jnp.concatenate([xb[:, :L], zero_ll], axis=1),
                              jnp.concatenate([zero_ll, xb[:, L:]], axis=1)], axis=0)
        return jnp.dot(y.astype(BF16), bd, preferred_element_type=F32)

    npair = s_scr.shape[0]
    wbw = npair * LANES
    c = {}

    def solve_stages(blocks):
        chains = [(n, p) for n in blocks for p in range(npair)]
        ex = {}
        for n in blocks:
            elem = elem_ref[n * L:(n + 1) * L, :]
            gsum = _seg_cumsum(elem, L)
            mixed = jnp.where((lane >= nh) & (lane < 2 * nh), gsum, elem)
            ex[n] = _dot_select_exact(mixed, esel_ref[...])
        yield
        for n, p in chains:
            rows = slice(n * L, (n + 1) * L)
            cols = slice(p * LANES, (p + 1) * LANES)
            g = ex[n][:, cols]
            bt = ex[n][:, wbw + p * LANES: wbw + (p + 1) * LANES]
            kp = k_ref[rows, cols]
            qp = q_ref[rows, cols]
            g_sw = pltpu.roll(g, HEAD_DIM, 1)
            b_sw = pltpu.roll(bt, HEAD_DIM, 1)
            g_t = g.T
            a_parts, qk_parts = [], []
            for e in range(2):
                sel = first if e == 0 else jnp.logical_not(first)
                gcol = jnp.where(sel, g, g_sw)
                bcol = jnp.where(sel, bt, b_sw)
                grow = g_t[e * HEAD_DIM:e * HEAD_DIM + 1, :]
                dec = jnp.exp(jnp.where(incl, gcol - grow, NEG_INF))
                kk = _dot_nt(jnp.where(sel, kp, 0.0), kp)
                qk_parts.append(_dot_nt(jnp.where(sel, qp, 0.0), kp) * dec)
                a_parts.append(jnp.where(strict, bcol * kk * dec, 0.0))
            a_cat = jnp.concatenate(a_parts, axis=1)
            eg = jnp.exp(g)
            glast = g[L - 1:L, :]
            c[n, p] = dict(a=a_cat, qk=jnp.concatenate(qk_parts, axis=1), glast=glast, qg=qp * eg,
                           kdec=kp * jnp.exp(glast - g),
                           r=jnp.concatenate([v_ref[rows, cols] * bt, kp * bt * eg], axis=1),
                           tm1=-jnp.where(xor2 < 2, a_cat, 0.0))
        yield
        s_blk = 2
        while s_blk < L:
            pm = {}
            for key in chains:
                nmat = jnp.where((xor2 >= s_blk) & (xor2 < 2 * s_blk), c[key]["a"], 0.0)
                pm[key] = nmat + dot_heads(c[key]["tm1"], nmat)
            yield
            for key in chains:
                c[key]["tm1"] = c[key]["tm1"] - pm[key] - dot_heads(pm[key], c[key]["tm1"])
            yield
            s_blk *= 2
        for key in chains:
            r = c[key]["r"]
            c[key]["uw"] = r + _dot(c[key]["tm1"], halves(r, first2))
        yield

    def state_stages(blocks):
        pairs = range(npair)
        for n in blocks:
            rows = slice(n * L, (n + 1) * L)
            s_in = [s_scr[p] if chained else s0_ref[n, p] for p in pairs]
            ws = [_dot(jnp.concatenate([c[n, p]["uw"][:, LANES:], c[n, p]["qg"]], axis=0), s_in[p]) for p in pairs]
            yield
            u = [c[n, p]["uw"][:, :LANES] - ws[p][:L] for p in pairs]
            o = [ws[p][L:] + _dot(c[n, p]["qk"], halves(u[p], first)) for p in pairs]
            yield
            for p in pairs:
                s_new = (s_in[p] * jnp.exp(c[n, p]["glast"])
                         + jnp.where(same_head, _dot(c[n, p]["kdec"].T, u[p]), 0.0))
                if chained:
                    s_scr[p] = s_new
                else:
                    sout_ref[n, p] = s_new
            yield
            for p in pairs:
                cols = slice(p * LANES, (p + 1) * LANES)
                ms = _dot(o[p] * o[p], hsum_ref[...]) * (1.0 / HEAD_DIM)
                bz = bz_ref[rows, cols]
                o_ref[rows, cols] = o[p] * lax.rsqrt(ms + 1e-6) * gb_ref[...] * (bz * _sigmoid(bz))
            yield

    groups = [list(range(g0, min(g0 + GDN_GROUP, nb))) for g0 in range(0, nb, GDN_GROUP)]
    pending = iter(())
    for grp in groups:
        for _ in solve_stages(grp):
            next(pending, None)
        for _ in pending:
            pass
        pending = state_stages(grp)
    for _ in pending:
        pass

    if chained:
        @pl.when(t == pl.num_programs(1) - 1)
        def _():
            sout_ref[...] = s_scr[...]


def _gdn(qb, kb, vb, bz, elem, s0, lw, *, nb):
    b, t, wb = qb.shape
    npair = wb // LANES
    chained = t > GDN_BLOCK
    if not chained:
        qb, kb, vb, bz, elem = (a.reshape(1, b * t, a.shape[-1]) for a in (qb, kb, vb, bz, elem))
        nb = _pick(b, (GDN_TILE_BLOCKS, 2, 1))
    rows, total = qb.shape[:2]
    tile = nb * GDN_BLOCK
    tok = lambda w: pl.BlockSpec((None, tile, w), lambda i, j: (i, j, 0))
    if chained:
        st = pl.BlockSpec((None, npair, LANES, LANES), lambda i, j: (i, 0, 0, 0))
    else:
        st = pl.BlockSpec((nb, npair, LANES, LANES), lambda i, j: (j, 0, 0, 0))
    consts = [lw["g_b_pair"], lw["esel"], lw["hsum128"]]
    ob, s_new = pl.pallas_call(
        functools.partial(_gdn_kernel, nb=nb, nh=lw["nh"], chained=chained),
        grid=(rows, total // tile),
        in_specs=[tok(wb), tok(wb), tok(wb), tok(wb), tok(LANES), st] + [_const_spec(c.shape) for c in consts],
        out_specs=[tok(wb), st],
        out_shape=[jax.ShapeDtypeStruct((rows, total, wb), F32),
                   jax.ShapeDtypeStruct((b, npair, LANES, LANES), F32)],
        scratch_shapes=[pltpu.VMEM((npair, LANES, LANES), F32)],
        compiler_params=_params("arbitrary", "arbitrary"),
        name="gdn",
    )(qb, kb, vb, bz, elem, s0, *consts)
    return ob.reshape(b, t, wb), s_new


def _outffn_kernel(oa_ref, ob_ref, oc_ref, x_ref, ga_ref, wout_ref, gpm_ref, gpf_ref, wfi_ref, wfo_ref,
                   gpo_ref, y_ref, *, dff, nsub):
    r = x_ref.shape[0] // nsub
    rows = [slice(i * r, (i + 1) * r) for i in range(nsub)]
    cat = [jnp.concatenate([_rms(oa_ref[rs, :], ga_ref[...]), ob_ref[rs, :], oc_ref[rs, :]], axis=-1).astype(BF16)
           for rs in rows]
    m = [jnp.dot(c, wout_ref[...], preferred_element_type=F32) for c in cat]
    x1 = [x_ref[rs, :] + _rms(mi, gpm_ref[...]) for rs, mi in zip(rows, m)]
    h = [_rms(xi, gpf_ref[...]).astype(BF16) for xi in x1]
    wch = dff // FFN_CHUNKS
    f = [None] * nsub
    for c in range(FFN_CHUNKS):
        gate = [jnp.dot(hi, wfi_ref[:, c * wch:(c + 1) * wch], preferred_element_type=F32) for hi in h]
        up = [jnp.dot(hi, wfi_ref[:, dff + c * wch:dff + (c + 1) * wch], preferred_element_type=F32) for hi in h]
        a = [(g * _sigmoid(g) * u).astype(BF16) for g, u in zip(gate, up)]
        part = [jnp.dot(ai, wfo_ref[c * wch:(c + 1) * wch, :], preferred_element_type=F32) for ai in a]
        f = part if c == 0 else [fi + pi for fi, pi in zip(f, part)]
    for rs, xi, fi in zip(rows, x1, f):
        y_ref[rs, :] = xi + _rms(fi, gpo_ref[...])


def _outffn(oa, ob, oc, x, lw, *, tm):
    n, d = x.shape
    dff = lw["w_ffn_out"].shape[0]
    tok = lambda w: pl.BlockSpec((tm, w), lambda i: (i, 0))
    consts = [lw["g_a_out"], lw["w_out"], lw["g_post_mix"], lw["g_pre_ffn"], lw["w_ffn_in"],
              lw["w_ffn_out"], lw["g_post_ffn"]]
    return pl.pallas_call(
        functools.partial(_outffn_kernel, dff=dff, nsub=2 if tm % (4 * SUBLANES) == 0 else 1),
        grid=(n // tm,),
        in_specs=[tok(oa.shape[1]), tok(ob.shape[1]), tok(oc.shape[1]), tok(d)]
                 + [_const_spec(c.shape) for c in consts],
        out_specs=tok(d),
        out_shape=jax.ShapeDtypeStruct((n, d), F32),
        compiler_params=_params("arbitrary"),
        name="outffn",
    )(oa, ob, oc, x, *consts)


def _block_ones(width):
    idx = jnp.arange(width) // HEAD_DIM
    return (idx[:, None] == idx[None, :]).astype(BF16)


def _layer_weights(l, prm, cms):
    w_in = prm["w_in"][l]
    nh = prm["b_f"].shape[1]
    wa = nh * HEAD_DIM
    wb = prm["a_log"].shape[1] * HEAD_DIM
    wc = prm["g_cv"].shape[1]
    ng = prm["w_s"].shape[1]
    assert prm["a_log"].shape[1] == nh and wa % LANES == 0 and wc % LANES == 0 and AUG * nh <= LANES
    sizes = (wa, wa, wa, nh, 3 * wb, nh, nh, wb, wc, wc)
    offs = [0]
    for sz in sizes:
        offs.append(offs[-1] + sz)
    w_in_t = w_in.T
    col = lambda i: w_in_t[offs[i]:offs[i + 1]]
    w_big = jnp.concatenate([col(4), col(8), col(9), col(0), col(1), col(2), col(7)], axis=0).astype(BF16)
    w_small = jnp.concatenate([col(3), col(5), col(6), jnp.zeros((LANES - 3 * nh, w_in.shape[0]), F32)],
                              axis=0).astype(BF16)
    zpad = jnp.zeros((LANES - 2 * nh,), F32)
    sp = jnp.zeros((SUBLANES, LANES), F32)
    sp = sp.at[0].set(jnp.concatenate([prm["b_f"][l], prm["dt_bias"][l], zpad]))
    sp = sp.at[1].set(jnp.concatenate([jnp.zeros((nh,), F32), prm["a_log"][l], zpad]))
    hl = jnp.arange(nh) * AUG
    sp = sp.at[2, (hl[:, None] + jnp.arange(3, 6)[None, :]).reshape(-1)].set(1.0)
    sp = sp.at[3, (hl[:, None] + jnp.arange(0, 3)[None, :]).reshape(-1)].set(1.0)
    pmat = jnp.zeros((3 * LANES, 2 * LANES), F32)
    for piece in range(3):
        pmat = pmat.at[piece * LANES + jnp.arange(nh), hl + piece].set(1.0)
        pmat = pmat.at[piece * LANES + jnp.arange(nh), LANES + hl + 3 + piece].set(1.0)
    row = lambda v: v.reshape(1, -1)
    ws_cat, bs_full = {}, {}
    for cm in cms:
        pos = jnp.arange(cm) // HEAD_DIM
        w = jnp.where(pos[None, :] <= pos[:, None], prm["w_s"][l][:, :cm, :cm], 0.0)
        pairs = [jnp.concatenate([w[2 * pp], w[2 * pp + 1]], axis=1) for pp in range(ng // 2)]
        kpad = max(LANES - 2 * cm, 0)
        ws_cat[cm] = jnp.pad(jnp.stack(pairs), ((0, 0), (0, 0), (0, kpad))).astype(BF16)
        bs_full[cm] = jnp.repeat(prm["b_s"][l][:, :cm].T, wc // ng, axis=1)
    src = jnp.arange(LANES)[:, None]
    dst = jnp.arange(wb)[None, :] // HEAD_DIM
    esel = jnp.concatenate([src == nh + dst, src == 2 * nh + dst], axis=1).astype(BF16)
    return dict(
        nh=nh, wa=wa, wb=wb, wc=wc,
        g_pre_mix=row(prm["g_pre_mix"][l]), w_big=w_big, w_small=w_small, sp=sp, conv_w=prm["conv_w"][l],
        g_cv=row(prm["g_cv"][l]), b_cv=row(prm["b_cv"][l]), ws_cat=ws_cat, bs_full=bs_full,
        g_c_out=row(prm["g_c_out"][l]), hsum=_block_ones(2 * LANES), hsum128=_block_ones(LANES), pmat=pmat.astype(BF16),
        g_b_pair=row(jnp.tile(prm["g_b_out"][l], LANES // HEAD_DIM)), esel=esel,
        g_a_out=row(prm["g_a_out"][l]), w_out=prm["w_out"][l].astype(BF16),
        g_post_mix=row(prm["g_post_mix"][l]), g_pre_ffn=row(prm["g_pre_ffn"][l]),
        w_ffn_in=prm["w_ffn_in"][l].astype(BF16), w_ffn_out=prm["w_ffn_out"][l].astype(BF16),
        g_post_ffn=row(prm["g_post_ffn"][l]))


def _pair_state(s):
    b, h, dk, dv = s.shape
    s = s.reshape(b, h // 2, 2, dk, dv)
    z = jnp.zeros_like(s[:, :, 0])
    top = jnp.concatenate([s[:, :, 0], z], axis=-1)
    bot = jnp.concatenate([z, s[:, :, 1]], axis=-1)
    return jnp.concatenate([top, bot], axis=-2)


def _unpair_state(sp):
    b, hp, _, _ = sp.shape
    s0 = sp[:, :, :HEAD_DIM, :HEAD_DIM]
    s1 = sp[:, :, HEAD_DIM:, HEAD_DIM:]
    return jnp.stack([s0, s1], axis=2).reshape(b, 2 * hp, HEAD_DIM, HEAD_DIM)


def _head_rows(cum, nh):
    b, t, _ = cum.shape
    return jnp.transpose(cum[:, :, :nh], (0, 2, 1)).reshape(b, nh // 2, 2, t)


def _pick(n, prefs):
    for c in prefs:
        if n % c == 0:
            return c
    return n


def _layer(x, lw, conv_prev, s0, cache, *, cm, layer, depth, kv_prev):
    b, t, d = x.shape
    nh, wb = lw["nh"], lw["wb"]
    pj = _inproj(x, lw, conv_prev, cm=cm, layer=layer, depth=depth, kv_prev=kv_prev)

    if cache is None:
        tq = _pick(t, (ATTN_TQ, 256, 128))
        kc = min(ATTN_KC, tq)
        qsplit = ATTN_QSPLIT if tq % (ATTN_QSPLIT * kc) == 0 else 1
        oa = _attn_prompt(pj["qaug"], pj["kaug"], pj["vab"], tq=tq, kc=kc, look=ATTN_LOOK, qsplit=qsplit)
    else:
        ck, cv, clogf_t = cache
        _, _, bs, past = clogf_t.shape
        excl = _exclusive_suffix_sum(clogf_t, layer)
        rrow = jnp.transpose(excl, (1, 0, 2)).reshape(bs, nh // 2, 2, past)
        crow = jnp.pad(_head_rows(pj["cum"], nh), ((0, 0), (0, 0), (0, 0), (0, LANES - t)))
        oa = _attn_sample(pj["qaug"], pj["kaug"], pj["vab"], pj["cum"], ck, cv, layer, rrow, crow)

    tp = -(-t // GDN_BLOCK) * GDN_BLOCK
    padt = lambda a: a if tp == t else jnp.pad(a, ((0, 0), (0, tp - t), (0, 0)))
    nb = _pick(tp // GDN_BLOCK, (GDN_TILE_BLOCKS, 2, 1))
    ob, s_new = _gdn(padt(pj["qb"]), padt(pj["kb"]), padt(pj["vb"]), padt(pj["bz"]), padt(pj["elem"]),
                     _pair_state(s0), lw, nb=nb)
    ob = ob[:, :t]

    n = b * t
    y = _outffn(oa.reshape(n, -1), ob.reshape(n, -1), pj["oc"].reshape(n, -1), x.reshape(n, d), lw,
                tm=_pick(n, (512, 256, 128, 64, 32, 16)))
    if pj["kv_time_minor"]:
        new_kv = (pj["ka"], pj["va"])
    else:
        new_kv = (pj["ka"].reshape(b, t, nh, HEAD_DIM), pj["va"].reshape(b, t, nh, HEAD_DIM))
    state = (new_kv[0], new_kv[1], pj["elem"][:, :, :nh],
             pj["conv_new"], _unpair_state(s_new), pj["vn"])
    return y.reshape(b, t, d), state, pj["kv_time_minor"]


def kernel(x_prompt, x_sample, cache_a_k, cache_a_v, cache_a_logf, state_b_conv, state_b_S, g_pre_mix, w_in, b_f, conv_w, a_log, dt_bias, g_b_out, g_a_out, g_cv, b_cv, w_s, b_s, g_c_out, w_out, g_post_mix, g_pre_ffn, w_ffn_in, w_ffn_out, g_post_ffn):
    prm = dict(g_pre_mix=g_pre_mix, w_in=w_in, b_f=b_f, conv_w=conv_w, a_log=a_log, dt_bias=dt_bias,
               g_b_out=g_b_out, g_a_out=g_a_out, g_cv=g_cv, b_cv=b_cv, w_s=w_s, b_s=b_s, g_c_out=g_c_out,
               w_out=w_out, g_post_mix=g_post_mix, g_pre_ffn=g_pre_ffn, w_ffn_in=w_ffn_in,
               w_ffn_out=w_ffn_out, g_post_ffn=g_post_ffn)
    depth = w_in.shape[0]
    bp, sp_len, _ = x_prompt.shape
    n_new = x_sample.shape[1]
    cm_p = w_s.shape[2]
    assert sp_len % cm_p == 0 and sp_len % GDN_BLOCK == 0 and n_new <= HEAD_DIM and n_new % SUBLANES == 0
    kw1 = conv_w.shape[1] - 1
    nhb = a_log.shape[1]
    yp, ys = x_prompt, x_sample
    outs_p, outs_s = [], []
    cache_kt = jnp.transpose(cache_a_k, (0, 1, 3, 4, 2))
    cache_vt = jnp.transpose(cache_a_v, (0, 1, 3, 4, 2))
    clogf_t = jnp.transpose(cache_a_logf, (0, 3, 1, 2))
    for l in range(depth):
        lw = _layer_weights(l, prm, (cm_p, n_new))
        conv0 = jnp.zeros((bp, kw1, conv_w.shape[2]), F32)
        s0 = jnp.zeros((bp, nhb, HEAD_DIM, HEAD_DIM), F32)
        kv_p = (outs_p[-1][0], outs_p[-1][1]) if outs_p else None
        kv_s = (outs_s[-1][0], outs_s[-1][1]) if outs_s else None
        yp, st_p, shared_p = _layer(yp, lw, conv0, s0, None, cm=cm_p, layer=l, depth=depth, kv_prev=kv_p)
        ys, st_s, shared_s = _layer(ys, lw, state_b_conv[l], state_b_S[l], (cache_kt, cache_vt, clogf_t),
                                    cm=n_new, layer=l, depth=depth, kv_prev=kv_s)
        outs_p.append(st_p)
        outs_s.append(st_s)
    stk = lambda outs, i: jnp.stack([o[i] for o in outs], axis=0)

    def new_cache(outs, i, shared):
        if not shared:
            return stk(outs, i)
        buf = outs[-1][i]
        dp, b, _, t = buf.shape
        return jnp.transpose(buf.reshape(dp, b, -1, HEAD_DIM, t), (0, 1, 4, 2, 3))

    return (yp, ys, new_cache(outs_p, 0, shared_p), new_cache(outs_p, 1, shared_p),
            stk(outs_p, 2), stk(outs_p, 3), stk(outs_p, 4),
            new_cache(outs_s, 0, shared_s), new_cache(outs_s, 1, shared_s),
            stk(outs_s, 2), stk(outs_s, 3), stk(outs_s, 4), stk(outs_s, 5))
```
